```python
import math
import jax, jax.numpy as jnp
from jax import lax
import numpy as np

D_MODEL = 2048
BATCH = 8
SEQ = 8192
DEPTH = 4

CHUNK = 64
N_MIXERS = 2
N_GDN_LAYERS = (DEPTH + 1) // 2
N_S5_LAYERS = DEPTH // 2
GDN_K_HEADS = 16
GDN_V_HEADS = 32
GDN_HEAD_K = 128
GDN_HEAD_V = 128
GDN_CONV = 4
GDN_KEY_DIM = GDN_K_HEADS * GDN_HEAD_K
GDN_VAL_DIM = GDN_V_HEADS * GDN_HEAD_V
GDN_CONV_DIM = 2 * GDN_KEY_DIM + GDN_VAL_DIM
GDN_PROJ_DIM = GDN_CONV_DIM + GDN_VAL_DIM + 2 * GDN_V_HEADS
S5_GROUP_CH = 16
S5_GROUPS = D_MODEL // S5_GROUP_CH
S5_STATE = 64
FF_DIM = 4 * D_MODEL
PLE_DIM = 256
NORM_EPS = 1e-6
L2_EPS = 1e-6

kernel_name = 'hybrid_gdn_s5_stream_encoder'


def rms_norm(x, g):
    xf = x.astype(jnp.float32)
    y = xf * lax.rsqrt(jnp.mean(xf * xf, axis=-1, keepdims=True) + NORM_EPS)
    return (y * g.astype(jnp.float32)).astype(x.dtype)


def l2_normalize(x):
    xf = x.astype(jnp.float32)
    return xf * lax.rsqrt(jnp.sum(xf * xf, axis=-1, keepdims=True) + L2_EPS)


def causal_depthwise_conv(x, w):
    k, c = w.shape
    return lax.conv_general_dilated(
        x, w[:, None, :].astype(x.dtype), window_strides=(1,), padding=[(k - 1, 0)],
        dimension_numbers=('NWC', 'WIO', 'NWC'), feature_group_count=c)


def gated_delta_rule_chunked(q, k, v, g, beta):
    bsz, seq, nh, dk = q.shape
    dv = v.shape[-1]
    nc = seq // CHUNK

    def to_chunks(t):
        return t.reshape(bsz, nc, CHUNK, nh, -1).transpose(0, 3, 1, 2, 4)

    q, k, v = to_chunks(q), to_chunks(k), to_chunks(v)
    g = to_chunks(g[..., None])[..., 0]
    beta = to_chunks(beta[..., None])[..., 0]
    gc = jnp.cumsum(g, axis=-1)
    idx = jnp.arange(CHUNK)
    causal = idx[:, None] >= idx[None, :]
    strict = idx[:, None] > idx[None, :]
    decay = jnp.where(causal, jnp.exp(jnp.minimum(gc[..., :, None] - gc[..., None, :], 0.0)), 0.0)
    kb = k * beta[..., None]
    m = jnp.where(strict, jnp.einsum('bhnid,bhnjd->bhnij', kb, k) * decay, 0.0)
    eye = jnp.eye(CHUNK, dtype=m.dtype)
    t_inv = lax.linalg.triangular_solve(eye + m, jnp.broadcast_to(eye, m.shape),
                                        left_side=True, lower=True, unit_diagonal=True)
    u = jnp.einsum('bhnij,bhnjd->bhnid', t_inv, v * beta[..., None])
    w = jnp.einsum('bhnij,bhnjd->bhnid', t_inv, kb * jnp.exp(gc)[..., None])
    qk = jnp.where(causal, jnp.einsum('bhnid,bhnjd->bhnij', q, k) * decay, 0.0)
    g_last = gc[..., -1]
    qg = q * jnp.exp(gc)[..., None]
    kd = k * jnp.exp(g_last[..., None] - gc)[..., None]

    def step(state, xs):
        qg_n, kd_n, w_n, u_n, qk_n, gl_n = xs
        v_new = u_n - jnp.einsum('bhck,bhkv->bhcv', w_n, state)
        o_n = (jnp.einsum('bhck,bhkv->bhcv', qg_n, state)
               + jnp.einsum('bhij,bhjv->bhiv', qk_n, v_new))
        state = state * jnp.exp(gl_n)[..., None, None] + jnp.einsum('bhck,bhcv->bhkv', kd_n, v_new)
        return state, o_n

    s0 = jnp.zeros((bsz, nh, dk, dv), jnp.float32)
    xs = tuple(jnp.moveaxis(t, 2, 0) for t in (qg, kd, w, u, qk, g_last))
    _, o = lax.scan(step, s0, xs)
    return o.transpose(1, 0, 3, 2, 4).reshape(bsz, seq, nh, dv)


def gdn_mixer(hn, w_in, conv_w, a_log, dt_bias, o_norm, w_out):
    bsz, seq, _ = hn.shape
    proj = hn @ w_in
    qkv = proj[..., :GDN_CONV_DIM]
    z = proj[..., GDN_CONV_DIM:GDN_CONV_DIM + GDN_VAL_DIM]
    b = proj[..., GDN_CONV_DIM + GDN_VAL_DIM:GDN_CONV_DIM + GDN_VAL_DIM + GDN_V_HEADS]
    a = proj[..., GDN_CONV_DIM + GDN_VAL_DIM + GDN_V_HEADS:]
    qkv = jax.nn.silu(causal_depthwise_conv(qkv, conv_w))
    q = qkv[..., :GDN_KEY_DIM].reshape(bsz, seq, GDN_K_HEADS, GDN_HEAD_K)
    k = qkv[..., GDN_KEY_DIM:2 * GDN_KEY_DIM].reshape(bsz, seq, GDN_K_HEADS, GDN_HEAD_K)
    v = qkv[..., 2 * GDN_KEY_DIM:].reshape(bsz, seq, GDN_V_HEADS, GDN_HEAD_V).astype(jnp.float32)
    rep = GDN_V_HEADS // GDN_K_HEADS
    q = jnp.repeat(l2_normalize(q), rep, axis=2) * (GDN_HEAD_K ** -0.5)
    k = jnp.repeat(l2_normalize(k), rep, axis=2)
    beta = jax.nn.sigmoid(b.astype(jnp.float32))
    g = -jnp.exp(a_log.astype(jnp.float32)) * jax.nn.softplus(a.astype(jnp.float32) + dt_bias.astype(jnp.float32))
    o = gated_delta_rule_chunked(q, k, v, g, beta)
    o = rms_norm(o, o_norm) * jax.nn.silu(z.reshape(bsz, seq, GDN_V_HEADS, GDN_HEAD_V).astype(jnp.float32))
    return o.reshape(bsz, seq, GDN_VAL_DIM).astype(hn.dtype) @ w_out


def s5_mixer(hn, w_in, lam_re, lam_im, log_step, b_re, b_im, c_re, c_im, d_skip, w_out):
    bsz, seq, _ = hn.shape
    u = (hn @ w_in).astype(jnp.float32)
    ug = u.reshape(bsz, seq, S5_GROUPS, S5_GROUP_CH).astype(jnp.complex64)
    lam = lax.complex(lam_re.astype(jnp.float32), lam_im.astype(jnp.float32))
    step = jnp.exp(log_step.astype(jnp.float32))[:, None]
    lam_bar = jnp.exp(lam * step)
    b_bar = lax.complex(b_re.astype(jnp.float32), b_im.astype(jnp.float32)) * ((lam_bar - 1.0) / lam)[..., None]
    bu = jnp.einsum('bsgc,gpc->bsgp', ug, b_bar)
    a = jnp.broadcast_to(lam_bar, bu.shape)

    def combine(left, right):
        a1, x1 = left
        a2, x2 = right
        return a1 * a2, a2 * x1 + x2

    _, states = lax.associative_scan(combine, (a, bu), axis=1)
    c = lax.complex(c_re.astype(jnp.float32), c_im.astype(jnp.float32))
    y = jnp.einsum('bsgp,gcp->bsgc', states, c).real.reshape(bsz, seq, D_MODEL)
    y = y + d_skip.astype(jnp.float32) * u
    h = jax.nn.gelu(y).astype(hn.dtype)
    vg = h @ w_out
    return vg[..., :D_MODEL] * jax.nn.sigmoid(vg[..., D_MODEL:])


def squared_relu_mlp(hn, w_up, w_down):
    return jnp.square(jax.nn.relu(hn @ w_up)) @ w_down


def _fwd_setup_inputs(seed: int = 0) -> dict:
    key = jax.random.key(seed)
    ks = iter(jax.random.split(key, 40))

    def normal(shape, scale):
        return jax.random.normal(next(ks), shape, jnp.float32) * scale

    def gain(shape):
        return 1.0 + normal(shape, 0.02)

    x = normal((BATCH, SEQ, D_MODEL), 1.0)
    p = normal((DEPTH, BATCH, SEQ, PLE_DIM), 1.0)
    norm_mix = gain((DEPTH, D_MODEL))
    norm_mlp = gain((DEPTH, D_MODEL))
    norm_ple = gain((DEPTH, D_MODEL))
    norm_final = gain((D_MODEL,))
    gdn_w_in = normal((N_GDN_LAYERS, D_MODEL, GDN_PROJ_DIM), D_MODEL ** -0.5)
    gdn_conv_w = normal((N_GDN_LAYERS, GDN_CONV, GDN_CONV_DIM), GDN_CONV ** -0.5)
    gdn_a_log = jnp.log(jax.random.uniform(next(ks), (N_GDN_LAYERS, GDN_V_HEADS), jnp.float32, 1.0, 16.0))
    dt = jnp.exp(jax.random.uniform(next(ks), (N_GDN_LAYERS, GDN_V_HEADS), jnp.float32,
                                    math.log(1e-3), math.log(1e-1)))
    gdn_dt_bias = dt + jnp.log(-jnp.expm1(-dt))
    gdn_o_norm = gain((N_GDN_LAYERS, GDN_HEAD_V))
    gdn_w_out = normal((N_GDN_LAYERS, GDN_VAL_DIM, D_MODEL), GDN_VAL_DIM ** -0.5)
    s5_w_in = normal((N_S5_LAYERS, D_MODEL, D_MODEL), D_MODEL ** -0.5)
    n_idx = jnp.arange(S5_STATE, dtype=jnp.float32)
    s5_lam_re = -0.5 + normal((N_S5_LAYERS, S5_GROUPS, S5_STATE), 0.01)
    s5_lam_im = math.pi * n_idx + normal((N_S5_LAYERS, S5_GROUPS, S5_STATE), 0.01)
    s5_log_step = jax.random.uniform(next(ks), (N_S5_LAYERS, S5_GROUPS), jnp.float32,
                                     math.log(1e-3), math.log(1e-1))
    s5_b_re = normal((N_S5_LAYERS, S5_GROUPS, S5_STATE, S5_GROUP_CH), (2 * S5_GROUP_CH) ** -0.5)
    s5_b_im = normal((N_S5_LAYERS, S5_GROUPS, S5_STATE, S5_GROUP_CH), (2 * S5_GROUP_CH) ** -0.5)
    s5_c_re = normal((N_S5_LAYERS, S5_GROUPS, S5_GROUP_CH, S5_STATE), S5_STATE ** -0.5)
    s5_c_im = normal((N_S5_LAYERS, S5_GROUPS, S5_GROUP_CH, S5_STATE), S5_STATE ** -0.5)
    s5_d = normal((N_S5_LAYERS, D_MODEL), 1.0)
    s5_w_out = normal((N_S5_LAYERS, D_MODEL, 2 * D_MODEL), D_MODEL ** -0.5)
    mlp_w_up = normal((DEPTH, D_MODEL, FF_DIM), D_MODEL ** -0.5)
    mlp_w_down = normal((DEPTH, FF_DIM, D_MODEL), FF_DIM ** -0.5)
    ple_w_proj = normal((DEPTH, PLE_DIM, D_MODEL), PLE_DIM ** -0.5)
    ple_w_gate = normal((DEPTH, D_MODEL, D_MODEL), D_MODEL ** -0.5)
    return {'x': x, 'p': p, 'norm_mix': norm_mix, 'norm_mlp': norm_mlp, 'norm_ple': norm_ple,
            'norm_final': norm_final, 'gdn_w_in': gdn_w_in, 'gdn_conv_w': gdn_conv_w,
            'gdn_a_log': gdn_a_log, 'gdn_dt_bias': gdn_dt_bias, 'gdn_o_norm': gdn_o_norm,
            'gdn_w_out': gdn_w_out, 's5_w_in': s5_w_in, 's5_lam_re': s5_lam_re, 's5_lam_im': s5_lam_im,
            's5_log_step': s5_log_step, 's5_b_re': s5_b_re, 's5_b_im': s5_b_im, 's5_c_re': s5_c_re,
            's5_c_im': s5_c_im, 's5_d': s5_d, 's5_w_out': s5_w_out, 'mlp_w_up': mlp_w_up,
            'mlp_w_down': mlp_w_down, 'ple_w_proj': ple_w_proj, 'ple_w_gate': ple_w_gate}


def _fwd_reference(x, p, norm_mix, norm_mlp, norm_ple, norm_final, gdn_w_in, gdn_conv_w, gdn_a_log,
              gdn_dt_bias, gdn_o_norm, gdn_w_out, s5_w_in, s5_lam_re, s5_lam_im, s5_log_step,
              s5_b_re, s5_b_im, s5_c_re, s5_c_im, s5_d, s5_w_out, mlp_w_up, mlp_w_down,
              ple_w_proj, ple_w_gate):
    h = x
    for i in range(DEPTH):
        j = i // N_MIXERS
        hn = rms_norm(h, norm_mix[i])
        if i % N_MIXERS == 0:
            mix = gdn_mixer(hn, gdn_w_in[j], gdn_conv_w[j], gdn_a_log[j], gdn_dt_bias[j],
                            gdn_o_norm[j], gdn_w_out[j])
        else:
            mix = s5_mixer(hn, s5_w_in[j], s5_lam_re[j], s5_lam_im[j], s5_log_step[j],
                           s5_b_re[j], s5_b_im[j], s5_c_re[j], s5_c_im[j], s5_d[j], s5_w_out[j])
        h = h + mix.astype(h.dtype)
        h = h + squared_relu_mlp(rms_norm(h, norm_mlp[i]), mlp_w_up[i], mlp_w_down[i]).astype(h.dtype)
        gate = jax.nn.sigmoid((rms_norm(h, norm_ple[i]) @ ple_w_gate[i]).astype(jnp.float32))
        h = h + (gate * (p[i] @ ple_w_proj[i]).astype(jnp.float32)).astype(h.dtype)
    return rms_norm(h, norm_final)


import jax as _jax
import jax.numpy as _jnp

TWIN_FORMAT = 'train_step'
FWD_PARAMS = ['x', 'p', 'norm_mix', 'norm_mlp', 'norm_ple', 'norm_final', 'gdn_w_in', 'gdn_conv_w', 'gdn_a_log', 'gdn_dt_bias', 'gdn_o_norm', 'gdn_w_out', 's5_w_in', 's5_lam_re', 's5_lam_im', 's5_log_step', 's5_b_re', 's5_b_im', 's5_c_re', 's5_c_im', 's5_d', 's5_w_out', 'mlp_w_up', 'mlp_w_down', 'ple_w_proj', 'ple_w_gate']
TWIN_WEIGHTS = ['norm_mix', 'norm_mlp', 'norm_ple', 'norm_final', 'gdn_w_in', 'gdn_conv_w', 'gdn_a_log', 'gdn_dt_bias', 'gdn_o_norm', 'gdn_w_out', 's5_w_in', 's5_lam_re', 's5_lam_im', 's5_log_step', 's5_b_re', 's5_b_im', 's5_c_re', 's5_c_im', 's5_d', 's5_w_out', 'mlp_w_up', 'mlp_w_down', 'ple_w_proj', 'ple_w_gate']
TWIN_DIFF_INPUT = 'x'
TWIN_INPUTS = ['x', 'p', 'norm_mix', 'norm_mlp', 'norm_ple', 'norm_final', 'gdn_w_in', 'gdn_conv_w', 'gdn_a_log', 'gdn_dt_bias', 'gdn_o_norm', 'gdn_w_out', 's5_w_in', 's5_lam_re', 's5_lam_im', 's5_log_step', 's5_b_re', 's5_b_im', 's5_c_re', 's5_c_im', 's5_d', 's5_w_out', 'mlp_w_up', 'mlp_w_down', 'ple_w_proj', 'ple_w_gate', 'loss_target', 'm_norm_mix', 'm_norm_mlp', 'm_norm_ple', 'm_norm_final', 'm_gdn_w_in', 'm_gdn_conv_w', 'm_gdn_a_log', 'm_gdn_dt_bias', 'm_gdn_o_norm', 'm_gdn_w_out', 'm_s5_w_in', 'm_s5_lam_re', 'm_s5_lam_im', 'm_s5_log_step', 'm_s5_b_re', 'm_s5_b_im', 'm_s5_c_re', 'm_s5_c_im', 'm_s5_d', 'm_s5_w_out', 'm_mlp_w_up', 'm_mlp_w_down', 'm_ple_w_proj', 'm_ple_w_gate', 'v_norm_mix', 'v_norm_mlp', 'v_norm_ple', 'v_norm_final', 'v_gdn_w_in', 'v_gdn_conv_w', 'v_gdn_a_log', 'v_gdn_dt_bias', 'v_gdn_o_norm', 'v_gdn_w_out', 'v_s5_w_in', 'v_s5_lam_re', 'v_s5_lam_im', 'v_s5_log_step', 'v_s5_b_re', 'v_s5_b_im', 'v_s5_c_re', 'v_s5_c_im', 'v_s5_d', 'v_s5_w_out', 'v_mlp_w_up', 'v_mlp_w_down', 'v_ple_w_proj', 'v_ple_w_gate']
TWIN_OUTPUTS = ['loss', 'grad_x', 'grad_norm_mix', 'grad_norm_mlp', 'grad_norm_ple', 'grad_norm_final', 'grad_gdn_w_in', 'grad_gdn_conv_w', 'grad_gdn_a_log', 'grad_gdn_dt_bias', 'grad_gdn_o_norm', 'grad_gdn_w_out', 'grad_s5_w_in', 'grad_s5_lam_re', 'grad_s5_lam_im', 'grad_s5_log_step', 'grad_s5_b_re', 'grad_s5_b_im', 'grad_s5_c_re', 'grad_s5_c_im', 'grad_s5_d', 'grad_s5_w_out', 'grad_mlp_w_up', 'grad_mlp_w_down', 'grad_ple_w_proj', 'grad_ple_w_gate', 'delta_norm_mix', 'delta_norm_mlp', 'delta_norm_ple', 'delta_norm_final', 'delta_gdn_w_in', 'delta_gdn_conv_w', 'delta_gdn_a_log', 'delta_gdn_dt_bias', 'delta_gdn_o_norm', 'delta_gdn_w_out', 'delta_s5_w_in', 'delta_s5_lam_re', 'delta_s5_lam_im', 'delta_s5_log_step', 'delta_s5_b_re', 'delta_s5_b_im', 'delta_s5_c_re', 'delta_s5_c_im', 'delta_s5_d', 'delta_s5_w_out', 'delta_mlp_w_up', 'delta_mlp_w_down', 'delta_ple_w_proj', 'delta_ple_w_gate', 'new_m_norm_mix', 'new_m_norm_mlp', 'new_m_norm_ple', 'new_m_norm_final', 'new_m_gdn_w_in', 'new_m_gdn_conv_w', 'new_m_gdn_a_log', 'new_m_gdn_dt_bias', 'new_m_gdn_o_norm', 'new_m_gdn_w_out', 'new_m_s5_w_in', 'new_m_s5_lam_re', 'new_m_s5_lam_im', 'new_m_s5_log_step', 'new_m_s5_b_re', 'new_m_s5_b_im', 'new_m_s5_c_re', 'new_m_s5_c_im', 'new_m_s5_d', 'new_m_s5_w_out', 'new_m_mlp_w_up', 'new_m_mlp_w_down', 'new_m_ple_w_proj', 'new_m_ple_w_gate', 'new_v_norm_mix', 'new_v_norm_mlp', 'new_v_norm_ple', 'new_v_norm_final', 'new_v_gdn_w_in', 'new_v_gdn_conv_w', 'new_v_gdn_a_log', 'new_v_gdn_dt_bias', 'new_v_gdn_o_norm', 'new_v_gdn_w_out', 'new_v_s5_w_in', 'new_v_s5_lam_re', 'new_v_s5_lam_im', 'new_v_s5_log_step', 'new_v_s5_b_re', 'new_v_s5_b_im', 'new_v_s5_c_re', 'new_v_s5_c_im', 'new_v_s5_d', 'new_v_s5_w_out', 'new_v_mlp_w_up', 'new_v_mlp_w_down', 'new_v_ple_w_proj', 'new_v_ple_w_gate']
TWIN_LEAF_KINDS = {'loss': 'loss', 'grad_x': 'grad_x', 'grad_norm_mix': 'grad_w', 'grad_norm_mlp': 'grad_w', 'grad_norm_ple': 'grad_w', 'grad_norm_final': 'grad_w', 'grad_gdn_w_in': 'grad_w', 'grad_gdn_conv_w': 'grad_w', 'grad_gdn_a_log': 'grad_w', 'grad_gdn_dt_bias': 'grad_w', 'grad_gdn_o_norm': 'grad_w', 'grad_gdn_w_out': 'grad_w', 'grad_s5_w_in': 'grad_w', 'grad_s5_lam_re': 'grad_w', 'grad_s5_lam_im': 'grad_w', 'grad_s5_log_step': 'grad_w', 'grad_s5_b_re': 'grad_w', 'grad_s5_b_im': 'grad_w', 'grad_s5_c_re': 'grad_w', 'grad_s5_c_im': 'grad_w', 'grad_s5_d': 'grad_w', 'grad_s5_w_out': 'grad_w', 'grad_mlp_w_up': 'grad_w', 'grad_mlp_w_down': 'grad_w', 'grad_ple_w_proj': 'grad_w', 'grad_ple_w_gate': 'grad_w', 'delta_norm_mix': 'delta_w', 'delta_norm_mlp': 'delta_w', 'delta_norm_ple': 'delta_w', 'delta_norm_final': 'delta_w', 'delta_gdn_w_in': 'delta_w', 'delta_gdn_conv_w': 'delta_w', 'delta_gdn_a_log': 'delta_w', 'delta_gdn_dt_bias': 'delta_w', 'delta_gdn_o_norm': 'delta_w', 'delta_gdn_w_out': 'delta_w', 'delta_s5_w_in': 'delta_w', 'delta_s5_lam_re': 'delta_w', 'delta_s5_lam_im': 'delta_w', 'delta_s5_log_step': 'delta_w', 'delta_s5_b_re': 'delta_w', 'delta_s5_b_im': 'delta_w', 'delta_s5_c_re': 'delta_w', 'delta_s5_c_im': 'delta_w', 'delta_s5_d': 'delta_w', 'delta_s5_w_out': 'delta_w', 'delta_mlp_w_up': 'delta_w', 'delta_mlp_w_down': 'delta_w', 'delta_ple_w_proj': 'delta_w', 'delta_ple_w_gate': 'delta_w', 'new_m_norm_mix': 'new_m', 'new_m_norm_mlp': 'new_m', 'new_m_norm_ple': 'new_m', 'new_m_norm_final': 'new_m', 'new_m_gdn_w_in': 'new_m', 'new_m_gdn_conv_w': 'new_m', 'new_m_gdn_a_log': 'new_m', 'new_m_gdn_dt_bias': 'new_m', 'new_m_gdn_o_norm': 'new_m', 'new_m_gdn_w_out': 'new_m', 'new_m_s5_w_in': 'new_m', 'new_m_s5_lam_re': 'new_m', 'new_m_s5_lam_im': 'new_m', 'new_m_s5_log_step': 'new_m', 'new_m_s5_b_re': 'new_m', 'new_m_s5_b_im': 'new_m', 'new_m_s5_c_re': 'new_m', 'new_m_s5_c_im': 'new_m', 'new_m_s5_d': 'new_m', 'new_m_s5_w_out': 'new_m', 'new_m_mlp_w_up': 'new_m', 'new_m_mlp_w_down': 'new_m', 'new_m_ple_w_proj': 'new_m', 'new_m_ple_w_gate': 'new_m', 'new_v_norm_mix': 'new_v', 'new_v_norm_mlp': 'new_v', 'new_v_norm_ple': 'new_v', 'new_v_norm_final': 'new_v', 'new_v_gdn_w_in': 'new_v', 'new_v_gdn_conv_w': 'new_v', 'new_v_gdn_a_log': 'new_v', 'new_v_gdn_dt_bias': 'new_v', 'new_v_gdn_o_norm': 'new_v', 'new_v_gdn_w_out': 'new_v', 'new_v_s5_w_in': 'new_v', 'new_v_s5_lam_re': 'new_v', 'new_v_s5_lam_im': 'new_v', 'new_v_s5_log_step': 'new_v', 'new_v_s5_b_re': 'new_v', 'new_v_s5_b_im': 'new_v', 'new_v_s5_c_re': 'new_v', 'new_v_s5_c_im': 'new_v', 'new_v_s5_d': 'new_v', 'new_v_s5_w_out': 'new_v', 'new_v_mlp_w_up': 'new_v', 'new_v_mlp_w_down': 'new_v', 'new_v_ple_w_proj': 'new_v', 'new_v_ple_w_gate': 'new_v'}


def _forward(args):
    return _fwd_reference(*[args[k] for k in FWD_PARAMS])


def _output_shape():
    def fwd():
        inp = _fwd_setup_inputs(0)
        return _fwd_reference(*[inp[k] for k in FWD_PARAMS])
    out = _jax.eval_shape(fwd)
    return out.shape, out.dtype

N_MICROBATCH = 1
ADAM_LR = 0.001
ADAM_B1 = 0.9
ADAM_B2 = 0.999
ADAM_EPS = 1e-08
ADAM_WD = 0.01
ADAM_STEP = 10
PER_EXAMPLE_BATCH_AXIS = {'x': 0, 'p': 1, 'loss_target': 0}
SHARED_INPUTS = []
_WEIGHT_DTYPES = {'norm_mix': _jnp.float32, 'norm_mlp': _jnp.float32, 'norm_ple': _jnp.float32, 'norm_final': _jnp.float32, 'gdn_w_in': _jnp.float32, 'gdn_conv_w': _jnp.float32, 'gdn_a_log': _jnp.float32, 'gdn_dt_bias': _jnp.float32, 'gdn_o_norm': _jnp.float32, 'gdn_w_out': _jnp.float32, 's5_w_in': _jnp.float32, 's5_lam_re': _jnp.float32, 's5_lam_im': _jnp.float32, 's5_log_step': _jnp.float32, 's5_b_re': _jnp.float32, 's5_b_im': _jnp.float32, 's5_c_re': _jnp.float32, 's5_c_im': _jnp.float32, 's5_d': _jnp.float32, 's5_w_out': _jnp.float32, 'mlp_w_up': _jnp.float32, 'mlp_w_down': _jnp.float32, 'ple_w_proj': _jnp.float32, 'ple_w_gate': _jnp.float32}
MOMENT_SCALE = {'norm_mix': 7.110805e-02, 'norm_mlp': 8.695655e-02, 'norm_ple': 1.263541e-02, 'norm_final': 3.265160e+01, 'gdn_w_in': 3.981517e-02, 'gdn_conv_w': 3.880235e-02, 'gdn_a_log': 2.697727e-01, 'gdn_dt_bias': 2.623132e-01, 'gdn_o_norm': 2.380922e-01, 'gdn_w_out': 5.949249e-02, 's5_w_in': 2.591728e-02, 's5_lam_re': 3.040837e-03, 's5_lam_im': 3.219458e-03, 's5_log_step': 1.188041e+00, 's5_b_re': 1.865832e-03, 's5_b_im': 1.684919e-03, 's5_c_re': 2.416736e-03, 's5_c_im': 2.656831e-03, 's5_d': 3.292613e-02, 's5_w_out': 2.171011e-02, 'mlp_w_up': 4.386382e-02, 'mlp_w_down': 9.140992e-02, 'ple_w_proj': 3.237715e-02, 'ple_w_gate': 1.281730e-02}


def _to_microbatches(a, axis):
    t = _jnp.moveaxis(a, axis, 0)
    t = t.reshape((N_MICROBATCH, t.shape[0] // N_MICROBATCH) + t.shape[1:])
    return _jnp.moveaxis(t, 1, axis + 1)


def setup_inputs(seed: int = 0) -> dict:
    inp = _fwd_setup_inputs(seed)
    key = _jax.random.fold_in(_jax.random.key(seed), 7919)
    shape, _ = _output_shape()
    out = dict(inp)
    out["loss_target"] = _jax.random.normal(_jax.random.fold_in(key, 0), shape, _jnp.float32)
    for i, name in enumerate(TWIN_WEIGHTS):
        w = inp[name].astype(_jnp.float32)
        if MOMENT_SCALE is None:
            s = _jnp.sqrt(_jnp.mean(_jnp.square(w)) + 1e-30)
        else:
            s = MOMENT_SCALE[name]
        km, kv = _jax.random.split(_jax.random.fold_in(key, i + 1))
        out[name] = w
        out["m_" + name] = s * _jax.random.normal(km, w.shape, _jnp.float32)
        out["v_" + name] = (s * s) * _jax.random.uniform(kv, w.shape, _jnp.float32, 0.5, 1.5)
    if N_MICROBATCH > 1:
        for name, axis in PER_EXAMPLE_BATCH_AXIS.items():
            out[name] = _to_microbatches(out[name], axis)
    return {'x': out['x'], 'p': out['p'], 'norm_mix': out['norm_mix'], 'norm_mlp': out['norm_mlp'], 'norm_ple': out['norm_ple'], 'norm_final': out['norm_final'], 'gdn_w_in': out['gdn_w_in'], 'gdn_conv_w': out['gdn_conv_w'], 'gdn_a_log': out['gdn_a_log'], 'gdn_dt_bias': out['gdn_dt_bias'], 'gdn_o_norm': out['gdn_o_norm'], 'gdn_w_out': out['gdn_w_out'], 's5_w_in': out['s5_w_in'], 's5_lam_re': out['s5_lam_re'], 's5_lam_im': out['s5_lam_im'], 's5_log_step': out['s5_log_step'], 's5_b_re': out['s5_b_re'], 's5_b_im': out['s5_b_im'], 's5_c_re': out['s5_c_re'], 's5_c_im': out['s5_c_im'], 's5_d': out['s5_d'], 's5_w_out': out['s5_w_out'], 'mlp_w_up': out['mlp_w_up'], 'mlp_w_down': out['mlp_w_down'], 'ple_w_proj': out['ple_w_proj'], 'ple_w_gate': out['ple_w_gate'], 'loss_target': out['loss_target'], 'm_norm_mix': out['m_norm_mix'], 'm_norm_mlp': out['m_norm_mlp'], 'm_norm_ple': out['m_norm_ple'], 'm_norm_final': out['m_norm_final'], 'm_gdn_w_in': out['m_gdn_w_in'], 'm_gdn_conv_w': out['m_gdn_conv_w'], 'm_gdn_a_log': out['m_gdn_a_log'], 'm_gdn_dt_bias': out['m_gdn_dt_bias'], 'm_gdn_o_norm': out['m_gdn_o_norm'], 'm_gdn_w_out': out['m_gdn_w_out'], 'm_s5_w_in': out['m_s5_w_in'], 'm_s5_lam_re': out['m_s5_lam_re'], 'm_s5_lam_im': out['m_s5_lam_im'], 'm_s5_log_step': out['m_s5_log_step'], 'm_s5_b_re': out['m_s5_b_re'], 'm_s5_b_im': out['m_s5_b_im'], 'm_s5_c_re': out['m_s5_c_re'], 'm_s5_c_im': out['m_s5_c_im'], 'm_s5_d': out['m_s5_d'], 'm_s5_w_out': out['m_s5_w_out'], 'm_mlp_w_up': out['m_mlp_w_up'], 'm_mlp_w_down': out['m_mlp_w_down'], 'm_ple_w_proj': out['m_ple_w_proj'], 'm_ple_w_gate': out['m_ple_w_gate'], 'v_norm_mix': out['v_norm_mix'], 'v_norm_mlp': out['v_norm_mlp'], 'v_norm_ple': out['v_norm_ple'], 'v_norm_final': out['v_norm_final'], 'v_gdn_w_in': out['v_gdn_w_in'], 'v_gdn_conv_w': out['v_gdn_conv_w'], 'v_gdn_a_log': out['v_gdn_a_log'], 'v_gdn_dt_bias': out['v_gdn_dt_bias'], 'v_gdn_o_norm': out['v_gdn_o_norm'], 'v_gdn_w_out': out['v_gdn_w_out'], 'v_s5_w_in': out['v_s5_w_in'], 'v_s5_lam_re': out['v_s5_lam_re'], 'v_s5_lam_im': out['v_s5_lam_im'], 'v_s5_log_step': out['v_s5_log_step'], 'v_s5_b_re': out['v_s5_b_re'], 'v_s5_b_im': out['v_s5_b_im'], 'v_s5_c_re': out['v_s5_c_re'], 'v_s5_c_im': out['v_s5_c_im'], 'v_s5_d': out['v_s5_d'], 'v_s5_w_out': out['v_s5_w_out'], 'v_mlp_w_up': out['v_mlp_w_up'], 'v_mlp_w_down': out['v_mlp_w_down'], 'v_ple_w_proj': out['v_ple_w_proj'], 'v_ple_w_gate': out['v_ple_w_gate']}


def _loss(weights, diff, rest, loss_target):
    with _jax.named_scope("forward"):
        args = {**rest, TWIN_DIFF_INPUT: diff, **{k: w.astype(_WEIGHT_DTYPES[k]) for k, w in weights.items()}}
        y = _forward(args)
    with _jax.named_scope("loss_head"):
        err = _jnp.square(y.astype(_jnp.float32) - loss_target)
        return 0.5 * _jnp.sum(_jnp.mean(err, axis=-1)) if err.ndim else 0.5 * err


def _adamw(w, g, m, v):
    m = ADAM_B1 * m + (1.0 - ADAM_B1) * g
    v = ADAM_B2 * v + (1.0 - ADAM_B2) * _jnp.square(g)
    m_hat = m / (1.0 - ADAM_B1 ** ADAM_STEP)
    v_hat = v / (1.0 - ADAM_B2 ** ADAM_STEP)
    delta = -ADAM_LR * (m_hat / (_jnp.sqrt(v_hat) + ADAM_EPS) + ADAM_WD * w)
    return delta, m, v


def reference(x, p, norm_mix, norm_mlp, norm_ple, norm_final, gdn_w_in, gdn_conv_w, gdn_a_log, gdn_dt_bias, gdn_o_norm, gdn_w_out, s5_w_in, s5_lam_re, s5_lam_im, s5_log_step, s5_b_re, s5_b_im, s5_c_re, s5_c_im, s5_d, s5_w_out, mlp_w_up, mlp_w_down, ple_w_proj, ple_w_gate, loss_target, m_norm_mix, m_norm_mlp, m_norm_ple, m_norm_final, m_gdn_w_in, m_gdn_conv_w, m_gdn_a_log, m_gdn_dt_bias, m_gdn_o_norm, m_gdn_w_out, m_s5_w_in, m_s5_lam_re, m_s5_lam_im, m_s5_log_step, m_s5_b_re, m_s5_b_im, m_s5_c_re, m_s5_c_im, m_s5_d, m_s5_w_out, m_mlp_w_up, m_mlp_w_down, m_ple_w_proj, m_ple_w_gate, v_norm_mix, v_norm_mlp, v_norm_ple, v_norm_final, v_gdn_w_in, v_gdn_conv_w, v_gdn_a_log, v_gdn_dt_bias, v_gdn_o_norm, v_gdn_w_out, v_s5_w_in, v_s5_lam_re, v_s5_lam_im, v_s5_log_step, v_s5_b_re, v_s5_b_im, v_s5_c_re, v_s5_c_im, v_s5_d, v_s5_w_out, v_mlp_w_up, v_mlp_w_down, v_ple_w_proj, v_ple_w_gate):
    given = dict(x=x, p=p, norm_mix=norm_mix, norm_mlp=norm_mlp, norm_ple=norm_ple, norm_final=norm_final, gdn_w_in=gdn_w_in, gdn_conv_w=gdn_conv_w, gdn_a_log=gdn_a_log, gdn_dt_bias=gdn_dt_bias, gdn_o_norm=gdn_o_norm, gdn_w_out=gdn_w_out, s5_w_in=s5_w_in, s5_lam_re=s5_lam_re, s5_lam_im=s5_lam_im, s5_log_step=s5_log_step, s5_b_re=s5_b_re, s5_b_im=s5_b_im, s5_c_re=s5_c_re, s5_c_im=s5_c_im, s5_d=s5_d, s5_w_out=s5_w_out, mlp_w_up=mlp_w_up, mlp_w_down=mlp_w_down, ple_w_proj=ple_w_proj, ple_w_gate=ple_w_gate, loss_target=loss_target, m_norm_mix=m_norm_mix, m_norm_mlp=m_norm_mlp, m_norm_ple=m_norm_ple, m_norm_final=m_norm_final, m_gdn_w_in=m_gdn_w_in, m_gdn_conv_w=m_gdn_conv_w, m_gdn_a_log=m_gdn_a_log, m_gdn_dt_bias=m_gdn_dt_bias, m_gdn_o_norm=m_gdn_o_norm, m_gdn_w_out=m_gdn_w_out, m_s5_w_in=m_s5_w_in, m_s5_lam_re=m_s5_lam_re, m_s5_lam_im=m_s5_lam_im, m_s5_log_step=m_s5_log_step, m_s5_b_re=m_s5_b_re, m_s5_b_im=m_s5_b_im, m_s5_c_re=m_s5_c_re, m_s5_c_im=m_s5_c_im, m_s5_d=m_s5_d, m_s5_w_out=m_s5_w_out, m_mlp_w_up=m_mlp_w_up, m_mlp_w_down=m_mlp_w_down, m_ple_w_proj=m_ple_w_proj, m_ple_w_gate=m_ple_w_gate, v_norm_mix=v_norm_mix, v_norm_mlp=v_norm_mlp, v_norm_ple=v_norm_ple, v_norm_final=v_norm_final, v_gdn_w_in=v_gdn_w_in, v_gdn_conv_w=v_gdn_conv_w, v_gdn_a_log=v_gdn_a_log, v_gdn_dt_bias=v_gdn_dt_bias, v_gdn_o_norm=v_gdn_o_norm, v_gdn_w_out=v_gdn_w_out, v_s5_w_in=v_s5_w_in, v_s5_lam_re=v_s5_lam_re, v_s5_lam_im=v_s5_lam_im, v_s5_log_step=v_s5_log_step, v_s5_b_re=v_s5_b_re, v_s5_b_im=v_s5_b_im, v_s5_c_re=v_s5_c_re, v_s5_c_im=v_s5_c_im, v_s5_d=v_s5_d, v_s5_w_out=v_s5_w_out, v_mlp_w_up=v_mlp_w_up, v_mlp_w_down=v_mlp_w_down, v_ple_w_proj=v_ple_w_proj, v_ple_w_gate=v_ple_w_gate)
    weights = {n: given[n] for n in TWIN_WEIGHTS}
    shared = {n: given[n] for n in SHARED_INPUTS}
    per_example = {n: given[n] for n in ['x', 'p']}
    grad_fn = _jax.value_and_grad(_loss, argnums=(0, 1))

    def one_microbatch(ex, loss_target):
        ex = dict(ex)
        diff = ex.pop(TWIN_DIFF_INPUT)
        return grad_fn(weights, diff, {**shared, **ex}, loss_target)

    if N_MICROBATCH == 1:
        loss, (grad_w, grad_x) = one_microbatch(per_example, given["loss_target"])
    else:
        def body(carry, xs):
            loss_sum, grad_sum = carry
            l_k, (gw_k, gx_k) = one_microbatch(xs[0], xs[1])
            with _jax.named_scope("update"):
                return (loss_sum + l_k, _jax.tree.map(_jnp.add, grad_sum, gw_k)), gx_k

        init = (_jnp.zeros((), _jnp.float32), _jax.tree.map(_jnp.zeros_like, weights))
        (loss, grad_w), grad_x = _jax.lax.scan(body, init, (per_example, given["loss_target"]))
    with _jax.named_scope("update"):
        delta_w, new_m, new_v = {}, {}, {}
        for n in TWIN_WEIGHTS:
            delta_w[n], new_m[n], new_v[n] = _adamw(weights[n], grad_w[n], given["m_" + n], given["v_" + n])
    return (loss, grad_x, *[grad_w[n] for n in TWIN_WEIGHTS], *[delta_w[n] for n in TWIN_WEIGHTS],
            *[new_m[n] for n in TWIN_WEIGHTS], *[new_v[n] for n in TWIN_WEIGHTS])
```

```python
import functools
import math

import jax
import jax.numpy as jnp
from jax import lax
from jax.experimental import pallas as pl
from jax.experimental.pallas import tpu as pltpu

F32, BF16 = jnp.float32, jnp.bfloat16
HI = lax.Precision.HIGHEST
NN, NT, TN = ((1,), (0,)), ((1,), (1,)), ((0,), (0,))

N_DEV = 8
MESH_AXES = ("x", "y", "c")
LANES = 128
V7X_VMEM_BYTES = 64 * 1024 * 1024
VMEM_LIMIT = V7X_VMEM_BYTES - 8 * 1024 * 1024
CHUNK = 64
HEAD = 128
S5_CH = 16
S5_STATE = 64
S5_GPB = LANES // S5_CH
S5_SPB = S5_GPB * S5_STATE
NORM_EPS = 1e-6
L2_EPS = 1e-6
ADAM_LR, ADAM_B1, ADAM_B2, ADAM_EPS, ADAM_WD, ADAM_STEP = 0.001, 0.9, 0.999, 1e-08, 0.01, 10
PACK_W = 1024
PACK_ROWS = 256


def _dot(a, b, dims, prec=None):
    return lax.dot_general(a, b, (dims, ((), ())), preferred_element_type=F32, precision=prec)


def _bdot(a, b, dims=NN):
    return _dot(a.astype(BF16), b.astype(BF16), dims)


def _hdot(a, b, dims=NN):
    return _dot(a, b, dims, HI)


def _call(body, grid, ins, outs, scratch=(), name=None, sem=None):
    res = pl.pallas_call(
        body,
        grid=grid,
        in_specs=[pl.BlockSpec(b, m) for _, b, m in ins],
        out_specs=[pl.BlockSpec(b, m) for _, _, b, m in outs],
        out_shape=[jax.ShapeDtypeStruct(s, d) for s, d, _, _ in outs],
        scratch_shapes=list(scratch),
        name=name,
        compiler_params=pltpu.CompilerParams(
            dimension_semantics=sem or ("arbitrary",) * len(grid), vmem_limit_bytes=VMEM_LIMIT),
    )(*[a for a, _, _ in ins])
    return res


def _tile(n, want):
    t = min(n, want)
    assert n % t == 0, (n, want)
    return t


def _accumulate(ref, val, first):
    @pl.when(first)
    def _():
        ref[...] = jnp.zeros_like(ref)
    ref[...] += val


def _mm(a, b, mode="nn", out_dtypes=(F32,), epi=None, extras=(), name="mm", tm=1024, tn=1024, tk=512):
    if mode == "nn":
        (M, K), (K2, N) = a.shape, b.shape
    elif mode == "nt":
        (M, K), (N, K2) = a.shape, b.shape
    else:
        (K, M), (K2, N) = a.shape, b.shape
    assert K == K2, (a.shape, b.shape, mode)
    tm, tn, tk = _tile(M, tm), _tile(N, tn), _tile(K, tk)
    nk = K // tk
    a_spec = ((tk, tm), lambda i, j, k: (k, i)) if mode == "tn" else ((tm, tk), lambda i, j, k: (i, k))
    b_spec = ((tn, tk), lambda i, j, k: (j, k)) if mode == "nt" else ((tk, tn), lambda i, j, k: (k, j))
    dims = {"nn": NN, "nt": NT, "tn": TN}[mode]
    n_ex, n_out = len(extras), len(out_dtypes)

    def body(*refs):
        a_ref, b_ref = refs[:2]
        ex = refs[2:2 + n_ex]
        outs = refs[2 + n_ex:2 + n_ex + n_out]
        acc = refs[-1]
        k = pl.program_id(2)

        @pl.when(k == 0)
        def _():
            acc[...] = jnp.zeros_like(acc)

        acc[...] += _bdot(a_ref[...], b_ref[...], dims)

        @pl.when(k == nk - 1)
        def _():
            res = acc[...]
            vals = epi(res, *[e[...] for e in ex]) if epi is not None else (res,)
            for r, v in zip(outs, vals):
                r[...] = v.astype(r.dtype)

    tile = lambda i, j, k: (i, j)
    return _call(
        body, (M // tm, N // tn, nk),
        [(a,) + a_spec, (b,) + b_spec] + [(e, (tm, tn), tile) for e in extras],
        [((M, N), d, (tm, tn), tile) for d in out_dtypes],
        scratch=[pltpu.VMEM((tm, tn), F32)], name=name,
        sem=("parallel", "parallel", "arbitrary"))


def _rms_fwd(h, g, tr=256):
    T, D = h.shape
    tr = _tile(T, tr)

    def body(h_ref, g_ref, o_ref):
        x = h_ref[...]
        r = lax.rsqrt(jnp.mean(x * x, axis=-1, keepdims=True) + NORM_EPS)
        o_ref[...] = (x * r * g_ref[...]).astype(BF16)

    row = lambda i: (i, 0)
    fix = lambda i: (0, 0)
    return _call(body, (T // tr,), [(h, (tr, D), row), (g.reshape(1, D), (1, D), fix)],
                 [((T, D), BF16, (tr, D), row)], name="rms_fwd", sem=("parallel",))[0]


def _rms_bwd_math(dy, x, g):
    r = lax.rsqrt(jnp.mean(x * x, axis=-1, keepdims=True) + NORM_EPS)
    xh = x * r
    dxh = dy * g
    dx = r * (dxh - xh * jnp.mean(dxh * xh, axis=-1, keepdims=True))
    dg = jnp.sum(dy * xh, axis=0, keepdims=True)
    return dx, dg


def _rms_bwd(dy, h, g, res, tr=256):
    T, D = h.shape
    tr = _tile(T, tr)

    def body(dy_ref, h_ref, g_ref, res_ref, dh_ref, dg_ref):
        dx, dg = _rms_bwd_math(dy_ref[...], h_ref[...], g_ref[...])
        dh_ref[...] = res_ref[...] + dx
        _accumulate(dg_ref, dg, pl.program_id(0) == 0)

    row = lambda i: (i, 0)
    fix = lambda i: (0, 0)
    return _call(body, (T // tr,),
                 [(dy, (tr, D), row), (h, (tr, D), row), (g.reshape(1, D), (1, D), fix), (res, (tr, D), row)],
                 [((T, D), F32, (tr, D), row), ((1, D), F32, (1, D), fix)], name="rms_bwd")


def _loss_fwd_bwd(h, g, tgt, tr=256):
    T, D = h.shape
    tr = _tile(T, tr)

    def body(h_ref, g_ref, t_ref, dh_ref, dg_ref, loss_ref):
        x, gg = h_ref[...], g_ref[...]
        r = lax.rsqrt(jnp.mean(x * x, axis=-1, keepdims=True) + NORM_EPS)
        diff = x * r * gg - t_ref[...]
        part = 0.5 * jnp.sum(jnp.mean(diff * diff, axis=-1, keepdims=True))
        dx, dg = _rms_bwd_math(diff * (1.0 / D), x, gg)
        dh_ref[...] = dx
        first = pl.program_id(0) == 0
        _accumulate(dg_ref, dg, first)
        _accumulate(loss_ref, jnp.full((1, LANES), part, F32), first)

    row = lambda i: (i, 0)
    fix = lambda i: (0, 0)
    return _call(body, (T // tr,),
                 [(h, (tr, D), row), (g.reshape(1, D), (1, D), fix), (tgt, (tr, D), row)],
                 [((T, D), F32, (tr, D), row), ((1, D), F32, (1, D), fix), ((1, LANES), F32, (1, LANES), fix)],
                 name="loss_fwd_bwd")


def _ple_fwd(h, s, pp, tr=256):
    T, D = h.shape
    tr = _tile(T, tr)

    def body(h_ref, s_ref, p_ref, o_ref):
        o_ref[...] = h_ref[...] + jax.nn.sigmoid(s_ref[...]) * p_ref[...]

    row = lambda i: (i, 0)
    return _call(body, (T // tr,), [(a, (tr, D), row) for a in (h, s, pp)],
                 [((T, D), F32, (tr, D), row)], name="ple_fwd", sem=("parallel",))[0]


def _ple_bwd(dh, s, pp, tr=256):
    T, D = dh.shape
    tr = _tile(T, tr)

    def body(dh_ref, s_ref, p_ref, ds_ref, dp_ref):
        d = dh_ref[...]
        gate = jax.nn.sigmoid(s_ref[...])
        ds_ref[...] = (d * p_ref[...] * gate * (1.0 - gate)).astype(BF16)
        dp_ref[...] = (d * gate).astype(BF16)

    row = lambda i: (i, 0)
    return _call(body, (T // tr,), [(a, (tr, D), row) for a in (dh, s, pp)],
                 [((T, D), BF16, (tr, D), row)] * 2, name="ple_bwd", sem=("parallel",))


def _conv_taps(xe, w, tr):
    c = w[3:4, :] * xe[8:, :]
    for j in range(3):
        c = c + w[j:j + 1, :] * pltpu.roll(xe, 3 - j, 0)[8:, :]
    return c


def _gdn_pre_fwd(pq, conv_w, hk, tr=1024):
    T, CD = pq.shape
    tr = _tile(T, tr)
    r8 = tr // 8

    def body(x_ref, halo_ref, w_ref, o_ref):
        j, r = pl.program_id(0), pl.program_id(1)
        halo = jnp.where(r > 0, halo_ref[...], 0.0)
        xe = jnp.concatenate([halo, x_ref[...]], axis=0)
        c = _conv_taps(xe, w_ref[...], tr)
        s = c * jax.nn.sigmoid(c)
        rn = lax.rsqrt(jnp.sum(s * s, axis=-1, keepdims=True) + L2_EPS)
        scale = jnp.where(j < hk, HEAD ** -0.5, 1.0)
        o_ref[...] = jnp.where(j < 2 * hk, s * rn * scale, s)

    tile = lambda j, r: (r, j)
    return _call(body, (CD // HEAD, T // tr),
                 [(pq, (tr, HEAD), tile), (pq, (8, HEAD), lambda j, r: (jnp.maximum(r * r8 - 1, 0), j)),
                  (conv_w, (4, HEAD), lambda j, r: (0, j))],
                 [((T, CD), F32, (tr, HEAD), tile)], name="gdn_pre_fwd", sem=("parallel", "parallel"))[0]


def _gdn_pre_bwd(dn, pq, conv_w, hk, tr=1024):
    T, CD = pq.shape
    tr = _tile(T, tr)
    r8 = tr // 8

    def body(dn_ref, x_ref, halo_ref, w_ref, dc_ref, dw_ref):
        j, r = pl.program_id(0), pl.program_id(1)
        halo = jnp.where(r > 0, halo_ref[...], 0.0)
        xe = jnp.concatenate([halo, x_ref[...]], axis=0)
        c = _conv_taps(xe, w_ref[...], tr)
        sig = jax.nn.sigmoid(c)
        s = c * sig
        rn = lax.rsqrt(jnp.sum(s * s, axis=-1, keepdims=True) + L2_EPS)
        scale = jnp.where(j < hk, HEAD ** -0.5, 1.0)
        d = dn_ref[...]
        y = s * rn
        dy = d * scale
        ds = jnp.where(j < 2 * hk, rn * (dy - y * jnp.sum(dy * y, axis=-1, keepdims=True)), d)
        dc = ds * sig * (1.0 + c * (1.0 - sig))
        dc_ref[...] = dc

        @pl.when(r == 0)
        def _():
            dw_ref[...] = jnp.zeros_like(dw_ref)

        for t in range(4):
            xs = xe[8:, :] if t == 3 else pltpu.roll(xe, 3 - t, 0)[8:, :]
            dw_ref[t:t + 1, :] += jnp.sum(dc * xs, axis=0, keepdims=True)

    tile = lambda j, r: (r, j)
    col = lambda j, r: (0, j)
    return _call(body, (CD // HEAD, T // tr),
                 [(dn, (tr, HEAD), tile), (pq, (tr, HEAD), tile),
                  (pq, (8, HEAD), lambda j, r: (jnp.maximum(r * r8 - 1, 0), j)), (conv_w, (4, HEAD), col)],
                 [((T, CD), F32, (tr, HEAD), tile), ((4, CD), F32, (4, HEAD), col)], name="gdn_pre_bwd")


def _gdn_conv_bwd(dc, conv_w, tr=1024):
    T, CD = dc.shape
    tr = _tile(T, tr)
    r8 = tr // 8
    n_r = T // tr

    def body(dc_ref, halo_ref, w_ref, dx_ref):
        r = pl.program_id(1)
        halo = jnp.where(r < n_r - 1, halo_ref[...], 0.0)
        de = jnp.concatenate([dc_ref[...], halo], axis=0)
        w = w_ref[...]
        dx = w[3:4, :] * de[:tr, :]
        for j in range(3):
            dx = dx + w[j:j + 1, :] * pltpu.roll(de, tr + 8 - (3 - j), 0)[:tr, :]
        dx_ref[...] = dx

    tile = lambda j, r: (r, j)
    return _call(body, (CD // HEAD, n_r),
                 [(dc, (tr, HEAD), tile), (dc, (8, HEAD), lambda j, r: (jnp.minimum((r + 1) * r8, T // 8 - 1), j)),
                  (conv_w, (4, HEAD), lambda j, r: (0, j))],
                 [((T, CD), F32, (tr, HEAD), tile)], name="gdn_conv_bwd", sem=("parallel", "parallel"))[0]


def _gates_fwd(ba, pv, hv, tr=1024):
    T = ba.shape[0]
    tr = _tile(T, tr)

    def body(x_ref, pv_ref, o_ref):
        x = x_ref[...]
        lane = lax.broadcasted_iota(jnp.int32, x.shape, 1)
        g = -jnp.exp(pv_ref[0:1, :]) * jax.nn.softplus(x + pv_ref[1:2, :])
        o_ref[...] = jnp.where(lane < hv, jax.nn.sigmoid(x), jnp.where(lane < 2 * hv, g, 0.0))

    row = lambda i: (i, 0)
    return _call(body, (T // tr,), [(ba, (tr, LANES), row), (pv, (2, LANES), lambda i: (0, 0))],
                 [((T, LANES), F32, (tr, LANES), row)], name="gates_fwd", sem=("parallel",))[0]


def _gates_bwd(dg2, ba, pv, hv, tr=1024):
    T = ba.shape[0]
    tr = _tile(T, tr)

    def body(d_ref, x_ref, pv_ref, dx_ref, dpv_ref):
        x, d = x_ref[...], d_ref[...]
        lane = lax.broadcasted_iota(jnp.int32, x.shape, 1)
        is_a = (lane >= hv) & (lane < 2 * hv)
        beta = jax.nn.sigmoid(x)
        neg_a = -jnp.exp(pv_ref[0:1, :])
        z = x + pv_ref[1:2, :]
        da = d * neg_a * jax.nn.sigmoid(z)
        dx_ref[...] = jnp.where(lane < hv, d * beta * (1.0 - beta), jnp.where(is_a, da, 0.0))
        first = pl.program_id(0) == 0

        @pl.when(first)
        def _():
            dpv_ref[...] = jnp.zeros_like(dpv_ref)

        dpv_ref[0:1, :] += jnp.sum(jnp.where(is_a, d * neg_a * jax.nn.softplus(z), 0.0), axis=0, keepdims=True)
        dpv_ref[1:2, :] += jnp.sum(jnp.where(is_a, da, 0.0), axis=0, keepdims=True)

    row = lambda i: (i, 0)
    fix = lambda i: (0, 0)
    return _call(body, (T // tr,), [(dg2, (tr, LANES), row), (ba, (tr, LANES), row), (pv, (2, LANES), fix)],
                 [((T, LANES), F32, (tr, LANES), row), ((2, LANES), F32, (2, LANES), fix)], name="gates_bwd")


def _ogate_fwd(o, z, o_norm, tr=512):
    T, VD = o.shape
    tr = _tile(T, tr)

    def body(o_ref, z_ref, g_ref, y_ref):
        x, zz = o_ref[...], z_ref[...]
        r = lax.rsqrt(jnp.mean(x * x, axis=-1, keepdims=True) + NORM_EPS)
        y_ref[...] = (x * r * g_ref[...] * (zz * jax.nn.sigmoid(zz))).astype(BF16)

    tile = lambda h, r: (r, h)
    return _call(body, (VD // HEAD, T // tr),
                 [(o, (tr, HEAD), tile), (z, (tr, HEAD), tile), (o_norm.reshape(1, HEAD), (1, HEAD), lambda h, r: (0, 0))],
                 [((T, VD), BF16, (tr, HEAD), tile)], name="ogate_fwd", sem=("parallel", "parallel"))[0]


def _ogate_bwd(dy, o, z, o_norm, tr=512):
    T, VD = o.shape
    tr = _tile(T, tr)

    def body(dy_ref, o_ref, z_ref, g_ref, do_ref, dz_ref, dg_ref):
        d, x, zz, g = dy_ref[...], o_ref[...], z_ref[...], g_ref[...]
        sig = jax.nn.sigmoid(zz)
        silu = zz * sig
        dx, dg = _rms_bwd_math(d * silu, x, g)
        r = lax.rsqrt(jnp.mean(x * x, axis=-1, keepdims=True) + NORM_EPS)
        do_ref[...] = dx
        dz_ref[...] = d * (x * r * g) * sig * (1.0 + zz * (1.0 - sig))
        _accumulate(dg_ref, dg, (pl.program_id(0) == 0) & (pl.program_id(1) == 0))

    tile = lambda h, r: (r, h)
    fix = lambda h, r: (0, 0)
    return _call(body, (VD // HEAD, T // tr),
                 [(dy, (tr, HEAD), tile), (o, (tr, HEAD), tile), (z, (tr, HEAD), tile), (o_norm.reshape(1, HEAD), (1, HEAD), fix)],
                 [((T, VD), F32, (tr, HEAD), tile), ((T, VD), F32, (tr, HEAD), tile), ((1, HEAD), F32, (1, HEAD), fix)],
                 name="ogate_bwd")


def _chunk_iota():
    return (lax.broadcasted_iota(jnp.int32, (CHUNK, CHUNK), 0), lax.broadcasted_iota(jnp.int32, (CHUNK, CHUNK), 1))


def _decay(gc):
    ri, ci = _chunk_iota()
    gcol = gc[:, :CHUNK]
    grow = jnp.sum(jnp.where(ri == ci, gcol, 0.0), axis=0, keepdims=True)
    return jnp.where(ri >= ci, jnp.exp(jnp.minimum(gcol - grow, 0.0)), 0.0)


def _rowsum(x):
    return jnp.broadcast_to(jnp.sum(x, axis=1, keepdims=True), (CHUNK, HEAD))


def _colsum(e):
    return _hdot(e, jnp.ones((CHUNK, HEAD), F32), TN)


def _unit_lower_inverse(m):
    ri, ci = _chunk_iota()
    p = -m
    x = jnp.where(ri == ci, 1.0, 0.0) + p
    for _ in range(int(math.log2(CHUNK)) - 1):
        p = _hdot(p, p)
        x = x + _hdot(x, p)
    return x


def _gdn_a_fwd(qkv, gb, bb, hk, hv, tr=512):
    T = qkv.shape[0]
    tr = _tile(T, tr)

    def body(k_ref, v_ref, g_ref, b_ref, u_ref, w_ref, gc_ref, ti_ref):
        ri, ci = _chunk_iota()
        ltri = jnp.where(ri >= ci, 1.0, 0.0)
        for c in range(tr // CHUNK):
            rows = pl.ds(c * CHUNK, CHUNK)
            k, v, beta = k_ref[rows, :], v_ref[rows, :], b_ref[rows, :]
            gc = _hdot(ltri, g_ref[rows, :])
            kb = k * beta
            m = jnp.where(ri > ci, _bdot(kb, k, NT) * _decay(gc), 0.0)
            tinv = _unit_lower_inverse(m)
            x = _bdot(tinv, jnp.concatenate([v * beta, kb * jnp.exp(gc)], axis=1))
            u_ref[rows, :] = x[:, :HEAD]
            w_ref[rows, :] = x[:, HEAD:]
            gc_ref[rows, :] = gc
            ti_ref[0, rows, :] = tinv

    tile = lambda h, r: (r, h)
    vd = hv * HEAD
    return _call(body, (hv, T // tr),
                 [(qkv, (tr, HEAD), lambda h, r: (r, hk + h // 2)), (qkv, (tr, HEAD), lambda h, r: (r, 2 * hk + h)),
                  (gb, (tr, HEAD), tile), (bb, (tr, HEAD), tile)],
                 [((T, vd), F32, (tr, HEAD), tile)] * 3 + [((hv, T, CHUNK), F32, (1, tr, CHUNK), lambda h, r: (h, r, 0))],
                 name="gdn_a_fwd", sem=("parallel", "parallel"))


def _gdn_b_fwd(qkv, u, w, gc, hk, hv, tr=512):
    T = qkv.shape[0]
    tr = _tile(T, tr)
    cpb = tr // CHUNK

    def body(q_ref, k_ref, u_ref, w_ref, gc_ref, o_ref, vn_ref, sall_ref, s_ref):
        ri, ci = _chunk_iota()

        @pl.when(pl.program_id(1) == 0)
        def _():
            s_ref[...] = jnp.zeros_like(s_ref)

        for c in range(cpb):
            rows = pl.ds(c * CHUNK, CHUNK)
            q, k, gc = q_ref[rows, :], k_ref[rows, :], gc_ref[rows, :]
            gl = gc[CHUNK - 1:CHUNK, :]
            qk = jnp.where(ri >= ci, _bdot(q, k, NT) * _decay(gc), 0.0)
            s = s_ref[...]
            sall_ref[0, c] = s
            vn = u_ref[rows, :] - _bdot(w_ref[rows, :], s)
            o_ref[rows, :] = _bdot(q * jnp.exp(gc), s) + _bdot(qk, vn)
            vn_ref[rows, :] = vn
            s_ref[...] = s * jnp.exp(gl) + _bdot(k * jnp.exp(gl - gc), vn, TN)

    tile = lambda h, r: (r, h)
    vd = hv * HEAD
    return _call(body, (hv, T // tr),
                 [(qkv, (tr, HEAD), lambda h, r: (r, h // 2)), (qkv, (tr, HEAD), lambda h, r: (r, hk + h // 2)),
                  (u, (tr, HEAD), tile), (w, (tr, HEAD), tile), (gc, (tr, HEAD), tile)],
                 [((T, vd), F32, (tr, HEAD), tile)] * 2 +
                 [((hv, T // CHUNK, HEAD, HEAD), F32, (1, cpb, HEAD, HEAD), lambda h, r: (h, r, 0, 0))],
                 scratch=[pltpu.VMEM((HEAD, HEAD), F32)], name="gdn_b_fwd", sem=("parallel", "arbitrary"))


def _gdn_b_bwd(do, qkv, w, gc, vn, sall, hk, hv, tr=512):
    T = qkv.shape[0]
    tr = _tile(T, tr)
    cpb = tr // CHUNK
    n_r = T // tr

    def body(do_ref, q_ref, k_ref, w_ref, gc_ref, vn_ref, sall_ref, dq_ref, dk_ref, dgc_ref, du_ref, dw_ref, ds_ref):
        ri, ci = _chunk_iota()

        @pl.when(pl.program_id(1) == 0)
        def _():
            ds_ref[...] = jnp.zeros_like(ds_ref)

        for c in reversed(range(cpb)):
            rows = pl.ds(c * CHUNK, CHUNK)
            d_o, q, k, w, gc, vn = do_ref[rows, :], q_ref[rows, :], k_ref[rows, :], w_ref[rows, :], gc_ref[rows, :], vn_ref[rows, :]
            s = sall_ref[0, c]
            ds_next = ds_ref[...]
            gl = gc[CHUNK - 1:CHUNK, :]
            egc, ekd, eg = jnp.exp(gc), jnp.exp(gl - gc), jnp.exp(gl)
            qg, kd = q * egc, k * ekd
            dec = _decay(gc)
            qk = jnp.where(ri >= ci, _bdot(q, k, NT) * dec, 0.0)
            d_qg = _bdot(d_o, s, NT)
            d_qk = jnp.where(ri >= ci, _bdot(d_o, vn, NT), 0.0)
            d_vn = _bdot(qk, d_o, TN) + _bdot(kd, ds_next)
            d_kd = _bdot(vn, ds_next, NT)
            d_eg = jnp.sum(s * ds_next)
            ds_ref[...] = ds_next * eg + _bdot(qg, d_o, TN) - _bdot(w, d_vn, TN)
            d_b = d_qk * dec
            e_q = d_qk * qk
            d_gl = jnp.sum(d_kd * kd) + d_eg * eg
            row = lax.broadcasted_iota(jnp.int32, (CHUNK, HEAD), 0)
            dq_ref[rows, :] = d_qg * egc + _bdot(d_b, k)
            dk_ref[rows, :] = d_kd * ekd + _bdot(d_b, q, TN)
            dgc_ref[rows, :] = (_rowsum(d_qg * qg) - _rowsum(d_kd * kd) + _rowsum(e_q) - _colsum(e_q)
                                + jnp.where(row == CHUNK - 1, d_gl, 0.0))
            du_ref[rows, :] = d_vn
            dw_ref[rows, :] = -_bdot(d_vn, s, NT)

    rtile = lambda h, r: (n_r - 1 - r, h)
    vd = hv * HEAD
    return _call(body, (hv, n_r),
                 [(do, (tr, HEAD), rtile), (qkv, (tr, HEAD), lambda h, r: (n_r - 1 - r, h // 2)),
                  (qkv, (tr, HEAD), lambda h, r: (n_r - 1 - r, hk + h // 2)),
                  (w, (tr, HEAD), rtile), (gc, (tr, HEAD), rtile), (vn, (tr, HEAD), rtile),
                  (sall, (1, cpb, HEAD, HEAD), lambda h, r: (h, n_r - 1 - r, 0, 0))],
                 [((T, vd), F32, (tr, HEAD), rtile)] * 5,
                 scratch=[pltpu.VMEM((HEAD, HEAD), F32)], name="gdn_b_bwd", sem=("parallel", "arbitrary"))


def _gdn_a_bwd(du, dw, dgc_b, qkv, bb, gc, tinv, u, w, hk, hv, tr=512):
    T = qkv.shape[0]
    tr = _tile(T, tr)

    def body(du_ref, dw_ref, dgcb_ref, k_ref, v_ref, b_ref, gc_ref, ti_ref, u_ref, w_ref, dk_ref, dv_ref, db_ref, dg_ref):
        ri, ci = _chunk_iota()
        utri = jnp.where(ci >= ri, 1.0, 0.0)
        for c in range(tr // CHUNK):
            rows = pl.ds(c * CHUNK, CHUNK)
            k, v, beta, gc = k_ref[rows, :], v_ref[rows, :], b_ref[rows, :], gc_ref[rows, :]
            egc = jnp.exp(gc)
            kb = k * beta
            dec = _decay(gc)
            m = jnp.where(ri > ci, _bdot(kb, k, NT) * dec, 0.0)
            d_r = _bdot(ti_ref[0, rows, :], jnp.concatenate([du_ref[rows, :], dw_ref[rows, :]], axis=1), TN)
            d_vb, d_kbe = d_r[:, :HEAD], d_r[:, HEAD:]
            x = jnp.concatenate([u_ref[rows, :], w_ref[rows, :]], axis=1)
            d_m = jnp.where(ri > ci, -_bdot(d_r, x, NT), 0.0)
            d_a = d_m * dec
            e_m = d_m * m
            d_kb = _bdot(d_a, k) + d_kbe * egc
            dk_ref[rows, :] = _bdot(d_a, kb, TN) + d_kb * beta
            dv_ref[rows, :] = d_vb * beta
            db_ref[rows, :] = _rowsum(d_vb * v) + _rowsum(d_kb * k)
            d_gc = _rowsum(e_m) - _colsum(e_m) + _rowsum(d_kbe * kb * egc) + dgcb_ref[rows, :]
            dg_ref[rows, :] = _hdot(utri, d_gc)

    tile = lambda h, r: (r, h)
    vd = hv * HEAD
    return _call(body, (hv, T // tr),
                 [(du, (tr, HEAD), tile), (dw, (tr, HEAD), tile), (dgc_b, (tr, HEAD), tile),
                  (qkv, (tr, HEAD), lambda h, r: (r, hk + h // 2)), (qkv, (tr, HEAD), lambda h, r: (r, 2 * hk + h)),
                  (bb, (tr, HEAD), tile), (gc, (tr, HEAD), tile), (tinv, (1, tr, CHUNK), lambda h, r: (h, r, 0)),
                  (u, (tr, HEAD), tile), (w, (tr, HEAD), tile)],
                 [((T, vd), F32, (tr, HEAD), tile)] * 4, name="gdn_a_bwd", sem=("parallel", "parallel"))


def _pair_sum(a, b_, tr=512):
    T, vd = a.shape
    tr = _tile(T, tr)

    def body(a0, a1, b0, b1, o_ref):
        o_ref[...] = (a0[...] + b0[...]) + (a1[...] + b1[...])

    even = lambda j, r: (r, 2 * j)
    odd = lambda j, r: (r, 2 * j + 1)
    return _call(body, (vd // HEAD // 2, T // tr),
                 [(a, (tr, HEAD), even), (a, (tr, HEAD), odd), (b_, (tr, HEAD), even), (b_, (tr, HEAD), odd)],
                 [((T, vd // 2), F32, (tr, HEAD), lambda j, r: (r, j))], name="gdn_pair_sum", sem=("parallel", "parallel"))[0]


def _s5_param_math(lr, li, ls, br, bi):
    step = jnp.exp(ls)
    zr, zi = lr * step, li * step
    mag = jnp.exp(zr)
    ar, ai = mag * jnp.cos(zi), mag * jnp.sin(zi)
    den = lr * lr + li * li
    nr, ni = ar - 1.0, ai
    cr, cim = (nr * lr + ni * li) / den, (ni * lr - nr * li) / den
    return ar, ai, br * cr - bi * cim, br * cim + bi * cr


def _s5_params_fwd(lr, li, ls, br, bi):
    G, P = lr.shape

    def body(lr_ref, li_ref, ls_ref, br_ref, bi_ref, ar_ref, ai_ref, bbr_ref, bbi_ref):
        ar, ai, bbr, bbi = _s5_param_math(lr_ref[...], li_ref[...], ls_ref[...], br_ref[...], bi_ref[...])
        ar_ref[...], ai_ref[...], bbr_ref[...], bbi_ref[...] = ar, ai, bbr, bbi

    shapes = [(G, P), (G, P), (G, 1), (S5_CH, G, P), (S5_CH, G, P)]
    z = lambda n: (lambda: (0,) * n)
    return _call(body, (), [(a, s, z(len(s))) for a, s in zip((lr, li, ls, br, bi), shapes)],
                 [(s, F32, s, z(len(s))) for s in (shapes[0], shapes[0], shapes[3], shapes[3])],
                 name="s5_params_fwd", sem=())


def _s5_params_bwd(lr, li, ls, br, bi, dar, dai, dbbr, dbbi):
    G, P = lr.shape

    def body(lr_ref, li_ref, ls_ref, br_ref, bi_ref, dar_ref, dai_ref, dbr_ref, dbi_ref, o0, o1, o2, o3, o4):
        _, vjp = jax.vjp(_s5_param_math, lr_ref[...], li_ref[...], ls_ref[...], br_ref[...], bi_ref[...])
        outs = vjp((dar_ref[...], dai_ref[...], dbr_ref[...], dbi_ref[...]))
        for r, v in zip((o0, o1, o2, o3, o4), outs):
            r[...] = v

    shapes = [(G, P), (G, P), (G, 1), (S5_CH, G, P), (S5_CH, G, P)]
    z = lambda n: (lambda: (0,) * n)
    ins = list(zip((lr, li, ls, br, bi), shapes)) + list(zip((dar, dai, dbbr, dbbi), (shapes[0], shapes[0], shapes[3], shapes[3])))
    return _call(body, (), [(a, s, z(len(s))) for a, s in ins], [(s, F32, s, z(len(s))) for s in shapes],
                 name="s5_params_bwd", sem=())


def _s5_bproj_fwd(u, bd_re, bd_im, tr=512):
    T, D = u.shape
    tr = _tile(T, tr)
    nb = D // LANES

    def body(u_ref, br_ref, bi_ref, or_ref, oi_ref):
        ub = u_ref[...]
        or_ref[...] = _bdot(ub, br_ref[0])
        oi_ref[...] = _bdot(ub, bi_ref[0])

    blk = lambda i, j: (j, 0, 0)
    return _call(body, (T // tr, nb),
                 [(u, (tr, LANES), lambda i, j: (i, j)), (bd_re, (1, LANES, S5_SPB), blk), (bd_im, (1, LANES, S5_SPB), blk)],
                 [((T, nb * S5_SPB), F32, (tr, S5_SPB), lambda i, j: (i, j))] * 2, name="s5_bproj_fwd", sem=("parallel", "parallel"))


def _s5_scan(br, bi, lam, reverse, xr=None, xi=None, tl=512, bw=256):
    T, NCH = br.shape
    tl, bw = _tile(T, tl), _tile(NCH, bw)
    n_t = T // tl
    l8 = tl // 8

    def body(*refs):
        if reverse:
            br_ref, bi_ref, lam_ref, sr_ref, si_ref, hr_ref, hi_ref, or_ref, oi_ref, dl_ref, pr, pi_, cr, ci_ = refs
        else:
            br_ref, bi_ref, lam_ref, or_ref, oi_ref, pr, pi_, cr, ci_ = refs
        t = pl.program_id(1)
        lr = lam_ref[0:1, :]
        li = -lam_ref[1:2, :] if reverse else lam_ref[1:2, :]
        row = lax.broadcasted_iota(jnp.int32, (tl, bw), 0)

        def scan(x_r, x_i):
            a_r, a_i = lr, li
            s = 1
            while s < tl:
                keep = (row < tl - s) if reverse else (row >= s)
                shift = tl - s if reverse else s
                s_r = jnp.where(keep, pltpu.roll(x_r, shift, 0), 0.0)
                s_i = jnp.where(keep, pltpu.roll(x_i, shift, 0), 0.0)
                x_r, x_i = x_r + a_r * s_r - a_i * s_i, x_i + a_r * s_i + a_i * s_r
                a_r, a_i = a_r * a_r - a_i * a_i, 2.0 * a_r * a_i
                s *= 2
            return x_r, x_i

        edge = tl - 1 if reverse else 0

        @pl.when(t == 0)
        def _():
            p_r, p_i = scan(jnp.where(row == edge, lr, 0.0), jnp.where(row == edge, li, 0.0))
            pr[...], pi_[...] = p_r, p_i
            cr[...] = jnp.zeros_like(cr)
            ci_[...] = jnp.zeros_like(ci_)

        x_r, x_i = scan(br_ref[...], bi_ref[...])
        c_r, c_i = cr[0:1, :], ci_[0:1, :]
        p_r, p_i = pr[...], pi_[...]
        x_r, x_i = x_r + p_r * c_r - p_i * c_i, x_i + p_r * c_i + p_i * c_r
        or_ref[...], oi_ref[...] = x_r, x_i
        last = tl - 1 - edge
        cr[0:1, :] = x_r[last:last + 1, :]
        ci_[0:1, :] = x_i[last:last + 1, :]
        if reverse:
            first_block = t == n_t - 1
            h_r = jnp.where(first_block, 0.0, hr_ref[7:8, :])
            h_i = jnp.where(first_block, 0.0, hi_ref[7:8, :])
            s_r = jnp.where(row == 0, h_r, pltpu.roll(sr_ref[...], 1, 0))
            s_i = jnp.where(row == 0, h_i, pltpu.roll(si_ref[...], 1, 0))

            @pl.when(t == 0)
            def _():
                dl_ref[...] = jnp.zeros_like(dl_ref)

            dl_ref[0:1, :] += jnp.sum(s_r * x_r + s_i * x_i, axis=0, keepdims=True)
            dl_ref[1:2, :] += jnp.sum(s_r * x_i - s_i * x_r, axis=0, keepdims=True)

    tmap = (lambda c, t: (n_t - 1 - t, c)) if reverse else (lambda c, t: (t, c))
    col = lambda c, t: (0, c)
    ins = [(br, (tl, bw), tmap), (bi, (tl, bw), tmap), (lam, (2, bw), col)]
    outs = [((T, NCH), F32, (tl, bw), tmap)] * 2
    if reverse:
        halo = lambda c, t: (jnp.maximum((n_t - 1 - t) * l8 - 1, 0), c)
        ins += [(xr, (tl, bw), tmap), (xi, (tl, bw), tmap), (xr, (8, bw), halo), (xi, (8, bw), halo)]
        outs += [((2, NCH), F32, (2, bw), col)]
    return _call(body, (NCH // bw, n_t), ins, outs,
                 scratch=[pltpu.VMEM((tl, bw), F32), pltpu.VMEM((tl, bw), F32), pltpu.VMEM((8, bw), F32), pltpu.VMEM((8, bw), F32)],
                 name="s5_scan_bwd" if reverse else "s5_scan_fwd", sem=("parallel", "arbitrary"))


def _s5_cproj_fwd(xr, xi, cd_re, cd_im, u, d, tr=512):
    T, D = u.shape
    tr = _tile(T, tr)

    def body(xr_ref, xi_ref, cr_ref, ci_ref, u_ref, d_ref, y_ref, h_ref):
        y = _bdot(xr_ref[...], cr_ref[0]) + _bdot(xi_ref[...], ci_ref[0]) + d_ref[...] * u_ref[...]
        y_ref[...] = y
        h_ref[...] = jax.nn.gelu(y).astype(BF16)

    tile = lambda i, j: (i, j)
    blk = lambda i, j: (j, 0, 0)
    return _call(body, (T // tr, D // LANES),
                 [(xr, (tr, S5_SPB), tile), (xi, (tr, S5_SPB), tile), (cd_re, (1, S5_SPB, LANES), blk), (cd_im, (1, S5_SPB, LANES), blk),
                  (u, (tr, LANES), tile), (d.reshape(1, D), (1, LANES), lambda i, j: (0, j))],
                 [((T, D), F32, (tr, LANES), tile), ((T, D), BF16, (tr, LANES), tile)], name="s5_cproj_fwd", sem=("parallel", "parallel"))


def _s5_cproj_bwd(dy, xr, xi, cd_re, cd_im, u, d, tr=512):
    T, D = u.shape
    tr = _tile(T, tr)
    nb = D // LANES

    def body(dy_ref, xr_ref, xi_ref, cr_ref, ci_ref, u_ref, d_ref, dxr_ref, dxi_ref, du_ref, dd_ref, dcr_ref, dci_ref):
        g = dy_ref[...]
        dxr_ref[...] = _bdot(g, cr_ref[0], NT)
        dxi_ref[...] = _bdot(g, ci_ref[0], NT)
        du_ref[...] = g * d_ref[...]
        first = pl.program_id(1) == 0
        _accumulate(dd_ref, jnp.sum(g * u_ref[...], axis=0, keepdims=True), first)

        @pl.when(first)
        def _():
            dcr_ref[...] = jnp.zeros_like(dcr_ref)
            dci_ref[...] = jnp.zeros_like(dci_ref)

        dcr_ref[0] += _bdot(xr_ref[...], g, TN)
        dci_ref[0] += _bdot(xi_ref[...], g, TN)

    tile = lambda j, i: (i, j)
    blk = lambda j, i: (j, 0, 0)
    col = lambda j, i: (0, j)
    return _call(body, (nb, T // tr),
                 [(dy, (tr, LANES), tile), (xr, (tr, S5_SPB), tile), (xi, (tr, S5_SPB), tile),
                  (cd_re, (1, S5_SPB, LANES), blk), (cd_im, (1, S5_SPB, LANES), blk), (u, (tr, LANES), tile), (d.reshape(1, D), (1, LANES), col)],
                 [((T, nb * S5_SPB), F32, (tr, S5_SPB), tile)] * 2 + [((T, D), F32, (tr, LANES), tile), ((1, D), F32, (1, LANES), col)]
                 + [((nb, S5_SPB, LANES), F32, (1, S5_SPB, LANES), blk)] * 2, name="s5_cproj_bwd")


def _s5_bproj_bwd(dbr, dbi, bd_re, bd_im, u, du_skip, tr=512):
    T, D = u.shape
    tr = _tile(T, tr)
    nb = D // LANES

    def body(gr_ref, gi_ref, br_ref, bi_ref, u_ref, ds_ref, du_ref, dbr_ref, dbi_ref):
        g_r, g_i, ub = gr_ref[...], gi_ref[...], u_ref[...]
        du_ref[...] = ds_ref[...] + _bdot(g_r, br_ref[0], NT) + _bdot(g_i, bi_ref[0], NT)

        @pl.when(pl.program_id(1) == 0)
        def _():
            dbr_ref[...] = jnp.zeros_like(dbr_ref)
            dbi_ref[...] = jnp.zeros_like(dbi_ref)

        dbr_ref[0] += _bdot(ub, g_r, TN)
        dbi_ref[0] += _bdot(ub, g_i, TN)

    tile = lambda j, i: (i, j)
    blk = lambda j, i: (j, 0, 0)
    return _call(body, (nb, T // tr),
                 [(dbr, (tr, S5_SPB), tile), (dbi, (tr, S5_SPB), tile), (bd_re, (1, LANES, S5_SPB), blk), (bd_im, (1, LANES, S5_SPB), blk),
                  (u, (tr, LANES), tile), (du_skip, (tr, LANES), tile)],
                 [((T, D), F32, (tr, LANES), tile)] + [((nb, LANES, S5_SPB), F32, (1, LANES, S5_SPB), blk)] * 2, name="s5_bproj_bwd")


def _s5_gate_fwd(h, vg, tr=256):
    T, D = h.shape
    tr = _tile(T, tr)

    def body(h_ref, a_ref, b_ref, o_ref):
        o_ref[...] = h_ref[...] + a_ref[...] * jax.nn.sigmoid(b_ref[...])

    row = lambda i: (i, 0)
    return _call(body, (T // tr,), [(h, (tr, D), row), (vg, (tr, D), row), (vg, (tr, D), lambda i: (i, 1))],
                 [((T, D), F32, (tr, D), row)], name="s5_gate_fwd", sem=("parallel",))[0]


def _s5_gate_bwd(dh, vg, tr=256):
    T, D = dh.shape
    tr = _tile(T, tr)

    def body(d_ref, a_ref, b_ref, o_ref):
        d = d_ref[...]
        sig = jax.nn.sigmoid(b_ref[...])
        o_ref[:, :D] = (d * sig).astype(BF16)
        o_ref[:, D:] = (d * a_ref[...] * sig * (1.0 - sig)).astype(BF16)

    row = lambda i: (i, 0)
    return _call(body, (T // tr,), [(dh, (tr, D), row), (vg, (tr, D), row), (vg, (tr, D), lambda i: (i, 1))],
                 [((T, 2 * D), BF16, (tr, 2 * D), row)], name="s5_gate_bwd", sem=("parallel",))[0]


def _block_diag(w, transpose):
    g, a, b = w.shape
    if transpose:
        w = w.transpose(0, 2, 1)
        a, b = b, a
    eye = jnp.eye(S5_GPB, dtype=w.dtype)
    return jnp.einsum("jgab,gh->jgahb", w.reshape(g // S5_GPB, S5_GPB, a, b), eye).reshape(g // S5_GPB, S5_GPB * a, S5_GPB * b)


def _block_diag_extract(wd, a, b, transpose):
    if transpose:
        a, b = b, a
    nb = wd.shape[0]
    eye = jnp.eye(S5_GPB, dtype=wd.dtype)
    w = jnp.einsum("jgahb,gh->jgab", wd.reshape(nb, S5_GPB, a, S5_GPB, b), eye).reshape(nb * S5_GPB, a, b)
    return w.transpose(0, 2, 1) if transpose else w


def _mesh_position():
    return lax.axis_index("x"), lax.axis_index("y"), lax.axis_index("c")


def _all_gather(block, name):
    def body(x_ref, out_ref, send_sems, recv_sems, local_sem):
        x, y, c = _mesh_position()
        me, sibling = (x, y, c), (x, y, 1 - c)
        chips = [(1 - x, y), (x, 1 - y), (1 - x, 1 - y)]

        def slot(px, py, pc):
            return out_ref.at[4 * px + 2 * py + pc]

        def copy(k, blk, to, src=None):
            return pltpu.make_async_remote_copy(
                src_ref=slot(*blk) if src is None else src, dst_ref=slot(*blk),
                send_sem=send_sems.at[k], recv_sem=recv_sems.at[k], device_id=to, device_id_type=pl.DeviceIdType.MESH)

        mine = pltpu.make_async_copy(x_ref, slot(*me), local_sem)
        mine.start()
        first = [copy(0, me, sibling, src=x_ref)]
        first += [copy(1 + j, me, (*chip, c), src=x_ref) for j, chip in enumerate(chips)]
        for cp in first:
            cp.start()
        passed = [copy(4 + j, (*chip, c), sibling) for j, chip in enumerate(chips)]
        for j, chip in enumerate(chips):
            copy(1 + j, (*chip, c), me).wait_recv()
            passed[j].start()
        copy(0, sibling, me).wait_recv()
        for j, chip in enumerate(chips):
            copy(4 + j, (*chip, 1 - c), me).wait_recv()
        for cp in first + passed:
            cp.wait_send()
        mine.wait()

    return pl.pallas_call(
        body, out_shape=jax.ShapeDtypeStruct((N_DEV,) + block.shape, block.dtype),
        in_specs=[pl.BlockSpec(memory_space=pl.ANY)], out_specs=pl.BlockSpec(memory_space=pl.ANY),
        scratch_shapes=[pltpu.SemaphoreType.DMA((N_DEV - 1,)), pltpu.SemaphoreType.DMA((N_DEV - 1,)), pltpu.SemaphoreType.DMA],
        name=name)(block)


def _all_to_all(parts, name):
    def body(g_ref, out_ref, send_sems, recv_sems, local_sem):
        x, y, c = _mesh_position()
        me = 4 * x + 2 * y + c
        mine = pltpu.make_async_copy(g_ref.at[me], out_ref.at[me], local_sem)
        mine.start()
        copies = []
        for k in range(1, N_DEV):
            px = 1 - x if k & 4 else x
            py = 1 - y if k & 2 else y
            pc = 1 - c if k & 1 else c
            copies.append(pltpu.make_async_remote_copy(
                src_ref=g_ref.at[4 * px + 2 * py + pc], dst_ref=out_ref.at[me],
                send_sem=send_sems.at[k - 1], recv_sem=recv_sems.at[k - 1],
                device_id=(px, py, pc), device_id_type=pl.DeviceIdType.MESH))
        for cp in copies:
            cp.start()
        for cp in copies:
            cp.wait()
        mine.wait()

    return pl.pallas_call(
        body, out_shape=jax.ShapeDtypeStruct(parts.shape, parts.dtype),
        in_specs=[pl.BlockSpec(memory_space=pl.ANY)], out_specs=pl.BlockSpec(memory_space=pl.ANY),
        scratch_shapes=[pltpu.SemaphoreType.DMA((N_DEV - 1,)), pltpu.SemaphoreType.DMA((N_DEV - 1,)), pltpu.SemaphoreType.DMA],
        name=name)(parts)


def _adamw(parts, w, m, v, name):
    R = w.shape[0]
    tr = _tile(R, PACK_ROWS)
    c1 = 1.0 - ADAM_B1 ** ADAM_STEP
    c2 = 1.0 - ADAM_B2 ** ADAM_STEP

    def body(p_ref, w_ref, m_ref, v_ref, g_ref, d_ref, nm_ref, nv_ref):
        g = p_ref[0].astype(F32)
        for s in range(1, N_DEV):
            g = g + p_ref[s].astype(F32)
        nm = ADAM_B1 * m_ref[...] + (1.0 - ADAM_B1) * g
        nv = ADAM_B2 * v_ref[...] + (1.0 - ADAM_B2) * (g * g)
        g_ref[...], nm_ref[...], nv_ref[...] = g, nm, nv
        d_ref[...] = -ADAM_LR * ((nm / c1) / (jnp.sqrt(nv / c2) + ADAM_EPS) + ADAM_WD * w_ref[...])

    row = lambda i: (i, 0)
    return _call(body, (R // tr,),
                 [(parts, (N_DEV, tr, PACK_W), lambda i: (0, i, 0)), (w, (tr, PACK_W), row), (m, (tr, PACK_W), row), (v, (tr, PACK_W), row)],
                 [((R, PACK_W), F32, (tr, PACK_W), row)] * 4, name=name, sem=("parallel",))


def _pack(flat_pieces, dtype, lead=()):
    cat = jnp.concatenate([p_.astype(dtype) for p_ in flat_pieces], axis=-1)
    n = cat.shape[-1]
    quantum = PACK_ROWS * PACK_W
    total = -(-n // quantum) * quantum
    cat = jnp.pad(cat, [(0, 0)] * len(lead) + [(0, total - n)])
    return cat.reshape(lead + (total // PACK_W, PACK_W)), n


SHARDED = {"gdn_w_in": 1, "gdn_conv_w": 1, "gdn_w_out": 0, "s5_w_in": 0, "s5_d": 1, "s5_w_out": 1,
           "mlp_w_up": 1, "mlp_w_down": 0, "ple_w_proj": 1, "ple_w_gate": 0}
GATHER_F32 = ("gdn_conv_w", "s5_d")
REPLICATED = ("norm_mix", "norm_mlp", "norm_ple", "norm_final", "gdn_a_log", "gdn_dt_bias", "gdn_o_norm",
              "s5_lam_re", "s5_lam_im", "s5_log_step", "s5_b_re", "s5_b_im", "s5_c_re", "s5_c_im")
WEIGHTS = ("norm_mix", "norm_mlp", "norm_ple", "norm_final", "gdn_w_in", "gdn_conv_w", "gdn_a_log", "gdn_dt_bias",
           "gdn_o_norm", "gdn_w_out", "s5_w_in", "s5_lam_re", "s5_lam_im", "s5_log_step", "s5_b_re", "s5_b_im",
           "s5_c_re", "s5_c_im", "s5_d", "s5_w_out", "mlp_w_up", "mlp_w_down", "ple_w_proj", "ple_w_gate")


def _layer_shards(arr):
    if arr.ndim == 2:
        return [arr[l][None, :] for l in range(arr.shape[0])]
    return [arr[l] for l in range(arr.shape[0])]


def _gather_weights(w):
    pieces, meta = [], []
    for name, axis in SHARDED.items():
        for l, sh in enumerate(_layer_shards(w[name])):
            bits = lax.bitcast_convert_type(sh, BF16) if name in GATHER_F32 else sh.astype(BF16)
            meta.append((name, l, axis, sh.shape, bits.shape))
            pieces.append(bits.reshape(-1))
    packed, _ = _pack(pieces, BF16)
    gathered = _all_gather(packed, "gather_weights").reshape(N_DEV, -1)
    full, off = {}, 0
    for name, l, axis, (r, c), bshape in meta:
        n = math.prod(bshape)
        blk = gathered[:, off:off + n].reshape((N_DEV,) + bshape)
        off += n
        if name in GATHER_F32:
            blk = lax.bitcast_convert_type(blk, F32)
        whole = blk.reshape(N_DEV * r, c) if axis == 0 else blk.transpose(1, 0, 2).reshape(r, N_DEV * c)
        full.setdefault(name, []).append(whole)
    return full


def _split_eight(g, axis):
    r, c = g.shape
    if axis == 0:
        return g.reshape(N_DEV, -1)
    return g.reshape(r, N_DEV, c // N_DEV).transpose(1, 0, 2).reshape(N_DEV, -1)


def kernel(x, p, norm_mix, norm_mlp, norm_ple, norm_final, gdn_w_in, gdn_conv_w, gdn_a_log, gdn_dt_bias, gdn_o_norm, gdn_w_out, s5_w_in, s5_lam_re, s5_lam_im, s5_log_step, s5_b_re, s5_b_im, s5_c_re, s5_c_im, s5_d, s5_w_out, mlp_w_up, mlp_w_down, ple_w_proj, ple_w_gate, loss_target, m_norm_mix, m_norm_mlp, m_norm_ple, m_norm_final, m_gdn_w_in, m_gdn_conv_w, m_gdn_a_log, m_gdn_dt_bias, m_gdn_o_norm, m_gdn_w_out, m_s5_w_in, m_s5_lam_re, m_s5_lam_im, m_s5_log_step, m_s5_b_re, m_s5_b_im, m_s5_c_re, m_s5_c_im, m_s5_d, m_s5_w_out, m_mlp_w_up, m_mlp_w_down, m_ple_w_proj, m_ple_w_gate, v_norm_mix, v_norm_mlp, v_norm_ple, v_norm_final, v_gdn_w_in, v_gdn_conv_w, v_gdn_a_log, v_gdn_dt_bias, v_gdn_o_norm, v_gdn_w_out, v_s5_w_in, v_s5_lam_re, v_s5_lam_im, v_s5_log_step, v_s5_b_re, v_s5_b_im, v_s5_c_re, v_s5_c_im, v_s5_d, v_s5_w_out, v_mlp_w_up, v_mlp_w_down, v_ple_w_proj, v_ple_w_gate):
    args = dict(locals())
    w = {n: args[n] for n in WEIGHTS}
    mom = {n: args["m_" + n] for n in WEIGHTS}
    vel = {n: args["v_" + n] for n in WEIGHTS}
    depth = norm_mix.shape[0]
    T, D = x.shape[1], x.shape[2]
    hv = gdn_a_log.shape[1]
    vd = hv * HEAD
    cd = gdn_conv_w.shape[2] * N_DEV
    hk = (cd - vd) // (2 * HEAD)
    kd = hk * HEAD
    assert hv == 2 * hk and 2 * hv <= LANES and T % CHUNK == 0
    G, P = s5_lam_re.shape[1], s5_lam_re.shape[2]
    assert P == S5_STATE and G * S5_CH == D and G % S5_GPB == 0

    full = _gather_weights(w)
    h = x[0]
    tgt = loss_target[0]
    grads = {n: [None] * w[n].shape[0] for n in WEIGHTS if n != "norm_final"}
    saved = []

    for i in range(depth):
        j = i // 2
        sv = {"h0": h}
        hn = _rms_fwd(h, norm_mix[i])
        sv["hn"] = hn
        if i % 2 == 0:
            w_in = full["gdn_w_in"][j]
            w_ba = jnp.pad(w_in[:, cd + vd:], ((0, 0), (0, LANES - 2 * hv)))
            pq = _mm(hn, w_in[:, :cd], name="gdn_in_qkv")[0]
            pz = _mm(hn, w_in[:, cd:cd + vd], name="gdn_in_z")[0]
            ba = _mm(hn, w_ba, name="gdn_in_ba")[0]
            conv_w = full["gdn_conv_w"][j]
            qkv = _gdn_pre_fwd(pq, conv_w, hk)
            pv = jnp.pad(jnp.stack([gdn_a_log[j], gdn_dt_bias[j]]), ((0, 0), (hv, LANES - 2 * hv)))
            g2 = _gates_fwd(ba, pv, hv)
            bb = jnp.repeat(g2[:, :hv], HEAD, axis=1)
            gb = jnp.repeat(g2[:, hv:2 * hv], HEAD, axis=1)
            u, ww, gc, tinv = _gdn_a_fwd(qkv, gb, bb, hk, hv)
            o, vn, sall = _gdn_b_fwd(qkv, u, ww, gc, hk, hv)
            on = _ogate_fwd(o, pz, gdn_o_norm[j])
            h = _mm(on, full["gdn_w_out"][j], epi=lambda acc, r: (r + acc,), extras=(h,), name="gdn_out")[0]
            sv.update(pq=pq, pz=pz, ba=ba, pv=pv, qkv=qkv, bb=bb, u=u, ww=ww, gc=gc, tinv=tinv, o=o, vn=vn, sall=sall, on=on)
        else:
            uu = _mm(hn, full["s5_w_in"][j], name="s5_in")[0]
            b_re_t, b_im_t = s5_b_re[j].transpose(2, 0, 1), s5_b_im[j].transpose(2, 0, 1)
            ls = s5_log_step[j].reshape(G, 1)
            ar, ai, bbr, bbi = _s5_params_fwd(s5_lam_re[j], s5_lam_im[j], ls, b_re_t, b_im_t)
            lam = jnp.stack([ar.reshape(-1), ai.reshape(-1)])
            bd_re = _block_diag(bbr.transpose(1, 2, 0), transpose=True).astype(BF16)
            bd_im = _block_diag(bbi.transpose(1, 2, 0), transpose=True).astype(BF16)
            cd_re = _block_diag(s5_c_re[j], transpose=True).astype(BF16)
            cd_im = _block_diag(-s5_c_im[j], transpose=True).astype(BF16)
            bur, bui = _s5_bproj_fwd(uu, bd_re, bd_im)
            xr, xi = _s5_scan(bur, bui, lam, reverse=False)
            dsk = full["s5_d"][j]
            yy, hact = _s5_cproj_fwd(xr, xi, cd_re, cd_im, uu, dsk)
            vg = _mm(hact, full["s5_w_out"][j], name="s5_out")[0]
            h = _s5_gate_fwd(h, vg)
            sv.update(uu=uu, b_re_t=b_re_t, b_im_t=b_im_t, ls=ls, lam=lam, bd_re=bd_re, bd_im=bd_im, cd_re=cd_re, cd_im=cd_im,
                      xr=xr, xi=xi, dsk=dsk, yy=yy, hact=hact, vg=vg)
        sv["h1"] = h
        hm = _rms_fwd(h, norm_mlp[i])
        up, act = _mm(hm, full["mlp_w_up"][i], out_dtypes=(F32, BF16),
                      epi=lambda acc: (acc, jnp.square(jnp.maximum(acc, 0.0))), name="mlp_up")
        h = _mm(act, full["mlp_w_down"][i], epi=lambda acc, r: (r + acc,), extras=(h,), name="mlp_down")[0]
        sv.update(hm=hm, up=up, act=act, h2=h)
        hp = _rms_fwd(h, norm_ple[i])
        s_gate = _mm(hp, full["ple_w_gate"][i], name="ple_gate")[0]
        pp = _mm(p[i, 0], full["ple_w_proj"][i], name="ple_proj")[0]
        h = _ple_fwd(h, s_gate, pp)
        sv.update(hp=hp, s_gate=s_gate, pp=pp)
        saved.append(sv)

    dh, d_norm_final, loss_part = _loss_fwd_bwd(h, norm_final, tgt)
    loss = lax.psum(loss_part[0, 0], MESH_AXES)

    for i in reversed(range(depth)):
        j = i // 2
        sv = saved[i]
        ds, dpp = _ple_bwd(dh, sv["s_gate"], sv["pp"])
        grads["ple_w_proj"][i] = _mm(p[i, 0], dpp, "tn", out_dtypes=(BF16,), name="ple_proj_dw")[0]
        grads["ple_w_gate"][i] = _mm(sv["hp"], ds, "tn", out_dtypes=(BF16,), name="ple_gate_dw")[0]
        d_hp = _mm(ds, full["ple_w_gate"][i], "nt", name="ple_gate_dx")[0]
        dh, grads["norm_ple"][i] = _rms_bwd(d_hp, sv["h2"], norm_ple[i], dh)
        grads["mlp_w_down"][i] = _mm(sv["act"], dh, "tn", out_dtypes=(BF16,), name="mlp_down_dw")[0]
        d_up = _mm(dh, full["mlp_w_down"][i], "nt", out_dtypes=(BF16,),
                   epi=lambda acc, up_: (acc * 2.0 * jnp.maximum(up_, 0.0),), extras=(sv["up"],), name="mlp_down_dx")[0]
        grads["mlp_w_up"][i] = _mm(sv["hm"], d_up, "tn", out_dtypes=(BF16,), name="mlp_up_dw")[0]
        d_hm = _mm(d_up, full["mlp_w_up"][i], "nt", name="mlp_up_dx")[0]
        dh, grads["norm_mlp"][i] = _rms_bwd(d_hm, sv["h1"], norm_mlp[i], dh)
        if i % 2 == 0:
            grads["gdn_w_out"][j] = _mm(sv["on"], dh, "tn", out_dtypes=(BF16,), name="gdn_out_dw")[0]
            d_on = _mm(dh, full["gdn_w_out"][j], "nt", name="gdn_out_dx")[0]
            d_o, d_z, grads["gdn_o_norm"][j] = _ogate_bwd(d_on, sv["o"], sv["pz"], gdn_o_norm[j])
            dq_b, dk_b, dgc_b, d_u, d_w = _gdn_b_bwd(d_o, sv["qkv"], sv["ww"], sv["gc"], sv["vn"], sv["sall"], hk, hv)
            dk_a, d_v, d_bb, d_gb = _gdn_a_bwd(d_u, d_w, dgc_b, sv["qkv"], sv["bb"], sv["gc"], sv["tinv"], sv["u"], sv["ww"], hk, hv)
            zeros = jnp.zeros_like(dq_b)
            d_qkv = jnp.concatenate([_pair_sum(dq_b, zeros), _pair_sum(dk_a, dk_b), d_v], axis=1)
            conv_w = full["gdn_conv_w"][j]
            d_c, d_convw = _gdn_pre_bwd(d_qkv, sv["pq"], conv_w, hk)
            grads["gdn_conv_w"][j] = d_convw
            d_pq = _gdn_conv_bwd(d_c, conv_w)
            d_g2 = jnp.pad(jnp.concatenate([d_bb[:, ::HEAD], d_gb[:, ::HEAD]], axis=1), ((0, 0), (0, LANES - 2 * hv)))
            d_ba, d_pv = _gates_bwd(d_g2, sv["ba"], sv["pv"], hv)
            grads["gdn_a_log"][j] = d_pv[0, hv:2 * hv]
            grads["gdn_dt_bias"][j] = d_pv[1, hv:2 * hv]
            hn = sv["hn"]
            w_in = full["gdn_w_in"][j]
            w_ba = jnp.pad(w_in[:, cd + vd:], ((0, 0), (0, LANES - 2 * hv)))
            dw_q = _mm(hn, d_pq, "tn", out_dtypes=(BF16,), name="gdn_in_qkv_dw")[0]
            dw_z = _mm(hn, d_z, "tn", out_dtypes=(BF16,), name="gdn_in_z_dw")[0]
            dw_ba = _mm(hn, d_ba, "tn", out_dtypes=(BF16,), name="gdn_in_ba_dw")[0]
            grads["gdn_w_in"][j] = jnp.concatenate([dw_q, dw_z, dw_ba[:, :2 * hv]], axis=1)
            add = lambda acc, r: (r + acc,)
            d_hn = _mm(d_pq, w_in[:, :cd], "nt", name="gdn_in_qkv_dx")[0]
            d_hn = _mm(d_z, w_in[:, cd:cd + vd], "nt", epi=add, extras=(d_hn,), name="gdn_in_z_dx")[0]
            d_hn = _mm(d_ba, w_ba, "nt", epi=add, extras=(d_hn,), name="gdn_in_ba_dx")[0]
        else:
            d_vg = _s5_gate_bwd(dh, sv["vg"])
            grads["s5_w_out"][j] = _mm(sv["hact"], d_vg, "tn", out_dtypes=(BF16,), name="s5_out_dw")[0]

            def gelu_bwd(acc, y_):
                _, vjp = jax.vjp(jax.nn.gelu, y_)
                return (vjp(acc)[0],)

            d_y = _mm(d_vg, full["s5_w_out"][j], "nt", epi=gelu_bwd, extras=(sv["yy"],), name="s5_out_dx")[0]
            d_xr, d_xi, du_skip, d_dsk, d_cdr, d_cdi = _s5_cproj_bwd(d_y, sv["xr"], sv["xi"], sv["cd_re"], sv["cd_im"], sv["uu"], sv["dsk"])
            grads["s5_d"][j] = d_dsk
            grads["s5_c_re"][j] = _block_diag_extract(d_cdr, S5_CH, S5_STATE, transpose=True)
            grads["s5_c_im"][j] = -_block_diag_extract(d_cdi, S5_CH, S5_STATE, transpose=True)
            d_bur, d_bui, d_lam = _s5_scan(d_xr, d_xi, sv["lam"], reverse=True, xr=sv["xr"], xi=sv["xi"])
            d_uu, d_bdr, d_bdi = _s5_bproj_bwd(d_bur, d_bui, sv["bd_re"], sv["bd_im"], sv["uu"], du_skip)
            d_bbr = _block_diag_extract(d_bdr, S5_STATE, S5_CH, transpose=True).transpose(2, 0, 1)
            d_bbi = _block_diag_extract(d_bdi, S5_STATE, S5_CH, transpose=True).transpose(2, 0, 1)
            d_lr, d_li, d_ls, d_br_t, d_bi_t = _s5_params_bwd(
                s5_lam_re[j], s5_lam_im[j], sv["ls"], sv["b_re_t"], sv["b_im_t"],
                d_lam[0].reshape(G, P), d_lam[1].reshape(G, P), d_bbr, d_bbi)
            grads["s5_lam_re"][j], grads["s5_lam_im"][j], grads["s5_log_step"][j] = d_lr, d_li, d_ls.reshape(G)
            grads["s5_b_re"][j], grads["s5_b_im"][j] = d_br_t.transpose(1, 2, 0), d_bi_t.transpose(1, 2, 0)
            grads["s5_w_in"][j] = _mm(sv["hn"], d_uu, "tn", out_dtypes=(BF16,), name="s5_in_dw")[0]
            d_hn = _mm(d_uu, full["s5_w_in"][j], "nt", name="s5_in_dx")[0]
        dh, grads["norm_mix"][i] = _rms_bwd(d_hn, sv["h0"], norm_mix[i], dh)

    out = {}
    pieces, meta = [], []
    for name, axis in SHARDED.items():
        for l, g in enumerate(grads[name]):
            g2 = g.reshape(1, -1) if g.ndim == 1 else g
            parts = _split_eight(g2, axis)
            meta.append((name, l, parts.shape[1]))
            pieces.append(parts)
    packed_g, _ = _pack(pieces, BF16, lead=(N_DEV,))
    recv = _all_to_all(packed_g, "exchange_grads")
    flat = lambda d: [s.reshape(-1) for name in SHARDED for s in _layer_shards(d[name])]
    pw, _ = _pack(flat(w), F32)
    pm, _ = _pack(flat(mom), F32)
    pvv, _ = _pack(flat(vel), F32)
    res = [r.reshape(-1) for r in _adamw(recv, pw, pm, pvv, "adamw_sharded")]
    off = 0
    per = {name: [[], [], [], []] for name in SHARDED}
    for name, l, n in meta:
        for k in range(4):
            per[name][k].append(res[k][off:off + n])
        off += n
    for name in SHARDED:
        out[name] = [jnp.stack(per[name][k]).reshape(w[name].shape) for k in range(4)]

    rep_g = {n: (d_norm_final[0] if n == "norm_final" else jnp.stack([g.reshape(w[n].shape[1:]) for g in grads[n]])) for n in REPLICATED}
    flat_r = lambda d: [d[n].reshape(-1) for n in REPLICATED]
    pg, _ = _pack(flat_r(rep_g), F32)
    parts_r = _all_gather(pg, "gather_small_grads")
    pw, _ = _pack(flat_r(w), F32)
    pm, _ = _pack(flat_r(mom), F32)
    pvv, _ = _pack(flat_r(vel), F32)
    res = [r.reshape(-1) for r in _adamw(parts_r, pw, pm, pvv, "adamw_replicated")]
    off = 0
    for name in REPLICATED:
        n = w[name].size
        out[name] = [res[k][off:off + n].reshape(w[name].shape) for k in range(4)]
        off += n

    grad_x = dh[None]
    return (loss, grad_x, *[out[n][0] for n in WEIGHTS], *[out[n][1] for n in WEIGHTS],
            *[out[n][2] for n in WEIGHTS], *[out[n][3] for n in WEIGHTS])
```

```python
import collections
import math

import jax
import jax.numpy as jnp
from jax import lax
from jax.experimental import pallas as pl
from jax.experimental.pallas import tpu as pltpu

F32, BF16 = jnp.float32, jnp.bfloat16
NN, NT, TN = ((1,), (0,)), ((1,), (1,)), ((0,), (0,))

N_DEV = 8
MESH_AXES = ("x", "y", "c")
LANES = 128
V7X_VMEM_BYTES = 64 * 1024 * 1024
VMEM_LIMIT = V7X_VMEM_BYTES - 8 * 1024 * 1024
CHUNK = 64
HEAD = 128
S5_CH = 16
S5_STATE = 64
S5_GPB = LANES // S5_CH
S5_SPB = S5_GPB * S5_STATE
NORM_EPS = 1e-6
L2_EPS = 1e-6
ADAM_LR, ADAM_B1, ADAM_B2, ADAM_EPS, ADAM_WD, ADAM_STEP = 0.001, 0.9, 0.999, 1e-08, 0.01, 10
PACK_W = 1024
PACK_ROWS = 256


def _dot(a, b, dims):
    return lax.dot_general(a, b, (dims, ((), ())), preferred_element_type=F32)


def _bdot(a, b, dims=NN):
    return _dot(a.astype(BF16), b.astype(BF16), dims)


def _call(body, grid, ins, outs, scratch=(), name=None, sem=None):
    res = pl.pallas_call(
        body,
        grid=grid,
        in_specs=[pl.BlockSpec(b, m) for _, b, m in ins],
        out_specs=[pl.BlockSpec(b, m) for _, _, b, m in outs],
        out_shape=[jax.ShapeDtypeStruct(s, d) for s, d, _, _ in outs],
        scratch_shapes=list(scratch),
        name=name,
        compiler_params=pltpu.CompilerParams(
            dimension_semantics=sem or ("arbitrary",) * len(grid), vmem_limit_bytes=VMEM_LIMIT),
    )(*[a for a, _, _ in ins])
    return res


def _tile(n, want):
    t = min(n, want)
    assert n % t == 0, (n, want)
    return t


def _accumulate(ref, val, first):
    @pl.when(first)
    def _():
        ref[...] = jnp.zeros_like(ref)
    ref[...] += val


class Sharded(collections.namedtuple("Sharded", "arr axis row0 rows")):
    @property
    def shape(self):
        c = self.arr.shape[2]
        return (self.rows, N_DEV * c) if self.axis == 1 else (N_DEV * self.rows, c)

    @property
    def units(self):
        return (math.gcd(self.rows, self.row0), self.arr.shape[2])


def _mm(a, b, mode="nn", out_dtypes=(F32,), epi=None, extras=(), name="mm", out_axis=None, tm=1024, tn=1024, tk=512):
    sh = isinstance(b, Sharded)
    b_rows, b_cols = b.shape
    u_rows, u_cols = b.units if sh else b.shape
    if mode == "nn":
        (M, K), (K2, N), (uk, un) = a.shape, (b_rows, b_cols), (u_rows, u_cols)
    elif mode == "nt":
        (M, K), (N, K2), (un, uk) = a.shape, (b_rows, b_cols), (u_rows, u_cols)
    else:
        (K, M), (K2, N), (uk, un) = a.shape, (b_rows, b_cols), (u_rows, u_cols)
    assert K == K2, (a.shape, b.shape, mode)
    um = M
    if out_axis == 0:
        um = M // N_DEV
    elif out_axis == 1:
        un = N // N_DEV
    tm, tn, tk = _tile(um, tm), _tile(un, tn), _tile(uk, tk)
    nk = K // tk
    a_spec = ((tk, tm), lambda i, j, k: (k, i)) if mode == "tn" else ((tm, tk), lambda i, j, k: (i, k))
    if not sh:
        b_arr = b
        b_spec = ((tn, tk), lambda i, j, k: (j, k)) if mode == "nt" else ((tk, tn), lambda i, j, k: (k, j))
    else:
        b_arr = b.arr
        tr_, tc_ = (tk, tn) if mode == "nn" else (tn, tk)
        r0, per_r, per_c = b.row0 // tr_, b.rows // tr_, b.arr.shape[2] // tc_
        assert b.row0 % tr_ == 0 and mode != "tn"
        if b.axis == 1:
            place = lambda r, c: (c // per_c, r0 + r, c % per_c)
        else:
            place = lambda r, c: (r // per_r, r0 + r % per_r, c)
        b_spec = ((None, tr_, tc_), (lambda i, j, k: place(k, j)) if mode == "nn" else (lambda i, j, k: place(j, k)))
    dims = {"nn": NN, "nt": NT, "tn": TN}[mode]
    n_ex, n_out = len(extras), len(out_dtypes)

    def body(*refs):
        a_ref, b_ref = refs[:2]
        ex = refs[2:2 + n_ex]
        outs = refs[2 + n_ex:2 + n_ex + n_out]
        acc = refs[-1]
        k = pl.program_id(2)

        @pl.when(k == 0)
        def _():
            acc[...] = jnp.zeros_like(acc)

        acc[...] += _bdot(a_ref[...], b_ref[...], dims)

        @pl.when(k == nk - 1)
        def _():
            res = acc[...]
            vals = epi(res, *[e[...] for e in ex]) if epi is not None else (res,)
            for r, v in zip(outs, vals):
                r[...] = v.astype(r.dtype)

    tile = lambda i, j, k: (i, j)
    if out_axis is None:
        out_shape, out_block, out_map = (M, N), (tm, tn), tile
    elif out_axis == 0:
        per = um // tm
        out_shape, out_block, out_map = (N_DEV, um, N), (None, tm, tn), lambda i, j, k: (i // per, i % per, j)
    else:
        per = un // tn
        out_shape, out_block, out_map = (N_DEV, M, un), (None, tm, tn), lambda i, j, k: (j // per, i, j % per)
    return _call(
        body, (M // tm, N // tn, nk),
        [(a,) + a_spec, (b_arr,) + b_spec] + [(e, (tm, tn), tile) for e in extras],
        [(out_shape, d, out_block, out_map) for d in out_dtypes],
        scratch=[pltpu.VMEM((tm, tn), F32)], name=name,
        sem=("parallel", "parallel", "arbitrary"))


def _rms_fwd(h, g, tr=256):
    T, D = h.shape
    tr = _tile(T, tr)

    def body(h_ref, g_ref, o_ref):
        x = h_ref[...]
        r = lax.rsqrt(jnp.mean(x * x, axis=-1, keepdims=True) + NORM_EPS)
        o_ref[...] = (x * r * g_ref[...]).astype(BF16)

    row = lambda i: (i, 0)
    fix = lambda i: (0, 0)
    return _call(body, (T // tr,), [(h, (tr, D), row), (g.reshape(1, D), (1, D), fix)],
                 [((T, D), BF16, (tr, D), row)], name="rms_fwd", sem=("parallel",))[0]


def _rms_bwd_math(dy, x, g):
    r = lax.rsqrt(jnp.mean(x * x, axis=-1, keepdims=True) + NORM_EPS)
    xh = x * r
    dxh = dy * g
    dx = r * (dxh - xh * jnp.mean(dxh * xh, axis=-1, keepdims=True))
    dg = jnp.sum(dy * xh, axis=0, keepdims=True)
    return dx, dg


def _rms_bwd(dy, h, g, res, tr=256):
    T, D = h.shape
    tr = _tile(T, tr)

    def body(dy_ref, h_ref, g_ref, res_ref, dh_ref, dg_ref):
        dx, dg = _rms_bwd_math(dy_ref[...], h_ref[...], g_ref[...])
        dh_ref[...] = res_ref[...] + dx
        _accumulate(dg_ref, dg, pl.program_id(0) == 0)

    row = lambda i: (i, 0)
    fix = lambda i: (0, 0)
    return _call(body, (T // tr,),
                 [(dy, (tr, D), row), (h, (tr, D), row), (g.reshape(1, D), (1, D), fix), (res, (tr, D), row)],
                 [((T, D), F32, (tr, D), row), ((1, D), F32, (1, D), fix)], name="rms_bwd")


def _loss_fwd_bwd(h, g, tgt, tr=256):
    T, D = h.shape
    tr = _tile(T, tr)

    def body(h_ref, g_ref, t_ref, dh_ref, dg_ref, loss_ref):
        x, gg = h_ref[...], g_ref[...]
        r = lax.rsqrt(jnp.mean(x * x, axis=-1, keepdims=True) + NORM_EPS)
        diff = x * r * gg - t_ref[...]
        part = 0.5 * jnp.sum(jnp.mean(diff * diff, axis=-1, keepdims=True))
        dx, dg = _rms_bwd_math(diff * (1.0 / D), x, gg)
        dh_ref[...] = dx
        first = pl.program_id(0) == 0
        _accumulate(dg_ref, dg, first)
        _accumulate(loss_ref, jnp.full((1, LANES), part, F32), first)

    row = lambda i: (i, 0)
    fix = lambda i: (0, 0)
    return _call(body, (T // tr,),
                 [(h, (tr, D), row), (g.reshape(1, D), (1, D), fix), (tgt, (tr, D), row)],
                 [((T, D), F32, (tr, D), row), ((1, D), F32, (1, D), fix), ((1, LANES), F32, (1, LANES), fix)],
                 name="loss_fwd_bwd")


def _ple_fwd(h, s, pp, tr=256):
    T, D = h.shape
    tr = _tile(T, tr)

    def body(h_ref, s_ref, p_ref, o_ref):
        o_ref[...] = h_ref[...] + jax.nn.sigmoid(s_ref[...]) * p_ref[...]

    row = lambda i: (i, 0)
    return _call(body, (T // tr,), [(a, (tr, D), row) for a in (h, s, pp)],
                 [((T, D), F32, (tr, D), row)], name="ple_fwd", sem=("parallel",))[0]


def _ple_bwd(dh, s, pp, tr=256):
    T, D = dh.shape
    tr = _tile(T, tr)

    def body(dh_ref, s_ref, p_ref, ds_ref, dp_ref):
        d = dh_ref[...]
        gate = jax.nn.sigmoid(s_ref[...])
        ds_ref[...] = (d * p_ref[...] * gate * (1.0 - gate)).astype(BF16)
        dp_ref[...] = (d * gate).astype(BF16)

    row = lambda i: (i, 0)
    return _call(body, (T // tr,), [(a, (tr, D), row) for a in (dh, s, pp)],
                 [((T, D), BF16, (tr, D), row)] * 2, name="ple_bwd", sem=("parallel",))


def _conv_taps(xe, w, tr):
    c = w[3:4, :] * xe[8:, :]
    for j in range(3):
        c = c + w[j:j + 1, :] * pltpu.roll(xe, 3 - j, 0)[8:, :]
    return c


def _gdn_pre_fwd(pq, conv_w, hk, tr=1024):
    T, CD = pq.shape
    tr = _tile(T, tr)
    r8 = tr // 8

    def body(x_ref, halo_ref, w_ref, o_ref):
        j, r = pl.program_id(0), pl.program_id(1)
        halo = jnp.where(r > 0, halo_ref[...], 0.0)
        xe = jnp.concatenate([halo, x_ref[...]], axis=0)
        c = _conv_taps(xe, w_ref[...], tr)
        s = c * jax.nn.sigmoid(c)
        rn = lax.rsqrt(jnp.sum(s * s, axis=-1, keepdims=True) + L2_EPS)
        scale = jnp.where(j < hk, HEAD ** -0.5, 1.0)
        o_ref[...] = jnp.where(j < 2 * hk, s * rn * scale, s)

    tile = lambda j, r: (r, j)
    return _call(body, (CD // HEAD, T // tr),
                 [(pq, (tr, HEAD), tile), (pq, (8, HEAD), lambda j, r: (jnp.maximum(r * r8 - 1, 0), j)),
                  (conv_w, (4, HEAD), lambda j, r: (0, j))],
                 [((T, CD), F32, (tr, HEAD), tile)], name="gdn_pre_fwd", sem=("parallel", "parallel"))[0]


def _gdn_pre_bwd(dn, pq, conv_w, hk, tr=1024):
    T, CD = pq.shape
    tr = _tile(T, tr)
    r8 = tr // 8

    def body(dn_ref, x_ref, halo_ref, w_ref, dc_ref, dw_ref):
        j, r = pl.program_id(0), pl.program_id(1)
        halo = jnp.where(r > 0, halo_ref[...], 0.0)
        xe = jnp.concatenate([halo, x_ref[...]], axis=0)
        c = _conv_taps(xe, w_ref[...], tr)
        sig = jax.nn.sigmoid(c)
        s = c * sig
        rn = lax.rsqrt(jnp.sum(s * s, axis=-1, keepdims=True) + L2_EPS)
        scale = jnp.where(j < hk, HEAD ** -0.5, 1.0)
        d = dn_ref[...]
        y = s * rn
        dy = d * scale
        ds = jnp.where(j < 2 * hk, rn * (dy - y * jnp.sum(dy * y, axis=-1, keepdims=True)), d)
        dc = ds * sig * (1.0 + c * (1.0 - sig))
        dc_ref[...] = dc

        @pl.when(r == 0)
        def _():
            dw_ref[...] = jnp.zeros_like(dw_ref)

        for t in range(4):
            xs = xe[8:, :] if t == 3 else pltpu.roll(xe, 3 - t, 0)[8:, :]
            dw_ref[t:t + 1, :] += jnp.sum(dc * xs, axis=0, keepdims=True)

    tile = lambda j, r: (r, j)
    col = lambda j, r: (0, j)
    return _call(body, (CD // HEAD, T // tr),
                 [(dn, (tr, HEAD), tile), (pq, (tr, HEAD), tile),
                  (pq, (8, HEAD), lambda j, r: (jnp.maximum(r * r8 - 1, 0), j)), (conv_w, (4, HEAD), col)],
                 [((T, CD), F32, (tr, HEAD), tile), ((4, CD), F32, (4, HEAD), col)], name="gdn_pre_bwd")


def _gdn_conv_bwd(dc, conv_w, tr=1024):
    T, CD = dc.shape
    tr = _tile(T, tr)
    r8 = tr // 8
    n_r = T // tr

    def body(dc_ref, halo_ref, w_ref, dx_ref):
        r = pl.program_id(1)
        halo = jnp.where(r < n_r - 1, halo_ref[...], 0.0)
        de = jnp.concatenate([dc_ref[...], halo], axis=0)
        w = w_ref[...]
        dx = w[3:4, :] * de[:tr, :]
        for j in range(3):
            dx = dx + w[j:j + 1, :] * pltpu.roll(de, tr + 8 - (3 - j), 0)[:tr, :]
        dx_ref[...] = dx

    tile = lambda j, r: (r, j)
    return _call(body, (CD // HEAD, n_r),
                 [(dc, (tr, HEAD), tile), (dc, (8, HEAD), lambda j, r: (jnp.minimum((r + 1) * r8, T // 8 - 1), j)),
                  (conv_w, (4, HEAD), lambda j, r: (0, j))],
                 [((T, CD), F32, (tr, HEAD), tile)], name="gdn_conv_bwd", sem=("parallel", "parallel"))[0]


def _gates_fwd(ba, pv, hv, tr=1024):
    T = ba.shape[0]
    tr = _tile(T, tr)

    def body(x_ref, pv_ref, o_ref):
        x = x_ref[...]
        lane = lax.broadcasted_iota(jnp.int32, x.shape, 1)
        g = -jnp.exp(pv_ref[0:1, :]) * jax.nn.softplus(x + pv_ref[1:2, :])
        o_ref[...] = jnp.where(lane < hv, jax.nn.sigmoid(x), jnp.where(lane < 2 * hv, g, 0.0))

    row = lambda i: (i, 0)
    return _call(body, (T // tr,), [(ba, (tr, LANES), row), (pv, (2, LANES), lambda i: (0, 0))],
                 [((T, LANES), F32, (tr, LANES), row)], name="gates_fwd", sem=("parallel",))[0]


def _gates_bwd(dg2, ba, pv, hv, tr=1024):
    T = ba.shape[0]
    tr = _tile(T, tr)

    def body(d_ref, x_ref, pv_ref, dx_ref, dpv_ref):
        x, d = x_ref[...], d_ref[...]
        lane = lax.broadcasted_iota(jnp.int32, x.shape, 1)
        is_a = (lane >= hv) & (lane < 2 * hv)
        beta = jax.nn.sigmoid(x)
        neg_a = -jnp.exp(pv_ref[0:1, :])
        z = x + pv_ref[1:2, :]
        da = d * neg_a * jax.nn.sigmoid(z)
        dx_ref[...] = jnp.where(lane < hv, d * beta * (1.0 - beta), jnp.where(is_a, da, 0.0))
        first = pl.program_id(0) == 0

        @pl.when(first)
        def _():
            dpv_ref[...] = jnp.zeros_like(dpv_ref)

        dpv_ref[0:1, :] += jnp.sum(jnp.where(is_a, d * neg_a * jax.nn.softplus(z), 0.0), axis=0, keepdims=True)
        dpv_ref[1:2, :] += jnp.sum(jnp.where(is_a, da, 0.0), axis=0, keepdims=True)

    row = lambda i: (i, 0)
    fix = lambda i: (0, 0)
    return _call(body, (T // tr,), [(dg2, (tr, LANES), row), (ba, (tr, LANES), row), (pv, (2, LANES), fix)],
                 [((T, LANES), F32, (tr, LANES), row), ((2, LANES), F32, (2, LANES), fix)], name="gates_bwd")


def _ogate_fwd(o, z, o_norm, tr=512):
    T, VD = o.shape
    tr = _tile(T, tr)

    def body(o_ref, z_ref, g_ref, y_ref):
        x, zz = o_ref[...], z_ref[...]
        r = lax.rsqrt(jnp.mean(x * x, axis=-1, keepdims=True) + NORM_EPS)
        y_ref[...] = (x * r * g_ref[...] * (zz * jax.nn.sigmoid(zz))).astype(BF16)

    tile = lambda h, r: (r, h)
    return _call(body, (VD // HEAD, T // tr),
                 [(o, (tr, HEAD), tile), (z, (tr, HEAD), tile), (o_norm.reshape(1, HEAD), (1, HEAD), lambda h, r: (0, 0))],
                 [((T, VD), BF16, (tr, HEAD), tile)], name="ogate_fwd", sem=("parallel", "parallel"))[0]


def _ogate_bwd(dy, o, z, o_norm, tr=512):
    T, VD = o.shape
    tr = _tile(T, tr)

    def body(dy_ref, o_ref, z_ref, g_ref, do_ref, dz_ref, dg_ref):
        d, x, zz, g = dy_ref[...], o_ref[...], z_ref[...], g_ref[...]
        sig = jax.nn.sigmoid(zz)
        silu = zz * sig
        dx, dg = _rms_bwd_math(d * silu, x, g)
        r = lax.rsqrt(jnp.mean(x * x, axis=-1, keepdims=True) + NORM_EPS)
        do_ref[...] = dx
        dz_ref[...] = d * (x * r * g) * sig * (1.0 + zz * (1.0 - sig))
        _accumulate(dg_ref, dg, (pl.program_id(0) == 0) & (pl.program_id(1) == 0))

    tile = lambda h, r: (r, h)
    fix = lambda h, r: (0, 0)
    return _call(body, (VD // HEAD, T // tr),
                 [(dy, (tr, HEAD), tile), (o, (tr, HEAD), tile), (z, (tr, HEAD), tile), (o_norm.reshape(1, HEAD), (1, HEAD), fix)],
                 [((T, VD), F32, (tr, HEAD), tile), ((T, VD), F32, (tr, HEAD), tile), ((1, HEAD), F32, (1, HEAD), fix)],
                 name="ogate_bwd")


def _chunk_iota():
    return (lax.broadcasted_iota(jnp.int32, (CHUNK, CHUNK), 0), lax.broadcasted_iota(jnp.int32, (CHUNK, CHUNK), 1))


def _decay(gc):
    ri, ci = _chunk_iota()
    gcol = gc[:, :CHUNK]
    grow = jnp.sum(jnp.where(ri == ci, gcol, 0.0), axis=0, keepdims=True)
    return jnp.where(ri >= ci, jnp.exp(jnp.minimum(gcol - grow, 0.0)), 0.0)


def _rowsum(x):
    return jnp.broadcast_to(jnp.sum(x, axis=1, keepdims=True), (CHUNK, HEAD))


def _split3(x):
    h1 = x.astype(BF16)
    r1 = x - h1.astype(F32)
    h2 = r1.astype(BF16)
    return h1, h2, (r1 - h2.astype(F32)).astype(BF16)


def _sel_dot(sel, x, dims=NN):
    s = sel.astype(BF16)
    h1, h2, h3 = _split3(x)
    if dims == NN:
        return _dot(s, h1, NN) + _dot(s, h2, NN) + _dot(s, h3, NN)
    return _dot(h1, s, dims) + _dot(h2, s, dims) + _dot(h3, s, dims)


def _colsum(e):
    return _sel_dot(jnp.ones((CHUNK, HEAD), F32), e, TN)


def _unit_lower_inverse(m):
    ri, ci = _chunk_iota()
    eye = jnp.where(ri == ci, 1.0, 0.0)
    p = -m
    x = eye + p
    for _ in range(int(math.log2(CHUNK)) - 1):
        p = _bdot(p, p)
        x = x + _bdot(x, p)
    m1, m2, _ = _split3(m)
    x1, x2, _ = _split3(x)
    resid = (eye - x) - (_dot(m1, x1, NN) + _dot(m1, x2, NN) + _dot(m2, x1, NN))
    return x + _bdot(x, resid)


def _gdn_a_fwd(qkv, gb, bb, hk, hv, tr=512):
    T = qkv.shape[0]
    tr = _tile(T, tr)

    def body(k_ref, v_ref, g_ref, b_ref, u_ref, w_ref, gc_ref, ti_ref):
        ri, ci = _chunk_iota()
        ltri = jnp.where(ri >= ci, 1.0, 0.0)
        for c in range(tr // CHUNK):
            rows = pl.ds(c * CHUNK, CHUNK)
            k, v, beta = k_ref[rows, :], v_ref[rows, :], b_ref[rows, :]
            gc = _sel_dot(ltri, g_ref[rows, :])
            kb = k * beta
            m = jnp.where(ri > ci, _bdot(kb, k, NT) * _decay(gc), 0.0)
            tinv = _unit_lower_inverse(m)
            x = _bdot(tinv, jnp.concatenate([v * beta, kb * jnp.exp(gc)], axis=1))
            u_ref[rows, :] = x[:, :HEAD]
            w_ref[rows, :] = x[:, HEAD:]
            gc_ref[rows, :] = gc
            ti_ref[0, rows, :] = tinv

    tile = lambda h, r: (r, h)
    vd = hv * HEAD
    return _call(body, (hv, T // tr),
                 [(qkv, (tr, HEAD), lambda h, r: (r, hk + h // 2)), (qkv, (tr, HEAD), lambda h, r: (r, 2 * hk + h)),
                  (gb, (tr, HEAD), tile), (bb, (tr, HEAD), tile)],
                 [((T, vd), F32, (tr, HEAD), tile)] * 3 + [((hv, T, CHUNK), F32, (1, tr, CHUNK), lambda h, r: (h, r, 0))],
                 name="gdn_a_fwd", sem=("parallel", "parallel"))


def _gdn_b_fwd(qkv, u, w, gc, hk, hv, tr=512):
    T = qkv.shape[0]
    tr = _tile(T, tr)
    cpb = tr // CHUNK

    def body(q_ref, k_ref, u_ref, w_ref, gc_ref, o_ref, vn_ref, sall_ref, s_ref):
        ri, ci = _chunk_iota()

        @pl.when(pl.program_id(1) == 0)
        def _():
            s_ref[...] = jnp.zeros_like(s_ref)

        for c in range(cpb):
            rows = pl.ds(c * CHUNK, CHUNK)
            q, k, gc = q_ref[rows, :], k_ref[rows, :], gc_ref[rows, :]
            gl = gc[CHUNK - 1:CHUNK, :]
            qk = jnp.where(ri >= ci, _bdot(q, k, NT) * _decay(gc), 0.0)
            s = s_ref[...]
            sall_ref[0, c] = s
            vn = u_ref[rows, :] - _bdot(w_ref[rows, :], s)
            o_ref[rows, :] = _bdot(q * jnp.exp(gc), s) + _bdot(qk, vn)
            vn_ref[rows, :] = vn
            s_ref[...] = s * jnp.exp(gl) + _bdot(k * jnp.exp(gl - gc), vn, TN)

    tile = lambda h, r: (r, h)
    vd = hv * HEAD
    return _call(body, (hv, T // tr),
                 [(qkv, (tr, HEAD), lambda h, r: (r, h // 2)), (qkv, (tr, HEAD), lambda h, r: (r, hk + h // 2)),
                  (u, (tr, HEAD), tile), (w, (tr, HEAD), tile), (gc, (tr, HEAD), tile)],
                 [((T, vd), F32, (tr, HEAD), tile)] * 2 +
                 [((hv, T // CHUNK, HEAD, HEAD), F32, (1, cpb, HEAD, HEAD), lambda h, r: (h, r, 0, 0))],
                 scratch=[pltpu.VMEM((HEAD, HEAD), F32)], name="gdn_b_fwd", sem=("parallel", "arbitrary"))


def _gdn_b_bwd(do, qkv, w, gc, vn, sall, hk, hv, tr=512):
    T = qkv.shape[0]
    tr = _tile(T, tr)
    cpb = tr // CHUNK
    n_r = T // tr

    def body(do_ref, q_ref, k_ref, w_ref, gc_ref, vn_ref, sall_ref, dq_ref, dk_ref, dgc_ref, du_ref, dw_ref, ds_ref):
        ri, ci = _chunk_iota()

        @pl.when(pl.program_id(1) == 0)
        def _():
            ds_ref[...] = jnp.zeros_like(ds_ref)

        for c in reversed(range(cpb)):
            rows = pl.ds(c * CHUNK, CHUNK)
            d_o, q, k, w, gc, vn = do_ref[rows, :], q_ref[rows, :], k_ref[rows, :], w_ref[rows, :], gc_ref[rows, :], vn_ref[rows, :]
            s = sall_ref[0, c]
            ds_next = ds_ref[...]
            gl = gc[CHUNK - 1:CHUNK, :]
            egc, ekd, eg = jnp.exp(gc), jnp.exp(gl - gc), jnp.exp(gl)
            qg, kd = q * egc, k * ekd
            dec = _decay(gc)
            qk = jnp.where(ri >= ci, _bdot(q, k, NT) * dec, 0.0)
            d_qg = _bdot(d_o, s, NT)
            d_qk = jnp.where(ri >= ci, _bdot(d_o, vn, NT), 0.0)
            d_vn = _bdot(qk, d_o, TN) + _bdot(kd, ds_next)
            d_kd = _bdot(vn, ds_next, NT)
            d_eg = jnp.sum(s * ds_next)
            ds_ref[...] = ds_next * eg + _bdot(qg, d_o, TN) - _bdot(w, d_vn, TN)
            d_b = d_qk * dec
            e_q = d_qk * qk
            d_gl = jnp.sum(d_kd * kd) + d_eg * eg
            row = lax.broadcasted_iota(jnp.int32, (CHUNK, HEAD), 0)
            dq_ref[rows, :] = d_qg * egc + _bdot(d_b, k)
            dk_ref[rows, :] = d_kd * ekd + _bdot(d_b, q, TN)
            dgc_ref[rows, :] = (_rowsum(d_qg * qg) - _rowsum(d_kd * kd) + _rowsum(e_q) - _colsum(e_q)
                                + jnp.where(row == CHUNK - 1, d_gl, 0.0))
            du_ref[rows, :] = d_vn
            dw_ref[rows, :] = -_bdot(d_vn, s, NT)

    rtile = lambda h, r: (n_r - 1 - r, h)
    vd = hv * HEAD
    return _call(body, (hv, n_r),
                 [(do, (tr, HEAD), rtile), (qkv, (tr, HEAD), lambda h, r: (n_r - 1 - r, h // 2)),
                  (qkv, (tr, HEAD), lambda h, r: (n_r - 1 - r, hk + h // 2)),
                  (w, (tr, HEAD), rtile), (gc, (tr, HEAD), rtile), (vn, (tr, HEAD), rtile),
                  (sall, (1, cpb, HEAD, HEAD), lambda h, r: (h, n_r - 1 - r, 0, 0))],
                 [((T, vd), F32, (tr, HEAD), rtile)] * 5,
                 scratch=[pltpu.VMEM((HEAD, HEAD), F32)], name="gdn_b_bwd", sem=("parallel", "arbitrary"))


def _gdn_a_bwd(du, dw, dgc_b, qkv, bb, gc, tinv, u, w, hk, hv, tr=512):
    T = qkv.shape[0]
    tr = _tile(T, tr)

    def body(du_ref, dw_ref, dgcb_ref, k_ref, v_ref, b_ref, gc_ref, ti_ref, u_ref, w_ref, dk_ref, dv_ref, db_ref, dg_ref):
        ri, ci = _chunk_iota()
        utri = jnp.where(ci >= ri, 1.0, 0.0)
        for c in range(tr // CHUNK):
            rows = pl.ds(c * CHUNK, CHUNK)
            k, v, beta, gc = k_ref[rows, :], v_ref[rows, :], b_ref[rows, :], gc_ref[rows, :]
            egc = jnp.exp(gc)
            kb = k * beta
            dec = _decay(gc)
            m = jnp.where(ri > ci, _bdot(kb, k, NT) * dec, 0.0)
            d_r = _bdot(ti_ref[0, rows, :], jnp.concatenate([du_ref[rows, :], dw_ref[rows, :]], axis=1), TN)
            d_vb, d_kbe = d_r[:, :HEAD], d_r[:, HEAD:]
            x = jnp.concatenate([u_ref[rows, :], w_ref[rows, :]], axis=1)
            d_m = jnp.where(ri > ci, -_bdot(d_r, x, NT), 0.0)
            d_a = d_m * dec
            e_m = d_m * m
            d_kb = _bdot(d_a, k) + d_kbe * egc
            dk_ref[rows, :] = _bdot(d_a, kb, TN) + d_kb * beta
            dv_ref[rows, :] = d_vb * beta
            db_ref[rows, :] = _rowsum(d_vb * v) + _rowsum(d_kb * k)
            d_gc = _rowsum(e_m) - _colsum(e_m) + _rowsum(d_kbe * kb * egc) + dgcb_ref[rows, :]
            dg_ref[rows, :] = _sel_dot(utri, d_gc)

    tile = lambda h, r: (r, h)
    vd = hv * HEAD
    return _call(body, (hv, T // tr),
                 [(du, (tr, HEAD), tile), (dw, (tr, HEAD), tile), (dgc_b, (tr, HEAD), tile),
                  (qkv, (tr, HEAD), lambda h, r: (r, hk + h // 2)), (qkv, (tr, HEAD), lambda h, r: (r, 2 * hk + h)),
                  (bb, (tr, HEAD), tile), (gc, (tr, HEAD), tile), (tinv, (1, tr, CHUNK), lambda h, r: (h, r, 0)),
                  (u, (tr, HEAD), tile), (w, (tr, HEAD), tile)],
                 [((T, vd), F32, (tr, HEAD), tile)] * 4, name="gdn_a_bwd", sem=("parallel", "parallel"))


def _pair_sum(a, b_, tr=512):
    T, vd = a.shape
    tr = _tile(T, tr)
    terms = [a] if b_ is None else [a, b_]
    n = len(terms)

    def body(*refs):
        acc = refs[0][...] + refs[1][...]
        for r in refs[2:2 * n]:
            acc = acc + r[...]
        refs[-1][...] = acc

    even = lambda j, r: (r, 2 * j)
    odd = lambda j, r: (r, 2 * j + 1)
    ins = [(t, (tr, HEAD), m) for t in terms for m in (even, odd)]
    return _call(body, (vd // HEAD // 2, T // tr), ins,
                 [((T, vd // 2), F32, (tr, HEAD), lambda j, r: (r, j))], name="gdn_pair_sum", sem=("parallel", "parallel"))[0]


def _s5_param_math(lr, li, ls, br, bi):
    step = jnp.exp(ls)
    zr, zi = lr * step, li * step
    mag = jnp.exp(zr)
    ar, ai = mag * jnp.cos(zi), mag * jnp.sin(zi)
    den = lr * lr + li * li
    nr, ni = ar - 1.0, ai
    cr, cim = (nr * lr + ni * li) / den, (ni * lr - nr * li) / den
    return ar, ai, br * cr - bi * cim, br * cim + bi * cr


def _s5_params_fwd(lr, li, ls, br, bi):
    G, P = lr.shape

    def body(lr_ref, li_ref, ls_ref, br_ref, bi_ref, ar_ref, ai_ref, bbr_ref, bbi_ref):
        ar, ai, bbr, bbi = _s5_param_math(lr_ref[...], li_ref[...], ls_ref[...], br_ref[...], bi_ref[...])
        ar_ref[...], ai_ref[...], bbr_ref[...], bbi_ref[...] = ar, ai, bbr, bbi

    shapes = [(G, P), (G, P), (G, 1), (S5_CH, G, P), (S5_CH, G, P)]
    z = lambda n: (lambda: (0,) * n)
    return _call(body, (), [(a, s, z(len(s))) for a, s in zip((lr, li, ls, br, bi), shapes)],
                 [(s, F32, s, z(len(s))) for s in (shapes[0], shapes[0], shapes[3], shapes[3])],
                 name="s5_params_fwd", sem=())


def _s5_params_bwd(lr, li, ls, br, bi, dar, dai, dbbr, dbbi):
    G, P = lr.shape

    def body(lr_ref, li_ref, ls_ref, br_ref, bi_ref, dar_ref, dai_ref, dbr_ref, dbi_ref, o0, o1, o2, o3, o4):
        _, vjp = jax.vjp(_s5_param_math, lr_ref[...], li_ref[...], ls_ref[...], br_ref[...], bi_ref[...])
        outs = vjp((dar_ref[...], dai_ref[...], dbr_ref[...], dbi_ref[...]))
        for r, v in zip((o0, o1, o2, o3, o4), outs):
            r[...] = v

    shapes = [(G, P), (G, P), (G, 1), (S5_CH, G, P), (S5_CH, G, P)]
    z = lambda n: (lambda: (0,) * n)
    ins = list(zip((lr, li, ls, br, bi), shapes)) + list(zip((dar, dai, dbbr, dbbi), (shapes[0], shapes[0], shapes[3], shapes[3])))
    return _call(body, (), [(a, s, z(len(s))) for a, s in ins], [(s, F32, s, z(len(s))) for s in shapes],
                 name="s5_params_bwd", sem=())


def _s5_bproj_fwd(u, bd_re, bd_im, tr=512):
    T, D = u.shape
    tr = _tile(T, tr)
    nb = D // LANES

    def body(u_ref, br_ref, bi_ref, or_ref, oi_ref):
        ub = u_ref[...]
        or_ref[...] = _bdot(ub, br_ref[0])
        oi_ref[...] = _bdot(ub, bi_ref[0])

    blk = lambda i, j: (j, 0, 0)
    return _call(body, (T // tr, nb),
                 [(u, (tr, LANES), lambda i, j: (i, j)), (bd_re, (1, LANES, S5_SPB), blk), (bd_im, (1, LANES, S5_SPB), blk)],
                 [((T, nb * S5_SPB), F32, (tr, S5_SPB), lambda i, j: (i, j))] * 2, name="s5_bproj_fwd", sem=("parallel", "parallel"))


def _s5_scan(br, bi, lam, reverse, xr=None, xi=None, tl=512, bw=256):
    T, NCH = br.shape
    tl, bw = _tile(T, tl), _tile(NCH, bw)
    n_t = T // tl
    l8 = tl // 8

    def body(*refs):
        if reverse:
            br_ref, bi_ref, lam_ref, sr_ref, si_ref, hr_ref, hi_ref, or_ref, oi_ref, dl_ref, pr, pi_, cr, ci_ = refs
        else:
            br_ref, bi_ref, lam_ref, or_ref, oi_ref, pr, pi_, cr, ci_ = refs
        t = pl.program_id(1)
        lr = lam_ref[0:1, :]
        li = -lam_ref[1:2, :] if reverse else lam_ref[1:2, :]
        row = lax.broadcasted_iota(jnp.int32, (tl, bw), 0)

        def scan(x_r, x_i):
            a_r, a_i = lr, li
            s = 1
            while s < tl:
                keep = (row < tl - s) if reverse else (row >= s)
                shift = tl - s if reverse else s
                s_r = jnp.where(keep, pltpu.roll(x_r, shift, 0), 0.0)
                s_i = jnp.where(keep, pltpu.roll(x_i, shift, 0), 0.0)
                x_r, x_i = x_r + a_r * s_r - a_i * s_i, x_i + a_r * s_i + a_i * s_r
                a_r, a_i = a_r * a_r - a_i * a_i, 2.0 * a_r * a_i
                s *= 2
            return x_r, x_i

        edge = tl - 1 if reverse else 0

        @pl.when(t == 0)
        def _():
            p_r, p_i = scan(jnp.where(row == edge, lr, 0.0), jnp.where(row == edge, li, 0.0))
            pr[...], pi_[...] = p_r, p_i
            cr[...] = jnp.zeros_like(cr)
            ci_[...] = jnp.zeros_like(ci_)

        x_r, x_i = scan(br_ref[...], bi_ref[...])
        c_r, c_i = cr[0:1, :], ci_[0:1, :]
        p_r, p_i = pr[...], pi_[...]
        x_r, x_i = x_r + p_r * c_r - p_i * c_i, x_i + p_r * c_i + p_i * c_r
        or_ref[...], oi_ref[...] = x_r, x_i
        last = tl - 1 - edge
        cr[0:1, :] = x_r[last:last + 1, :]
        ci_[0:1, :] = x_i[last:last + 1, :]
        if reverse:
            first_block = t == n_t - 1
            h_r = jnp.where(first_block, 0.0, hr_ref[7:8, :])
            h_i = jnp.where(first_block, 0.0, hi_ref[7:8, :])
            s_r = jnp.where(row == 0, h_r, pltpu.roll(sr_ref[...], 1, 0))
            s_i = jnp.where(row == 0, h_i, pltpu.roll(si_ref[...], 1, 0))

            @pl.when(t == 0)
            def _():
                dl_ref[...] = jnp.zeros_like(dl_ref)

            dl_ref[0:1, :] += jnp.sum(s_r * x_r + s_i * x_i, axis=0, keepdims=True)
            dl_ref[1:2, :] += jnp.sum(s_r * x_i - s_i * x_r, axis=0, keepdims=True)

    tmap = (lambda c, t: (n_t - 1 - t, c)) if reverse else (lambda c, t: (t, c))
    col = lambda c, t: (0, c)
    ins = [(br, (tl, bw), tmap), (bi, (tl, bw), tmap), (lam, (2, bw), col)]
    outs = [((T, NCH), F32, (tl, bw), tmap)] * 2
    if reverse:
        halo = lambda c, t: (jnp.maximum((n_t - 1 - t) * l8 - 1, 0), c)
        ins += [(xr, (tl, bw), tmap), (xi, (tl, bw), tmap), (xr, (8, bw), halo), (xi, (8, bw), halo)]
        outs += [((2, NCH), F32, (2, bw), col)]
    return _call(body, (NCH // bw, n_t), ins, outs,
                 scratch=[pltpu.VMEM((tl, bw), F32), pltpu.VMEM((tl, bw), F32), pltpu.VMEM((8, bw), F32), pltpu.VMEM((8, bw), F32)],
                 name="s5_scan_bwd" if reverse else "s5_scan_fwd", sem=("parallel", "arbitrary"))


def _s5_cproj_fwd(xr, xi, cd_re, cd_im, u, d, tr=512):
    T, D = u.shape
    tr = _tile(T, tr)

    def body(xr_ref, xi_ref, cr_ref, ci_ref, u_ref, d_ref, y_ref, h_ref):
        y = _bdot(xr_ref[...], cr_ref[0]) + _bdot(xi_ref[...], ci_ref[0]) + d_ref[...] * u_ref[...]
        y_ref[...] = y
        h_ref[...] = jax.nn.gelu(y).astype(BF16)

    tile = lambda i, j: (i, j)
    blk = lambda i, j: (j, 0, 0)
    return _call(body, (T // tr, D // LANES),
                 [(xr, (tr, S5_SPB), tile), (xi, (tr, S5_SPB), tile), (cd_re, (1, S5_SPB, LANES), blk), (cd_im, (1, S5_SPB, LANES), blk),
                  (u, (tr, LANES), tile), (d.reshape(1, D), (1, LANES), lambda i, j: (0, j))],
                 [((T, D), F32, (tr, LANES), tile), ((T, D), BF16, (tr, LANES), tile)], name="s5_cproj_fwd", sem=("parallel", "parallel"))


def _s5_cproj_bwd(dy, xr, xi, cd_re, cd_im, u, d, tr=512):
    T, D = u.shape
    tr = _tile(T, tr)
    nb = D // LANES

    def body(dy_ref, xr_ref, xi_ref, cr_ref, ci_ref, u_ref, d_ref, dxr_ref, dxi_ref, du_ref, dd_ref, dcr_ref, dci_ref):
        g = dy_ref[...]
        dxr_ref[...] = _bdot(g, cr_ref[0], NT)
        dxi_ref[...] = _bdot(g, ci_ref[0], NT)
        du_ref[...] = g * d_ref[...]
        first = pl.program_id(1) == 0
        _accumulate(dd_ref, jnp.sum(g * u_ref[...], axis=0, keepdims=True), first)

        @pl.when(first)
        def _():
            dcr_ref[...] = jnp.zeros_like(dcr_ref)
            dci_ref[...] = jnp.zeros_like(dci_ref)

        dcr_ref[0] += _bdot(xr_ref[...], g, TN)
        dci_ref[0] += _bdot(xi_ref[...], g, TN)

    tile = lambda j, i: (i, j)
    blk = lambda j, i: (j, 0, 0)
    col = lambda j, i: (0, j)
    return _call(body, (nb, T // tr),
                 [(dy, (tr, LANES), tile), (xr, (tr, S5_SPB), tile), (xi, (tr, S5_SPB), tile),
                  (cd_re, (1, S5_SPB, LANES), blk), (cd_im, (1, S5_SPB, LANES), blk), (u, (tr, LANES), tile), (d.reshape(1, D), (1, LANES), col)],
                 [((T, nb * S5_SPB), F32, (tr, S5_SPB), tile)] * 2 + [((T, D), F32, (tr, LANES), tile), ((1, D), F32, (1, LANES), col)]
                 + [((nb, S5_SPB, LANES), F32, (1, S5_SPB, LANES), blk)] * 2, name="s5_cproj_bwd")


def _s5_bproj_bwd(dbr, dbi, bd_re, bd_im, u, du_skip, tr=512):
    T, D = u.shape
    tr = _tile(T, tr)
    nb = D // LANES

    def body(gr_ref, gi_ref, br_ref, bi_ref, u_ref, ds_ref, du_ref, dbr_ref, dbi_ref):
        g_r, g_i, ub = gr_ref[...], gi_ref[...], u_ref[...]
        du_ref[...] = ds_ref[...] + _bdot(g_r, br_ref[0], NT) + _bdot(g_i, bi_ref[0], NT)

        @pl.when(pl.program_id(1) == 0)
        def _():
            dbr_ref[...] = jnp.zeros_like(dbr_ref)
            dbi_ref[...] = jnp.zeros_like(dbi_ref)

        dbr_ref[0] += _bdot(ub, g_r, TN)
        dbi_ref[0] += _bdot(ub, g_i, TN)

    tile = lambda j, i: (i, j)
    blk = lambda j, i: (j, 0, 0)
    return _call(body, (nb, T // tr),
                 [(dbr, (tr, S5_SPB), tile), (dbi, (tr, S5_SPB), tile), (bd_re, (1, LANES, S5_SPB), blk), (bd_im, (1, LANES, S5_SPB), blk),
                  (u, (tr, LANES), tile), (du_skip, (tr, LANES), tile)],
                 [((T, D), F32, (tr, LANES), tile)] + [((nb, LANES, S5_SPB), F32, (1, LANES, S5_SPB), blk)] * 2, name="s5_bproj_bwd")


def _s5_gate_fwd(h, vg, tr=256):
    T, D = h.shape
    tr = _tile(T, tr)

    def body(h_ref, a_ref, b_ref, o_ref):
        o_ref[...] = h_ref[...] + a_ref[...] * jax.nn.sigmoid(b_ref[...])

    row = lambda i: (i, 0)
    return _call(body, (T // tr,), [(h, (tr, D), row), (vg, (tr, D), row), (vg, (tr, D), lambda i: (i, 1))],
                 [((T, D), F32, (tr, D), row)], name="s5_gate_fwd", sem=("parallel",))[0]


def _s5_gate_bwd(dh, vg, tr=256):
    T, D = dh.shape
    tr = _tile(T, tr)

    def body(d_ref, a_ref, b_ref, o_ref):
        d = d_ref[...]
        sig = jax.nn.sigmoid(b_ref[...])
        o_ref[:, :D] = (d * sig).astype(BF16)
        o_ref[:, D:] = (d * a_ref[...] * sig * (1.0 - sig)).astype(BF16)

    row = lambda i: (i, 0)
    return _call(body, (T // tr,), [(dh, (tr, D), row), (vg, (tr, D), row), (vg, (tr, D), lambda i: (i, 1))],
                 [((T, 2 * D), BF16, (tr, 2 * D), row)], name="s5_gate_bwd", sem=("parallel",))[0]


def _block_diag(w, transpose):
    g, a, b = w.shape
    if transpose:
        w = w.transpose(0, 2, 1)
        a, b = b, a
    eye = jnp.eye(S5_GPB, dtype=w.dtype)
    return jnp.einsum("jgab,gh->jgahb", w.reshape(g // S5_GPB, S5_GPB, a, b), eye).reshape(g // S5_GPB, S5_GPB * a, S5_GPB * b)


def _block_diag_extract(wd, a, b, transpose):
    if transpose:
        a, b = b, a
    nb = wd.shape[0]
    eye = jnp.eye(S5_GPB, dtype=wd.dtype)
    w = jnp.einsum("jgahb,gh->jgab", wd.reshape(nb, S5_GPB, a, S5_GPB, b), eye).reshape(nb * S5_GPB, a, b)
    return w.transpose(0, 2, 1) if transpose else w


def _mesh_position():
    return lax.axis_index("x"), lax.axis_index("y"), lax.axis_index("c")


def _my_index():
    x, y, c = _mesh_position()
    return 4 * x + 2 * y + c


def _hbm_call(body, arrays, out_shapes, n_sems, name):
    n = len(arrays)
    return pl.pallas_call(
        body, out_shape=[jax.ShapeDtypeStruct(s, d) for s, d in out_shapes],
        in_specs=[pl.BlockSpec(memory_space=pl.ANY)] * n, out_specs=[pl.BlockSpec(memory_space=pl.ANY)] * len(out_shapes),
        scratch_shapes=[pltpu.SemaphoreType.DMA((n_sems,)), pltpu.SemaphoreType.DMA((n_sems,)), pltpu.SemaphoreType.DMA((n,))],
        name=name)(*arrays)


def _all_gather(blocks, name):
    n = len(blocks)
    per = N_DEV - 1

    def body(*refs):
        x_refs, out_refs = refs[:n], refs[n:2 * n]
        send_sems, recv_sems, local_sems = refs[2 * n:]
        x, y, c = _mesh_position()
        me, sibling = (x, y, c), (x, y, 1 - c)
        chips = [(1 - x, y), (x, 1 - y), (1 - x, 1 - y)]

        def copy(a, k, blk, to, src=None):
            slot = out_refs[a].at[4 * blk[0] + 2 * blk[1] + blk[2]]
            return pltpu.make_async_remote_copy(
                src_ref=slot if src is None else src, dst_ref=slot, send_sem=send_sems.at[a * per + k],
                recv_sem=recv_sems.at[a * per + k], device_id=to, device_id_type=pl.DeviceIdType.MESH)

        mine = [pltpu.make_async_copy(x_refs[a], out_refs[a].at[4 * x + 2 * y + c], local_sems.at[a]) for a in range(n)]
        for cp in mine:
            cp.start()
        first = []
        for a in range(n):
            first.append(copy(a, 0, me, sibling, src=x_refs[a]))
            first += [copy(a, 1 + j, me, (*chip, c), src=x_refs[a]) for j, chip in enumerate(chips)]
        for cp in first:
            cp.start()
        passed = []
        for j, chip in enumerate(chips):
            for a in range(n):
                copy(a, 1 + j, (*chip, c), me).wait_recv()
                passed.append(copy(a, 4 + j, (*chip, c), sibling))
                passed[-1].start()
        for a in range(n):
            copy(a, 0, sibling, me).wait_recv()
            for j, chip in enumerate(chips):
                copy(a, 4 + j, (*chip, 1 - c), me).wait_recv()
        for cp in first + passed:
            cp.wait_send()
        for cp in mine:
            cp.wait()

    return _hbm_call(body, blocks, [((N_DEV,) + b.shape, b.dtype) for b in blocks], n * per, name)


def _all_to_all(parts, name):
    n = len(parts)
    per = N_DEV - 1

    def body(*refs):
        g_refs, out_refs = refs[:n], refs[n:2 * n]
        send_sems, recv_sems, local_sems = refs[2 * n:]
        x, y, c = _mesh_position()
        me = 4 * x + 2 * y + c
        mine = [pltpu.make_async_copy(g_refs[a].at[me], out_refs[a].at[me], local_sems.at[a]) for a in range(n)]
        for cp in mine:
            cp.start()
        copies = []
        for k in range(1, N_DEV):
            px = 1 - x if k & 4 else x
            py = 1 - y if k & 2 else y
            pc = 1 - c if k & 1 else c
            for a in range(n):
                copies.append(pltpu.make_async_remote_copy(
                    src_ref=g_refs[a].at[4 * px + 2 * py + pc], dst_ref=out_refs[a].at[me],
                    send_sem=send_sems.at[a * per + k - 1], recv_sem=recv_sems.at[a * per + k - 1],
                    device_id=(px, py, pc), device_id_type=pl.DeviceIdType.MESH))
        for cp in copies:
            cp.start()
        for cp in copies:
            cp.wait()
        for cp in mine:
            cp.wait()

    return _hbm_call(body, parts, [(p_.shape, p_.dtype) for p_ in parts], n * per, name)


def _adamw_math(g, w, m, v):
    nm = ADAM_B1 * m + (1.0 - ADAM_B1) * g
    nv = ADAM_B2 * v + (1.0 - ADAM_B2) * (g * g)
    c1 = 1.0 - ADAM_B1 ** ADAM_STEP
    c2 = 1.0 - ADAM_B2 ** ADAM_STEP
    return -ADAM_LR * ((nm / c1) / (jnp.sqrt(nv / c2) + ADAM_EPS) + ADAM_WD * w), nm, nv


def _adamw(parts, row0, w, m, v, name, window_n8=None):
    R, C = w.shape
    cw = parts.shape[2]
    cap = min(PACK_ROWS, 1 << (((PACK_ROWS * PACK_W) // cw).bit_length() - 1))
    tr = math.gcd(math.gcd(R, cap), row0 or R)
    assert R % tr == 0 and row0 % tr == 0

    def body(p_ref, w_ref, m_ref, v_ref, g_ref, d_ref, nm_ref, nv_ref):
        g = p_ref[0].astype(F32)
        for s in range(1, N_DEV):
            g = g + p_ref[s].astype(F32)
        if window_n8 is not None:
            off = (window_n8 * _my_index()) % LANES
            g = pltpu.roll(g, (cw - off) % cw, 1)[:, :C]
        d, nm, nv = _adamw_math(g, w_ref[...], m_ref[...], v_ref[...])
        g_ref[...], d_ref[...], nm_ref[...], nv_ref[...] = g, d, nm, nv

    row = lambda i: (i, 0)
    r0 = row0 // tr
    return _call(body, (R // tr,),
                 [(parts, (N_DEV, tr, cw), lambda i: (0, r0 + i, 0)), (w, (tr, C), row), (m, (tr, C), row), (v, (tr, C), row)],
                 [((R, C), F32, (tr, C), row)] * 4, name=name, sem=("parallel",))


def _window_geometry(n8):
    offs = [(d * n8) % LANES for d in range(N_DEV)]
    starts = [(d * n8) // LANES for d in range(N_DEV)]
    blocks = max(-(-(o + n8) // LANES) for o in offs)
    return starts, blocks, max(starts) + blocks


def _to_window(wpad, n8, tr=256):
    R, cw = wpad.shape
    tr = _tile(R, tr)

    def body(x_ref, o_ref):
        o_ref[...] = pltpu.roll(x_ref[...], (n8 * _my_index()) % LANES, 1).astype(BF16)

    row = lambda i: (i, 0)
    return _call(body, (R // tr,), [(wpad, (tr, cw), row)], [((R, cw), BF16, (tr, cw), row)], name="to_window", sem=("parallel",))[0]


def _from_windows(win, row0, rows, n8, tr=128):
    starts, blocks, total = _window_geometry(n8)
    cw = win.shape[2]
    tr = _tile(rows, tr)
    r0 = row0 // tr

    def body(w_ref, o_ref):
        o_ref[...] = jnp.zeros_like(o_ref)
        for d in range(N_DEV):
            cols = pl.ds(starts[d] * LANES, cw)
            o_ref[:, cols] = (o_ref[:, cols].astype(F32) + w_ref[d].astype(F32)).astype(BF16)

    return _call(body, (rows // tr,), [(win, (N_DEV, tr, cw), lambda i: (0, r0 + i, 0))],
                 [((rows, total * LANES), BF16, (tr, total * LANES), lambda i: (i, 0))], name="from_windows", sem=("parallel",))[0]


def _pack(flat_pieces, dtype, lead=()):
    cat = jnp.concatenate([p_.astype(dtype) for p_ in flat_pieces], axis=-1)
    n = cat.shape[-1]
    quantum = PACK_ROWS * PACK_W
    total = -(-n // quantum) * quantum
    cat = jnp.pad(cat, [(0, 0)] * len(lead) + [(0, total - n)])
    return cat.reshape(lead + (total // PACK_W, PACK_W)), n


REPLICATED = ("norm_mix", "norm_mlp", "norm_ple", "norm_final", "gdn_a_log", "gdn_dt_bias", "gdn_o_norm",
              "s5_lam_re", "s5_lam_im", "s5_log_step", "s5_b_re", "s5_b_im", "s5_c_re", "s5_c_im")
WEIGHTS = ("norm_mix", "norm_mlp", "norm_ple", "norm_final", "gdn_w_in", "gdn_conv_w", "gdn_a_log", "gdn_dt_bias",
           "gdn_o_norm", "gdn_w_out", "s5_w_in", "s5_lam_re", "s5_lam_im", "s5_log_step", "s5_b_re", "s5_b_im",
           "s5_c_re", "s5_c_im", "s5_d", "s5_w_out", "mlp_w_up", "mlp_w_down", "ple_w_proj", "ple_w_gate")
ROW_GROUP = ("mlp_w_down", "gdn_w_out", "s5_w_in", "ple_w_gate")
COL_SHARDED = ("mlp_w_up", "s5_w_out", "ple_w_proj")


def _rows2d(a):
    return a.reshape(-1, a.shape[-1])


def _misc_pack(conv, s5d):
    cw = conv.shape[-1]
    rows = jnp.concatenate([_rows2d(conv), jnp.pad(s5d, ((0, 0), (0, cw - s5d.shape[-1])))], axis=0)
    return jnp.pad(rows, ((0, -rows.shape[0] % 8), (0, 0)))


def _misc_unpack(a, conv_rows, s5d_shape):
    return a[:conv_rows], a[conv_rows:conv_rows + s5d_shape[0], :s5d_shape[1]]


def kernel(x, p, norm_mix, norm_mlp, norm_ple, norm_final, gdn_w_in, gdn_conv_w, gdn_a_log, gdn_dt_bias, gdn_o_norm, gdn_w_out, s5_w_in, s5_lam_re, s5_lam_im, s5_log_step, s5_b_re, s5_b_im, s5_c_re, s5_c_im, s5_d, s5_w_out, mlp_w_up, mlp_w_down, ple_w_proj, ple_w_gate, loss_target, m_norm_mix, m_norm_mlp, m_norm_ple, m_norm_final, m_gdn_w_in, m_gdn_conv_w, m_gdn_a_log, m_gdn_dt_bias, m_gdn_o_norm, m_gdn_w_out, m_s5_w_in, m_s5_lam_re, m_s5_lam_im, m_s5_log_step, m_s5_b_re, m_s5_b_im, m_s5_c_re, m_s5_c_im, m_s5_d, m_s5_w_out, m_mlp_w_up, m_mlp_w_down, m_ple_w_proj, m_ple_w_gate, v_norm_mix, v_norm_mlp, v_norm_ple, v_norm_final, v_gdn_w_in, v_gdn_conv_w, v_gdn_a_log, v_gdn_dt_bias, v_gdn_o_norm, v_gdn_w_out, v_s5_w_in, v_s5_lam_re, v_s5_lam_im, v_s5_log_step, v_s5_b_re, v_s5_b_im, v_s5_c_re, v_s5_c_im, v_s5_d, v_s5_w_out, v_mlp_w_up, v_mlp_w_down, v_ple_w_proj, v_ple_w_gate):
    args = dict(locals())
    w = {n: args[n] for n in WEIGHTS}
    mom = {n: args["m_" + n] for n in WEIGHTS}
    vel = {n: args["v_" + n] for n in WEIGHTS}
    depth = norm_mix.shape[0]
    T, D = x.shape[1], x.shape[2]
    hv = gdn_a_log.shape[1]
    vd = hv * HEAD
    n_gdn, n_s5 = gdn_w_in.shape[0], s5_w_in.shape[0]
    cw = gdn_conv_w.shape[2]
    cd = cw * N_DEV
    hk = (cd - vd) // (2 * HEAD)
    assert hv == 2 * hk and 2 * hv <= LANES and T % CHUNK == 0
    G, P = s5_lam_re.shape[1], s5_lam_re.shape[2]
    assert P == S5_STATE and G * S5_CH == D and G % S5_GPB == 0 and D // N_DEV <= cw
    n8 = gdn_w_in.shape[2]
    win_starts, win_blocks, win_total = _window_geometry(n8)
    cwin = win_blocks * LANES
    assert win_total * LANES == cd + vd + LANES

    row_off, off = {}, 0
    for n in ROW_GROUP:
        row_off[n] = off
        off += w[n].shape[0] * w[n].shape[1]
    gathered = _all_gather(
        [_to_window(jnp.pad(_rows2d(gdn_w_in), ((0, 0), (0, cwin - n8))), n8),
         jnp.concatenate([_rows2d(w[n]) for n in ROW_GROUP], axis=0).astype(BF16)]
        + [_rows2d(w[n]).astype(BF16) for n in COL_SHARDED] + [_misc_pack(gdn_conv_w, s5_d)], "gather_weights")
    g_win, g_row, g_misc = gathered[0], gathered[1], gathered[-1]
    g_col = dict(zip(COL_SHARDED, gathered[2:-1]))
    conv_full = g_misc[:, :n_gdn * 4].reshape(N_DEV, n_gdn, 4, cw).transpose(1, 2, 0, 3).reshape(n_gdn, 4, cd)
    s5d_full = g_misc[:, n_gdn * 4:n_gdn * 4 + n_s5, :D // N_DEV].transpose(1, 0, 2).reshape(n_s5, D)

    def weight(name, l):
        r = w[name].shape[1]
        if name in ROW_GROUP:
            return Sharded(g_row, 0, row_off[name] + l * r, r)
        return Sharded(g_col[name], 1, l * r, r)

    h = x[0]
    tgt = loss_target[0]
    grads = {n: [None] * w[n].shape[0] for n in WEIGHTS if n != "norm_final"}
    saved = []
    add = lambda acc, r: (r + acc,)

    for i in range(depth):
        j = i // 2
        sv = {"h0": h}
        hn = _rms_fwd(h, norm_mix[i])
        sv["hn"] = hn
        if i % 2 == 0:
            w_in = _from_windows(g_win, j * D, D, n8)
            pq = _mm(hn, w_in[:, :cd], name="gdn_in_qkv")[0]
            pz = _mm(hn, w_in[:, cd:cd + vd], name="gdn_in_z")[0]
            ba = _mm(hn, w_in[:, cd + vd:], name="gdn_in_ba")[0]
            qkv = _gdn_pre_fwd(pq, conv_full[j], hk)
            pv = jnp.pad(jnp.stack([gdn_a_log[j], gdn_dt_bias[j]]), ((0, 0), (hv, LANES - 2 * hv)))
            g2 = _gates_fwd(ba, pv, hv)
            bb = jnp.repeat(g2[:, :hv], HEAD, axis=1)
            gb = jnp.repeat(g2[:, hv:2 * hv], HEAD, axis=1)
            u, ww, gc, tinv = _gdn_a_fwd(qkv, gb, bb, hk, hv)
            o, vn, sall = _gdn_b_fwd(qkv, u, ww, gc, hk, hv)
            on = _ogate_fwd(o, pz, gdn_o_norm[j])
            h = _mm(on, weight("gdn_w_out", j), epi=add, extras=(h,), name="gdn_out")[0]
            sv.update(w_in=w_in, pq=pq, pz=pz, ba=ba, pv=pv, qkv=qkv, bb=bb, u=u, ww=ww, gc=gc, tinv=tinv, o=o, vn=vn, sall=sall, on=on)
        else:
            uu = _mm(hn, weight("s5_w_in", j), name="s5_in")[0]
            b_re_t, b_im_t = s5_b_re[j].transpose(2, 0, 1), s5_b_im[j].transpose(2, 0, 1)
            ls = s5_log_step[j].reshape(G, 1)
            ar, ai, bbr, bbi = _s5_params_fwd(s5_lam_re[j], s5_lam_im[j], ls, b_re_t, b_im_t)
            lam = jnp.stack([ar.reshape(-1), ai.reshape(-1)])
            bd_re = _block_diag(bbr.transpose(1, 2, 0), transpose=True).astype(BF16)
            bd_im = _block_diag(bbi.transpose(1, 2, 0), transpose=True).astype(BF16)
            cd_re = _block_diag(s5_c_re[j], transpose=True).astype(BF16)
            cd_im = _block_diag(-s5_c_im[j], transpose=True).astype(BF16)
            bur, bui = _s5_bproj_fwd(uu, bd_re, bd_im)
            xr, xi = _s5_scan(bur, bui, lam, reverse=False)
            dsk = s5d_full[j]
            yy, hact = _s5_cproj_fwd(xr, xi, cd_re, cd_im, uu, dsk)
            vg = _mm(hact, weight("s5_w_out", j), name="s5_out")[0]
            h = _s5_gate_fwd(h, vg)
            sv.update(uu=uu, b_re_t=b_re_t, b_im_t=b_im_t, ls=ls, lam=lam, bd_re=bd_re, bd_im=bd_im, cd_re=cd_re, cd_im=cd_im,
                      xr=xr, xi=xi, dsk=dsk, yy=yy, hact=hact, vg=vg)
        sv["h1"] = h
        hm = _rms_fwd(h, norm_mlp[i])
        up, act = _mm(hm, weight("mlp_w_up", i), out_dtypes=(F32, BF16),
                      epi=lambda acc: (acc, jnp.square(jnp.maximum(acc, 0.0))), name="mlp_up")
        h = _mm(act, weight("mlp_w_down", i), epi=add, extras=(h,), name="mlp_down")[0]
        sv.update(hm=hm, up=up, act=act, h2=h)
        hp = _rms_fwd(h, norm_ple[i])
        s_gate = _mm(hp, weight("ple_w_gate", i), name="ple_gate")[0]
        pp = _mm(p[i, 0], weight("ple_w_proj", i), name="ple_proj")[0]
        h = _ple_fwd(h, s_gate, pp)
        sv.update(hp=hp, s_gate=s_gate, pp=pp)
        saved.append(sv)

    dh, d_norm_final, loss_part = _loss_fwd_bwd(h, norm_final, tgt)
    loss = lax.psum(loss_part[0, 0], MESH_AXES)

    dw = lambda a, b_, axis, name: _mm(a, b_, "tn", out_dtypes=(BF16,), out_axis=axis, name=name)[0]
    for i in reversed(range(depth)):
        j = i // 2
        sv = saved[i]
        ds, dpp = _ple_bwd(dh, sv["s_gate"], sv["pp"])
        grads["ple_w_proj"][i] = dw(p[i, 0], dpp, 1, "ple_proj_dw")
        grads["ple_w_gate"][i] = dw(sv["hp"], ds, 0, "ple_gate_dw")
        d_hp = _mm(ds, weight("ple_w_gate", i), "nt", name="ple_gate_dx")[0]
        dh, grads["norm_ple"][i] = _rms_bwd(d_hp, sv["h2"], norm_ple[i], dh)
        grads["mlp_w_down"][i] = dw(sv["act"], dh, 0, "mlp_down_dw")
        d_up = _mm(dh, weight("mlp_w_down", i), "nt", out_dtypes=(BF16,),
                   epi=lambda acc, up_: (acc * 2.0 * jnp.maximum(up_, 0.0),), extras=(sv["up"],), name="mlp_down_dx")[0]
        grads["mlp_w_up"][i] = dw(sv["hm"], d_up, 1, "mlp_up_dw")
        d_hm = _mm(d_up, weight("mlp_w_up", i), "nt", name="mlp_up_dx")[0]
        dh, grads["norm_mlp"][i] = _rms_bwd(d_hm, sv["h1"], norm_mlp[i], dh)
        if i % 2 == 0:
            grads["gdn_w_out"][j] = dw(sv["on"], dh, 0, "gdn_out_dw")
            d_on = _mm(dh, weight("gdn_w_out", j), "nt", name="gdn_out_dx")[0]
            d_o, d_z, grads["gdn_o_norm"][j] = _ogate_bwd(d_on, sv["o"], sv["pz"], gdn_o_norm[j])
            dq_b, dk_b, dgc_b, d_u, d_w = _gdn_b_bwd(d_o, sv["qkv"], sv["ww"], sv["gc"], sv["vn"], sv["sall"], hk, hv)
            dk_a, d_v, d_bb, d_gb = _gdn_a_bwd(d_u, d_w, dgc_b, sv["qkv"], sv["bb"], sv["gc"], sv["tinv"], sv["u"], sv["ww"], hk, hv)
            d_qkv = jnp.concatenate([_pair_sum(dq_b, None), _pair_sum(dk_a, dk_b), d_v], axis=1)
            d_c, grads["gdn_conv_w"][j] = _gdn_pre_bwd(d_qkv, sv["pq"], conv_full[j], hk)
            d_pq = _gdn_conv_bwd(d_c, conv_full[j])
            d_g2 = jnp.pad(jnp.concatenate([d_bb[:, ::HEAD], d_gb[:, ::HEAD]], axis=1), ((0, 0), (0, LANES - 2 * hv)))
            d_ba, d_pv = _gates_bwd(d_g2, sv["ba"], sv["pv"], hv)
            grads["gdn_a_log"][j] = d_pv[0, hv:2 * hv]
            grads["gdn_dt_bias"][j] = d_pv[1, hv:2 * hv]
            hn, w_in = sv["hn"], sv["w_in"]
            dw_nat = jnp.concatenate([dw(hn, d_pq, None, "gdn_in_qkv_dw"), dw(hn, d_z, None, "gdn_in_z_dw"),
                                      dw(hn, d_ba, None, "gdn_in_ba_dw")], axis=1)
            grads["gdn_w_in"][j] = jnp.stack([dw_nat[:, s * LANES:s * LANES + cwin] for s in win_starts])
            d_hn = _mm(d_pq, w_in[:, :cd], "nt", name="gdn_in_qkv_dx")[0]
            d_hn = _mm(d_z, w_in[:, cd:cd + vd], "nt", epi=add, extras=(d_hn,), name="gdn_in_z_dx")[0]
            d_hn = _mm(d_ba, w_in[:, cd + vd:], "nt", epi=add, extras=(d_hn,), name="gdn_in_ba_dx")[0]
        else:
            d_vg = _s5_gate_bwd(dh, sv["vg"])
            grads["s5_w_out"][j] = dw(sv["hact"], d_vg, 1, "s5_out_dw")

            def gelu_bwd(acc, y_):
                _, vjp = jax.vjp(jax.nn.gelu, y_)
                return (vjp(acc)[0],)

            d_y = _mm(d_vg, weight("s5_w_out", j), "nt", epi=gelu_bwd, extras=(sv["yy"],), name="s5_out_dx")[0]
            d_xr, d_xi, du_skip, d_dsk, d_cdr, d_cdi = _s5_cproj_bwd(d_y, sv["xr"], sv["xi"], sv["cd_re"], sv["cd_im"], sv["uu"], sv["dsk"])
            grads["s5_d"][j] = d_dsk
            grads["s5_c_re"][j] = _block_diag_extract(d_cdr, S5_CH, S5_STATE, transpose=True)
            grads["s5_c_im"][j] = -_block_diag_extract(d_cdi, S5_CH, S5_STATE, transpose=True)
            d_bur, d_bui, d_lam = _s5_scan(d_xr, d_xi, sv["lam"], reverse=True, xr=sv["xr"], xi=sv["xi"])
            d_uu, d_bdr, d_bdi = _s5_bproj_bwd(d_bur, d_bui, sv["bd_re"], sv["bd_im"], sv["uu"], du_skip)
            d_bbr = _block_diag_extract(d_bdr, S5_STATE, S5_CH, transpose=True).transpose(2, 0, 1)
            d_bbi = _block_diag_extract(d_bdi, S5_STATE, S5_CH, transpose=True).transpose(2, 0, 1)
            d_lr, d_li, d_ls, d_br_t, d_bi_t = _s5_params_bwd(
                s5_lam_re[j], s5_lam_im[j], sv["ls"], sv["b_re_t"], sv["b_im_t"],
                d_lam[0].reshape(G, P), d_lam[1].reshape(G, P), d_bbr, d_bbi)
            grads["s5_lam_re"][j], grads["s5_lam_im"][j], grads["s5_log_step"][j] = d_lr, d_li, d_ls.reshape(G)
            grads["s5_b_re"][j], grads["s5_b_im"][j] = d_br_t.transpose(1, 2, 0), d_bi_t.transpose(1, 2, 0)
            grads["s5_w_in"][j] = dw(sv["hn"], d_uu, 0, "s5_in_dw")
            d_hn = _mm(d_uu, weight("s5_w_in", j), "nt", name="s5_in_dx")[0]
        dh, grads["norm_mix"][i] = _rms_bwd(d_hn, sv["h0"], norm_mix[i], dh)

    out = {}
    layers = lambda name: jnp.concatenate(grads[name], axis=1)
    d_conv = jnp.stack(grads["gdn_conv_w"]).reshape(n_gdn * 4, N_DEV, cw).transpose(1, 0, 2)
    d_s5d = jnp.stack([g.reshape(N_DEV, D // N_DEV) for g in grads["s5_d"]], axis=1)
    d_misc = jnp.concatenate([d_conv, jnp.pad(d_s5d, ((0, 0), (0, 0), (0, cw - D // N_DEV)))], axis=1)
    d_misc = jnp.pad(d_misc, ((0, 0), (0, -d_misc.shape[1] % 8), (0, 0)))
    recv = _all_to_all(
        [layers("gdn_w_in"), jnp.concatenate([layers(n) for n in ROW_GROUP], axis=1)] + [layers(n) for n in COL_SHARDED] + [d_misc],
        "exchange_grads")
    r_win, r_row, r_misc = recv[0], recv[1], recv[-1]
    r_col = dict(zip(COL_SHARDED, recv[2:-1]))

    def update(name, parts, row0, **kw):
        res = _adamw(parts, row0, _rows2d(w[name]), _rows2d(mom[name]), _rows2d(vel[name]), "adamw_" + name, **kw)
        out[name] = [r.reshape(w[name].shape) for r in res]

    update("gdn_w_in", r_win, 0, window_n8=n8)
    for n in ROW_GROUP:
        update(n, r_row, row_off[n])
    for n in COL_SHARDED:
        update(n, r_col[n], 0)
    res = _adamw(r_misc, 0, _misc_pack(gdn_conv_w, s5_d), _misc_pack(mom["gdn_conv_w"], mom["s5_d"]),
                 _misc_pack(vel["gdn_conv_w"], vel["s5_d"]), "adamw_misc")
    unpacked = [_misc_unpack(r, n_gdn * 4, s5_d.shape) for r in res]
    out["gdn_conv_w"] = [u_[0].reshape(gdn_conv_w.shape) for u_ in unpacked]
    out["s5_d"] = [u_[1] for u_ in unpacked]

    rep_g = {n: (d_norm_final[0] if n == "norm_final" else jnp.stack([g.reshape(w[n].shape[1:]) for g in grads[n]])) for n in REPLICATED}
    flat_r = lambda d: [d[n].reshape(-1) for n in REPLICATED]
    pg, _ = _pack(flat_r(rep_g), F32)
    parts_r = _all_gather([pg], "gather_small_grads")[0]
    pw, _ = _pack(flat_r(w), F32)
    pm, _ = _pack(flat_r(mom), F32)
    pvv, _ = _pack(flat_r(vel), F32)
    res = [r.reshape(-1) for r in _adamw(parts_r, 0, pw, pm, pvv, "adamw_replicated")]
    off = 0
    for name in REPLICATED:
        n = w[name].size
        out[name] = [res[k][off:off + n].reshape(w[name].shape) for k in range(4)]
        off += n

    grad_x = dh[None]
    return (loss, grad_x, *[out[n][0] for n in WEIGHTS], *[out[n][1] for n in WEIGHTS],
            *[out[n][2] for n in WEIGHTS], *[out[n][3] for n in WEIGHTS])
```

```python
import collections
import math

import jax
import jax.numpy as jnp
from jax import lax
from jax.experimental import pallas as pl
from jax.experimental.pallas import tpu as pltpu

F32, BF16 = jnp.float32, jnp.bfloat16
NN, NT, TN = ((1,), (0,)), ((1,), (1,)), ((0,), (0,))

N_DEV = 8
MESH_AXES = ("x", "y", "c")
LANES = 128
V7X_VMEM_BYTES = 64 * 1024 * 1024
VMEM_LIMIT = V7X_VMEM_BYTES - 8 * 1024 * 1024
CHUNK = 64
HEAD = 128
SUPER = 256
S5_CH = 16
S5_STATE = 64
S5_GPB = LANES // S5_CH
S5_SPB = S5_GPB * S5_STATE
NORM_EPS = 1e-6
L2_EPS = 1e-6
ADAM_LR, ADAM_B1, ADAM_B2, ADAM_EPS, ADAM_WD, ADAM_STEP = 0.001, 0.9, 0.999, 1e-08, 0.01, 10
PACK_W = 1024
PACK_ROWS = 256


def _dot(a, b, dims):
    return lax.dot_general(a, b, (dims, ((), ())), preferred_element_type=F32)


def _bdot(a, b, dims=NN):
    return _dot(a.astype(BF16), b.astype(BF16), dims)


def _call(body, grid, ins, outs, scratch=(), name=None, sem=None):
    res = pl.pallas_call(
        body,
        grid=grid,
        in_specs=[pl.BlockSpec(b, m) for _, b, m in ins],
        out_specs=[pl.BlockSpec(b, m) for _, _, b, m in outs],
        out_shape=[jax.ShapeDtypeStruct(s, d) for s, d, _, _ in outs],
        scratch_shapes=list(scratch),
        name=name,
        compiler_params=pltpu.CompilerParams(
            dimension_semantics=sem or ("arbitrary",) * len(grid), vmem_limit_bytes=VMEM_LIMIT),
    )(*[a for a, _, _ in ins])
    return res


def _tile(n, want):
    t = min(n, want)
    assert n % t == 0, (n, want)
    return t


def _accumulate(ref, val, first):
    @pl.when(first)
    def _():
        ref[...] = jnp.zeros_like(ref)
    ref[...] += val


class Sharded(collections.namedtuple("Sharded", "arr axis row0 rows")):
    @property
    def shape(self):
        c = self.arr.shape[2]
        return (self.rows, N_DEV * c) if self.axis == 1 else (N_DEV * self.rows, c)

    @property
    def units(self):
        return (math.gcd(self.rows, self.row0), self.arr.shape[2])


def _mm(a, b, mode="nn", out_dtypes=(F32,), epi=None, extras=(), name="mm", out_axis=None, tm=1024, tn=1024, tk=2048):
    sh = isinstance(b, Sharded)
    b_rows, b_cols = b.shape
    u_rows, u_cols = b.units if sh else b.shape
    if mode == "nn":
        (M, K), (K2, N), (uk, un) = a.shape, (b_rows, b_cols), (u_rows, u_cols)
    elif mode == "nt":
        (M, K), (N, K2), (un, uk) = a.shape, (b_rows, b_cols), (u_rows, u_cols)
    else:
        (K, M), (K2, N), (uk, un) = a.shape, (b_rows, b_cols), (u_rows, u_cols)
    assert K == K2, (a.shape, b.shape, mode)
    um = M
    if out_axis == 0:
        um = M // N_DEV
    elif out_axis == 1:
        un = N // N_DEV
    tm, tn = _tile(um, tm), _tile(un, tn)
    span = 1
    if sh and ((mode == "nn" and b.axis == 0 and uk == b.rows) or (mode == "nt" and b.axis == 1)) and uk < min(K, tk):
        span = min(K, tk) // uk
        assert N_DEV % span == 0
    tk = _tile(uk, tk)
    nk = K // (tk * span)
    a_spec = ((tk, tm), lambda i, j, k: (k, i)) if mode == "tn" else ((tm, tk * span), lambda i, j, k: (i, k))
    if not sh:
        b_arr = b
        b_spec = ((tn, tk), lambda i, j, k: (j, k)) if mode == "nt" else ((tk, tn), lambda i, j, k: (k, j))
    else:
        b_arr = b.arr
        tr_, tc_ = (tk, tn) if mode == "nn" else (tn, tk)
        r0, per_r, per_c = b.row0 // tr_, b.rows // tr_, b.arr.shape[2] // tc_
        assert b.row0 % tr_ == 0 and mode != "tn"
        if span > 1:
            place = (lambda r, c: (r, r0, c)) if b.axis == 0 else (lambda r, c: (c, r0 + r, 0))
        elif b.axis == 1:
            place = lambda r, c: (c // per_c, r0 + r, c % per_c)
        else:
            place = lambda r, c: (r // per_r, r0 + r % per_r, c)
        b_spec = ((span, tr_, tc_), (lambda i, j, k: place(k, j)) if mode == "nn" else (lambda i, j, k: place(j, k)))
    dims = {"nn": NN, "nt": NT, "tn": TN}[mode]
    n_ex, n_out = len(extras), len(out_dtypes)

    def body(*refs):
        a_ref, b_ref = refs[:2]
        ex = refs[2:2 + n_ex]
        outs = refs[2 + n_ex:2 + n_ex + n_out]

        def product():
            if not sh:
                return _bdot(a_ref[...], b_ref[...], dims)
            part = _bdot(a_ref[:, :tk], b_ref[0], dims)
            for s in range(1, span):
                part = part + _bdot(a_ref[:, s * tk:(s + 1) * tk], b_ref[s], dims)
            return part

        def finish(res):
            vals = epi(res, *[e[...] for e in ex]) if epi is not None else (res,)
            for r, v in zip(outs, vals):
                r[...] = v.astype(r.dtype)

        if nk == 1:
            finish(product())
            return
        acc = refs[-1]
        k = pl.program_id(2)

        @pl.when(k == 0)
        def _():
            acc[...] = jnp.zeros_like(acc)

        acc[...] += product()

        @pl.when(k == nk - 1)
        def _():
            finish(acc[...])

    tile = lambda i, j, k: (i, j)
    if out_axis is None:
        out_shape, out_block, out_map = (M, N), (tm, tn), tile
    elif out_axis == 0:
        per = um // tm
        out_shape, out_block, out_map = (N_DEV, um, N), (None, tm, tn), lambda i, j, k: (i // per, i % per, j)
    else:
        per = un // tn
        out_shape, out_block, out_map = (N_DEV, M, un), (None, tm, tn), lambda i, j, k: (j // per, i, j % per)
    return _call(
        body, (M // tm, N // tn, nk),
        [(a,) + a_spec, (b_arr,) + b_spec] + [(e, (tm, tn), tile) for e in extras],
        [(out_shape, d, out_block, out_map) for d in out_dtypes],
        scratch=[pltpu.VMEM((tm, tn), F32)] if nk > 1 else [], name=name,
        sem=("parallel", "parallel", "arbitrary"))


def _rms_fwd(h, g, tr=256):
    T, D = h.shape
    tr = _tile(T, tr)

    def body(h_ref, g_ref, o_ref):
        x = h_ref[...]
        r = lax.rsqrt(jnp.mean(x * x, axis=-1, keepdims=True) + NORM_EPS)
        o_ref[...] = (x * r * g_ref[...]).astype(BF16)

    row = lambda i: (i, 0)
    fix = lambda i: (0, 0)
    return _call(body, (T // tr,), [(h, (tr, D), row), (g.reshape(1, D), (1, D), fix)],
                 [((T, D), BF16, (tr, D), row)], name="rms_fwd", sem=("parallel",))[0]


def _rms_bwd_math(dy, x, g):
    r = lax.rsqrt(jnp.mean(x * x, axis=-1, keepdims=True) + NORM_EPS)
    xh = x * r
    dxh = dy * g
    dx = r * (dxh - xh * jnp.mean(dxh * xh, axis=-1, keepdims=True))
    dg = jnp.sum(dy * xh, axis=0, keepdims=True)
    return dx, dg


def _rms_bwd(dy, h, g, res, tr=256):
    T, D = h.shape
    tr = _tile(T, tr)

    def body(dy_ref, h_ref, g_ref, res_ref, dh_ref, dhb_ref, dg_ref):
        dx, dg = _rms_bwd_math(dy_ref[...], h_ref[...], g_ref[...])
        dh = res_ref[...] + dx
        dh_ref[...] = dh
        dhb_ref[...] = dh.astype(BF16)
        _accumulate(dg_ref, dg, pl.program_id(0) == 0)

    row = lambda i: (i, 0)
    fix = lambda i: (0, 0)
    return _call(body, (T // tr,),
                 [(dy, (tr, D), row), (h, (tr, D), row), (g.reshape(1, D), (1, D), fix), (res, (tr, D), row)],
                 [((T, D), F32, (tr, D), row), ((T, D), BF16, (tr, D), row), ((1, D), F32, (1, D), fix)], name="rms_bwd")


def _loss_fwd_bwd(h, g, tgt, tr=256):
    T, D = h.shape
    tr = _tile(T, tr)

    def body(h_ref, g_ref, t_ref, dh_ref, dg_ref, loss_ref):
        x, gg = h_ref[...], g_ref[...]
        r = lax.rsqrt(jnp.mean(x * x, axis=-1, keepdims=True) + NORM_EPS)
        diff = x * r * gg - t_ref[...]
        part = 0.5 * jnp.sum(jnp.mean(diff * diff, axis=-1, keepdims=True))
        dx, dg = _rms_bwd_math(diff * (1.0 / D), x, gg)
        dh_ref[...] = dx
        first = pl.program_id(0) == 0
        _accumulate(dg_ref, dg, first)
        _accumulate(loss_ref, jnp.full((1, LANES), part, F32), first)

    row = lambda i: (i, 0)
    fix = lambda i: (0, 0)
    return _call(body, (T // tr,),
                 [(h, (tr, D), row), (g.reshape(1, D), (1, D), fix), (tgt, (tr, D), row)],
                 [((T, D), F32, (tr, D), row), ((1, D), F32, (1, D), fix), ((1, LANES), F32, (1, LANES), fix)],
                 name="loss_fwd_bwd")


def _ple_fwd(h, s, pp, tr=256):
    T, D = h.shape
    tr = _tile(T, tr)

    def body(h_ref, s_ref, p_ref, o_ref):
        o_ref[...] = h_ref[...] + jax.nn.sigmoid(s_ref[...]) * p_ref[...]

    row = lambda i: (i, 0)
    return _call(body, (T // tr,), [(a, (tr, D), row) for a in (h, s, pp)],
                 [((T, D), F32, (tr, D), row)], name="ple_fwd", sem=("parallel",))[0]


def _ple_bwd(dh, s, pp, tr=256):
    T, D = dh.shape
    tr = _tile(T, tr)

    def body(dh_ref, s_ref, p_ref, ds_ref, dp_ref):
        d = dh_ref[...]
        gate = jax.nn.sigmoid(s_ref[...])
        ds_ref[...] = (d * p_ref[...] * gate * (1.0 - gate)).astype(BF16)
        dp_ref[...] = (d * gate).astype(BF16)

    row = lambda i: (i, 0)
    return _call(body, (T // tr,), [(a, (tr, D), row) for a in (dh, s, pp)],
                 [((T, D), BF16, (tr, D), row)] * 2, name="ple_bwd", sem=("parallel",))


def _conv_taps(xe, w, tr):
    c = w[3:4, :] * xe[8:, :]
    for j in range(3):
        c = c + w[j:j + 1, :] * pltpu.roll(xe, 3 - j, 0)[8:, :]
    return c


def _gdn_pre_fwd(pq, conv_w, hk, tr=1024):
    T, CD = pq.shape
    tr = _tile(T, tr)
    r8 = tr // 8

    def body(x_ref, halo_ref, w_ref, o_ref):
        j, r = pl.program_id(0), pl.program_id(1)
        halo = jnp.where(r > 0, halo_ref[...], 0.0)
        xe = jnp.concatenate([halo, x_ref[...]], axis=0)
        c = _conv_taps(xe, w_ref[...], tr)
        s = c * jax.nn.sigmoid(c)
        rn = lax.rsqrt(jnp.sum(s * s, axis=-1, keepdims=True) + L2_EPS)
        scale = jnp.where(j < hk, HEAD ** -0.5, 1.0)
        o_ref[...] = jnp.where(j < 2 * hk, s * rn * scale, s)

    tile = lambda j, r: (r, j)
    return _call(body, (CD // HEAD, T // tr),
                 [(pq, (tr, HEAD), tile), (pq, (8, HEAD), lambda j, r: (jnp.maximum(r * r8 - 1, 0), j)),
                  (conv_w, (4, HEAD), lambda j, r: (0, j))],
                 [((T, CD), F32, (tr, HEAD), tile)], name="gdn_pre_fwd", sem=("parallel", "parallel"))[0]


def _gdn_pre_bwd(dn, pq, conv_w, hk, tr=1024):
    T, CD = pq.shape
    tr = _tile(T, tr)
    r8 = tr // 8

    def body(dn_ref, x_ref, halo_ref, w_ref, dc_ref, dw_ref):
        j, r = pl.program_id(0), pl.program_id(1)
        halo = jnp.where(r > 0, halo_ref[...], 0.0)
        xe = jnp.concatenate([halo, x_ref[...]], axis=0)
        c = _conv_taps(xe, w_ref[...], tr)
        sig = jax.nn.sigmoid(c)
        s = c * sig
        rn = lax.rsqrt(jnp.sum(s * s, axis=-1, keepdims=True) + L2_EPS)
        scale = jnp.where(j < hk, HEAD ** -0.5, 1.0)
        d = dn_ref[...]
        y = s * rn
        dy = d * scale
        ds = jnp.where(j < 2 * hk, rn * (dy - y * jnp.sum(dy * y, axis=-1, keepdims=True)), d)
        dc = ds * sig * (1.0 + c * (1.0 - sig))
        dc_ref[...] = dc

        @pl.when(r == 0)
        def _():
            dw_ref[...] = jnp.zeros_like(dw_ref)

        for t in range(4):
            xs = xe[8:, :] if t == 3 else pltpu.roll(xe, 3 - t, 0)[8:, :]
            dw_ref[t:t + 1, :] += jnp.sum(dc * xs, axis=0, keepdims=True)

    tile = lambda j, r: (r, j)
    col = lambda j, r: (0, j)
    return _call(body, (CD // HEAD, T // tr),
                 [(dn, (tr, HEAD), tile), (pq, (tr, HEAD), tile),
                  (pq, (8, HEAD), lambda j, r: (jnp.maximum(r * r8 - 1, 0), j)), (conv_w, (4, HEAD), col)],
                 [((T, CD), F32, (tr, HEAD), tile), ((4, CD), F32, (4, HEAD), col)], name="gdn_pre_bwd")


def _gdn_conv_bwd(dc, conv_w, tr=1024):
    T, CD = dc.shape
    tr = _tile(T, tr)
    r8 = tr // 8
    n_r = T // tr

    def body(dc_ref, halo_ref, w_ref, dx_ref):
        r = pl.program_id(1)
        halo = jnp.where(r < n_r - 1, halo_ref[...], 0.0)
        de = jnp.concatenate([dc_ref[...], halo], axis=0)
        w = w_ref[...]
        dx = w[3:4, :] * de[:tr, :]
        for j in range(3):
            dx = dx + w[j:j + 1, :] * pltpu.roll(de, tr + 8 - (3 - j), 0)[:tr, :]
        dx_ref[...] = dx.astype(BF16)

    tile = lambda j, r: (r, j)
    return _call(body, (CD // HEAD, n_r),
                 [(dc, (tr, HEAD), tile), (dc, (8, HEAD), lambda j, r: (jnp.minimum((r + 1) * r8, T // 8 - 1), j)),
                  (conv_w, (4, HEAD), lambda j, r: (0, j))],
                 [((T, CD), BF16, (tr, HEAD), tile)], name="gdn_conv_bwd", sem=("parallel", "parallel"))[0]


def _gates_fwd(ba, pv, hv, tr=1024):
    T = ba.shape[0]
    tr = _tile(T, tr)

    def body(x_ref, pv_ref, o_ref):
        x = x_ref[...]
        lane = lax.broadcasted_iota(jnp.int32, x.shape, 1)
        g = -jnp.exp(pv_ref[0:1, :]) * jax.nn.softplus(x + pv_ref[1:2, :])
        o_ref[...] = jnp.where(lane < hv, jax.nn.sigmoid(x), jnp.where(lane < 2 * hv, g, 0.0))

    row = lambda i: (i, 0)
    return _call(body, (T // tr,), [(ba, (tr, LANES), row), (pv, (2, LANES), lambda i: (0, 0))],
                 [((T, LANES), F32, (tr, LANES), row)], name="gates_fwd", sem=("parallel",))[0]


def _gates_bwd(dg2, ba, pv, hv, tr=1024):
    T = ba.shape[0]
    tr = _tile(T, tr)

    def body(d_ref, x_ref, pv_ref, dx_ref, dpv_ref):
        x, d = x_ref[...], d_ref[...]
        lane = lax.broadcasted_iota(jnp.int32, x.shape, 1)
        is_a = (lane >= hv) & (lane < 2 * hv)
        beta = jax.nn.sigmoid(x)
        neg_a = -jnp.exp(pv_ref[0:1, :])
        z = x + pv_ref[1:2, :]
        da = d * neg_a * jax.nn.sigmoid(z)
        dx_ref[...] = jnp.where(lane < hv, d * beta * (1.0 - beta), jnp.where(is_a, da, 0.0))
        first = pl.program_id(0) == 0

        @pl.when(first)
        def _():
            dpv_ref[...] = jnp.zeros_like(dpv_ref)

        dpv_ref[0:1, :] += jnp.sum(jnp.where(is_a, d * neg_a * jax.nn.softplus(z), 0.0), axis=0, keepdims=True)
        dpv_ref[1:2, :] += jnp.sum(jnp.where(is_a, da, 0.0), axis=0, keepdims=True)

    row = lambda i: (i, 0)
    fix = lambda i: (0, 0)
    return _call(body, (T // tr,), [(dg2, (tr, LANES), row), (ba, (tr, LANES), row), (pv, (2, LANES), fix)],
                 [((T, LANES), F32, (tr, LANES), row), ((2, LANES), F32, (2, LANES), fix)], name="gates_bwd")


def _ogate_fwd(o, z, o_norm, tr=512):
    T, VD = o.shape
    tr = _tile(T, tr)

    def body(o_ref, z_ref, g_ref, y_ref):
        x, zz = o_ref[...], z_ref[...]
        r = lax.rsqrt(jnp.mean(x * x, axis=-1, keepdims=True) + NORM_EPS)
        y_ref[...] = (x * r * g_ref[...] * (zz * jax.nn.sigmoid(zz))).astype(BF16)

    tile = lambda h, r: (r, h)
    return _call(body, (VD // HEAD, T // tr),
                 [(o, (tr, HEAD), tile), (z, (tr, HEAD), tile), (o_norm.reshape(1, HEAD), (1, HEAD), lambda h, r: (0, 0))],
                 [((T, VD), BF16, (tr, HEAD), tile)], name="ogate_fwd", sem=("parallel", "parallel"))[0]


def _ogate_bwd(dy, o, z, o_norm, tr=512):
    T, VD = o.shape
    tr = _tile(T, tr)

    def body(dy_ref, o_ref, z_ref, g_ref, do_ref, dz_ref, dg_ref):
        d, x, zz, g = dy_ref[...], o_ref[...], z_ref[...], g_ref[...]
        sig = jax.nn.sigmoid(zz)
        silu = zz * sig
        dx, dg = _rms_bwd_math(d * silu, x, g)
        r = lax.rsqrt(jnp.mean(x * x, axis=-1, keepdims=True) + NORM_EPS)
        do_ref[...] = dx
        dz_ref[...] = (d * (x * r * g) * sig * (1.0 + zz * (1.0 - sig))).astype(BF16)
        _accumulate(dg_ref, dg, (pl.program_id(0) == 0) & (pl.program_id(1) == 0))

    tile = lambda h, r: (r, h)
    fix = lambda h, r: (0, 0)
    return _call(body, (VD // HEAD, T // tr),
                 [(dy, (tr, HEAD), tile), (o, (tr, HEAD), tile), (z, (tr, HEAD), tile), (o_norm.reshape(1, HEAD), (1, HEAD), fix)],
                 [((T, VD), F32, (tr, HEAD), tile), ((T, VD), BF16, (tr, HEAD), tile), ((1, HEAD), F32, (1, HEAD), fix)],
                 name="ogate_bwd")


def _chunk_iota():
    return (lax.broadcasted_iota(jnp.int32, (CHUNK, CHUNK), 0), lax.broadcasted_iota(jnp.int32, (CHUNK, CHUNK), 1))


def _decay(gc):
    ri, ci = _chunk_iota()
    gcol = gc[:, :CHUNK]
    grow = jnp.sum(jnp.where(ri == ci, gcol, 0.0), axis=0, keepdims=True)
    return jnp.where(ri >= ci, jnp.exp(jnp.minimum(gcol - grow, 0.0)), 0.0)


def _rowsum(x):
    return jnp.broadcast_to(jnp.sum(x, axis=1, keepdims=True), (x.shape[0], HEAD))


def _split3(x):
    h1 = x.astype(BF16)
    r1 = x - h1.astype(F32)
    h2 = r1.astype(BF16)
    return h1, h2, (r1 - h2.astype(F32)).astype(BF16)


def _sel_dot(sel, xs, dims=NN):
    s = sel.astype(BF16)
    parts = [_split3(x) for x in xs]
    if dims == NN:
        return [_dot(s, h1, NN) + _dot(s, h2, NN) + _dot(s, h3, NN) for h1, h2, h3 in parts]
    return [_dot(h1, s, dims) + _dot(h2, s, dims) + _dot(h3, s, dims) for h1, h2, h3 in parts]


def _colsum(es):
    return _sel_dot(jnp.ones((es[0].shape[0], HEAD), F32), es, TN)


def _super_iota():
    ri = lax.broadcasted_iota(jnp.int32, (SUPER, SUPER), 0)
    ci = lax.broadcasted_iota(jnp.int32, (SUPER, SUPER), 1)
    shift = int(math.log2(CHUNK))
    return ri, ci, jnp.right_shift(ri, shift) == jnp.right_shift(ci, shift)


def _decay_super(gc, ri, ci, same):
    gcol = jnp.concatenate([gc] * (SUPER // HEAD), axis=1)
    grow = jnp.sum(jnp.where(ri == ci, gcol, 0.0), axis=0, keepdims=True)
    return jnp.where(same & (ri >= ci), jnp.exp(jnp.minimum(gcol - grow, 0.0)), 0.0)


def _unit_lower_inverse(ms, eye):
    ps = [-m for m in ms]
    xs = [eye + p for p in ps]
    for _ in range(int(math.log2(CHUNK)) - 1):
        ps = [_bdot(p, p) for p in ps]
        xs = [x + _bdot(x, p) for x, p in zip(xs, ps)]
    resid = []
    for m, x in zip(ms, xs):
        (m1, m2, _), (x1, x2, _) = _split3(m), _split3(x)
        resid.append((eye - x) - (_dot(m1, x1, NN) + _dot(m1, x2, NN) + _dot(m2, x1, NN)))
    return [x + _bdot(x, r) for x, r in zip(xs, resid)]


def _gdn_a_fwd(qkv, gb, bb, hk, hv, tr=1024):
    T = qkv.shape[0]
    tr = _tile(T, tr)
    assert tr % SUPER == 0

    def body(k_ref, v_ref, g_ref, b_ref, u_ref, w_ref, gc_ref, ti_ref):
        ri, ci, same = _super_iota()
        ltri = jnp.where(same & (ri >= ci), 1.0, 0.0)
        eye = jnp.where(ri == ci, 1.0, 0.0)
        rows = [pl.ds(s * SUPER, SUPER) for s in range(tr // SUPER)]
        ks, vs, betas = [k_ref[r, :] for r in rows], [v_ref[r, :] for r in rows], [b_ref[r, :] for r in rows]
        gcs = _sel_dot(ltri, [g_ref[r, :] for r in rows])
        kbs = [k * beta for k, beta in zip(ks, betas)]
        ms = [jnp.where(same & (ri > ci), _bdot(kb, k, NT) * _decay_super(gc, ri, ci, same), 0.0)
              for kb, k, gc in zip(kbs, ks, gcs)]
        tinvs = _unit_lower_inverse(ms, eye)
        xs = [_bdot(tinv, jnp.concatenate([v * beta, kb * jnp.exp(gc)], axis=1))
              for tinv, v, beta, kb, gc in zip(tinvs, vs, betas, kbs, gcs)]
        for r, x, gc, tinv in zip(rows, xs, gcs, tinvs):
            u_ref[r, :] = x[:, :HEAD]
            w_ref[r, :] = x[:, HEAD:]
            gc_ref[r, :] = gc
            ti_ref[0, r, :] = tinv.astype(BF16)

    tile = lambda h, r: (r, h)
    vd = hv * HEAD
    return _call(body, (hv, T // tr),
                 [(qkv, (tr, HEAD), lambda h, r: (r, hk + h // 2)), (qkv, (tr, HEAD), lambda h, r: (r, 2 * hk + h)),
                  (gb, (tr, HEAD), tile), (bb, (tr, HEAD), tile)],
                 [((T, vd), F32, (tr, HEAD), tile)] * 3 + [((hv, T, SUPER), BF16, (1, tr, SUPER), lambda h, r: (h, r, 0))],
                 name="gdn_a_fwd", sem=("parallel", "parallel"))


def _gdn_b_fwd(qkv, u, w, gc, hk, hv, tr=512):
    T = qkv.shape[0]
    tr = _tile(T, tr)
    cpb = tr // CHUNK

    def body(q_ref, k_ref, u_ref, w_ref, gc_ref, o_ref, vn_ref, sall_ref, s_ref):
        ri, ci = _chunk_iota()

        @pl.when(pl.program_id(1) == 0)
        def _():
            s_ref[...] = jnp.zeros_like(s_ref)

        for c in range(cpb):
            rows = pl.ds(c * CHUNK, CHUNK)
            q, k, gc = q_ref[rows, :], k_ref[rows, :], gc_ref[rows, :]
            gl = gc[CHUNK - 1:CHUNK, :]
            qk = jnp.where(ri >= ci, _bdot(q, k, NT) * _decay(gc), 0.0)
            s = s_ref[...]
            sall_ref[0, c] = s
            vn = u_ref[rows, :] - _bdot(w_ref[rows, :], s)
            o_ref[rows, :] = _bdot(q * jnp.exp(gc), s) + _bdot(qk, vn)
            vn_ref[rows, :] = vn
            s_ref[...] = s * jnp.exp(gl) + _bdot(k * jnp.exp(gl - gc), vn, TN)

    tile = lambda h, r: (r, h)
    vd = hv * HEAD
    return _call(body, (hv, T // tr),
                 [(qkv, (tr, HEAD), lambda h, r: (r, h // 2)), (qkv, (tr, HEAD), lambda h, r: (r, hk + h // 2)),
                  (u, (tr, HEAD), tile), (w, (tr, HEAD), tile), (gc, (tr, HEAD), tile)],
                 [((T, vd), F32, (tr, HEAD), tile)] * 2 +
                 [((hv, T // CHUNK, HEAD, HEAD), F32, (1, cpb, HEAD, HEAD), lambda h, r: (h, r, 0, 0))],
                 scratch=[pltpu.VMEM((HEAD, HEAD), F32)], name="gdn_b_fwd", sem=("parallel", "arbitrary"))


def _gdn_b_bwd(do, qkv, w, gc, vn, sall, hk, hv, tr=512):
    T = qkv.shape[0]
    tr = _tile(T, tr)
    cpb = tr // CHUNK
    n_r = T // tr

    def body(do_ref, q_ref, k_ref, w_ref, gc_ref, vn_ref, sall_ref, dq_ref, dk_ref, dgc_ref, du_ref, dw_ref, ds_ref):
        ri, ci = _chunk_iota()

        @pl.when(pl.program_id(1) == 0)
        def _():
            ds_ref[...] = jnp.zeros_like(ds_ref)

        for c in reversed(range(cpb)):
            rows = pl.ds(c * CHUNK, CHUNK)
            d_o, q, k, w, gc, vn = do_ref[rows, :], q_ref[rows, :], k_ref[rows, :], w_ref[rows, :], gc_ref[rows, :], vn_ref[rows, :]
            s = sall_ref[0, c]
            ds_next = ds_ref[...]
            gl = gc[CHUNK - 1:CHUNK, :]
            egc, ekd, eg = jnp.exp(gc), jnp.exp(gl - gc), jnp.exp(gl)
            qg, kd = q * egc, k * ekd
            dec = _decay(gc)
            qk = jnp.where(ri >= ci, _bdot(q, k, NT) * dec, 0.0)
            d_qg = _bdot(d_o, s, NT)
            d_qk = jnp.where(ri >= ci, _bdot(d_o, vn, NT), 0.0)
            d_vn = _bdot(qk, d_o, TN) + _bdot(kd, ds_next)
            d_kd = _bdot(vn, ds_next, NT)
            d_eg = jnp.sum(s * ds_next)
            ds_ref[...] = ds_next * eg + _bdot(qg, d_o, TN) - _bdot(w, d_vn, TN)
            d_b = d_qk * dec
            e_q = d_qk * qk
            d_gl = jnp.sum(d_kd * kd) + d_eg * eg
            row = lax.broadcasted_iota(jnp.int32, (CHUNK, HEAD), 0)
            dq_ref[rows, :] = d_qg * egc + _bdot(d_b, k)
            dk_ref[rows, :] = d_kd * ekd + _bdot(d_b, q, TN)
            dgc_ref[rows, :] = (_rowsum(d_qg * qg) - _rowsum(d_kd * kd) + _rowsum(e_q) - _colsum([e_q])[0]
                                + jnp.where(row == CHUNK - 1, d_gl, 0.0))
            du_ref[rows, :] = d_vn
            dw_ref[rows, :] = -_bdot(d_vn, s, NT)

    rtile = lambda h, r: (n_r - 1 - r, h)
    vd = hv * HEAD
    return _call(body, (hv, n_r),
                 [(do, (tr, HEAD), rtile), (qkv, (tr, HEAD), lambda h, r: (n_r - 1 - r, h // 2)),
                  (qkv, (tr, HEAD), lambda h, r: (n_r - 1 - r, hk + h // 2)),
                  (w, (tr, HEAD), rtile), (gc, (tr, HEAD), rtile), (vn, (tr, HEAD), rtile),
                  (sall, (1, cpb, HEAD, HEAD), lambda h, r: (h, n_r - 1 - r, 0, 0))],
                 [((T, vd), F32, (tr, HEAD), rtile)] * 5,
                 scratch=[pltpu.VMEM((HEAD, HEAD), F32)], name="gdn_b_bwd", sem=("parallel", "arbitrary"))


def _gdn_a_bwd(du, dw, dgc_b, qkv, bb, gc, tinv, u, w, hk, hv, tr=1024):
    T = qkv.shape[0]
    tr = _tile(T, tr)
    assert tr % SUPER == 0

    def body(du_ref, dw_ref, dgcb_ref, k_ref, v_ref, b_ref, gc_ref, ti_ref, u_ref, w_ref, dk_ref, dv_ref, db_ref, dg_ref):
        ri, ci, same = _super_iota()
        utri = jnp.where(same & (ci >= ri), 1.0, 0.0)
        strict = same & (ri > ci)
        rows = [pl.ds(s * SUPER, SUPER) for s in range(tr // SUPER)]
        ks, vs, betas, gcs = ([ref[r, :] for r in rows] for ref in (k_ref, v_ref, b_ref, gc_ref))
        egcs = [jnp.exp(gc) for gc in gcs]
        kbs = [k * beta for k, beta in zip(ks, betas)]
        decs = [_decay_super(gc, ri, ci, same) for gc in gcs]
        ms = [jnp.where(strict, _bdot(kb, k, NT) * dec, 0.0) for kb, k, dec in zip(kbs, ks, decs)]
        d_rs = [_bdot(ti_ref[0, r, :], jnp.concatenate([du_ref[r, :], dw_ref[r, :]], axis=1), TN) for r in rows]
        d_ms = [jnp.where(strict, -_bdot(d_r, jnp.concatenate([u_ref[r, :], w_ref[r, :]], axis=1), NT), 0.0)
                for d_r, r in zip(d_rs, rows)]
        d_as = [d_m * dec for d_m, dec in zip(d_ms, decs)]
        e_ms = [d_m * m for d_m, m in zip(d_ms, ms)]
        d_kbs = [_bdot(d_a, k) + d_r[:, HEAD:] * egc for d_a, k, d_r, egc in zip(d_as, ks, d_rs, egcs)]
        dks = [_bdot(d_a, kb, TN) + d_kb * beta for d_a, kb, d_kb, beta in zip(d_as, kbs, d_kbs, betas)]
        cols = _colsum(e_ms)
        d_gcs = [_rowsum(e_m) - col + _rowsum(d_r[:, HEAD:] * kb * egc) + dgcb_ref[r, :]
                 for e_m, col, d_r, kb, egc, r in zip(e_ms, cols, d_rs, kbs, egcs, rows)]
        dgs = _sel_dot(utri, d_gcs)
        for r, dk, d_r, beta, v, d_kb, k, dg in zip(rows, dks, d_rs, betas, vs, d_kbs, ks, dgs):
            dk_ref[r, :] = dk
            dv_ref[r, :] = d_r[:, :HEAD] * beta
            db_ref[r, :] = _rowsum(d_r[:, :HEAD] * v) + _rowsum(d_kb * k)
            dg_ref[r, :] = dg

    tile = lambda h, r: (r, h)
    vd = hv * HEAD
    return _call(body, (hv, T // tr),
                 [(du, (tr, HEAD), tile), (dw, (tr, HEAD), tile), (dgc_b, (tr, HEAD), tile),
                  (qkv, (tr, HEAD), lambda h, r: (r, hk + h // 2)), (qkv, (tr, HEAD), lambda h, r: (r, 2 * hk + h)),
                  (bb, (tr, HEAD), tile), (gc, (tr, HEAD), tile), (tinv, (1, tr, SUPER), lambda h, r: (h, r, 0)),
                  (u, (tr, HEAD), tile), (w, (tr, HEAD), tile)],
                 [((T, vd), F32, (tr, HEAD), tile)] * 4, name="gdn_a_bwd", sem=("parallel", "parallel"))


def _pair_sum(a, b_, tr=512):
    T, vd = a.shape
    tr = _tile(T, tr)
    terms = [a] if b_ is None else [a, b_]
    n = len(terms)

    def body(*refs):
        acc = refs[0][...] + refs[1][...]
        for r in refs[2:2 * n]:
            acc = acc + r[...]
        refs[-1][...] = acc

    even = lambda j, r: (r, 2 * j)
    odd = lambda j, r: (r, 2 * j + 1)
    ins = [(t, (tr, HEAD), m) for t in terms for m in (even, odd)]
    return _call(body, (vd // HEAD // 2, T // tr), ins,
                 [((T, vd // 2), F32, (tr, HEAD), lambda j, r: (r, j))], name="gdn_pair_sum", sem=("parallel", "parallel"))[0]


def _s5_param_math(lr, li, ls, br, bi):
    step = jnp.exp(ls)
    zr, zi = lr * step, li * step
    mag = jnp.exp(zr)
    ar, ai = mag * jnp.cos(zi), mag * jnp.sin(zi)
    den = lr * lr + li * li
    nr, ni = ar - 1.0, ai
    cr, cim = (nr * lr + ni * li) / den, (ni * lr - nr * li) / den
    return ar, ai, br * cr - bi * cim, br * cim + bi * cr


def _s5_params_fwd(lr, li, ls, br, bi):
    G, P = lr.shape

    def body(lr_ref, li_ref, ls_ref, br_ref, bi_ref, ar_ref, ai_ref, bbr_ref, bbi_ref):
        ar, ai, bbr, bbi = _s5_param_math(lr_ref[...], li_ref[...], ls_ref[...], br_ref[...], bi_ref[...])
        ar_ref[...], ai_ref[...], bbr_ref[...], bbi_ref[...] = ar, ai, bbr, bbi

    shapes = [(G, P), (G, P), (G, 1), (S5_CH, G, P), (S5_CH, G, P)]
    z = lambda n: (lambda: (0,) * n)
    return _call(body, (), [(a, s, z(len(s))) for a, s in zip((lr, li, ls, br, bi), shapes)],
                 [(s, F32, s, z(len(s))) for s in (shapes[0], shapes[0], shapes[3], shapes[3])],
                 name="s5_params_fwd", sem=())


def _s5_params_bwd(lr, li, ls, br, bi, dar, dai, dbbr, dbbi):
    G, P = lr.shape

    def body(lr_ref, li_ref, ls_ref, br_ref, bi_ref, dar_ref, dai_ref, dbr_ref, dbi_ref, o0, o1, o2, o3, o4):
        _, vjp = jax.vjp(_s5_param_math, lr_ref[...], li_ref[...], ls_ref[...], br_ref[...], bi_ref[...])
        outs = vjp((dar_ref[...], dai_ref[...], dbr_ref[...], dbi_ref[...]))
        for r, v in zip((o0, o1, o2, o3, o4), outs):
            r[...] = v

    shapes = [(G, P), (G, P), (G, 1), (S5_CH, G, P), (S5_CH, G, P)]
    z = lambda n: (lambda: (0,) * n)
    ins = list(zip((lr, li, ls, br, bi), shapes)) + list(zip((dar, dai, dbbr, dbbi), (shapes[0], shapes[0], shapes[3], shapes[3])))
    return _call(body, (), [(a, s, z(len(s))) for a, s in ins], [(s, F32, s, z(len(s))) for s in shapes],
                 name="s5_params_bwd", sem=())


def _s5_bproj_fwd(u, bd_re, bd_im, tr=512):
    T, D = u.shape
    tr = _tile(T, tr)
    nb = D // LANES

    def body(u_ref, br_ref, bi_ref, or_ref, oi_ref):
        ub = u_ref[...]
        or_ref[...] = _bdot(ub, br_ref[0])
        oi_ref[...] = _bdot(ub, bi_ref[0])

    blk = lambda i, j: (j, 0, 0)
    return _call(body, (T // tr, nb),
                 [(u, (tr, LANES), lambda i, j: (i, j)), (bd_re, (1, LANES, S5_SPB), blk), (bd_im, (1, LANES, S5_SPB), blk)],
                 [((T, nb * S5_SPB), F32, (tr, S5_SPB), lambda i, j: (i, j))] * 2, name="s5_bproj_fwd", sem=("parallel", "parallel"))


def _s5_scan(br, bi, lam, reverse, xr=None, xi=None, tl=512, bw=256):
    T, NCH = br.shape
    tl, bw = _tile(T, tl), _tile(NCH, bw)
    n_t = T // tl
    l8 = tl // 8

    def body(*refs):
        if reverse:
            br_ref, bi_ref, lam_ref, sr_ref, si_ref, hr_ref, hi_ref, or_ref, oi_ref, dl_ref, pr, pi_, cr, ci_ = refs
        else:
            br_ref, bi_ref, lam_ref, or_ref, oi_ref, pr, pi_, cr, ci_ = refs
        t = pl.program_id(1)
        lr = lam_ref[0:1, :]
        li = -lam_ref[1:2, :] if reverse else lam_ref[1:2, :]
        row = lax.broadcasted_iota(jnp.int32, (tl, bw), 0)

        def scan(x_r, x_i):
            a_r, a_i = lr, li
            s = 1
            while s < tl:
                keep = (row < tl - s) if reverse else (row >= s)
                shift = tl - s if reverse else s
                s_r = jnp.where(keep, pltpu.roll(x_r, shift, 0), 0.0)
                s_i = jnp.where(keep, pltpu.roll(x_i, shift, 0), 0.0)
                x_r, x_i = x_r + a_r * s_r - a_i * s_i, x_i + a_r * s_i + a_i * s_r
                a_r, a_i = a_r * a_r - a_i * a_i, 2.0 * a_r * a_i
                s *= 2
            return x_r, x_i

        edge = tl - 1 if reverse else 0

        @pl.when(t == 0)
        def _():
            p_r, p_i = scan(jnp.where(row == edge, lr, 0.0), jnp.where(row == edge, li, 0.0))
            pr[...], pi_[...] = p_r, p_i
            cr[...] = jnp.zeros_like(cr)
            ci_[...] = jnp.zeros_like(ci_)

        x_r, x_i = scan(br_ref[...], bi_ref[...])
        c_r, c_i = cr[0:1, :], ci_[0:1, :]
        p_r, p_i = pr[...], pi_[...]
        x_r, x_i = x_r + p_r * c_r - p_i * c_i, x_i + p_r * c_i + p_i * c_r
        or_ref[...], oi_ref[...] = x_r, x_i
        last = tl - 1 - edge
        cr[0:1, :] = x_r[last:last + 1, :]
        ci_[0:1, :] = x_i[last:last + 1, :]
        if reverse:
            first_block = t == n_t - 1
            h_r = jnp.where(first_block, 0.0, hr_ref[7:8, :])
            h_i = jnp.where(first_block, 0.0, hi_ref[7:8, :])
            s_r = jnp.where(row == 0, h_r, pltpu.roll(sr_ref[...], 1, 0))
            s_i = jnp.where(row == 0, h_i, pltpu.roll(si_ref[...], 1, 0))

            @pl.when(t == 0)
            def _():
                dl_ref[...] = jnp.zeros_like(dl_ref)

            dl_ref[0:1, :] += jnp.sum(s_r * x_r + s_i * x_i, axis=0, keepdims=True)
            dl_ref[1:2, :] += jnp.sum(s_r * x_i - s_i * x_r, axis=0, keepdims=True)

    tmap = (lambda c, t: (n_t - 1 - t, c)) if reverse else (lambda c, t: (t, c))
    col = lambda c, t: (0, c)
    ins = [(br, (tl, bw), tmap), (bi, (tl, bw), tmap), (lam, (2, bw), col)]
    outs = [((T, NCH), F32, (tl, bw), tmap)] * 2
    if reverse:
        halo = lambda c, t: (jnp.maximum((n_t - 1 - t) * l8 - 1, 0), c)
        ins += [(xr, (tl, bw), tmap), (xi, (tl, bw), tmap), (xr, (8, bw), halo), (xi, (8, bw), halo)]
        outs += [((2, NCH), F32, (2, bw), col)]
    return _call(body, (NCH // bw, n_t), ins, outs,
                 scratch=[pltpu.VMEM((tl, bw), F32), pltpu.VMEM((tl, bw), F32), pltpu.VMEM((8, bw), F32), pltpu.VMEM((8, bw), F32)],
                 name="s5_scan_bwd" if reverse else "s5_scan_fwd", sem=("parallel", "arbitrary"))


def _s5_cproj_fwd(xr, xi, cd_re, cd_im, u, d, tr=512):
    T, D = u.shape
    tr = _tile(T, tr)

    def body(xr_ref, xi_ref, cr_ref, ci_ref, u_ref, d_ref, y_ref, h_ref):
        y = _bdot(xr_ref[...], cr_ref[0]) + _bdot(xi_ref[...], ci_ref[0]) + d_ref[...] * u_ref[...]
        y_ref[...] = y
        h_ref[...] = jax.nn.gelu(y).astype(BF16)

    tile = lambda i, j: (i, j)
    blk = lambda i, j: (j, 0, 0)
    return _call(body, (T // tr, D // LANES),
                 [(xr, (tr, S5_SPB), tile), (xi, (tr, S5_SPB), tile), (cd_re, (1, S5_SPB, LANES), blk), (cd_im, (1, S5_SPB, LANES), blk),
                  (u, (tr, LANES), tile), (d.reshape(1, D), (1, LANES), lambda i, j: (0, j))],
                 [((T, D), F32, (tr, LANES), tile), ((T, D), BF16, (tr, LANES), tile)], name="s5_cproj_fwd", sem=("parallel", "parallel"))


def _s5_cproj_bwd(dy, xr, xi, cd_re, cd_im, u, d, tr=512):
    T, D = u.shape
    tr = _tile(T, tr)
    nb = D // LANES

    def body(dy_ref, xr_ref, xi_ref, cr_ref, ci_ref, u_ref, d_ref, dxr_ref, dxi_ref, du_ref, dd_ref, dcr_ref, dci_ref):
        g = dy_ref[...]
        dxr_ref[...] = _bdot(g, cr_ref[0], NT)
        dxi_ref[...] = _bdot(g, ci_ref[0], NT)
        du_ref[...] = g * d_ref[...]
        first = pl.program_id(1) == 0
        _accumulate(dd_ref, jnp.sum(g * u_ref[...], axis=0, keepdims=True), first)

        @pl.when(first)
        def _():
            dcr_ref[...] = jnp.zeros_like(dcr_ref)
            dci_ref[...] = jnp.zeros_like(dci_ref)

        dcr_ref[0] += _bdot(xr_ref[...], g, TN)
        dci_ref[0] += _bdot(xi_ref[...], g, TN)

    tile = lambda j, i: (i, j)
    blk = lambda j, i: (j, 0, 0)
    col = lambda j, i: (0, j)
    return _call(body, (nb, T // tr),
                 [(dy, (tr, LANES), tile), (xr, (tr, S5_SPB), tile), (xi, (tr, S5_SPB), tile),
                  (cd_re, (1, S5_SPB, LANES), blk), (cd_im, (1, S5_SPB, LANES), blk), (u, (tr, LANES), tile), (d.reshape(1, D), (1, LANES), col)],
                 [((T, nb * S5_SPB), F32, (tr, S5_SPB), tile)] * 2 + [((T, D), F32, (tr, LANES), tile), ((1, D), F32, (1, LANES), col)]
                 + [((nb, S5_SPB, LANES), F32, (1, S5_SPB, LANES), blk)] * 2, name="s5_cproj_bwd")


def _s5_bproj_bwd(dbr, dbi, bd_re, bd_im, u, du_skip, tr=512):
    T, D = u.shape
    tr = _tile(T, tr)
    nb = D // LANES

    def body(gr_ref, gi_ref, br_ref, bi_ref, u_ref, ds_ref, du_ref, dbr_ref, dbi_ref):
        g_r, g_i, ub = gr_ref[...], gi_ref[...], u_ref[...]
        du_ref[...] = (ds_ref[...] + _bdot(g_r, br_ref[0], NT) + _bdot(g_i, bi_ref[0], NT)).astype(BF16)

        @pl.when(pl.program_id(1) == 0)
        def _():
            dbr_ref[...] = jnp.zeros_like(dbr_ref)
            dbi_ref[...] = jnp.zeros_like(dbi_ref)

        dbr_ref[0] += _bdot(ub, g_r, TN)
        dbi_ref[0] += _bdot(ub, g_i, TN)

    tile = lambda j, i: (i, j)
    blk = lambda j, i: (j, 0, 0)
    return _call(body, (nb, T // tr),
                 [(dbr, (tr, S5_SPB), tile), (dbi, (tr, S5_SPB), tile), (bd_re, (1, LANES, S5_SPB), blk), (bd_im, (1, LANES, S5_SPB), blk),
                  (u, (tr, LANES), tile), (du_skip, (tr, LANES), tile)],
                 [((T, D), BF16, (tr, LANES), tile)] + [((nb, LANES, S5_SPB), F32, (1, LANES, S5_SPB), blk)] * 2, name="s5_bproj_bwd")


def _s5_gate_fwd(h, vg, tr=256):
    T, D = h.shape
    tr = _tile(T, tr)

    def body(h_ref, a_ref, b_ref, o_ref):
        o_ref[...] = h_ref[...] + a_ref[...] * jax.nn.sigmoid(b_ref[...])

    row = lambda i: (i, 0)
    return _call(body, (T // tr,), [(h, (tr, D), row), (vg, (tr, D), row), (vg, (tr, D), lambda i: (i, 1))],
                 [((T, D), F32, (tr, D), row)], name="s5_gate_fwd", sem=("parallel",))[0]


def _s5_gate_bwd(dh, vg, tr=256):
    T, D = dh.shape
    tr = _tile(T, tr)

    def body(d_ref, a_ref, b_ref, o_ref):
        d = d_ref[...]
        sig = jax.nn.sigmoid(b_ref[...])
        o_ref[:, :D] = (d * sig).astype(BF16)
        o_ref[:, D:] = (d * a_ref[...] * sig * (1.0 - sig)).astype(BF16)

    row = lambda i: (i, 0)
    return _call(body, (T // tr,), [(dh, (tr, D), row), (vg, (tr, D), row), (vg, (tr, D), lambda i: (i, 1))],
                 [((T, 2 * D), BF16, (tr, 2 * D), row)], name="s5_gate_bwd", sem=("parallel",))[0]


def _block_diag(w, transpose):
    g, a, b = w.shape
    if transpose:
        w = w.transpose(0, 2, 1)
        a, b = b, a
    eye = jnp.eye(S5_GPB, dtype=w.dtype)
    return jnp.einsum("jgab,gh->jgahb", w.reshape(g // S5_GPB, S5_GPB, a, b), eye).reshape(g // S5_GPB, S5_GPB * a, S5_GPB * b)


def _block_diag_extract(wd, a, b, transpose):
    if transpose:
        a, b = b, a
    nb = wd.shape[0]
    eye = jnp.eye(S5_GPB, dtype=wd.dtype)
    w = jnp.einsum("jgahb,gh->jgab", wd.reshape(nb, S5_GPB, a, S5_GPB, b), eye).reshape(nb * S5_GPB, a, b)
    return w.transpose(0, 2, 1) if transpose else w


def _mesh_position():
    return lax.axis_index("x"), lax.axis_index("y"), lax.axis_index("c")


def _my_index():
    x, y, c = _mesh_position()
    return 4 * x + 2 * y + c


def _hbm_call(body, arrays, out_shapes, n_sems, name):
    n = len(arrays)
    return pl.pallas_call(
        body, out_shape=[jax.ShapeDtypeStruct(s, d) for s, d in out_shapes],
        in_specs=[pl.BlockSpec(memory_space=pl.ANY)] * n, out_specs=[pl.BlockSpec(memory_space=pl.ANY)] * len(out_shapes),
        scratch_shapes=[pltpu.SemaphoreType.DMA((n_sems,)), pltpu.SemaphoreType.DMA((n_sems,)), pltpu.SemaphoreType.DMA((n,))],
        name=name)(*arrays)


def _all_gather(blocks, name):
    n = len(blocks)
    per = N_DEV - 1

    def body(*refs):
        x_refs, out_refs = refs[:n], refs[n:2 * n]
        send_sems, recv_sems, local_sems = refs[2 * n:]
        x, y, c = _mesh_position()
        me, sibling = (x, y, c), (x, y, 1 - c)
        chips = [(1 - x, y), (x, 1 - y), (1 - x, 1 - y)]

        def copy(a, k, blk, to, src=None):
            slot = out_refs[a].at[4 * blk[0] + 2 * blk[1] + blk[2]]
            return pltpu.make_async_remote_copy(
                src_ref=slot if src is None else src, dst_ref=slot, send_sem=send_sems.at[a * per + k],
                recv_sem=recv_sems.at[a * per + k], device_id=to, device_id_type=pl.DeviceIdType.MESH)

        mine = [pltpu.make_async_copy(x_refs[a], out_refs[a].at[4 * x + 2 * y + c], local_sems.at[a]) for a in range(n)]
        for cp in mine:
            cp.start()
        first = []
        for a in range(n):
            first.append(copy(a, 0, me, sibling, src=x_refs[a]))
            first += [copy(a, 1 + j, me, (*chip, c), src=x_refs[a]) for j, chip in enumerate(chips)]
        for cp in first:
            cp.start()
        passed = []
        for j, chip in enumerate(chips):
            for a in range(n):
                copy(a, 1 + j, (*chip, c), me).wait_recv()
                passed.append(copy(a, 4 + j, (*chip, c), sibling))
                passed[-1].start()
        for a in range(n):
            copy(a, 0, sibling, me).wait_recv()
            for j, chip in enumerate(chips):
                copy(a, 4 + j, (*chip, 1 - c), me).wait_recv()
        for cp in first + passed:
            cp.wait_send()
        for cp in mine:
            cp.wait()

    return _hbm_call(body, blocks, [((N_DEV,) + b.shape, b.dtype) for b in blocks], n * per, name)


def _all_to_all(parts, name):
    n = len(parts)
    per = N_DEV - 1

    def body(*refs):
        g_refs, out_refs = refs[:n], refs[n:2 * n]
        send_sems, recv_sems, local_sems = refs[2 * n:]
        x, y, c = _mesh_position()
        me = 4 * x + 2 * y + c
        mine = [pltpu.make_async_copy(g_refs[a].at[me], out_refs[a].at[me], local_sems.at[a]) for a in range(n)]
        for cp in mine:
            cp.start()
        copies = []
        for k in range(1, N_DEV):
            px = 1 - x if k & 4 else x
            py = 1 - y if k & 2 else y
            pc = 1 - c if k & 1 else c
            for a in range(n):
                copies.append(pltpu.make_async_remote_copy(
                    src_ref=g_refs[a].at[4 * px + 2 * py + pc], dst_ref=out_refs[a].at[me],
                    send_sem=send_sems.at[a * per + k - 1], recv_sem=recv_sems.at[a * per + k - 1],
                    device_id=(px, py, pc), device_id_type=pl.DeviceIdType.MESH))
        for cp in copies:
            cp.start()
        for cp in copies:
            cp.wait()
        for cp in mine:
            cp.wait()

    return _hbm_call(body, parts, [(p_.shape, p_.dtype) for p_ in parts], n * per, name)


def _adamw_math(g, w, m, v):
    nm = ADAM_B1 * m + (1.0 - ADAM_B1) * g
    nv = ADAM_B2 * v + (1.0 - ADAM_B2) * (g * g)
    c1 = 1.0 - ADAM_B1 ** ADAM_STEP
    c2 = 1.0 - ADAM_B2 ** ADAM_STEP
    return -ADAM_LR * ((nm / c1) / (jnp.sqrt(nv / c2) + ADAM_EPS) + ADAM_WD * w), nm, nv


def _adamw(parts, row0, w, m, v, name, window_n8=None):
    R, C = w.shape
    cw = parts.shape[2]
    cap = min(PACK_ROWS, 1 << (((PACK_ROWS * PACK_W) // cw).bit_length() - 1))
    tr = math.gcd(math.gcd(R, cap), row0 or R)
    assert R % tr == 0 and row0 % tr == 0

    def body(p_ref, w_ref, m_ref, v_ref, g_ref, d_ref, nm_ref, nv_ref):
        g = p_ref[0].astype(F32)
        for s in range(1, N_DEV):
            g = g + p_ref[s].astype(F32)
        if window_n8 is not None:
            off = (window_n8 * _my_index()) % LANES
            g = pltpu.roll(g, (cw - off) % cw, 1)[:, :C]
        d, nm, nv = _adamw_math(g, w_ref[...], m_ref[...], v_ref[...])
        g_ref[...], d_ref[...], nm_ref[...], nv_ref[...] = g, d, nm, nv

    row = lambda i: (i, 0)
    r0 = row0 // tr
    return _call(body, (R // tr,),
                 [(parts, (N_DEV, tr, cw), lambda i: (0, r0 + i, 0)), (w, (tr, C), row), (m, (tr, C), row), (v, (tr, C), row)],
                 [((R, C), F32, (tr, C), row)] * 4, name=name, sem=("parallel",))


def _window_geometry(n8):
    offs = [(d * n8) % LANES for d in range(N_DEV)]
    starts = [(d * n8) // LANES for d in range(N_DEV)]
    blocks = max(-(-(o + n8) // LANES) for o in offs)
    return starts, blocks, max(starts) + blocks


def _to_window(wpad, n8, tr=256):
    R, cw = wpad.shape
    tr = _tile(R, tr)

    def body(x_ref, o_ref):
        o_ref[...] = pltpu.roll(x_ref[...], (n8 * _my_index()) % LANES, 1).astype(BF16)

    row = lambda i: (i, 0)
    return _call(body, (R // tr,), [(wpad, (tr, cw), row)], [((R, cw), BF16, (tr, cw), row)], name="to_window", sem=("parallel",))[0]


def _from_windows(win, row0, rows, n8, tr=128):
    starts, blocks, total = _window_geometry(n8)
    cw = win.shape[2]
    tr = _tile(rows, tr)
    r0 = row0 // tr

    def body(w_ref, o_ref):
        o_ref[...] = jnp.zeros_like(o_ref)
        for d in range(N_DEV):
            cols = pl.ds(starts[d] * LANES, cw)
            o_ref[:, cols] = (o_ref[:, cols].astype(F32) + w_ref[d].astype(F32)).astype(BF16)

    return _call(body, (rows // tr,), [(win, (N_DEV, tr, cw), lambda i: (0, r0 + i, 0))],
                 [((rows, total * LANES), BF16, (tr, total * LANES), lambda i: (i, 0))], name="from_windows", sem=("parallel",))[0]


def _pack(flat_pieces, dtype, lead=()):
    cat = jnp.concatenate([p_.astype(dtype) for p_ in flat_pieces], axis=-1)
    n = cat.shape[-1]
    quantum = PACK_ROWS * PACK_W
    total = -(-n // quantum) * quantum
    cat = jnp.pad(cat, [(0, 0)] * len(lead) + [(0, total - n)])
    return cat.reshape(lead + (total // PACK_W, PACK_W)), n


REPLICATED = ("norm_mix", "norm_mlp", "norm_ple", "norm_final", "gdn_a_log", "gdn_dt_bias", "gdn_o_norm",
              "s5_lam_re", "s5_lam_im", "s5_log_step", "s5_b_re", "s5_b_im", "s5_c_re", "s5_c_im")
WEIGHTS = ("norm_mix", "norm_mlp", "norm_ple", "norm_final", "gdn_w_in", "gdn_conv_w", "gdn_a_log", "gdn_dt_bias",
           "gdn_o_norm", "gdn_w_out", "s5_w_in", "s5_lam_re", "s5_lam_im", "s5_log_step", "s5_b_re", "s5_b_im",
           "s5_c_re", "s5_c_im", "s5_d", "s5_w_out", "mlp_w_up", "mlp_w_down", "ple_w_proj", "ple_w_gate")
ROW_GROUP = ("mlp_w_down", "gdn_w_out", "s5_w_in", "ple_w_gate")
COL_SHARDED = ("mlp_w_up", "s5_w_out", "ple_w_proj")


def _rows2d(a):
    return a.reshape(-1, a.shape[-1])


def _misc_pack(conv, s5d):
    cw = conv.shape[-1]
    rows = jnp.concatenate([_rows2d(conv), jnp.pad(s5d, ((0, 0), (0, cw - s5d.shape[-1])))], axis=0)
    return jnp.pad(rows, ((0, -rows.shape[0] % 8), (0, 0)))


def _misc_unpack(a, conv_rows, s5d_shape):
    return a[:conv_rows], a[conv_rows:conv_rows + s5d_shape[0], :s5d_shape[1]]


def kernel(x, p, norm_mix, norm_mlp, norm_ple, norm_final, gdn_w_in, gdn_conv_w, gdn_a_log, gdn_dt_bias, gdn_o_norm, gdn_w_out, s5_w_in, s5_lam_re, s5_lam_im, s5_log_step, s5_b_re, s5_b_im, s5_c_re, s5_c_im, s5_d, s5_w_out, mlp_w_up, mlp_w_down, ple_w_proj, ple_w_gate, loss_target, m_norm_mix, m_norm_mlp, m_norm_ple, m_norm_final, m_gdn_w_in, m_gdn_conv_w, m_gdn_a_log, m_gdn_dt_bias, m_gdn_o_norm, m_gdn_w_out, m_s5_w_in, m_s5_lam_re, m_s5_lam_im, m_s5_log_step, m_s5_b_re, m_s5_b_im, m_s5_c_re, m_s5_c_im, m_s5_d, m_s5_w_out, m_mlp_w_up, m_mlp_w_down, m_ple_w_proj, m_ple_w_gate, v_norm_mix, v_norm_mlp, v_norm_ple, v_norm_final, v_gdn_w_in, v_gdn_conv_w, v_gdn_a_log, v_gdn_dt_bias, v_gdn_o_norm, v_gdn_w_out, v_s5_w_in, v_s5_lam_re, v_s5_lam_im, v_s5_log_step, v_s5_b_re, v_s5_b_im, v_s5_c_re, v_s5_c_im, v_s5_d, v_s5_w_out, v_mlp_w_up, v_mlp_w_down, v_ple_w_proj, v_ple_w_gate):
    args = dict(locals())
    w = {n: args[n] for n in WEIGHTS}
    mom = {n: args["m_" + n] for n in WEIGHTS}
    vel = {n: args["v_" + n] for n in WEIGHTS}
    depth = norm_mix.shape[0]
    T, D = x.shape[1], x.shape[2]
    hv = gdn_a_log.shape[1]
    vd = hv * HEAD
    n_gdn, n_s5 = gdn_w_in.shape[0], s5_w_in.shape[0]
    cw = gdn_conv_w.shape[2]
    cd = cw * N_DEV
    hk = (cd - vd) // (2 * HEAD)
    assert hv == 2 * hk and 2 * hv <= LANES and T % SUPER == 0 and SUPER % CHUNK == 0
    G, P = s5_lam_re.shape[1], s5_lam_re.shape[2]
    assert P == S5_STATE and G * S5_CH == D and G % S5_GPB == 0 and D // N_DEV <= cw
    n8 = gdn_w_in.shape[2]
    win_starts, win_blocks, win_total = _window_geometry(n8)
    cwin = win_blocks * LANES
    assert win_total * LANES == cd + vd + LANES

    row_off, off = {}, 0
    for n in ROW_GROUP:
        row_off[n] = off
        off += w[n].shape[0] * w[n].shape[1]
    gathered = _all_gather(
        [_to_window(jnp.pad(_rows2d(gdn_w_in), ((0, 0), (0, cwin - n8))), n8),
         jnp.concatenate([_rows2d(w[n]) for n in ROW_GROUP], axis=0).astype(BF16)]
        + [_rows2d(w[n]).astype(BF16) for n in COL_SHARDED] + [_misc_pack(gdn_conv_w, s5_d)], "gather_weights")
    g_win, g_row, g_misc = gathered[0], gathered[1], gathered[-1]
    g_col = dict(zip(COL_SHARDED, gathered[2:-1]))
    conv_full = g_misc[:, :n_gdn * 4].reshape(N_DEV, n_gdn, 4, cw).transpose(1, 2, 0, 3).reshape(n_gdn, 4, cd)
    s5d_full = g_misc[:, n_gdn * 4:n_gdn * 4 + n_s5, :D // N_DEV].transpose(1, 0, 2).reshape(n_s5, D)

    def weight(name, l):
        r = w[name].shape[1]
        if name in ROW_GROUP:
            return Sharded(g_row, 0, row_off[name] + l * r, r)
        return Sharded(g_col[name], 1, l * r, r)

    h = x[0]
    tgt = loss_target[0]
    grads = {n: [None] * w[n].shape[0] for n in WEIGHTS if n != "norm_final"}
    saved = []
    add = lambda acc, r: (r + acc,)

    for i in range(depth):
        j = i // 2
        sv = {"h0": h}
        hn = _rms_fwd(h, norm_mix[i])
        sv["hn"] = hn
        if i % 2 == 0:
            w_in = _from_windows(g_win, j * D, D, n8)
            pq = _mm(hn, w_in[:, :cd], name="gdn_in_qkv")[0]
            pz = _mm(hn, w_in[:, cd:cd + vd], name="gdn_in_z")[0]
            ba = _mm(hn, w_in[:, cd + vd:], name="gdn_in_ba")[0]
            qkv = _gdn_pre_fwd(pq, conv_full[j], hk)
            pv = jnp.pad(jnp.stack([gdn_a_log[j], gdn_dt_bias[j]]), ((0, 0), (hv, LANES - 2 * hv)))
            g2 = _gates_fwd(ba, pv, hv)
            bb = jnp.repeat(g2[:, :hv], HEAD, axis=1)
            gb = jnp.repeat(g2[:, hv:2 * hv], HEAD, axis=1)
            u, ww, gc, tinv = _gdn_a_fwd(qkv, gb, bb, hk, hv)
            o, vn, sall = _gdn_b_fwd(qkv, u, ww, gc, hk, hv)
            on = _ogate_fwd(o, pz, gdn_o_norm[j])
            h = _mm(on, weight("gdn_w_out", j), epi=add, extras=(h,), name="gdn_out")[0]
            sv.update(w_in=w_in, pq=pq, pz=pz, ba=ba, pv=pv, qkv=qkv, bb=bb, u=u, ww=ww, gc=gc, tinv=tinv, o=o, vn=vn, sall=sall, on=on)
        else:
            uu = _mm(hn, weight("s5_w_in", j), name="s5_in")[0]
            b_re_t, b_im_t = s5_b_re[j].transpose(2, 0, 1), s5_b_im[j].transpose(2, 0, 1)
            ls = s5_log_step[j].reshape(G, 1)
            ar, ai, bbr, bbi = _s5_params_fwd(s5_lam_re[j], s5_lam_im[j], ls, b_re_t, b_im_t)
            lam = jnp.stack([ar.reshape(-1), ai.reshape(-1)])
            bd_re = _block_diag(bbr.transpose(1, 2, 0), transpose=True).astype(BF16)
            bd_im = _block_diag(bbi.transpose(1, 2, 0), transpose=True).astype(BF16)
            cd_re = _block_diag(s5_c_re[j], transpose=True).astype(BF16)
            cd_im = _block_diag(-s5_c_im[j], transpose=True).astype(BF16)
            bur, bui = _s5_bproj_fwd(uu, bd_re, bd_im)
            xr, xi = _s5_scan(bur, bui, lam, reverse=False)
            dsk = s5d_full[j]
            yy, hact = _s5_cproj_fwd(xr, xi, cd_re, cd_im, uu, dsk)
            vg = _mm(hact, weight("s5_w_out", j), name="s5_out")[0]
            h = _s5_gate_fwd(h, vg)
            sv.update(uu=uu, b_re_t=b_re_t, b_im_t=b_im_t, ls=ls, lam=lam, bd_re=bd_re, bd_im=bd_im, cd_re=cd_re, cd_im=cd_im,
                      xr=xr, xi=xi, dsk=dsk, yy=yy, hact=hact, vg=vg)
        sv["h1"] = h
        hm = _rms_fwd(h, norm_mlp[i])
        up, act = _mm(hm, weight("mlp_w_up", i), out_dtypes=(F32, BF16),
                      epi=lambda acc: (acc, jnp.square(jnp.maximum(acc, 0.0))), name="mlp_up")
        h = _mm(act, weight("mlp_w_down", i), epi=add, extras=(h,), name="mlp_down")[0]
        sv.update(hm=hm, up=up, act=act, h2=h)
        hp = _rms_fwd(h, norm_ple[i])
        s_gate = _mm(hp, weight("ple_w_gate", i), name="ple_gate")[0]
        pp = _mm(p[i, 0], weight("ple_w_proj", i), name="ple_proj")[0]
        h = _ple_fwd(h, s_gate, pp)
        sv.update(hp=hp, s_gate=s_gate, pp=pp)
        saved.append(sv)

    dh, d_norm_final, loss_part = _loss_fwd_bwd(h, norm_final, tgt)
    loss = lax.psum(loss_part[0, 0], MESH_AXES)

    dw = lambda a, b_, axis, name: _mm(a, b_, "tn", out_dtypes=(BF16,), out_axis=axis, name=name)[0]
    for i in reversed(range(depth)):
        j = i // 2
        sv = saved[i]
        ds, dpp = _ple_bwd(dh, sv["s_gate"], sv["pp"])
        grads["ple_w_proj"][i] = dw(p[i, 0], dpp, 1, "ple_proj_dw")
        grads["ple_w_gate"][i] = dw(sv["hp"], ds, 0, "ple_gate_dw")
        d_hp = _mm(ds, weight("ple_w_gate", i), "nt", name="ple_gate_dx")[0]
        dh, dh_b, grads["norm_ple"][i] = _rms_bwd(d_hp, sv["h2"], norm_ple[i], dh)
        grads["mlp_w_down"][i] = dw(sv["act"], dh_b, 0, "mlp_down_dw")
        d_up = _mm(dh_b, weight("mlp_w_down", i), "nt", out_dtypes=(BF16,),
                   epi=lambda acc, up_: (acc * 2.0 * jnp.maximum(up_, 0.0),), extras=(sv["up"],), name="mlp_down_dx")[0]
        grads["mlp_w_up"][i] = dw(sv["hm"], d_up, 1, "mlp_up_dw")
        d_hm = _mm(d_up, weight("mlp_w_up", i), "nt", name="mlp_up_dx")[0]
        dh, dh_b, grads["norm_mlp"][i] = _rms_bwd(d_hm, sv["h1"], norm_mlp[i], dh)
        if i % 2 == 0:
            grads["gdn_w_out"][j] = dw(sv["on"], dh_b, 0, "gdn_out_dw")
            d_on = _mm(dh_b, weight("gdn_w_out", j), "nt", name="gdn_out_dx")[0]
            d_o, d_z, grads["gdn_o_norm"][j] = _ogate_bwd(d_on, sv["o"], sv["pz"], gdn_o_norm[j])
            dq_b, dk_b, dgc_b, d_u, d_w = _gdn_b_bwd(d_o, sv["qkv"], sv["ww"], sv["gc"], sv["vn"], sv["sall"], hk, hv)
            dk_a, d_v, d_bb, d_gb = _gdn_a_bwd(d_u, d_w, dgc_b, sv["qkv"], sv["bb"], sv["gc"], sv["tinv"], sv["u"], sv["ww"], hk, hv)
            d_qkv = jnp.concatenate([_pair_sum(dq_b, None), _pair_sum(dk_a, dk_b), d_v], axis=1)
            d_c, grads["gdn_conv_w"][j] = _gdn_pre_bwd(d_qkv, sv["pq"], conv_full[j], hk)
            d_pq = _gdn_conv_bwd(d_c, conv_full[j])
            d_g2 = jnp.pad(jnp.concatenate([d_bb[:, ::HEAD], d_gb[:, ::HEAD]], axis=1), ((0, 0), (0, LANES - 2 * hv)))
            d_ba, d_pv = _gates_bwd(d_g2, sv["ba"], sv["pv"], hv)
            grads["gdn_a_log"][j] = d_pv[0, hv:2 * hv]
            grads["gdn_dt_bias"][j] = d_pv[1, hv:2 * hv]
            hn, w_in = sv["hn"], sv["w_in"]
            dw_nat = jnp.concatenate([dw(hn, d_pq, None, "gdn_in_qkv_dw"), dw(hn, d_z, None, "gdn_in_z_dw"),
                                      dw(hn, d_ba, None, "gdn_in_ba_dw")], axis=1)
            grads["gdn_w_in"][j] = jnp.stack([dw_nat[:, s * LANES:s * LANES + cwin] for s in win_starts])
            d_hn = _mm(d_pq, w_in[:, :cd], "nt", name="gdn_in_qkv_dx")[0]
            d_hn = _mm(d_z, w_in[:, cd:cd + vd], "nt", epi=add, extras=(d_hn,), name="gdn_in_z_dx")[0]
            d_hn = _mm(d_ba, w_in[:, cd + vd:], "nt", epi=add, extras=(d_hn,), name="gdn_in_ba_dx")[0]
        else:
            d_vg = _s5_gate_bwd(dh, sv["vg"])
            grads["s5_w_out"][j] = dw(sv["hact"], d_vg, 1, "s5_out_dw")

            def gelu_bwd(acc, y_):
                _, vjp = jax.vjp(jax.nn.gelu, y_)
                return (vjp(acc)[0],)

            d_y = _mm(d_vg, weight("s5_w_out", j), "nt", epi=gelu_bwd, extras=(sv["yy"],), name="s5_out_dx")[0]
            d_xr, d_xi, du_skip, d_dsk, d_cdr, d_cdi = _s5_cproj_bwd(d_y, sv["xr"], sv["xi"], sv["cd_re"], sv["cd_im"], sv["uu"], sv["dsk"])
            grads["s5_d"][j] = d_dsk
            grads["s5_c_re"][j] = _block_diag_extract(d_cdr, S5_CH, S5_STATE, transpose=True)
            grads["s5_c_im"][j] = -_block_diag_extract(d_cdi, S5_CH, S5_STATE, transpose=True)
            d_bur, d_bui, d_lam = _s5_scan(d_xr, d_xi, sv["lam"], reverse=True, xr=sv["xr"], xi=sv["xi"])
            d_uu, d_bdr, d_bdi = _s5_bproj_bwd(d_bur, d_bui, sv["bd_re"], sv["bd_im"], sv["uu"], du_skip)
            d_bbr = _block_diag_extract(d_bdr, S5_STATE, S5_CH, transpose=True).transpose(2, 0, 1)
            d_bbi = _block_diag_extract(d_bdi, S5_STATE, S5_CH, transpose=True).transpose(2, 0, 1)
            d_lr, d_li, d_ls, d_br_t, d_bi_t = _s5_params_bwd(
                s5_lam_re[j], s5_lam_im[j], sv["ls"], sv["b_re_t"], sv["b_im_t"],
                d_lam[0].reshape(G, P), d_lam[1].reshape(G, P), d_bbr, d_bbi)
            grads["s5_lam_re"][j], grads["s5_lam_im"][j], grads["s5_log_step"][j] = d_lr, d_li, d_ls.reshape(G)
            grads["s5_b_re"][j], grads["s5_b_im"][j] = d_br_t.transpose(1, 2, 0), d_bi_t.transpose(1, 2, 0)
            grads["s5_w_in"][j] = dw(sv["hn"], d_uu, 0, "s5_in_dw")
            d_hn = _mm(d_uu, weight("s5_w_in", j), "nt", name="s5_in_dx")[0]
        dh, _, grads["norm_mix"][i] = _rms_bwd(d_hn, sv["h0"], norm_mix[i], dh)

    out = {}
    layers = lambda name: jnp.concatenate(grads[name], axis=1)
    d_conv = jnp.stack(grads["gdn_conv_w"]).reshape(n_gdn * 4, N_DEV, cw).transpose(1, 0, 2)
    d_s5d = jnp.stack([g.reshape(N_DEV, D // N_DEV) for g in grads["s5_d"]], axis=1)
    d_misc = jnp.concatenate([d_conv, jnp.pad(d_s5d, ((0, 0), (0, 0), (0, cw - D // N_DEV)))], axis=1)
    d_misc = jnp.pad(d_misc, ((0, 0), (0, -d_misc.shape[1] % 8), (0, 0)))
    recv = _all_to_all(
        [layers("gdn_w_in"), jnp.concatenate([layers(n) for n in ROW_GROUP], axis=1)] + [layers(n) for n in COL_SHARDED] + [d_misc],
        "exchange_grads")
    r_win, r_row, r_misc = recv[0], recv[1], recv[-1]
    r_col = dict(zip(COL_SHARDED, recv[2:-1]))

    def update(name, parts, row0, **kw):
        res = _adamw(parts, row0, _rows2d(w[name]), _rows2d(mom[name]), _rows2d(vel[name]), "adamw_" + name, **kw)
        out[name] = [r.reshape(w[name].shape) for r in res]

    update("gdn_w_in", r_win, 0, window_n8=n8)
    for n in ROW_GROUP:
        update(n, r_row, row_off[n])
    for n in COL_SHARDED:
        update(n, r_col[n], 0)
    res = _adamw(r_misc, 0, _misc_pack(gdn_conv_w, s5_d), _misc_pack(mom["gdn_conv_w"], mom["s5_d"]),
                 _misc_pack(vel["gdn_conv_w"], vel["s5_d"]), "adamw_misc")
    unpacked = [_misc_unpack(r, n_gdn * 4, s5_d.shape) for r in res]
    out["gdn_conv_w"] = [u_[0].reshape(gdn_conv_w.shape) for u_ in unpacked]
    out["s5_d"] = [u_[1] for u_ in unpacked]

    rep_g = {n: (d_norm_final[0] if n == "norm_final" else jnp.stack([g.reshape(w[n].shape[1:]) for g in grads[n]])) for n in REPLICATED}
    flat_r = lambda d: [d[n].reshape(-1) for n in REPLICATED]
    pg, _ = _pack(flat_r(rep_g), F32)
    parts_r = _all_gather([pg], "gather_small_grads")[0]
    pw, _ = _pack(flat_r(w), F32)
    pm, _ = _pack(flat_r(mom), F32)
    pvv, _ = _pack(flat_r(vel), F32)
    res = [r.reshape(-1) for r in _adamw(parts_r, 0, pw, pm, pvv, "adamw_replicated")]
    off = 0
    for name in REPLICATED:
        n = w[name].size
        out[name] = [res[k][off:off + n].reshape(w[name].shape) for k in range(4)]
        off += n

    grad_x = dh[None]
    return (loss, grad_x, *[out[n][0] for n in WEIGHTS], *[out[n][1] for n in WEIGHTS],
            *[out[n][2] for n in WEIGHTS], *[out[n][3] for n in WEIGHTS])
```

```python
import collections
import math

import jax
import jax.numpy as jnp
from jax import lax
from jax.experimental import pallas as pl
from jax.experimental.pallas import tpu as pltpu

F32, BF16 = jnp.float32, jnp.bfloat16
NN, NT, TN = ((1,), (0,)), ((1,), (1,)), ((0,), (0,))

N_DEV = 8
MESH_AXES = ("x", "y", "c")
LANES = 128
V7X_VMEM_BYTES = 64 * 1024 * 1024
VMEM_LIMIT = V7X_VMEM_BYTES - 8 * 1024 * 1024
CHUNK = 64
HEAD = 128
SUPER = 256
S5_CH = 16
S5_STATE = 64
S5_GPB = LANES // S5_CH
S5_SPB = S5_GPB * S5_STATE
NORM_EPS = 1e-6
L2_EPS = 1e-6
ADAM_LR, ADAM_B1, ADAM_B2, ADAM_EPS, ADAM_WD, ADAM_STEP = 0.001, 0.9, 0.999, 1e-08, 0.01, 10
PACK_W = 1024
PACK_ROWS = 256


def _dot(a, b, dims):
    return lax.dot_general(a, b, (dims, ((), ())), preferred_element_type=F32)


def _bdot(a, b, dims=NN):
    return _dot(a.astype(BF16), b.astype(BF16), dims)


def _call(body, grid, ins, outs, scratch=(), name=None, sem=None):
    res = pl.pallas_call(
        body,
        grid=grid,
        in_specs=[pl.BlockSpec(b, m) for _, b, m in ins],
        out_specs=[pl.BlockSpec(b, m) for _, _, b, m in outs],
        out_shape=[jax.ShapeDtypeStruct(s, d) for s, d, _, _ in outs],
        scratch_shapes=list(scratch),
        name=name,
        compiler_params=pltpu.CompilerParams(
            dimension_semantics=sem or ("arbitrary",) * len(grid), vmem_limit_bytes=VMEM_LIMIT),
    )(*[a for a, _, _ in ins])
    return res


def _tile(n, want):
    t = min(n, want)
    assert n % t == 0, (n, want)
    return t


def _accumulate(ref, val, first):
    @pl.when(first)
    def _():
        ref[...] = jnp.zeros_like(ref)
    ref[...] += val


class Sharded(collections.namedtuple("Sharded", "arr axis row0 rows")):
    @property
    def shape(self):
        c = self.arr.shape[2]
        return (self.rows, N_DEV * c) if self.axis == 1 else (N_DEV * self.rows, c)

    @property
    def units(self):
        return (math.gcd(self.rows, self.row0), self.arr.shape[2])


def _mm(a, b, mode="nn", out_dtypes=(F32,), epi=None, extras=(), name="mm", out_axis=None, tm=1024, tn=1024, tk=2048):
    sh = isinstance(b, Sharded)
    b_rows, b_cols = b.shape
    u_rows, u_cols = b.units if sh else b.shape
    if mode == "nn":
        (M, K), (K2, N), (uk, un) = a.shape, (b_rows, b_cols), (u_rows, u_cols)
    elif mode == "nt":
        (M, K), (N, K2), (un, uk) = a.shape, (b_rows, b_cols), (u_rows, u_cols)
    else:
        (K, M), (K2, N), (uk, un) = a.shape, (b_rows, b_cols), (u_rows, u_cols)
    assert K == K2, (a.shape, b.shape, mode)
    um = M
    if out_axis == 0:
        um = M // N_DEV
    elif out_axis == 1:
        un = N // N_DEV
    tm, tn = _tile(um, tm), _tile(un, tn)
    span = 1
    if sh and ((mode == "nn" and b.axis == 0 and uk == b.rows) or (mode == "nt" and b.axis == 1)) and uk < min(K, tk):
        span = min(K, tk) // uk
        assert N_DEV % span == 0
    tk = _tile(uk, tk)
    nk = K // (tk * span)
    a_spec = ((tk, tm), lambda i, j, k: (k, i)) if mode == "tn" else ((tm, tk * span), lambda i, j, k: (i, k))
    if not sh:
        b_arr = b
        b_spec = ((tn, tk), lambda i, j, k: (j, k)) if mode == "nt" else ((tk, tn), lambda i, j, k: (k, j))
    else:
        b_arr = b.arr
        tr_, tc_ = (tk, tn) if mode == "nn" else (tn, tk)
        r0, per_r, per_c = b.row0 // tr_, b.rows // tr_, b.arr.shape[2] // tc_
        assert b.row0 % tr_ == 0 and mode != "tn"
        if span > 1:
            place = (lambda r, c: (r, r0, c)) if b.axis == 0 else (lambda r, c: (c, r0 + r, 0))
        elif b.axis == 1:
            place = lambda r, c: (c // per_c, r0 + r, c % per_c)
        else:
            place = lambda r, c: (r // per_r, r0 + r % per_r, c)
        b_spec = ((span, tr_, tc_), (lambda i, j, k: place(k, j)) if mode == "nn" else (lambda i, j, k: place(j, k)))
    dims = {"nn": NN, "nt": NT, "tn": TN}[mode]
    n_ex, n_out = len(extras), len(out_dtypes)

    def body(*refs):
        a_ref, b_ref = refs[:2]
        ex = refs[2:2 + n_ex]
        outs = refs[2 + n_ex:2 + n_ex + n_out]

        def product():
            if not sh:
                return _bdot(a_ref[...], b_ref[...], dims)
            part = _bdot(a_ref[:, :tk], b_ref[0], dims)
            for s in range(1, span):
                part = part + _bdot(a_ref[:, s * tk:(s + 1) * tk], b_ref[s], dims)
            return part

        def finish(res):
            vals = epi(res, *[e[...] for e in ex]) if epi is not None else (res,)
            for r, v in zip(outs, vals):
                r[...] = v.astype(r.dtype)

        if nk == 1:
            finish(product())
            return
        acc = refs[-1]
        k = pl.program_id(2)

        @pl.when(k == 0)
        def _():
            acc[...] = jnp.zeros_like(acc)

        acc[...] += product()

        @pl.when(k == nk - 1)
        def _():
            finish(acc[...])

    tile = lambda i, j, k: (i, j)
    if out_axis is None:
        out_shape, out_block, out_map = (M, N), (tm, tn), tile
    elif out_axis == 0:
        per = um // tm
        out_shape, out_block, out_map = (N_DEV, um, N), (None, tm, tn), lambda i, j, k: (i // per, i % per, j)
    else:
        per = un // tn
        out_shape, out_block, out_map = (N_DEV, M, un), (None, tm, tn), lambda i, j, k: (j // per, i, j % per)
    return _call(
        body, (M // tm, N // tn, nk),
        [(a,) + a_spec, (b_arr,) + b_spec] + [(e, (tm, tn), tile) for e in extras],
        [(out_shape, d, out_block, out_map) for d in out_dtypes],
        scratch=[pltpu.VMEM((tm, tn), F32)] if nk > 1 else [], name=name,
        sem=("parallel", "parallel", "arbitrary"))


def _rms_fwd(h, g, tr=256):
    T, D = h.shape
    tr = _tile(T, tr)

    def body(h_ref, g_ref, o_ref):
        x = h_ref[...]
        r = lax.rsqrt(jnp.mean(x * x, axis=-1, keepdims=True) + NORM_EPS)
        o_ref[...] = (x * r * g_ref[...]).astype(BF16)

    row = lambda i: (i, 0)
    fix = lambda i: (0, 0)
    return _call(body, (T // tr,), [(h, (tr, D), row), (g.reshape(1, D), (1, D), fix)],
                 [((T, D), BF16, (tr, D), row)], name="rms_fwd", sem=("parallel",))[0]


def _rms_bwd_math(dy, x, g):
    r = lax.rsqrt(jnp.mean(x * x, axis=-1, keepdims=True) + NORM_EPS)
    xh = x * r
    dxh = dy * g
    dx = r * (dxh - xh * jnp.mean(dxh * xh, axis=-1, keepdims=True))
    dg = jnp.sum(dy * xh, axis=0, keepdims=True)
    return dx, dg


def _rms_bwd(dy, h, g, res, tr=256):
    T, D = h.shape
    tr = _tile(T, tr)

    def body(dy_ref, h_ref, g_ref, res_ref, dh_ref, dhb_ref, dg_ref):
        dx, dg = _rms_bwd_math(dy_ref[...], h_ref[...], g_ref[...])
        dh = res_ref[...] + dx
        dh_ref[...] = dh
        dhb_ref[...] = dh.astype(BF16)
        _accumulate(dg_ref, dg, pl.program_id(0) == 0)

    row = lambda i: (i, 0)
    fix = lambda i: (0, 0)
    return _call(body, (T // tr,),
                 [(dy, (tr, D), row), (h, (tr, D), row), (g.reshape(1, D), (1, D), fix), (res, (tr, D), row)],
                 [((T, D), F32, (tr, D), row), ((T, D), BF16, (tr, D), row), ((1, D), F32, (1, D), fix)], name="rms_bwd")


def _loss_fwd_bwd(h, g, tgt, tr=256):
    T, D = h.shape
    tr = _tile(T, tr)

    def body(h_ref, g_ref, t_ref, dh_ref, dg_ref, loss_ref):
        x, gg = h_ref[...], g_ref[...]
        r = lax.rsqrt(jnp.mean(x * x, axis=-1, keepdims=True) + NORM_EPS)
        diff = x * r * gg - t_ref[...]
        part = 0.5 * jnp.sum(jnp.mean(diff * diff, axis=-1, keepdims=True))
        dx, dg = _rms_bwd_math(diff * (1.0 / D), x, gg)
        dh_ref[...] = dx
        first = pl.program_id(0) == 0
        _accumulate(dg_ref, dg, first)
        _accumulate(loss_ref, jnp.full((1, LANES), part, F32), first)

    row = lambda i: (i, 0)
    fix = lambda i: (0, 0)
    return _call(body, (T // tr,),
                 [(h, (tr, D), row), (g.reshape(1, D), (1, D), fix), (tgt, (tr, D), row)],
                 [((T, D), F32, (tr, D), row), ((1, D), F32, (1, D), fix), ((1, LANES), F32, (1, LANES), fix)],
                 name="loss_fwd_bwd")


def _ple_fwd(h, s, pp, tr=256):
    T, D = h.shape
    tr = _tile(T, tr)

    def body(h_ref, s_ref, p_ref, o_ref):
        o_ref[...] = h_ref[...] + jax.nn.sigmoid(s_ref[...]) * p_ref[...]

    row = lambda i: (i, 0)
    return _call(body, (T // tr,), [(a, (tr, D), row) for a in (h, s, pp)],
                 [((T, D), F32, (tr, D), row)], name="ple_fwd", sem=("parallel",))[0]


def _ple_bwd(dh, s, pp, tr=256):
    T, D = dh.shape
    tr = _tile(T, tr)

    def body(dh_ref, s_ref, p_ref, ds_ref, dp_ref):
        d = dh_ref[...]
        gate = jax.nn.sigmoid(s_ref[...])
        ds_ref[...] = (d * p_ref[...] * gate * (1.0 - gate)).astype(BF16)
        dp_ref[...] = (d * gate).astype(BF16)

    row = lambda i: (i, 0)
    return _call(body, (T // tr,), [(a, (tr, D), row) for a in (dh, s, pp)],
                 [((T, D), BF16, (tr, D), row)] * 2, name="ple_bwd", sem=("parallel",))


def _conv_taps(xe, w, tr):
    c = w[3:4, :] * xe[8:, :]
    for j in range(3):
        c = c + w[j:j + 1, :] * pltpu.roll(xe, 3 - j, 0)[8:, :]
    return c


def _gdn_pre_fwd(pq, conv_w, hk, tr=1024):
    T, CD = pq.shape
    tr = _tile(T, tr)
    r8 = tr // 8

    def body(x_ref, halo_ref, w_ref, o_ref):
        j, r = pl.program_id(0), pl.program_id(1)
        halo = jnp.where(r > 0, halo_ref[...], 0.0)
        xe = jnp.concatenate([halo, x_ref[...]], axis=0)
        c = _conv_taps(xe, w_ref[...], tr)
        s = c * jax.nn.sigmoid(c)
        rn = lax.rsqrt(jnp.sum(s * s, axis=-1, keepdims=True) + L2_EPS)
        scale = jnp.where(j < hk, HEAD ** -0.5, 1.0)
        o_ref[...] = jnp.where(j < 2 * hk, s * rn * scale, s)

    tile = lambda j, r: (r, j)
    return _call(body, (CD // HEAD, T // tr),
                 [(pq, (tr, HEAD), tile), (pq, (8, HEAD), lambda j, r: (jnp.maximum(r * r8 - 1, 0), j)),
                  (conv_w, (4, HEAD), lambda j, r: (0, j))],
                 [((T, CD), F32, (tr, HEAD), tile)], name="gdn_pre_fwd", sem=("parallel", "parallel"))[0]


def _gdn_pre_bwd(dn, pq, conv_w, hk, tr=1024):
    T, CD = pq.shape
    tr = _tile(T, tr)
    r8 = tr // 8

    def body(dn_ref, x_ref, halo_ref, w_ref, dc_ref, dw_ref):
        j, r = pl.program_id(0), pl.program_id(1)
        halo = jnp.where(r > 0, halo_ref[...], 0.0)
        xe = jnp.concatenate([halo, x_ref[...]], axis=0)
        c = _conv_taps(xe, w_ref[...], tr)
        sig = jax.nn.sigmoid(c)
        s = c * sig
        rn = lax.rsqrt(jnp.sum(s * s, axis=-1, keepdims=True) + L2_EPS)
        scale = jnp.where(j < hk, HEAD ** -0.5, 1.0)
        d = dn_ref[...]
        y = s * rn
        dy = d * scale
        ds = jnp.where(j < 2 * hk, rn * (dy - y * jnp.sum(dy * y, axis=-1, keepdims=True)), d)
        dc = ds * sig * (1.0 + c * (1.0 - sig))
        dc_ref[...] = dc

        @pl.when(r == 0)
        def _():
            dw_ref[...] = jnp.zeros_like(dw_ref)

        for t in range(4):
            xs = xe[8:, :] if t == 3 else pltpu.roll(xe, 3 - t, 0)[8:, :]
            dw_ref[t:t + 1, :] += jnp.sum(dc * xs, axis=0, keepdims=True)

    tile = lambda j, r: (r, j)
    col = lambda j, r: (0, j)
    return _call(body, (CD // HEAD, T // tr),
                 [(dn, (tr, HEAD), tile), (pq, (tr, HEAD), tile),
                  (pq, (8, HEAD), lambda j, r: (jnp.maximum(r * r8 - 1, 0), j)), (conv_w, (4, HEAD), col)],
                 [((T, CD), F32, (tr, HEAD), tile), ((4, CD), F32, (4, HEAD), col)], name="gdn_pre_bwd")


def _gdn_conv_bwd(dc, conv_w, tr=1024):
    T, CD = dc.shape
    tr = _tile(T, tr)
    r8 = tr // 8
    n_r = T // tr

    def body(dc_ref, halo_ref, w_ref, dx_ref):
        r = pl.program_id(1)
        halo = jnp.where(r < n_r - 1, halo_ref[...], 0.0)
        de = jnp.concatenate([dc_ref[...], halo], axis=0)
        w = w_ref[...]
        dx = w[3:4, :] * de[:tr, :]
        for j in range(3):
            dx = dx + w[j:j + 1, :] * pltpu.roll(de, tr + 8 - (3 - j), 0)[:tr, :]
        dx_ref[...] = dx.astype(BF16)

    tile = lambda j, r: (r, j)
    return _call(body, (CD // HEAD, n_r),
                 [(dc, (tr, HEAD), tile), (dc, (8, HEAD), lambda j, r: (jnp.minimum((r + 1) * r8, T // 8 - 1), j)),
                  (conv_w, (4, HEAD), lambda j, r: (0, j))],
                 [((T, CD), BF16, (tr, HEAD), tile)], name="gdn_conv_bwd", sem=("parallel", "parallel"))[0]


def _gates_fwd(ba, pv, hv, tr=1024):
    T = ba.shape[0]
    tr = _tile(T, tr)

    def body(x_ref, pv_ref, o_ref):
        x = x_ref[...]
        lane = lax.broadcasted_iota(jnp.int32, x.shape, 1)
        g = -jnp.exp(pv_ref[0:1, :]) * jax.nn.softplus(x + pv_ref[1:2, :])
        o_ref[...] = jnp.where(lane < hv, jax.nn.sigmoid(x), jnp.where(lane < 2 * hv, g, 0.0))

    row = lambda i: (i, 0)
    return _call(body, (T // tr,), [(ba, (tr, LANES), row), (pv, (2, LANES), lambda i: (0, 0))],
                 [((T, LANES), F32, (tr, LANES), row)], name="gates_fwd", sem=("parallel",))[0]


def _gates_bwd(dg2, ba, pv, hv, tr=1024):
    T = ba.shape[0]
    tr = _tile(T, tr)

    def body(d_ref, x_ref, pv_ref, dx_ref, dpv_ref):
        x, d = x_ref[...], d_ref[...]
        lane = lax.broadcasted_iota(jnp.int32, x.shape, 1)
        is_a = (lane >= hv) & (lane < 2 * hv)
        beta = jax.nn.sigmoid(x)
        neg_a = -jnp.exp(pv_ref[0:1, :])
        z = x + pv_ref[1:2, :]
        da = d * neg_a * jax.nn.sigmoid(z)
        dx_ref[...] = jnp.where(lane < hv, d * beta * (1.0 - beta), jnp.where(is_a, da, 0.0))
        first = pl.program_id(0) == 0

        @pl.when(first)
        def _():
            dpv_ref[...] = jnp.zeros_like(dpv_ref)

        dpv_ref[0:1, :] += jnp.sum(jnp.where(is_a, d * neg_a * jax.nn.softplus(z), 0.0), axis=0, keepdims=True)
        dpv_ref[1:2, :] += jnp.sum(jnp.where(is_a, da, 0.0), axis=0, keepdims=True)

    row = lambda i: (i, 0)
    fix = lambda i: (0, 0)
    return _call(body, (T // tr,), [(dg2, (tr, LANES), row), (ba, (tr, LANES), row), (pv, (2, LANES), fix)],
                 [((T, LANES), F32, (tr, LANES), row), ((2, LANES), F32, (2, LANES), fix)], name="gates_bwd")


def _ogate_fwd(o, z, o_norm, tr=512):
    T, VD = o.shape
    tr = _tile(T, tr)

    def body(o_ref, z_ref, g_ref, y_ref):
        x, zz = o_ref[...], z_ref[...]
        r = lax.rsqrt(jnp.mean(x * x, axis=-1, keepdims=True) + NORM_EPS)
        y_ref[...] = (x * r * g_ref[...] * (zz * jax.nn.sigmoid(zz))).astype(BF16)

    tile = lambda h, r: (r, h)
    return _call(body, (VD // HEAD, T // tr),
                 [(o, (tr, HEAD), tile), (z, (tr, HEAD), tile), (o_norm.reshape(1, HEAD), (1, HEAD), lambda h, r: (0, 0))],
                 [((T, VD), BF16, (tr, HEAD), tile)], name="ogate_fwd", sem=("parallel", "parallel"))[0]


def _ogate_bwd(dy, o, z, o_norm, tr=512):
    T, VD = o.shape
    tr = _tile(T, tr)

    def body(dy_ref, o_ref, z_ref, g_ref, do_ref, dz_ref, dg_ref):
        d, x, zz, g = dy_ref[...], o_ref[...], z_ref[...], g_ref[...]
        sig = jax.nn.sigmoid(zz)
        silu = zz * sig
        dx, dg = _rms_bwd_math(d * silu, x, g)
        r = lax.rsqrt(jnp.mean(x * x, axis=-1, keepdims=True) + NORM_EPS)
        do_ref[...] = dx
        dz_ref[...] = (d * (x * r * g) * sig * (1.0 + zz * (1.0 - sig))).astype(BF16)
        _accumulate(dg_ref, dg, (pl.program_id(0) == 0) & (pl.program_id(1) == 0))

    tile = lambda h, r: (r, h)
    fix = lambda h, r: (0, 0)
    return _call(body, (VD // HEAD, T // tr),
                 [(dy, (tr, HEAD), tile), (o, (tr, HEAD), tile), (z, (tr, HEAD), tile), (o_norm.reshape(1, HEAD), (1, HEAD), fix)],
                 [((T, VD), F32, (tr, HEAD), tile), ((T, VD), BF16, (tr, HEAD), tile), ((1, HEAD), F32, (1, HEAD), fix)],
                 name="ogate_bwd")


def _chunk_iota():
    return (lax.broadcasted_iota(jnp.int32, (CHUNK, CHUNK), 0), lax.broadcasted_iota(jnp.int32, (CHUNK, CHUNK), 1))


def _decay(gc):
    ri, ci = _chunk_iota()
    gcol = gc[:, :CHUNK]
    grow = jnp.sum(jnp.where(ri == ci, gcol, 0.0), axis=0, keepdims=True)
    return jnp.where(ri >= ci, jnp.exp(jnp.minimum(gcol - grow, 0.0)), 0.0)


def _rowsum(x):
    return jnp.broadcast_to(jnp.sum(x, axis=1, keepdims=True), (x.shape[0], HEAD))


def _split3(x):
    h1 = x.astype(BF16)
    r1 = x - h1.astype(F32)
    h2 = r1.astype(BF16)
    return h1, h2, (r1 - h2.astype(F32)).astype(BF16)


def _sel_dot(sel, xs, dims=NN):
    s = sel.astype(BF16)
    parts = [_split3(x) for x in xs]
    if dims == NN:
        return [_dot(s, h1, NN) + _dot(s, h2, NN) + _dot(s, h3, NN) for h1, h2, h3 in parts]
    return [_dot(h1, s, dims) + _dot(h2, s, dims) + _dot(h3, s, dims) for h1, h2, h3 in parts]


def _colsum(es):
    return _sel_dot(jnp.ones((es[0].shape[0], HEAD), F32), es, TN)


def _super_iota():
    ri = lax.broadcasted_iota(jnp.int32, (SUPER, SUPER), 0)
    ci = lax.broadcasted_iota(jnp.int32, (SUPER, SUPER), 1)
    shift = int(math.log2(CHUNK))
    return ri, ci, jnp.right_shift(ri, shift) == jnp.right_shift(ci, shift)


def _decay_super(gc, ri, ci, same):
    gcol = jnp.concatenate([gc] * (SUPER // HEAD), axis=1)
    grow = jnp.sum(jnp.where(ri == ci, gcol, 0.0), axis=0, keepdims=True)
    return jnp.where(same & (ri >= ci), jnp.exp(jnp.minimum(gcol - grow, 0.0)), 0.0)


def _unit_lower_inverse(ms, eye):
    ps = [-m for m in ms]
    xs = [eye + p for p in ps]
    for _ in range(int(math.log2(CHUNK)) - 1):
        ps = [_bdot(p, p) for p in ps]
        xs = [x + _bdot(x, p) for x, p in zip(xs, ps)]
    resid = []
    for m, x in zip(ms, xs):
        (m1, m2, _), (x1, x2, _) = _split3(m), _split3(x)
        resid.append((eye - x) - (_dot(m1, x1, NN) + _dot(m1, x2, NN) + _dot(m2, x1, NN)))
    return [x + _bdot(x, r) for x, r in zip(xs, resid)]


def _gdn_a_fwd(qkv, gb, bb, hk, hv, tr=1024):
    T = qkv.shape[0]
    tr = _tile(T, tr)
    assert tr % SUPER == 0

    def body(k_ref, v_ref, g_ref, b_ref, u_ref, w_ref, gc_ref, ti_ref):
        ri, ci, same = _super_iota()
        ltri = jnp.where(same & (ri >= ci), 1.0, 0.0)
        eye = jnp.where(ri == ci, 1.0, 0.0)
        rows = [pl.ds(s * SUPER, SUPER) for s in range(tr // SUPER)]
        ks, vs, betas = [k_ref[r, :] for r in rows], [v_ref[r, :] for r in rows], [b_ref[r, :] for r in rows]
        gcs = _sel_dot(ltri, [g_ref[r, :] for r in rows])
        kbs = [k * beta for k, beta in zip(ks, betas)]
        ms = [jnp.where(same & (ri > ci), _bdot(kb, k, NT) * _decay_super(gc, ri, ci, same), 0.0)
              for kb, k, gc in zip(kbs, ks, gcs)]
        tinvs = _unit_lower_inverse(ms, eye)
        xs = [_bdot(tinv, jnp.concatenate([v * beta, kb * jnp.exp(gc)], axis=1))
              for tinv, v, beta, kb, gc in zip(tinvs, vs, betas, kbs, gcs)]
        for r, x, gc, tinv in zip(rows, xs, gcs, tinvs):
            u_ref[r, :] = x[:, :HEAD]
            w_ref[r, :] = x[:, HEAD:]
            gc_ref[r, :] = gc
            ti_ref[0, r, :] = tinv.astype(BF16)

    tile = lambda h, r: (r, h)
    vd = hv * HEAD
    return _call(body, (hv, T // tr),
                 [(qkv, (tr, HEAD), lambda h, r: (r, hk + h // 2)), (qkv, (tr, HEAD), lambda h, r: (r, 2 * hk + h)),
                  (gb, (tr, HEAD), tile), (bb, (tr, HEAD), tile)],
                 [((T, vd), F32, (tr, HEAD), tile)] * 3 + [((hv, T, SUPER), BF16, (1, tr, SUPER), lambda h, r: (h, r, 0))],
                 name="gdn_a_fwd", sem=("parallel", "parallel"))


def _gdn_b_fwd(qkv, u, w, gc, hk, hv, tr=512):
    T = qkv.shape[0]
    tr = _tile(T, tr)
    cpb = tr // CHUNK

    def body(q_ref, k_ref, u_ref, w_ref, gc_ref, o_ref, vn_ref, sall_ref, s_ref):
        ri, ci = _chunk_iota()

        @pl.when(pl.program_id(1) == 0)
        def _():
            s_ref[...] = jnp.zeros_like(s_ref)

        for c in range(cpb):
            rows = pl.ds(c * CHUNK, CHUNK)
            q, k, gc = q_ref[rows, :], k_ref[rows, :], gc_ref[rows, :]
            gl = gc[CHUNK - 1:CHUNK, :]
            qk = jnp.where(ri >= ci, _bdot(q, k, NT) * _decay(gc), 0.0)
            s = s_ref[...]
            sall_ref[0, c] = s
            vn = u_ref[rows, :] - _bdot(w_ref[rows, :], s)
            o_ref[rows, :] = _bdot(q * jnp.exp(gc), s) + _bdot(qk, vn)
            vn_ref[rows, :] = vn
            s_ref[...] = s * jnp.exp(gl) + _bdot(k * jnp.exp(gl - gc), vn, TN)

    tile = lambda h, r: (r, h)
    vd = hv * HEAD
    return _call(body, (hv, T // tr),
                 [(qkv, (tr, HEAD), lambda h, r: (r, h // 2)), (qkv, (tr, HEAD), lambda h, r: (r, hk + h // 2)),
                  (u, (tr, HEAD), tile), (w, (tr, HEAD), tile), (gc, (tr, HEAD), tile)],
                 [((T, vd), F32, (tr, HEAD), tile)] * 2 +
                 [((hv, T // CHUNK, HEAD, HEAD), F32, (1, cpb, HEAD, HEAD), lambda h, r: (h, r, 0, 0))],
                 scratch=[pltpu.VMEM((HEAD, HEAD), F32)], name="gdn_b_fwd", sem=("parallel", "arbitrary"))


def _gdn_b_bwd(do, qkv, w, gc, vn, sall, hk, hv, tr=512):
    T = qkv.shape[0]
    tr = _tile(T, tr)
    cpb = tr // CHUNK
    n_r = T // tr

    def body(do_ref, q_ref, k_ref, w_ref, gc_ref, vn_ref, sall_ref, dq_ref, dk_ref, dgc_ref, du_ref, dw_ref, ds_ref):
        ri, ci = _chunk_iota()

        @pl.when(pl.program_id(1) == 0)
        def _():
            ds_ref[...] = jnp.zeros_like(ds_ref)

        for c in reversed(range(cpb)):
            rows = pl.ds(c * CHUNK, CHUNK)
            d_o, q, k, w, gc, vn = do_ref[rows, :], q_ref[rows, :], k_ref[rows, :], w_ref[rows, :], gc_ref[rows, :], vn_ref[rows, :]
            s = sall_ref[0, c]
            ds_next = ds_ref[...]
            gl = gc[CHUNK - 1:CHUNK, :]
            egc, ekd, eg = jnp.exp(gc), jnp.exp(gl - gc), jnp.exp(gl)
            qg, kd = q * egc, k * ekd
            dec = _decay(gc)
            qk = jnp.where(ri >= ci, _bdot(q, k, NT) * dec, 0.0)
            d_qg = _bdot(d_o, s, NT)
            d_qk = jnp.where(ri >= ci, _bdot(d_o, vn, NT), 0.0)
            d_vn = _bdot(qk, d_o, TN) + _bdot(kd, ds_next)
            d_kd = _bdot(vn, ds_next, NT)
            d_eg = jnp.sum(s * ds_next)
            ds_ref[...] = ds_next * eg + _bdot(qg, d_o, TN) - _bdot(w, d_vn, TN)
            d_b = d_qk * dec
            e_q = d_qk * qk
            d_gl = jnp.sum(d_kd * kd) + d_eg * eg
            row = lax.broadcasted_iota(jnp.int32, (CHUNK, HEAD), 0)
            dq_ref[rows, :] = d_qg * egc + _bdot(d_b, k)
            dk_ref[rows, :] = d_kd * ekd + _bdot(d_b, q, TN)
            dgc_ref[rows, :] = (_rowsum(d_qg * qg) - _rowsum(d_kd * kd) + _rowsum(e_q) - _colsum([e_q])[0]
                                + jnp.where(row == CHUNK - 1, d_gl, 0.0))
            du_ref[rows, :] = d_vn
            dw_ref[rows, :] = -_bdot(d_vn, s, NT)

    rtile = lambda h, r: (n_r - 1 - r, h)
    vd = hv * HEAD
    return _call(body, (hv, n_r),
                 [(do, (tr, HEAD), rtile), (qkv, (tr, HEAD), lambda h, r: (n_r - 1 - r, h // 2)),
                  (qkv, (tr, HEAD), lambda h, r: (n_r - 1 - r, hk + h // 2)),
                  (w, (tr, HEAD), rtile), (gc, (tr, HEAD), rtile), (vn, (tr, HEAD), rtile),
                  (sall, (1, cpb, HEAD, HEAD), lambda h, r: (h, n_r - 1 - r, 0, 0))],
                 [((T, vd), F32, (tr, HEAD), rtile)] * 5,
                 scratch=[pltpu.VMEM((HEAD, HEAD), F32)], name="gdn_b_bwd", sem=("parallel", "arbitrary"))


def _gdn_a_bwd(du, dw, dgc_b, qkv, bb, gc, tinv, u, w, hk, hv, tr=1024):
    T = qkv.shape[0]
    tr = _tile(T, tr)
    assert tr % SUPER == 0

    def body(du_ref, dw_ref, dgcb_ref, k_ref, v_ref, b_ref, gc_ref, ti_ref, u_ref, w_ref, dk_ref, dv_ref, db_ref, dg_ref):
        ri, ci, same = _super_iota()
        utri = jnp.where(same & (ci >= ri), 1.0, 0.0)
        strict = same & (ri > ci)
        rows = [pl.ds(s * SUPER, SUPER) for s in range(tr // SUPER)]
        ks, vs, betas, gcs = ([ref[r, :] for r in rows] for ref in (k_ref, v_ref, b_ref, gc_ref))
        egcs = [jnp.exp(gc) for gc in gcs]
        kbs = [k * beta for k, beta in zip(ks, betas)]
        decs = [_decay_super(gc, ri, ci, same) for gc in gcs]
        ms = [jnp.where(strict, _bdot(kb, k, NT) * dec, 0.0) for kb, k, dec in zip(kbs, ks, decs)]
        d_rs = [_bdot(ti_ref[0, r, :], jnp.concatenate([du_ref[r, :], dw_ref[r, :]], axis=1), TN) for r in rows]
        d_ms = [jnp.where(strict, -_bdot(d_r, jnp.concatenate([u_ref[r, :], w_ref[r, :]], axis=1), NT), 0.0)
                for d_r, r in zip(d_rs, rows)]
        d_as = [d_m * dec for d_m, dec in zip(d_ms, decs)]
        e_ms = [d_m * m for d_m, m in zip(d_ms, ms)]
        d_kbs = [_bdot(d_a, k) + d_r[:, HEAD:] * egc for d_a, k, d_r, egc in zip(d_as, ks, d_rs, egcs)]
        dks = [_bdot(d_a, kb, TN) + d_kb * beta for d_a, kb, d_kb, beta in zip(d_as, kbs, d_kbs, betas)]
        cols = _colsum(e_ms)
        d_gcs = [_rowsum(e_m) - col + _rowsum(d_r[:, HEAD:] * kb * egc) + dgcb_ref[r, :]
                 for e_m, col, d_r, kb, egc, r in zip(e_ms, cols, d_rs, kbs, egcs, rows)]
        dgs = _sel_dot(utri, d_gcs)
        for r, dk, d_r, beta, v, d_kb, k, dg in zip(rows, dks, d_rs, betas, vs, d_kbs, ks, dgs):
            dk_ref[r, :] = dk
            dv_ref[r, :] = d_r[:, :HEAD] * beta
            db_ref[r, :] = _rowsum(d_r[:, :HEAD] * v) + _rowsum(d_kb * k)
            dg_ref[r, :] = dg

    tile = lambda h, r: (r, h)
    vd = hv * HEAD
    return _call(body, (hv, T // tr),
                 [(du, (tr, HEAD), tile), (dw, (tr, HEAD), tile), (dgc_b, (tr, HEAD), tile),
                  (qkv, (tr, HEAD), lambda h, r: (r, hk + h // 2)), (qkv, (tr, HEAD), lambda h, r: (r, 2 * hk + h)),
                  (bb, (tr, HEAD), tile), (gc, (tr, HEAD), tile), (tinv, (1, tr, SUPER), lambda h, r: (h, r, 0)),
                  (u, (tr, HEAD), tile), (w, (tr, HEAD), tile)],
                 [((T, vd), F32, (tr, HEAD), tile)] * 4, name="gdn_a_bwd", sem=("parallel", "parallel"))


def _pair_sum(a, b_, tr=512):
    T, vd = a.shape
    tr = _tile(T, tr)
    terms = [a] if b_ is None else [a, b_]
    n = len(terms)

    def body(*refs):
        acc = refs[0][...] + refs[1][...]
        for r in refs[2:2 * n]:
            acc = acc + r[...]
        refs[-1][...] = acc

    even = lambda j, r: (r, 2 * j)
    odd = lambda j, r: (r, 2 * j + 1)
    ins = [(t, (tr, HEAD), m) for t in terms for m in (even, odd)]
    return _call(body, (vd // HEAD // 2, T // tr), ins,
                 [((T, vd // 2), F32, (tr, HEAD), lambda j, r: (r, j))], name="gdn_pair_sum", sem=("parallel", "parallel"))[0]


def _s5_param_math(lr, li, ls, br, bi):
    step = jnp.exp(ls)
    zr, zi = lr * step, li * step
    mag = jnp.exp(zr)
    ar, ai = mag * jnp.cos(zi), mag * jnp.sin(zi)
    den = lr * lr + li * li
    nr, ni = ar - 1.0, ai
    cr, cim = (nr * lr + ni * li) / den, (ni * lr - nr * li) / den
    return ar, ai, br * cr - bi * cim, br * cim + bi * cr


def _s5_params_fwd(lr, li, ls, br, bi):
    G, P = lr.shape

    def body(lr_ref, li_ref, ls_ref, br_ref, bi_ref, ar_ref, ai_ref, bbr_ref, bbi_ref):
        ar, ai, bbr, bbi = _s5_param_math(lr_ref[...], li_ref[...], ls_ref[...], br_ref[...], bi_ref[...])
        ar_ref[...], ai_ref[...], bbr_ref[...], bbi_ref[...] = ar, ai, bbr, bbi

    shapes = [(G, P), (G, P), (G, 1), (S5_CH, G, P), (S5_CH, G, P)]
    z = lambda n: (lambda: (0,) * n)
    return _call(body, (), [(a, s, z(len(s))) for a, s in zip((lr, li, ls, br, bi), shapes)],
                 [(s, F32, s, z(len(s))) for s in (shapes[0], shapes[0], shapes[3], shapes[3])],
                 name="s5_params_fwd", sem=())


def _s5_params_bwd(lr, li, ls, br, bi, dar, dai, dbbr, dbbi):
    G, P = lr.shape

    def body(lr_ref, li_ref, ls_ref, br_ref, bi_ref, dar_ref, dai_ref, dbr_ref, dbi_ref, o0, o1, o2, o3, o4):
        _, vjp = jax.vjp(_s5_param_math, lr_ref[...], li_ref[...], ls_ref[...], br_ref[...], bi_ref[...])
        outs = vjp((dar_ref[...], dai_ref[...], dbr_ref[...], dbi_ref[...]))
        for r, v in zip((o0, o1, o2, o3, o4), outs):
            r[...] = v

    shapes = [(G, P), (G, P), (G, 1), (S5_CH, G, P), (S5_CH, G, P)]
    z = lambda n: (lambda: (0,) * n)
    ins = list(zip((lr, li, ls, br, bi), shapes)) + list(zip((dar, dai, dbbr, dbbi), (shapes[0], shapes[0], shapes[3], shapes[3])))
    return _call(body, (), [(a, s, z(len(s))) for a, s in ins], [(s, F32, s, z(len(s))) for s in shapes],
                 name="s5_params_bwd", sem=())


def _s5_bproj_fwd(u, bd_re, bd_im, tr=512):
    T, D = u.shape
    tr = _tile(T, tr)
    nb = D // LANES

    def body(u_ref, br_ref, bi_ref, or_ref, oi_ref):
        ub = u_ref[...]
        or_ref[...] = _bdot(ub, br_ref[0])
        oi_ref[...] = _bdot(ub, bi_ref[0])

    blk = lambda i, j: (j, 0, 0)
    return _call(body, (T // tr, nb),
                 [(u, (tr, LANES), lambda i, j: (i, j)), (bd_re, (1, LANES, S5_SPB), blk), (bd_im, (1, LANES, S5_SPB), blk)],
                 [((T, nb * S5_SPB), F32, (tr, S5_SPB), lambda i, j: (i, j))] * 2, name="s5_bproj_fwd", sem=("parallel", "parallel"))


def _s5_scan(br, bi, lam, reverse, xr=None, xi=None, tl=512, bw=512):
    T, NCH = br.shape
    tl, bw = _tile(T, tl), _tile(NCH, bw)
    n_t, n_g = T // tl, tl // 8

    def body(*refs):
        if reverse:
            br_ref, bi_ref, lam_ref, sr_ref, si_ref, hr_ref, hi_ref, or_ref, oi_ref, dl_ref, cr, ci_ = refs
        else:
            br_ref, bi_ref, lam_ref, or_ref, oi_ref, cr, ci_ = refs
        t = pl.program_id(1)
        a_r = lam_ref[0:1, :]
        a_i = -lam_ref[1:2, :] if reverse else lam_ref[1:2, :]
        powers = [(a_r, a_i)]
        for _ in range(2):
            p_r, p_i = powers[-1]
            powers.append((p_r * p_r - p_i * p_i, 2.0 * p_r * p_i))
        row = lax.broadcasted_iota(jnp.int32, (8, bw), 0)

        def scan8(x_r, x_i):
            for level, (p_r, p_i) in enumerate(powers):
                s = 1 << level
                keep = (row < 8 - s) if reverse else (row >= s)
                s_r = jnp.where(keep, pltpu.roll(x_r, 8 - s if reverse else s, 0), 0.0)
                s_i = jnp.where(keep, pltpu.roll(x_i, 8 - s if reverse else s, 0), 0.0)
                x_r, x_i = x_r + p_r * s_r - p_i * s_i, x_i + p_r * s_i + p_i * s_r
            return x_r, x_i

        edge = 7 if reverse else 0
        tab_r, tab_i = scan8(jnp.where(row == edge, a_r, 0.0), jnp.where(row == edge, a_i, 0.0))

        @pl.when(t == 0)
        def _():
            cr[...] = jnp.zeros_like(cr)
            ci_[...] = jnp.zeros_like(ci_)
            if reverse:
                dl_ref[...] = jnp.zeros_like(dl_ref)

        if reverse:
            first_block = t == n_t - 1
            halo_r = jnp.where(first_block, 0.0, hr_ref[7:8, :])
            halo_i = jnp.where(first_block, 0.0, hi_ref[7:8, :])

        def group(n, carry):
            g = n_g - 1 - n if reverse else n
            rows = pl.ds(pl.multiple_of(g * 8, 8), 8)
            c_r, c_i = carry[0], carry[1]
            x_r, x_i = scan8(br_ref[rows, :], bi_ref[rows, :])
            x_r, x_i = x_r + tab_r * c_r - tab_i * c_i, x_i + tab_r * c_i + tab_i * c_r
            or_ref[rows, :], oi_ref[rows, :] = x_r, x_i
            out = 7 - edge
            nxt = (x_r[out:out + 1, :], x_i[out:out + 1, :])
            if not reverse:
                return nxt
            before = pl.ds(pl.multiple_of(jnp.maximum(g * 8 - 8, 0), 8), 8)
            h_r = jnp.where(g > 0, sr_ref[before, :][7:8, :], halo_r)
            h_i = jnp.where(g > 0, si_ref[before, :][7:8, :], halo_i)
            s_r = jnp.where(row == 0, h_r, pltpu.roll(sr_ref[rows, :], 1, 0))
            s_i = jnp.where(row == 0, h_i, pltpu.roll(si_ref[rows, :], 1, 0))
            return nxt + (carry[2] + s_r * x_r + s_i * x_i, carry[3] + s_r * x_i - s_i * x_r)

        init = (cr[0:1, :], ci_[0:1, :])
        if reverse:
            init = init + (jnp.zeros((8, bw), F32), jnp.zeros((8, bw), F32))
        fin = lax.fori_loop(0, n_g, group, init, unroll=4 if n_g % 4 == 0 else 1)
        cr[0:1, :], ci_[0:1, :] = fin[0], fin[1]
        if reverse:
            dl_ref[0:1, :] += jnp.sum(fin[2], axis=0, keepdims=True)
            dl_ref[1:2, :] += jnp.sum(fin[3], axis=0, keepdims=True)

    tmap = (lambda c, t: (n_t - 1 - t, c)) if reverse else (lambda c, t: (t, c))
    col = lambda c, t: (0, c)
    ins = [(br, (tl, bw), tmap), (bi, (tl, bw), tmap), (lam, (2, bw), col)]
    outs = [((T, NCH), F32, (tl, bw), tmap)] * 2
    if reverse:
        halo = lambda c, t: (jnp.maximum((n_t - 1 - t) * n_g - 1, 0), c)
        ins += [(xr, (tl, bw), tmap), (xi, (tl, bw), tmap), (xr, (8, bw), halo), (xi, (8, bw), halo)]
        outs += [((2, NCH), F32, (2, bw), col)]
    return _call(body, (NCH // bw, n_t), ins, outs, scratch=[pltpu.VMEM((8, bw), F32), pltpu.VMEM((8, bw), F32)],
                 name="s5_scan_bwd" if reverse else "s5_scan_fwd", sem=("parallel", "arbitrary"))


def _s5_cproj_fwd(xr, xi, cd_re, cd_im, u, d, tr=512):
    T, D = u.shape
    tr = _tile(T, tr)

    def body(xr_ref, xi_ref, cr_ref, ci_ref, u_ref, d_ref, y_ref, h_ref):
        y = _bdot(xr_ref[...], cr_ref[0]) + _bdot(xi_ref[...], ci_ref[0]) + d_ref[...] * u_ref[...]
        y_ref[...] = y
        h_ref[...] = jax.nn.gelu(y).astype(BF16)

    tile = lambda i, j: (i, j)
    blk = lambda i, j: (j, 0, 0)
    return _call(body, (T // tr, D // LANES),
                 [(xr, (tr, S5_SPB), tile), (xi, (tr, S5_SPB), tile), (cd_re, (1, S5_SPB, LANES), blk), (cd_im, (1, S5_SPB, LANES), blk),
                  (u, (tr, LANES), tile), (d.reshape(1, D), (1, LANES), lambda i, j: (0, j))],
                 [((T, D), F32, (tr, LANES), tile), ((T, D), BF16, (tr, LANES), tile)], name="s5_cproj_fwd", sem=("parallel", "parallel"))


def _s5_cproj_bwd(dy, xr, xi, cd_re, cd_im, u, d, tr=512):
    T, D = u.shape
    tr = _tile(T, tr)
    nb = D // LANES

    def body(dy_ref, xr_ref, xi_ref, cr_ref, ci_ref, u_ref, d_ref, dxr_ref, dxi_ref, du_ref, dd_ref, dcr_ref, dci_ref):
        g = dy_ref[...]
        dxr_ref[...] = _bdot(g, cr_ref[0], NT)
        dxi_ref[...] = _bdot(g, ci_ref[0], NT)
        du_ref[...] = g * d_ref[...]
        first = pl.program_id(1) == 0
        _accumulate(dd_ref, jnp.sum(g * u_ref[...], axis=0, keepdims=True), first)

        @pl.when(first)
        def _():
            dcr_ref[...] = jnp.zeros_like(dcr_ref)
            dci_ref[...] = jnp.zeros_like(dci_ref)

        dcr_ref[0] += _bdot(xr_ref[...], g, TN)
        dci_ref[0] += _bdot(xi_ref[...], g, TN)

    tile = lambda j, i: (i, j)
    blk = lambda j, i: (j, 0, 0)
    col = lambda j, i: (0, j)
    return _call(body, (nb, T // tr),
                 [(dy, (tr, LANES), tile), (xr, (tr, S5_SPB), tile), (xi, (tr, S5_SPB), tile),
                  (cd_re, (1, S5_SPB, LANES), blk), (cd_im, (1, S5_SPB, LANES), blk), (u, (tr, LANES), tile), (d.reshape(1, D), (1, LANES), col)],
                 [((T, nb * S5_SPB), F32, (tr, S5_SPB), tile)] * 2 + [((T, D), F32, (tr, LANES), tile), ((1, D), F32, (1, LANES), col)]
                 + [((nb, S5_SPB, LANES), F32, (1, S5_SPB, LANES), blk)] * 2, name="s5_cproj_bwd")


def _s5_bproj_bwd(dbr, dbi, bd_re, bd_im, u, du_skip, tr=512):
    T, D = u.shape
    tr = _tile(T, tr)
    nb = D // LANES

    def body(gr_ref, gi_ref, br_ref, bi_ref, u_ref, ds_ref, du_ref, dbr_ref, dbi_ref):
        g_r, g_i, ub = gr_ref[...], gi_ref[...], u_ref[...]
        du_ref[...] = (ds_ref[...] + _bdot(g_r, br_ref[0], NT) + _bdot(g_i, bi_ref[0], NT)).astype(BF16)

        @pl.when(pl.program_id(1) == 0)
        def _():
            dbr_ref[...] = jnp.zeros_like(dbr_ref)
            dbi_ref[...] = jnp.zeros_like(dbi_ref)

        dbr_ref[0] += _bdot(ub, g_r, TN)
        dbi_ref[0] += _bdot(ub, g_i, TN)

    tile = lambda j, i: (i, j)
    blk = lambda j, i: (j, 0, 0)
    return _call(body, (nb, T // tr),
                 [(dbr, (tr, S5_SPB), tile), (dbi, (tr, S5_SPB), tile), (bd_re, (1, LANES, S5_SPB), blk), (bd_im, (1, LANES, S5_SPB), blk),
                  (u, (tr, LANES), tile), (du_skip, (tr, LANES), tile)],
                 [((T, D), BF16, (tr, LANES), tile)] + [((nb, LANES, S5_SPB), F32, (1, LANES, S5_SPB), blk)] * 2, name="s5_bproj_bwd")


def _s5_gate_fwd(h, vg, tr=256):
    T, D = h.shape
    tr = _tile(T, tr)

    def body(h_ref, a_ref, b_ref, o_ref):
        o_ref[...] = h_ref[...] + a_ref[...] * jax.nn.sigmoid(b_ref[...])

    row = lambda i: (i, 0)
    return _call(body, (T // tr,), [(h, (tr, D), row), (vg, (tr, D), row), (vg, (tr, D), lambda i: (i, 1))],
                 [((T, D), F32, (tr, D), row)], name="s5_gate_fwd", sem=("parallel",))[0]


def _s5_gate_bwd(dh, vg, tr=256):
    T, D = dh.shape
    tr = _tile(T, tr)

    def body(d_ref, a_ref, b_ref, o_ref):
        d = d_ref[...]
        sig = jax.nn.sigmoid(b_ref[...])
        o_ref[:, :D] = (d * sig).astype(BF16)
        o_ref[:, D:] = (d * a_ref[...] * sig * (1.0 - sig)).astype(BF16)

    row = lambda i: (i, 0)
    return _call(body, (T // tr,), [(dh, (tr, D), row), (vg, (tr, D), row), (vg, (tr, D), lambda i: (i, 1))],
                 [((T, 2 * D), BF16, (tr, 2 * D), row)], name="s5_gate_bwd", sem=("parallel",))[0]


def _block_diag(w, transpose):
    g, a, b = w.shape
    if transpose:
        w = w.transpose(0, 2, 1)
        a, b = b, a
    eye = jnp.eye(S5_GPB, dtype=w.dtype)
    return jnp.einsum("jgab,gh->jgahb", w.reshape(g // S5_GPB, S5_GPB, a, b), eye).reshape(g // S5_GPB, S5_GPB * a, S5_GPB * b)


def _block_diag_extract(wd, a, b, transpose):
    if transpose:
        a, b = b, a
    nb = wd.shape[0]
    eye = jnp.eye(S5_GPB, dtype=wd.dtype)
    w = jnp.einsum("jgahb,gh->jgab", wd.reshape(nb, S5_GPB, a, S5_GPB, b), eye).reshape(nb * S5_GPB, a, b)
    return w.transpose(0, 2, 1) if transpose else w


def _mesh_position():
    return lax.axis_index("x"), lax.axis_index("y"), lax.axis_index("c")


def _my_index():
    x, y, c = _mesh_position()
    return 4 * x + 2 * y + c


def _hbm_call(body, arrays, out_shapes, n_sems, name):
    n = len(arrays)
    return pl.pallas_call(
        body, out_shape=[jax.ShapeDtypeStruct(s, d) for s, d in out_shapes],
        in_specs=[pl.BlockSpec(memory_space=pl.ANY)] * n, out_specs=[pl.BlockSpec(memory_space=pl.ANY)] * len(out_shapes),
        scratch_shapes=[pltpu.SemaphoreType.DMA((n_sems,)), pltpu.SemaphoreType.DMA((n_sems,)), pltpu.SemaphoreType.DMA((n,))],
        name=name)(*arrays)


def _all_gather(blocks, name):
    n = len(blocks)
    per = N_DEV - 1

    def body(*refs):
        x_refs, out_refs = refs[:n], refs[n:2 * n]
        send_sems, recv_sems, local_sems = refs[2 * n:]
        x, y, c = _mesh_position()
        me, sibling = (x, y, c), (x, y, 1 - c)
        chips = [(1 - x, y), (x, 1 - y), (1 - x, 1 - y)]

        def copy(a, k, blk, to, src=None):
            slot = out_refs[a].at[4 * blk[0] + 2 * blk[1] + blk[2]]
            return pltpu.make_async_remote_copy(
                src_ref=slot if src is None else src, dst_ref=slot, send_sem=send_sems.at[a * per + k],
                recv_sem=recv_sems.at[a * per + k], device_id=to, device_id_type=pl.DeviceIdType.MESH)

        mine = [pltpu.make_async_copy(x_refs[a], out_refs[a].at[4 * x + 2 * y + c], local_sems.at[a]) for a in range(n)]
        for cp in mine:
            cp.start()
        first = []
        for a in range(n):
            first.append(copy(a, 0, me, sibling, src=x_refs[a]))
            first += [copy(a, 1 + j, me, (*chip, c), src=x_refs[a]) for j, chip in enumerate(chips)]
        for cp in first:
            cp.start()
        passed = []
        for j, chip in enumerate(chips):
            for a in range(n):
                copy(a, 1 + j, (*chip, c), me).wait_recv()
                passed.append(copy(a, 4 + j, (*chip, c), sibling))
                passed[-1].start()
        for a in range(n):
            copy(a, 0, sibling, me).wait_recv()
            for j, chip in enumerate(chips):
                copy(a, 4 + j, (*chip, 1 - c), me).wait_recv()
        for cp in first + passed:
            cp.wait_send()
        for cp in mine:
            cp.wait()

    return _hbm_call(body, blocks, [((N_DEV,) + b.shape, b.dtype) for b in blocks], n * per, name)


def _pair_exchange(parts, name):
    n = len(parts)

    def body(*refs):
        g_refs, own_refs, got_refs = refs[:n], refs[n:2 * n], refs[2 * n:3 * n]
        send_sems, recv_sems, local_sems = refs[3 * n:]
        x, y, c = _mesh_position()
        places = [(x, y), (1 - x, y), (x, 1 - y), (1 - x, 1 - y)]
        local, remote = [], []
        for a in range(n):
            for k, (px, py) in enumerate(places):
                local.append(pltpu.make_async_copy(g_refs[a].at[4 * px + 2 * py + c], own_refs[a].at[k], local_sems.at[4 * a + k]))
                remote.append(pltpu.make_async_remote_copy(
                    src_ref=g_refs[a].at[4 * px + 2 * py + 1 - c], dst_ref=got_refs[a].at[k],
                    send_sem=send_sems.at[4 * a + k], recv_sem=recv_sems.at[4 * a + k],
                    device_id=(x, y, 1 - c), device_id_type=pl.DeviceIdType.MESH))
        for cp in remote + local:
            cp.start()
        for cp in remote + local:
            cp.wait()

    shapes = [((4,) + p_.shape[1:], p_.dtype) for p_ in parts]
    res = pl.pallas_call(
        body, out_shape=[jax.ShapeDtypeStruct(s, d) for s, d in shapes + shapes],
        in_specs=[pl.BlockSpec(memory_space=pl.ANY)] * n, out_specs=[pl.BlockSpec(memory_space=pl.ANY)] * (2 * n),
        scratch_shapes=[pltpu.SemaphoreType.DMA((4 * n,))] * 3, name=name)(*parts)
    return res[:n], res[n:]


def _chip_sums(own, got, tr=PACK_ROWS):
    _, R, cw = own.shape
    cap = min(tr, 1 << (((PACK_ROWS * PACK_W) // cw).bit_length() - 1))
    tr = math.gcd(R, cap)

    def body(a_ref, b_ref, o_ref):
        o_ref[...] = (a_ref[...].astype(F32) + b_ref[...].astype(F32)).astype(o_ref.dtype)

    blk = lambda j, i: (1 + j, i, 0)
    return _call(body, (3, R // tr), [(own, (1, tr, cw), blk), (got, (1, tr, cw), blk)],
                 [((3, R, cw), own.dtype, (1, tr, cw), lambda j, i: (j, i, 0))], name="chip_sums", sem=("parallel", "parallel"))[0]


def _chip_exchange(sums, name):
    n = len(sums)

    def body(*refs):
        s_refs, out_refs = refs[:n], refs[n:2 * n]
        send_sems, recv_sems = refs[2 * n:]
        x, y, c = _mesh_position()
        chips = [(1 - x, y), (x, 1 - y), (1 - x, 1 - y)]
        copies = [pltpu.make_async_remote_copy(
            src_ref=s_refs[a].at[j], dst_ref=out_refs[a].at[j], send_sem=send_sems.at[3 * a + j], recv_sem=recv_sems.at[3 * a + j],
            device_id=(*chip, c), device_id_type=pl.DeviceIdType.MESH) for a in range(n) for j, chip in enumerate(chips)]
        for cp in copies:
            cp.start()
        for cp in copies:
            cp.wait()

    return pl.pallas_call(
        body, out_shape=[jax.ShapeDtypeStruct(s.shape, s.dtype) for s in sums],
        in_specs=[pl.BlockSpec(memory_space=pl.ANY)] * n, out_specs=[pl.BlockSpec(memory_space=pl.ANY)] * n,
        scratch_shapes=[pltpu.SemaphoreType.DMA((3 * n,))] * 2, name=name)(*sums)


def _adamw_math(g, w, m, v):
    nm = ADAM_B1 * m + (1.0 - ADAM_B1) * g
    nv = ADAM_B2 * v + (1.0 - ADAM_B2) * (g * g)
    c1 = 1.0 - ADAM_B1 ** ADAM_STEP
    c2 = 1.0 - ADAM_B2 ** ADAM_STEP
    return -ADAM_LR * ((nm / c1) / (jnp.sqrt(nv / c2) + ADAM_EPS) + ADAM_WD * w), nm, nv


def _adamw(parts, row0, w, m, v, name, window_n8=None):
    R, C = w.shape
    cw = parts[0][0].shape[2]
    cap = min(PACK_ROWS, 1 << (((PACK_ROWS * PACK_W) // cw).bit_length() - 1))
    tr = math.gcd(math.gcd(R, cap), row0 or R)
    assert R % tr == 0 and row0 % tr == 0
    n = len(parts)

    def body(*refs):
        p_refs = refs[:n]
        w_ref, m_ref, v_ref, g_ref, d_ref, nm_ref, nv_ref = refs[n:]
        g = None
        for p_ref, (_, slots) in zip(p_refs, parts):
            for s in range(len(slots)):
                term = p_ref[s].astype(F32)
                g = term if g is None else g + term
        if window_n8 is not None:
            off = (window_n8 * _my_index()) % LANES
            g = pltpu.roll(g, (cw - off) % cw, 1)[:, :C]
        d, nm, nv = _adamw_math(g, w_ref[...], m_ref[...], v_ref[...])
        g_ref[...], d_ref[...], nm_ref[...], nv_ref[...] = g, d, nm, nv

    row = lambda i: (i, 0)
    r0 = row0 // tr
    ins = []
    for arr, slots in parts:
        assert list(slots) == list(range(slots[0], slots[0] + len(slots))) and slots[0] % len(slots) == 0
        s0 = slots[0] // len(slots)
        ins.append((arr, (len(slots), tr, cw), lambda i, s0=s0: (s0, r0 + i, 0)))
    return _call(body, (R // tr,), ins + [(w, (tr, C), row), (m, (tr, C), row), (v, (tr, C), row)],
                 [((R, C), F32, (tr, C), row)] * 4, name=name, sem=("parallel",))


def _window_geometry(n8):
    offs = [(d * n8) % LANES for d in range(N_DEV)]
    starts = [(d * n8) // LANES for d in range(N_DEV)]
    blocks = max(-(-(o + n8) // LANES) for o in offs)
    return starts, blocks, max(starts) + blocks


def _to_window(wpad, n8, tr=256):
    R, cw = wpad.shape
    tr = _tile(R, tr)

    def body(x_ref, o_ref):
        o_ref[...] = pltpu.roll(x_ref[...], (n8 * _my_index()) % LANES, 1).astype(BF16)

    row = lambda i: (i, 0)
    return _call(body, (R // tr,), [(wpad, (tr, cw), row)], [((R, cw), BF16, (tr, cw), row)], name="to_window", sem=("parallel",))[0]


def _from_windows(win, row0, rows, n8, tr=128):
    starts, blocks, total = _window_geometry(n8)
    cw = win.shape[2]
    tr = _tile(rows, tr)
    r0 = row0 // tr

    def body(w_ref, o_ref):
        o_ref[...] = jnp.zeros_like(o_ref)
        for d in range(N_DEV):
            cols = pl.ds(starts[d] * LANES, cw)
            o_ref[:, cols] = (o_ref[:, cols].astype(F32) + w_ref[d].astype(F32)).astype(BF16)

    return _call(body, (rows // tr,), [(win, (N_DEV, tr, cw), lambda i: (0, r0 + i, 0))],
                 [((rows, total * LANES), BF16, (tr, total * LANES), lambda i: (i, 0))], name="from_windows", sem=("parallel",))[0]


def _pack(flat_pieces, dtype, lead=()):
    cat = jnp.concatenate([p_.astype(dtype) for p_ in flat_pieces], axis=-1)
    n = cat.shape[-1]
    quantum = PACK_ROWS * PACK_W
    total = -(-n // quantum) * quantum
    cat = jnp.pad(cat, [(0, 0)] * len(lead) + [(0, total - n)])
    return cat.reshape(lead + (total // PACK_W, PACK_W)), n


REPLICATED = ("norm_mix", "norm_mlp", "norm_ple", "norm_final", "gdn_a_log", "gdn_dt_bias", "gdn_o_norm",
              "s5_lam_re", "s5_lam_im", "s5_log_step", "s5_b_re", "s5_b_im", "s5_c_re", "s5_c_im")
WEIGHTS = ("norm_mix", "norm_mlp", "norm_ple", "norm_final", "gdn_w_in", "gdn_conv_w", "gdn_a_log", "gdn_dt_bias",
           "gdn_o_norm", "gdn_w_out", "s5_w_in", "s5_lam_re", "s5_lam_im", "s5_log_step", "s5_b_re", "s5_b_im",
           "s5_c_re", "s5_c_im", "s5_d", "s5_w_out", "mlp_w_up", "mlp_w_down", "ple_w_proj", "ple_w_gate")
ROW_GROUP = ("mlp_w_down", "gdn_w_out", "s5_w_in", "ple_w_gate")
COL_SHARDED = ("mlp_w_up", "s5_w_out", "ple_w_proj")


def _rows2d(a):
    return a.reshape(-1, a.shape[-1])


def _misc_pack(conv, s5d):
    cw = conv.shape[-1]
    rows = jnp.concatenate([_rows2d(conv), jnp.pad(s5d, ((0, 0), (0, cw - s5d.shape[-1])))], axis=0)
    return jnp.pad(rows, ((0, -rows.shape[0] % 8), (0, 0)))


def _misc_unpack(a, conv_rows, s5d_shape):
    return a[:conv_rows], a[conv_rows:conv_rows + s5d_shape[0], :s5d_shape[1]]


def kernel(x, p, norm_mix, norm_mlp, norm_ple, norm_final, gdn_w_in, gdn_conv_w, gdn_a_log, gdn_dt_bias, gdn_o_norm, gdn_w_out, s5_w_in, s5_lam_re, s5_lam_im, s5_log_step, s5_b_re, s5_b_im, s5_c_re, s5_c_im, s5_d, s5_w_out, mlp_w_up, mlp_w_down, ple_w_proj, ple_w_gate, loss_target, m_norm_mix, m_norm_mlp, m_norm_ple, m_norm_final, m_gdn_w_in, m_gdn_conv_w, m_gdn_a_log, m_gdn_dt_bias, m_gdn_o_norm, m_gdn_w_out, m_s5_w_in, m_s5_lam_re, m_s5_lam_im, m_s5_log_step, m_s5_b_re, m_s5_b_im, m_s5_c_re, m_s5_c_im, m_s5_d, m_s5_w_out, m_mlp_w_up, m_mlp_w_down, m_ple_w_proj, m_ple_w_gate, v_norm_mix, v_norm_mlp, v_norm_ple, v_norm_final, v_gdn_w_in, v_gdn_conv_w, v_gdn_a_log, v_gdn_dt_bias, v_gdn_o_norm, v_gdn_w_out, v_s5_w_in, v_s5_lam_re, v_s5_lam_im, v_s5_log_step, v_s5_b_re, v_s5_b_im, v_s5_c_re, v_s5_c_im, v_s5_d, v_s5_w_out, v_mlp_w_up, v_mlp_w_down, v_ple_w_proj, v_ple_w_gate):
    args = dict(locals())
    w = {n: args[n] for n in WEIGHTS}
    mom = {n: args["m_" + n] for n in WEIGHTS}
    vel = {n: args["v_" + n] for n in WEIGHTS}
    depth = norm_mix.shape[0]
    T, D = x.shape[1], x.shape[2]
    hv = gdn_a_log.shape[1]
    vd = hv * HEAD
    n_gdn, n_s5 = gdn_w_in.shape[0], s5_w_in.shape[0]
    cw = gdn_conv_w.shape[2]
    cd = cw * N_DEV
    hk = (cd - vd) // (2 * HEAD)
    assert hv == 2 * hk and 2 * hv <= LANES and T % SUPER == 0 and SUPER % CHUNK == 0
    G, P = s5_lam_re.shape[1], s5_lam_re.shape[2]
    assert P == S5_STATE and G * S5_CH == D and G % S5_GPB == 0 and D // N_DEV <= cw
    n8 = gdn_w_in.shape[2]
    win_starts, win_blocks, win_total = _window_geometry(n8)
    cwin = win_blocks * LANES
    assert win_total * LANES == cd + vd + LANES

    row_off, off = {}, 0
    for n in ROW_GROUP:
        row_off[n] = off
        off += w[n].shape[0] * w[n].shape[1]
    gathered = _all_gather(
        [_to_window(jnp.pad(_rows2d(gdn_w_in), ((0, 0), (0, cwin - n8))), n8),
         jnp.concatenate([_rows2d(w[n]) for n in ROW_GROUP], axis=0).astype(BF16)]
        + [_rows2d(w[n]).astype(BF16) for n in COL_SHARDED] + [_misc_pack(gdn_conv_w, s5_d)], "gather_weights")
    g_win, g_row, g_misc = gathered[0], gathered[1], gathered[-1]
    g_col = dict(zip(COL_SHARDED, gathered[2:-1]))
    conv_full = g_misc[:, :n_gdn * 4].reshape(N_DEV, n_gdn, 4, cw).transpose(1, 2, 0, 3).reshape(n_gdn, 4, cd)
    s5d_full = g_misc[:, n_gdn * 4:n_gdn * 4 + n_s5, :D // N_DEV].transpose(1, 0, 2).reshape(n_s5, D)

    def weight(name, l):
        r = w[name].shape[1]
        if name in ROW_GROUP:
            return Sharded(g_row, 0, row_off[name] + l * r, r)
        return Sharded(g_col[name], 1, l * r, r)

    h = x[0]
    tgt = loss_target[0]
    grads = {n: [None] * w[n].shape[0] for n in WEIGHTS if n != "norm_final"}
    saved = []
    add = lambda acc, r: (r + acc,)

    for i in range(depth):
        j = i // 2
        sv = {"h0": h}
        hn = _rms_fwd(h, norm_mix[i])
        sv["hn"] = hn
        if i % 2 == 0:
            w_in = _from_windows(g_win, j * D, D, n8)
            pq = _mm(hn, w_in[:, :cd], name="gdn_in_qkv")[0]
            pz = _mm(hn, w_in[:, cd:cd + vd], name="gdn_in_z")[0]
            ba = _mm(hn, w_in[:, cd + vd:], name="gdn_in_ba")[0]
            qkv = _gdn_pre_fwd(pq, conv_full[j], hk)
            pv = jnp.pad(jnp.stack([gdn_a_log[j], gdn_dt_bias[j]]), ((0, 0), (hv, LANES - 2 * hv)))
            g2 = _gates_fwd(ba, pv, hv)
            bb = jnp.repeat(g2[:, :hv], HEAD, axis=1)
            gb = jnp.repeat(g2[:, hv:2 * hv], HEAD, axis=1)
            u, ww, gc, tinv = _gdn_a_fwd(qkv, gb, bb, hk, hv)
            o, vn, sall = _gdn_b_fwd(qkv, u, ww, gc, hk, hv)
            on = _ogate_fwd(o, pz, gdn_o_norm[j])
            h = _mm(on, weight("gdn_w_out", j), epi=add, extras=(h,), name="gdn_out")[0]
            sv.update(w_in=w_in, pq=pq, pz=pz, ba=ba, pv=pv, qkv=qkv, bb=bb, u=u, ww=ww, gc=gc, tinv=tinv, o=o, vn=vn, sall=sall, on=on)
        else:
            uu = _mm(hn, weight("s5_w_in", j), name="s5_in")[0]
            b_re_t, b_im_t = s5_b_re[j].transpose(2, 0, 1), s5_b_im[j].transpose(2, 0, 1)
            ls = s5_log_step[j].reshape(G, 1)
            ar, ai, bbr, bbi = _s5_params_fwd(s5_lam_re[j], s5_lam_im[j], ls, b_re_t, b_im_t)
            lam = jnp.stack([ar.reshape(-1), ai.reshape(-1)])
            bd_re = _block_diag(bbr.transpose(1, 2, 0), transpose=True).astype(BF16)
            bd_im = _block_diag(bbi.transpose(1, 2, 0), transpose=True).astype(BF16)
            cd_re = _block_diag(s5_c_re[j], transpose=True).astype(BF16)
            cd_im = _block_diag(-s5_c_im[j], transpose=True).astype(BF16)
            bur, bui = _s5_bproj_fwd(uu, bd_re, bd_im)
            xr, xi = _s5_scan(bur, bui, lam, reverse=False)
            dsk = s5d_full[j]
            yy, hact = _s5_cproj_fwd(xr, xi, cd_re, cd_im, uu, dsk)
            vg = _mm(hact, weight("s5_w_out", j), name="s5_out")[0]
            h = _s5_gate_fwd(h, vg)
            sv.update(uu=uu, b_re_t=b_re_t, b_im_t=b_im_t, ls=ls, lam=lam, bd_re=bd_re, bd_im=bd_im, cd_re=cd_re, cd_im=cd_im,
                      xr=xr, xi=xi, dsk=dsk, yy=yy, hact=hact, vg=vg)
        sv["h1"] = h
        hm = _rms_fwd(h, norm_mlp[i])
        up, act = _mm(hm, weight("mlp_w_up", i), out_dtypes=(F32, BF16),
                      epi=lambda acc: (acc, jnp.square(jnp.maximum(acc, 0.0))), name="mlp_up")
        h = _mm(act, weight("mlp_w_down", i), epi=add, extras=(h,), name="mlp_down")[0]
        sv.update(hm=hm, up=up, act=act, h2=h)
        hp = _rms_fwd(h, norm_ple[i])
        s_gate = _mm(hp, weight("ple_w_gate", i), name="ple_gate")[0]
        pp = _mm(p[i, 0], weight("ple_w_proj", i), name="ple_proj")[0]
        h = _ple_fwd(h, s_gate, pp)
        sv.update(hp=hp, s_gate=s_gate, pp=pp)
        saved.append(sv)

    dh, d_norm_final, loss_part = _loss_fwd_bwd(h, norm_final, tgt)
    loss = lax.psum(loss_part[0, 0], MESH_AXES)

    dw = lambda a, b_, axis, name: _mm(a, b_, "tn", out_dtypes=(BF16,), out_axis=axis, name=name)[0]
    for i in reversed(range(depth)):
        j = i // 2
        sv = saved[i]
        ds, dpp = _ple_bwd(dh, sv["s_gate"], sv["pp"])
        grads["ple_w_proj"][i] = dw(p[i, 0], dpp, 1, "ple_proj_dw")
        grads["ple_w_gate"][i] = dw(sv["hp"], ds, 0, "ple_gate_dw")
        d_hp = _mm(ds, weight("ple_w_gate", i), "nt", name="ple_gate_dx")[0]
        dh, dh_b, grads["norm_ple"][i] = _rms_bwd(d_hp, sv["h2"], norm_ple[i], dh)
        grads["mlp_w_down"][i] = dw(sv["act"], dh_b, 0, "mlp_down_dw")
        d_up = _mm(dh_b, weight("mlp_w_down", i), "nt", out_dtypes=(BF16,),
                   epi=lambda acc, up_: (acc * 2.0 * jnp.maximum(up_, 0.0),), extras=(sv["up"],), name="mlp_down_dx")[0]
        grads["mlp_w_up"][i] = dw(sv["hm"], d_up, 1, "mlp_up_dw")
        d_hm = _mm(d_up, weight("mlp_w_up", i), "nt", name="mlp_up_dx")[0]
        dh, dh_b, grads["norm_mlp"][i] = _rms_bwd(d_hm, sv["h1"], norm_mlp[i], dh)
        if i % 2 == 0:
            grads["gdn_w_out"][j] = dw(sv["on"], dh_b, 0, "gdn_out_dw")
            d_on = _mm(dh_b, weight("gdn_w_out", j), "nt", name="gdn_out_dx")[0]
            d_o, d_z, grads["gdn_o_norm"][j] = _ogate_bwd(d_on, sv["o"], sv["pz"], gdn_o_norm[j])
            dq_b, dk_b, dgc_b, d_u, d_w = _gdn_b_bwd(d_o, sv["qkv"], sv["ww"], sv["gc"], sv["vn"], sv["sall"], hk, hv)
            dk_a, d_v, d_bb, d_gb = _gdn_a_bwd(d_u, d_w, dgc_b, sv["qkv"], sv["bb"], sv["gc"], sv["tinv"], sv["u"], sv["ww"], hk, hv)
            d_qkv = jnp.concatenate([_pair_sum(dq_b, None), _pair_sum(dk_a, dk_b), d_v], axis=1)
            d_c, grads["gdn_conv_w"][j] = _gdn_pre_bwd(d_qkv, sv["pq"], conv_full[j], hk)
            d_pq = _gdn_conv_bwd(d_c, conv_full[j])
            d_g2 = jnp.pad(jnp.concatenate([d_bb[:, ::HEAD], d_gb[:, ::HEAD]], axis=1), ((0, 0), (0, LANES - 2 * hv)))
            d_ba, d_pv = _gates_bwd(d_g2, sv["ba"], sv["pv"], hv)
            grads["gdn_a_log"][j] = d_pv[0, hv:2 * hv]
            grads["gdn_dt_bias"][j] = d_pv[1, hv:2 * hv]
            hn, w_in = sv["hn"], sv["w_in"]
            dw_nat = jnp.concatenate([dw(hn, d_pq, None, "gdn_in_qkv_dw"), dw(hn, d_z, None, "gdn_in_z_dw"),
                                      dw(hn, d_ba, None, "gdn_in_ba_dw")], axis=1)
            grads["gdn_w_in"][j] = jnp.stack([dw_nat[:, s * LANES:s * LANES + cwin] for s in win_starts])
            d_hn = _mm(d_pq, w_in[:, :cd], "nt", name="gdn_in_qkv_dx")[0]
            d_hn = _mm(d_z, w_in[:, cd:cd + vd], "nt", epi=add, extras=(d_hn,), name="gdn_in_z_dx")[0]
            d_hn = _mm(d_ba, w_in[:, cd + vd:], "nt", epi=add, extras=(d_hn,), name="gdn_in_ba_dx")[0]
        else:
            d_vg = _s5_gate_bwd(dh, sv["vg"])
            grads["s5_w_out"][j] = dw(sv["hact"], d_vg, 1, "s5_out_dw")

            def gelu_bwd(acc, y_):
                _, vjp = jax.vjp(jax.nn.gelu, y_)
                return (vjp(acc)[0],)

            d_y = _mm(d_vg, weight("s5_w_out", j), "nt", epi=gelu_bwd, extras=(sv["yy"],), name="s5_out_dx")[0]
            d_xr, d_xi, du_skip, d_dsk, d_cdr, d_cdi = _s5_cproj_bwd(d_y, sv["xr"], sv["xi"], sv["cd_re"], sv["cd_im"], sv["uu"], sv["dsk"])
            grads["s5_d"][j] = d_dsk
            grads["s5_c_re"][j] = _block_diag_extract(d_cdr, S5_CH, S5_STATE, transpose=True)
            grads["s5_c_im"][j] = -_block_diag_extract(d_cdi, S5_CH, S5_STATE, transpose=True)
            d_bur, d_bui, d_lam = _s5_scan(d_xr, d_xi, sv["lam"], reverse=True, xr=sv["xr"], xi=sv["xi"])
            d_uu, d_bdr, d_bdi = _s5_bproj_bwd(d_bur, d_bui, sv["bd_re"], sv["bd_im"], sv["uu"], du_skip)
            d_bbr = _block_diag_extract(d_bdr, S5_STATE, S5_CH, transpose=True).transpose(2, 0, 1)
            d_bbi = _block_diag_extract(d_bdi, S5_STATE, S5_CH, transpose=True).transpose(2, 0, 1)
            d_lr, d_li, d_ls, d_br_t, d_bi_t = _s5_params_bwd(
                s5_lam_re[j], s5_lam_im[j], sv["ls"], sv["b_re_t"], sv["b_im_t"],
                d_lam[0].reshape(G, P), d_lam[1].reshape(G, P), d_bbr, d_bbi)
            grads["s5_lam_re"][j], grads["s5_lam_im"][j], grads["s5_log_step"][j] = d_lr, d_li, d_ls.reshape(G)
            grads["s5_b_re"][j], grads["s5_b_im"][j] = d_br_t.transpose(1, 2, 0), d_bi_t.transpose(1, 2, 0)
            grads["s5_w_in"][j] = dw(sv["hn"], d_uu, 0, "s5_in_dw")
            d_hn = _mm(d_uu, weight("s5_w_in", j), "nt", name="s5_in_dx")[0]
        dh, _, grads["norm_mix"][i] = _rms_bwd(d_hn, sv["h0"], norm_mix[i], dh)

    out = {}
    layers = lambda name: jnp.concatenate(grads[name], axis=1)
    d_conv = jnp.stack(grads["gdn_conv_w"]).reshape(n_gdn * 4, N_DEV, cw).transpose(1, 0, 2)
    d_s5d = jnp.stack([g.reshape(N_DEV, D // N_DEV) for g in grads["s5_d"]], axis=1)
    d_misc = jnp.concatenate([d_conv, jnp.pad(d_s5d, ((0, 0), (0, 0), (0, cw - D // N_DEV)))], axis=1)
    d_misc = jnp.pad(d_misc, ((0, 0), (0, -d_misc.shape[1] % 8), (0, 0)))
    contributions = ([layers("gdn_w_in"), jnp.concatenate([layers(n) for n in ROW_GROUP], axis=1)]
                     + [layers(n) for n in COL_SHARDED] + [d_misc])
    own, got = _pair_exchange(contributions, "exchange_grads_pair")
    arrived = _chip_exchange([_chip_sums(o, g) for o, g in zip(own, got)], "exchange_grads_chips")
    recv = [[(o, (0,)), (g, (0,)), (r, (0, 1, 2))] for o, g, r in zip(own, got, arrived)]
    r_win, r_row, r_misc = recv[0], recv[1], recv[-1]
    r_col = dict(zip(COL_SHARDED, recv[2:-1]))

    def update(name, parts, row0, **kw):
        res = _adamw(parts, row0, _rows2d(w[name]), _rows2d(mom[name]), _rows2d(vel[name]), "adamw_" + name, **kw)
        out[name] = [r.reshape(w[name].shape) for r in res]

    update("gdn_w_in", r_win, 0, window_n8=n8)
    for n in ROW_GROUP:
        update(n, r_row, row_off[n])
    for n in COL_SHARDED:
        update(n, r_col[n], 0)
    res = _adamw(r_misc, 0, _misc_pack(gdn_conv_w, s5_d), _misc_pack(mom["gdn_conv_w"], mom["s5_d"]),
                 _misc_pack(vel["gdn_conv_w"], vel["s5_d"]), "adamw_misc")
    unpacked = [_misc_unpack(r, n_gdn * 4, s5_d.shape) for r in res]
    out["gdn_conv_w"] = [u_[0].reshape(gdn_conv_w.shape) for u_ in unpacked]
    out["s5_d"] = [u_[1] for u_ in unpacked]

    rep_g = {n: (d_norm_final[0] if n == "norm_final" else jnp.stack([g.reshape(w[n].shape[1:]) for g in grads[n]])) for n in REPLICATED}
    flat_r = lambda d: [d[n].reshape(-1) for n in REPLICATED]
    pg, _ = _pack(flat_r(rep_g), F32)
    parts_r = _all_gather([pg], "gather_small_grads")[0]
    pw, _ = _pack(flat_r(w), F32)
    pm, _ = _pack(flat_r(mom), F32)
    pvv, _ = _pack(flat_r(vel), F32)
    res = [r.reshape(-1) for r in _adamw([(parts_r, tuple(range(N_DEV)))], 0, pw, pm, pvv, "adamw_replicated")]
    off = 0
    for name in REPLICATED:
        n = w[name].size
        out[name] = [res[k][off:off + n].reshape(w[name].shape) for k in range(4)]
        off += n

    grad_x = dh[None]
    return (loss, grad_x, *[out[n][0] for n in WEIGHTS], *[out[n][1] for n in WEIGHTS],
            *[out[n][2] for n in WEIGHTS], *[out[n][3] for n in WEIGHTS])
```

```python
import collections
import math

import jax
import jax.numpy as jnp
from jax import lax
from jax.experimental import pallas as pl
from jax.experimental.pallas import tpu as pltpu

F32, BF16 = jnp.float32, jnp.bfloat16
NN, NT, TN = ((1,), (0,)), ((1,), (1,)), ((0,), (0,))

N_DEV = 8
MESH_AXES = ("x", "y", "c")
LANES = 128
V7X_VMEM_BYTES = 64 * 1024 * 1024
VMEM_LIMIT = V7X_VMEM_BYTES - 8 * 1024 * 1024
CHUNK = 64
HEAD = 128
SUPER = 256
S5_CH = 16
S5_STATE = 64
S5_GPB = LANES // S5_CH
S5_SPB = S5_GPB * S5_STATE
NORM_EPS = 1e-6
L2_EPS = 1e-6
ADAM_LR, ADAM_B1, ADAM_B2, ADAM_EPS, ADAM_WD, ADAM_STEP = 0.001, 0.9, 0.999, 1e-08, 0.01, 10
PACK_W = 1024
PACK_ROWS = 256


def _dot(a, b, dims):
    return lax.dot_general(a, b, (dims, ((), ())), preferred_element_type=F32)


def _bdot(a, b, dims=NN):
    return _dot(a.astype(BF16), b.astype(BF16), dims)


def _call(body, grid, ins, outs, scratch=(), name=None, sem=None):
    res = pl.pallas_call(
        body,
        grid=grid,
        in_specs=[pl.BlockSpec(b, m) for _, b, m in ins],
        out_specs=[pl.BlockSpec(b, m) for _, _, b, m in outs],
        out_shape=[jax.ShapeDtypeStruct(s, d) for s, d, _, _ in outs],
        scratch_shapes=list(scratch),
        name=name,
        compiler_params=pltpu.CompilerParams(
            dimension_semantics=sem or ("arbitrary",) * len(grid), vmem_limit_bytes=VMEM_LIMIT),
    )(*[a for a, _, _ in ins])
    return res


def _tile(n, want):
    t = min(n, want)
    assert n % t == 0, (n, want)
    return t


def _accumulate(ref, val, first):
    @pl.when(first)
    def _():
        ref[...] = jnp.zeros_like(ref)
    ref[...] += val


class Sharded(collections.namedtuple("Sharded", "arr axis row0 rows")):
    @property
    def shape(self):
        c = self.arr.shape[2]
        return (self.rows, N_DEV * c) if self.axis == 1 else (N_DEV * self.rows, c)

    @property
    def units(self):
        return (math.gcd(self.rows, self.row0), self.arr.shape[2])


def _mm(a, b, mode="nn", out_dtypes=(F32,), epi=None, extras=(), name="mm", out_axis=None, tm=1024, tn=1024, tk=2048):
    sh = isinstance(b, Sharded)
    b_rows, b_cols = b.shape
    u_rows, u_cols = b.units if sh else b.shape
    if mode == "nn":
        (M, K), (K2, N), (uk, un) = a.shape, (b_rows, b_cols), (u_rows, u_cols)
    elif mode == "nt":
        (M, K), (N, K2), (un, uk) = a.shape, (b_rows, b_cols), (u_rows, u_cols)
    else:
        (K, M), (K2, N), (uk, un) = a.shape, (b_rows, b_cols), (u_rows, u_cols)
    assert K == K2, (a.shape, b.shape, mode)
    um = M
    if out_axis == 0:
        um = M // N_DEV
    elif out_axis == 1:
        un = N // N_DEV
    tm, tn = _tile(um, tm), _tile(un, tn)
    span = 1
    if sh and ((mode == "nn" and b.axis == 0 and uk == b.rows) or (mode == "nt" and b.axis == 1)) and uk < min(K, tk):
        span = min(K, tk) // uk
        assert N_DEV % span == 0
    tk = _tile(uk, tk)
    nk = K // (tk * span)
    a_spec = ((tk, tm), lambda i, j, k: (k, i)) if mode == "tn" else ((tm, tk * span), lambda i, j, k: (i, k))
    if not sh:
        b_arr = b
        b_spec = ((tn, tk), lambda i, j, k: (j, k)) if mode == "nt" else ((tk, tn), lambda i, j, k: (k, j))
    else:
        b_arr = b.arr
        tr_, tc_ = (tk, tn) if mode == "nn" else (tn, tk)
        r0, per_r, per_c = b.row0 // tr_, b.rows // tr_, b.arr.shape[2] // tc_
        assert b.row0 % tr_ == 0 and mode != "tn"
        if span > 1:
            place = (lambda r, c: (r, r0, c)) if b.axis == 0 else (lambda r, c: (c, r0 + r, 0))
        elif b.axis == 1:
            place = lambda r, c: (c // per_c, r0 + r, c % per_c)
        else:
            place = lambda r, c: (r // per_r, r0 + r % per_r, c)
        b_spec = ((span, tr_, tc_), (lambda i, j, k: place(k, j)) if mode == "nn" else (lambda i, j, k: place(j, k)))
    dims = {"nn": NN, "nt": NT, "tn": TN}[mode]
    n_ex, n_out = len(extras), len(out_dtypes)

    def body(*refs):
        a_ref, b_ref = refs[:2]
        ex = refs[2:2 + n_ex]
        outs = refs[2 + n_ex:2 + n_ex + n_out]

        def product():
            if not sh:
                return _bdot(a_ref[...], b_ref[...], dims)
            part = _bdot(a_ref[:, :tk], b_ref[0], dims)
            for s in range(1, span):
                part = part + _bdot(a_ref[:, s * tk:(s + 1) * tk], b_ref[s], dims)
            return part

        def finish(res):
            vals = epi(res, *[e[...] for e in ex]) if epi is not None else (res,)
            for r, v in zip(outs, vals):
                r[...] = v.astype(r.dtype)

        if nk == 1:
            finish(product())
            return
        acc = refs[-1]
        k = pl.program_id(2)

        @pl.when(k == 0)
        def _():
            acc[...] = jnp.zeros_like(acc)

        acc[...] += product()

        @pl.when(k == nk - 1)
        def _():
            finish(acc[...])

    tile = lambda i, j, k: (i, j)
    if out_axis is None:
        out_shape, out_block, out_map = (M, N), (tm, tn), tile
    elif out_axis == 0:
        per = um // tm
        out_shape, out_block, out_map = (N_DEV, um, N), (None, tm, tn), lambda i, j, k: (i // per, i % per, j)
    else:
        per = un // tn
        out_shape, out_block, out_map = (N_DEV, M, un), (None, tm, tn), lambda i, j, k: (j // per, i, j % per)
    return _call(
        body, (M // tm, N // tn, nk),
        [(a,) + a_spec, (b_arr,) + b_spec] + [(e, (tm, tn), tile) for e in extras],
        [(out_shape, d, out_block, out_map) for d in out_dtypes],
        scratch=[pltpu.VMEM((tm, tn), F32)] if nk > 1 else [], name=name,
        sem=("parallel", "parallel", "arbitrary"))


def _rms_fwd(h, g, tr=256):
    T, D = h.shape
    tr = _tile(T, tr)

    def body(h_ref, g_ref, o_ref):
        x = h_ref[...]
        r = lax.rsqrt(jnp.mean(x * x, axis=-1, keepdims=True) + NORM_EPS)
        o_ref[...] = (x * r * g_ref[...]).astype(BF16)

    row = lambda i: (i, 0)
    fix = lambda i: (0, 0)
    return _call(body, (T // tr,), [(h, (tr, D), row), (g.reshape(1, D), (1, D), fix)],
                 [((T, D), BF16, (tr, D), row)], name="rms_fwd", sem=("parallel",))[0]


def _rms_bwd_math(dy, x, g):
    r = lax.rsqrt(jnp.mean(x * x, axis=-1, keepdims=True) + NORM_EPS)
    xh = x * r
    dxh = dy * g
    dx = r * (dxh - xh * jnp.mean(dxh * xh, axis=-1, keepdims=True))
    dg = jnp.sum(dy * xh, axis=0, keepdims=True)
    return dx, dg


def _rms_bwd(dy, h, g, res, tr=256):
    T, D = h.shape
    tr = _tile(T, tr)

    def body(dy_ref, h_ref, g_ref, res_ref, dh_ref, dhb_ref, dg_ref):
        dx, dg = _rms_bwd_math(dy_ref[...], h_ref[...], g_ref[...])
        dh = res_ref[...] + dx
        dh_ref[...] = dh
        dhb_ref[...] = dh.astype(BF16)
        _accumulate(dg_ref, dg, pl.program_id(0) == 0)

    row = lambda i: (i, 0)
    fix = lambda i: (0, 0)
    return _call(body, (T // tr,),
                 [(dy, (tr, D), row), (h, (tr, D), row), (g.reshape(1, D), (1, D), fix), (res, (tr, D), row)],
                 [((T, D), F32, (tr, D), row), ((T, D), BF16, (tr, D), row), ((1, D), F32, (1, D), fix)], name="rms_bwd")


def _loss_fwd_bwd(h, g, tgt, tr=256):
    T, D = h.shape
    tr = _tile(T, tr)

    def body(h_ref, g_ref, t_ref, dh_ref, dg_ref, loss_ref):
        x, gg = h_ref[...], g_ref[...]
        r = lax.rsqrt(jnp.mean(x * x, axis=-1, keepdims=True) + NORM_EPS)
        diff = x * r * gg - t_ref[...]
        part = 0.5 * jnp.sum(jnp.mean(diff * diff, axis=-1, keepdims=True))
        dx, dg = _rms_bwd_math(diff * (1.0 / D), x, gg)
        dh_ref[...] = dx
        first = pl.program_id(0) == 0
        _accumulate(dg_ref, dg, first)
        _accumulate(loss_ref, jnp.full((1, LANES), part, F32), first)

    row = lambda i: (i, 0)
    fix = lambda i: (0, 0)
    return _call(body, (T // tr,),
                 [(h, (tr, D), row), (g.reshape(1, D), (1, D), fix), (tgt, (tr, D), row)],
                 [((T, D), F32, (tr, D), row), ((1, D), F32, (1, D), fix), ((1, LANES), F32, (1, LANES), fix)],
                 name="loss_fwd_bwd")


def _ple_fwd(h, s, pp, tr=256):
    T, D = h.shape
    tr = _tile(T, tr)

    def body(h_ref, s_ref, p_ref, o_ref):
        o_ref[...] = h_ref[...] + jax.nn.sigmoid(s_ref[...]) * p_ref[...]

    row = lambda i: (i, 0)
    return _call(body, (T // tr,), [(a, (tr, D), row) for a in (h, s, pp)],
                 [((T, D), F32, (tr, D), row)], name="ple_fwd", sem=("parallel",))[0]


def _ple_bwd(dh, s, pp, tr=256):
    T, D = dh.shape
    tr = _tile(T, tr)

    def body(dh_ref, s_ref, p_ref, ds_ref, dp_ref):
        d = dh_ref[...]
        gate = jax.nn.sigmoid(s_ref[...])
        ds_ref[...] = (d * p_ref[...] * gate * (1.0 - gate)).astype(BF16)
        dp_ref[...] = (d * gate).astype(BF16)

    row = lambda i: (i, 0)
    return _call(body, (T // tr,), [(a, (tr, D), row) for a in (dh, s, pp)],
                 [((T, D), BF16, (tr, D), row)] * 2, name="ple_bwd", sem=("parallel",))


def _conv_taps(xe, w, tr):
    c = w[3:4, :] * xe[8:, :]
    for j in range(3):
        c = c + w[j:j + 1, :] * pltpu.roll(xe, 3 - j, 0)[8:, :]
    return c


def _gdn_pre_fwd(pq, conv_w, hk, tr=1024):
    T, CD = pq.shape
    tr = _tile(T, tr)
    r8 = tr // 8

    def body(x_ref, halo_ref, w_ref, o_ref):
        j, r = pl.program_id(0), pl.program_id(1)
        halo = jnp.where(r > 0, halo_ref[...], 0.0)
        xe = jnp.concatenate([halo, x_ref[...]], axis=0)
        c = _conv_taps(xe, w_ref[...], tr)
        s = c * jax.nn.sigmoid(c)
        rn = lax.rsqrt(jnp.sum(s * s, axis=-1, keepdims=True) + L2_EPS)
        scale = jnp.where(j < hk, HEAD ** -0.5, 1.0)
        o_ref[...] = jnp.where(j < 2 * hk, s * rn * scale, s)

    tile = lambda j, r: (r, j)
    return _call(body, (CD // HEAD, T // tr),
                 [(pq, (tr, HEAD), tile), (pq, (8, HEAD), lambda j, r: (jnp.maximum(r * r8 - 1, 0), j)),
                  (conv_w, (4, HEAD), lambda j, r: (0, j))],
                 [((T, CD), F32, (tr, HEAD), tile)], name="gdn_pre_fwd", sem=("parallel", "parallel"))[0]


def _gdn_pre_bwd(dn, pq, conv_w, hk, tr=1024):
    T, CD = pq.shape
    tr = _tile(T, tr)
    r8 = tr // 8

    def body(dn_ref, x_ref, halo_ref, w_ref, dc_ref, dw_ref):
        j, r = pl.program_id(0), pl.program_id(1)
        halo = jnp.where(r > 0, halo_ref[...], 0.0)
        xe = jnp.concatenate([halo, x_ref[...]], axis=0)
        c = _conv_taps(xe, w_ref[...], tr)
        sig = jax.nn.sigmoid(c)
        s = c * sig
        rn = lax.rsqrt(jnp.sum(s * s, axis=-1, keepdims=True) + L2_EPS)
        scale = jnp.where(j < hk, HEAD ** -0.5, 1.0)
        d = dn_ref[...]
        y = s * rn
        dy = d * scale
        ds = jnp.where(j < 2 * hk, rn * (dy - y * jnp.sum(dy * y, axis=-1, keepdims=True)), d)
        dc = ds * sig * (1.0 + c * (1.0 - sig))
        dc_ref[...] = dc

        @pl.when(r == 0)
        def _():
            dw_ref[...] = jnp.zeros_like(dw_ref)

        for t in range(4):
            xs = xe[8:, :] if t == 3 else pltpu.roll(xe, 3 - t, 0)[8:, :]
            dw_ref[t:t + 1, :] += jnp.sum(dc * xs, axis=0, keepdims=True)

    tile = lambda j, r: (r, j)
    col = lambda j, r: (0, j)
    return _call(body, (CD // HEAD, T // tr),
                 [(dn, (tr, HEAD), tile), (pq, (tr, HEAD), tile),
                  (pq, (8, HEAD), lambda j, r: (jnp.maximum(r * r8 - 1, 0), j)), (conv_w, (4, HEAD), col)],
                 [((T, CD), F32, (tr, HEAD), tile), ((4, CD), F32, (4, HEAD), col)], name="gdn_pre_bwd")


def _gdn_conv_bwd(dc, conv_w, tr=1024):
    T, CD = dc.shape
    tr = _tile(T, tr)
    r8 = tr // 8
    n_r = T // tr

    def body(dc_ref, halo_ref, w_ref, dx_ref):
        r = pl.program_id(1)
        halo = jnp.where(r < n_r - 1, halo_ref[...], 0.0)
        de = jnp.concatenate([dc_ref[...], halo], axis=0)
        w = w_ref[...]
        dx = w[3:4, :] * de[:tr, :]
        for j in range(3):
            dx = dx + w[j:j + 1, :] * pltpu.roll(de, tr + 8 - (3 - j), 0)[:tr, :]
        dx_ref[...] = dx.astype(BF16)

    tile = lambda j, r: (r, j)
    return _call(body, (CD // HEAD, n_r),
                 [(dc, (tr, HEAD), tile), (dc, (8, HEAD), lambda j, r: (jnp.minimum((r + 1) * r8, T // 8 - 1), j)),
                  (conv_w, (4, HEAD), lambda j, r: (0, j))],
                 [((T, CD), BF16, (tr, HEAD), tile)], name="gdn_conv_bwd", sem=("parallel", "parallel"))[0]


def _gates_fwd(ba, pv, hv, tr=1024):
    T = ba.shape[0]
    tr = _tile(T, tr)

    def body(x_ref, pv_ref, o_ref):
        x = x_ref[...]
        lane = lax.broadcasted_iota(jnp.int32, x.shape, 1)
        g = -jnp.exp(pv_ref[0:1, :]) * jax.nn.softplus(x + pv_ref[1:2, :])
        o_ref[...] = jnp.where(lane < hv, jax.nn.sigmoid(x), jnp.where(lane < 2 * hv, g, 0.0))

    row = lambda i: (i, 0)
    return _call(body, (T // tr,), [(ba, (tr, LANES), row), (pv, (2, LANES), lambda i: (0, 0))],
                 [((T, LANES), F32, (tr, LANES), row)], name="gates_fwd", sem=("parallel",))[0]


def _gates_bwd(dg2, ba, pv, hv, tr=1024):
    T = ba.shape[0]
    tr = _tile(T, tr)

    def body(d_ref, x_ref, pv_ref, dx_ref, dpv_ref):
        x, d = x_ref[...], d_ref[...]
        lane = lax.broadcasted_iota(jnp.int32, x.shape, 1)
        is_a = (lane >= hv) & (lane < 2 * hv)
        beta = jax.nn.sigmoid(x)
        neg_a = -jnp.exp(pv_ref[0:1, :])
        z = x + pv_ref[1:2, :]
        da = d * neg_a * jax.nn.sigmoid(z)
        dx_ref[...] = jnp.where(lane < hv, d * beta * (1.0 - beta), jnp.where(is_a, da, 0.0))
        first = pl.program_id(0) == 0

        @pl.when(first)
        def _():
            dpv_ref[...] = jnp.zeros_like(dpv_ref)

        dpv_ref[0:1, :] += jnp.sum(jnp.where(is_a, d * neg_a * jax.nn.softplus(z), 0.0), axis=0, keepdims=True)
        dpv_ref[1:2, :] += jnp.sum(jnp.where(is_a, da, 0.0), axis=0, keepdims=True)

    row = lambda i: (i, 0)
    fix = lambda i: (0, 0)
    return _call(body, (T // tr,), [(dg2, (tr, LANES), row), (ba, (tr, LANES), row), (pv, (2, LANES), fix)],
                 [((T, LANES), F32, (tr, LANES), row), ((2, LANES), F32, (2, LANES), fix)], name="gates_bwd")


def _ogate_fwd(o, z, o_norm, tr=512):
    T, VD = o.shape
    tr = _tile(T, tr)

    def body(o_ref, z_ref, g_ref, y_ref):
        x, zz = o_ref[...], z_ref[...]
        r = lax.rsqrt(jnp.mean(x * x, axis=-1, keepdims=True) + NORM_EPS)
        y_ref[...] = (x * r * g_ref[...] * (zz * jax.nn.sigmoid(zz))).astype(BF16)

    tile = lambda h, r: (r, h)
    return _call(body, (VD // HEAD, T // tr),
                 [(o, (tr, HEAD), tile), (z, (tr, HEAD), tile), (o_norm.reshape(1, HEAD), (1, HEAD), lambda h, r: (0, 0))],
                 [((T, VD), BF16, (tr, HEAD), tile)], name="ogate_fwd", sem=("parallel", "parallel"))[0]


def _ogate_bwd(dy, o, z, o_norm, tr=512):
    T, VD = o.shape
    tr = _tile(T, tr)

    def body(dy_ref, o_ref, z_ref, g_ref, do_ref, dz_ref, dg_ref):
        d, x, zz, g = dy_ref[...], o_ref[...], z_ref[...], g_ref[...]
        sig = jax.nn.sigmoid(zz)
        silu = zz * sig
        dx, dg = _rms_bwd_math(d * silu, x, g)
        r = lax.rsqrt(jnp.mean(x * x, axis=-1, keepdims=True) + NORM_EPS)
        do_ref[...] = dx
        dz_ref[...] = (d * (x * r * g) * sig * (1.0 + zz * (1.0 - sig))).astype(BF16)
        _accumulate(dg_ref, dg, (pl.program_id(0) == 0) & (pl.program_id(1) == 0))

    tile = lambda h, r: (r, h)
    fix = lambda h, r: (0, 0)
    return _call(body, (VD // HEAD, T // tr),
                 [(dy, (tr, HEAD), tile), (o, (tr, HEAD), tile), (z, (tr, HEAD), tile), (o_norm.reshape(1, HEAD), (1, HEAD), fix)],
                 [((T, VD), F32, (tr, HEAD), tile), ((T, VD), BF16, (tr, HEAD), tile), ((1, HEAD), F32, (1, HEAD), fix)],
                 name="ogate_bwd")


def _chunk_iota():
    return (lax.broadcasted_iota(jnp.int32, (CHUNK, CHUNK), 0), lax.broadcasted_iota(jnp.int32, (CHUNK, CHUNK), 1))


def _decay(gc):
    ri, ci = _chunk_iota()
    gcol = gc[:, :CHUNK]
    grow = jnp.sum(jnp.where(ri == ci, gcol, 0.0), axis=0, keepdims=True)
    return jnp.where(ri >= ci, jnp.exp(jnp.minimum(gcol - grow, 0.0)), 0.0)


def _rowsum(x):
    return jnp.broadcast_to(jnp.sum(x, axis=1, keepdims=True), (x.shape[0], HEAD))


def _split3(x):
    h1 = x.astype(BF16)
    r1 = x - h1.astype(F32)
    h2 = r1.astype(BF16)
    return h1, h2, (r1 - h2.astype(F32)).astype(BF16)


def _sel_dot(sel, xs, dims=NN):
    s = sel.astype(BF16)
    parts = [_split3(x) for x in xs]
    if dims == NN:
        return [_dot(s, h1, NN) + _dot(s, h2, NN) + _dot(s, h3, NN) for h1, h2, h3 in parts]
    return [_dot(h1, s, dims) + _dot(h2, s, dims) + _dot(h3, s, dims) for h1, h2, h3 in parts]


def _colsum(es):
    return _sel_dot(jnp.ones((es[0].shape[0], HEAD), F32), es, TN)


def _super_iota():
    ri = lax.broadcasted_iota(jnp.int32, (SUPER, SUPER), 0)
    ci = lax.broadcasted_iota(jnp.int32, (SUPER, SUPER), 1)
    shift = int(math.log2(CHUNK))
    return ri, ci, jnp.right_shift(ri, shift) == jnp.right_shift(ci, shift)


def _decay_super(gc, ri, ci, same):
    gcol = jnp.concatenate([gc] * (SUPER // HEAD), axis=1)
    grow = jnp.sum(jnp.where(ri == ci, gcol, 0.0), axis=0, keepdims=True)
    return jnp.where(same & (ri >= ci), jnp.exp(jnp.minimum(gcol - grow, 0.0)), 0.0)


def _unit_lower_inverse(ms, eye):
    ps = [-m for m in ms]
    xs = [eye + p for p in ps]
    for _ in range(int(math.log2(CHUNK)) - 1):
        ps = [_bdot(p, p) for p in ps]
        xs = [x + _bdot(x, p) for x, p in zip(xs, ps)]
    resid = []
    for m, x in zip(ms, xs):
        (m1, m2, _), (x1, x2, _) = _split3(m), _split3(x)
        resid.append((eye - x) - (_dot(m1, x1, NN) + _dot(m1, x2, NN) + _dot(m2, x1, NN)))
    return [x + _bdot(x, r) for x, r in zip(xs, resid)]


def _gdn_a_fwd(qkv, gb, bb, hk, hv, tr=1024):
    T = qkv.shape[0]
    tr = _tile(T, tr)
    assert tr % SUPER == 0

    def body(k_ref, v_ref, g_ref, b_ref, u_ref, w_ref, gc_ref, ti_ref):
        ri, ci, same = _super_iota()
        ltri = jnp.where(same & (ri >= ci), 1.0, 0.0)
        eye = jnp.where(ri == ci, 1.0, 0.0)
        rows = [pl.ds(s * SUPER, SUPER) for s in range(tr // SUPER)]
        ks, vs, betas = [k_ref[r, :] for r in rows], [v_ref[r, :] for r in rows], [b_ref[r, :] for r in rows]
        gcs = _sel_dot(ltri, [g_ref[r, :] for r in rows])
        kbs = [k * beta for k, beta in zip(ks, betas)]
        ms = [jnp.where(same & (ri > ci), _bdot(kb, k, NT) * _decay_super(gc, ri, ci, same), 0.0)
              for kb, k, gc in zip(kbs, ks, gcs)]
        tinvs = _unit_lower_inverse(ms, eye)
        xs = [_bdot(tinv, jnp.concatenate([v * beta, kb * jnp.exp(gc)], axis=1))
              for tinv, v, beta, kb, gc in zip(tinvs, vs, betas, kbs, gcs)]
        for r, x, gc, tinv in zip(rows, xs, gcs, tinvs):
            u_ref[r, :] = x[:, :HEAD]
            w_ref[r, :] = x[:, HEAD:]
            gc_ref[r, :] = gc
            ti_ref[0, r, :] = tinv.astype(BF16)

    tile = lambda h, r: (r, h)
    vd = hv * HEAD
    return _call(body, (hv, T // tr),
                 [(qkv, (tr, HEAD), lambda h, r: (r, hk + h // 2)), (qkv, (tr, HEAD), lambda h, r: (r, 2 * hk + h)),
                  (gb, (tr, HEAD), tile), (bb, (tr, HEAD), tile)],
                 [((T, vd), F32, (tr, HEAD), tile)] * 3 + [((hv, T, SUPER), BF16, (1, tr, SUPER), lambda h, r: (h, r, 0))],
                 name="gdn_a_fwd", sem=("parallel", "parallel"))


def _gdn_b_fwd(qkv, u, w, gc, hk, hv, tr=512):
    T = qkv.shape[0]
    tr = _tile(T, tr)
    cpb = tr // CHUNK

    def body(q_ref, k_ref, u_ref, w_ref, gc_ref, o_ref, vn_ref, sall_ref, s_ref):
        ri, ci = _chunk_iota()

        @pl.when(pl.program_id(1) == 0)
        def _():
            s_ref[...] = jnp.zeros_like(s_ref)

        rows = [pl.ds(c * CHUNK, CHUNK) for c in range(cpb)]
        qs, ks, us, ws, gcs = ([ref[r, :] for r in rows] for ref in (q_ref, k_ref, u_ref, w_ref, gc_ref))
        gls = [gc[CHUNK - 1:CHUNK, :] for gc in gcs]
        kws = [_bdot(k * jnp.exp(gl - gc), jnp.concatenate([w_, u_], axis=1), TN)
               for k, gl, gc, w_, u_ in zip(ks, gls, gcs, ws, us)]
        qks = [jnp.where(ri >= ci, _bdot(q, k, NT) * _decay(gc), 0.0) for q, k, gc in zip(qs, ks, gcs)]
        s = s_ref[...]
        states = []
        for kw, gl in zip(kws, gls):
            states.append(s)
            s = s * jnp.exp(gl) - _bdot(kw[:, :HEAD], s) + kw[:, HEAD:]
        s_ref[...] = s
        vns = [u_ - _bdot(w_, st) for u_, w_, st in zip(us, ws, states)]
        outs = [_bdot(q * jnp.exp(gc), st) + _bdot(qk, vn) for q, gc, st, qk, vn in zip(qs, gcs, states, qks, vns)]
        for c, (r, o, vn, st) in enumerate(zip(rows, outs, vns, states)):
            o_ref[r, :] = o
            vn_ref[r, :] = vn
            sall_ref[0, c] = st

    tile = lambda h, r: (r, h)
    vd = hv * HEAD
    return _call(body, (hv, T // tr),
                 [(qkv, (tr, HEAD), lambda h, r: (r, h // 2)), (qkv, (tr, HEAD), lambda h, r: (r, hk + h // 2)),
                  (u, (tr, HEAD), tile), (w, (tr, HEAD), tile), (gc, (tr, HEAD), tile)],
                 [((T, vd), F32, (tr, HEAD), tile)] * 2 +
                 [((hv, T // CHUNK, HEAD, HEAD), F32, (1, cpb, HEAD, HEAD), lambda h, r: (h, r, 0, 0))],
                 scratch=[pltpu.VMEM((HEAD, HEAD), F32)], name="gdn_b_fwd", sem=("parallel", "arbitrary"))


def _gdn_b_bwd(do, qkv, w, gc, vn, sall, hk, hv, tr=512):
    T = qkv.shape[0]
    tr = _tile(T, tr)
    cpb = tr // CHUNK
    n_r = T // tr

    def body(do_ref, q_ref, k_ref, w_ref, gc_ref, vn_ref, sall_ref, dq_ref, dk_ref, dgc_ref, du_ref, dw_ref, ds_ref):
        ri, ci = _chunk_iota()
        row = lax.broadcasted_iota(jnp.int32, (CHUNK, HEAD), 0)

        @pl.when(pl.program_id(1) == 0)
        def _():
            ds_ref[...] = jnp.zeros_like(ds_ref)

        rows = [pl.ds(c * CHUNK, CHUNK) for c in range(cpb)]
        d_os, qs, ks, ws, gcs, vns = ([ref[r, :] for r in rows] for ref in (do_ref, q_ref, k_ref, w_ref, gc_ref, vn_ref))
        ss = [sall_ref[0, c] for c in range(cpb)]
        gls = [gc[CHUNK - 1:CHUNK, :] for gc in gcs]
        egcs = [jnp.exp(gc) for gc in gcs]
        ekds = [jnp.exp(gl - gc) for gl, gc in zip(gls, gcs)]
        egs = [jnp.exp(gl) for gl in gls]
        qgs = [q * e for q, e in zip(qs, egcs)]
        kds = [k * e for k, e in zip(ks, ekds)]
        decs = [_decay(gc) for gc in gcs]
        qks = [jnp.where(ri >= ci, _bdot(q, k, NT) * dec, 0.0) for q, k, dec in zip(qs, ks, decs)]
        wkds = [_bdot(w_, kd, TN) for w_, kd in zip(ws, kds)]
        qk_dos = [_bdot(qk, d_o, TN) for qk, d_o in zip(qks, d_os)]
        consts = [_bdot(qg, d_o, TN) - _bdot(w_, qd, TN) for qg, d_o, w_, qd in zip(qgs, d_os, ws, qk_dos)]
        ds = ds_ref[...]
        ds_nexts = [None] * cpb
        for c in reversed(range(cpb)):
            ds_nexts[c] = ds
            ds = ds * egs[c] - _bdot(wkds[c], ds) + consts[c]
        ds_ref[...] = ds
        d_vns = [qd + _bdot(kd, dsn) for qd, kd, dsn in zip(qk_dos, kds, ds_nexts)]
        d_kds = [_bdot(vn, dsn, NT) for vn, dsn in zip(vns, ds_nexts)]
        d_qgs = [_bdot(d_o, s, NT) for d_o, s in zip(d_os, ss)]
        d_qks = [jnp.where(ri >= ci, _bdot(d_o, vn, NT), 0.0) for d_o, vn in zip(d_os, vns)]
        e_qs = [d_qk * qk for d_qk, qk in zip(d_qks, qks)]
        cols = _colsum(e_qs)
        d_bs = [d_qk * dec for d_qk, dec in zip(d_qks, decs)]
        dqs = [d_qg * egc + _bdot(d_b, k) for d_qg, egc, d_b, k in zip(d_qgs, egcs, d_bs, ks)]
        dks = [d_kd * ekd + _bdot(d_b, q, TN) for d_kd, ekd, d_b, q in zip(d_kds, ekds, d_bs, qs)]
        dws = [-_bdot(d_vn, s, NT) for d_vn, s in zip(d_vns, ss)]
        for c, r in enumerate(rows):
            d_gl = jnp.sum(d_kds[c] * kds[c]) + jnp.sum(ss[c] * ds_nexts[c]) * egs[c]
            dq_ref[r, :] = dqs[c]
            dk_ref[r, :] = dks[c]
            dgc_ref[r, :] = (_rowsum(d_qgs[c] * qgs[c]) - _rowsum(d_kds[c] * kds[c]) + _rowsum(e_qs[c]) - cols[c]
                             + jnp.where(row == CHUNK - 1, d_gl, 0.0))
            du_ref[r, :] = d_vns[c]
            dw_ref[r, :] = dws[c]

    rtile = lambda h, r: (n_r - 1 - r, h)
    vd = hv * HEAD
    return _call(body, (hv, n_r),
                 [(do, (tr, HEAD), rtile), (qkv, (tr, HEAD), lambda h, r: (n_r - 1 - r, h // 2)),
                  (qkv, (tr, HEAD), lambda h, r: (n_r - 1 - r, hk + h // 2)),
                  (w, (tr, HEAD), rtile), (gc, (tr, HEAD), rtile), (vn, (tr, HEAD), rtile),
                  (sall, (1, cpb, HEAD, HEAD), lambda h, r: (h, n_r - 1 - r, 0, 0))],
                 [((T, vd), F32, (tr, HEAD), rtile)] * 5,
                 scratch=[pltpu.VMEM((HEAD, HEAD), F32)], name="gdn_b_bwd", sem=("parallel", "arbitrary"))


def _gdn_a_bwd(du, dw, dgc_b, qkv, bb, gc, tinv, u, w, hk, hv, tr=1024):
    T = qkv.shape[0]
    tr = _tile(T, tr)
    assert tr % SUPER == 0

    def body(du_ref, dw_ref, dgcb_ref, k_ref, v_ref, b_ref, gc_ref, ti_ref, u_ref, w_ref, dk_ref, dv_ref, db_ref, dg_ref):
        ri, ci, same = _super_iota()
        utri = jnp.where(same & (ci >= ri), 1.0, 0.0)
        strict = same & (ri > ci)
        rows = [pl.ds(s * SUPER, SUPER) for s in range(tr // SUPER)]
        ks, vs, betas, gcs = ([ref[r, :] for r in rows] for ref in (k_ref, v_ref, b_ref, gc_ref))
        egcs = [jnp.exp(gc) for gc in gcs]
        kbs = [k * beta for k, beta in zip(ks, betas)]
        decs = [_decay_super(gc, ri, ci, same) for gc in gcs]
        ms = [jnp.where(strict, _bdot(kb, k, NT) * dec, 0.0) for kb, k, dec in zip(kbs, ks, decs)]
        d_rs = [_bdot(ti_ref[0, r, :], jnp.concatenate([du_ref[r, :], dw_ref[r, :]], axis=1), TN) for r in rows]
        d_ms = [jnp.where(strict, -_bdot(d_r, jnp.concatenate([u_ref[r, :], w_ref[r, :]], axis=1), NT), 0.0)
                for d_r, r in zip(d_rs, rows)]
        d_as = [d_m * dec for d_m, dec in zip(d_ms, decs)]
        e_ms = [d_m * m for d_m, m in zip(d_ms, ms)]
        d_kbs = [_bdot(d_a, k) + d_r[:, HEAD:] * egc for d_a, k, d_r, egc in zip(d_as, ks, d_rs, egcs)]
        dks = [_bdot(d_a, kb, TN) + d_kb * beta for d_a, kb, d_kb, beta in zip(d_as, kbs, d_kbs, betas)]
        cols = _colsum(e_ms)
        d_gcs = [_rowsum(e_m) - col + _rowsum(d_r[:, HEAD:] * kb * egc) + dgcb_ref[r, :]
                 for e_m, col, d_r, kb, egc, r in zip(e_ms, cols, d_rs, kbs, egcs, rows)]
        dgs = _sel_dot(utri, d_gcs)
        for r, dk, d_r, beta, v, d_kb, k, dg in zip(rows, dks, d_rs, betas, vs, d_kbs, ks, dgs):
            dk_ref[r, :] = dk
            dv_ref[r, :] = d_r[:, :HEAD] * beta
            db_ref[r, :] = _rowsum(d_r[:, :HEAD] * v) + _rowsum(d_kb * k)
            dg_ref[r, :] = dg

    tile = lambda h, r: (r, h)
    vd = hv * HEAD
    return _call(body, (hv, T // tr),
                 [(du, (tr, HEAD), tile), (dw, (tr, HEAD), tile), (dgc_b, (tr, HEAD), tile),
                  (qkv, (tr, HEAD), lambda h, r: (r, hk + h // 2)), (qkv, (tr, HEAD), lambda h, r: (r, 2 * hk + h)),
                  (bb, (tr, HEAD), tile), (gc, (tr, HEAD), tile), (tinv, (1, tr, SUPER), lambda h, r: (h, r, 0)),
                  (u, (tr, HEAD), tile), (w, (tr, HEAD), tile)],
                 [((T, vd), F32, (tr, HEAD), tile)] * 4, name="gdn_a_bwd", sem=("parallel", "parallel"))


def _pair_sum(a, b_, tr=512):
    T, vd = a.shape
    tr = _tile(T, tr)
    terms = [a] if b_ is None else [a, b_]
    n = len(terms)

    def body(*refs):
        acc = refs[0][...] + refs[1][...]
        for r in refs[2:2 * n]:
            acc = acc + r[...]
        refs[-1][...] = acc

    even = lambda j, r: (r, 2 * j)
    odd = lambda j, r: (r, 2 * j + 1)
    ins = [(t, (tr, HEAD), m) for t in terms for m in (even, odd)]
    return _call(body, (vd // HEAD // 2, T // tr), ins,
                 [((T, vd // 2), F32, (tr, HEAD), lambda j, r: (r, j))], name="gdn_pair_sum", sem=("parallel", "parallel"))[0]


def _s5_param_math(lr, li, ls, br, bi):
    step = jnp.exp(ls)
    zr, zi = lr * step, li * step
    mag = jnp.exp(zr)
    ar, ai = mag * jnp.cos(zi), mag * jnp.sin(zi)
    den = lr * lr + li * li
    nr, ni = ar - 1.0, ai
    cr, cim = (nr * lr + ni * li) / den, (ni * lr - nr * li) / den
    return ar, ai, br * cr - bi * cim, br * cim + bi * cr


def _s5_params_fwd(lr, li, ls, br, bi):
    G, P = lr.shape

    def body(lr_ref, li_ref, ls_ref, br_ref, bi_ref, ar_ref, ai_ref, bbr_ref, bbi_ref):
        ar, ai, bbr, bbi = _s5_param_math(lr_ref[...], li_ref[...], ls_ref[...], br_ref[...], bi_ref[...])
        ar_ref[...], ai_ref[...], bbr_ref[...], bbi_ref[...] = ar, ai, bbr, bbi

    shapes = [(G, P), (G, P), (G, 1), (S5_CH, G, P), (S5_CH, G, P)]
    z = lambda n: (lambda: (0,) * n)
    return _call(body, (), [(a, s, z(len(s))) for a, s in zip((lr, li, ls, br, bi), shapes)],
                 [(s, F32, s, z(len(s))) for s in (shapes[0], shapes[0], shapes[3], shapes[3])],
                 name="s5_params_fwd", sem=())


def _s5_params_bwd(lr, li, ls, br, bi, dar, dai, dbbr, dbbi):
    G, P = lr.shape

    def body(lr_ref, li_ref, ls_ref, br_ref, bi_ref, dar_ref, dai_ref, dbr_ref, dbi_ref, o0, o1, o2, o3, o4):
        _, vjp = jax.vjp(_s5_param_math, lr_ref[...], li_ref[...], ls_ref[...], br_ref[...], bi_ref[...])
        outs = vjp((dar_ref[...], dai_ref[...], dbr_ref[...], dbi_ref[...]))
        for r, v in zip((o0, o1, o2, o3, o4), outs):
            r[...] = v

    shapes = [(G, P), (G, P), (G, 1), (S5_CH, G, P), (S5_CH, G, P)]
    z = lambda n: (lambda: (0,) * n)
    ins = list(zip((lr, li, ls, br, bi), shapes)) + list(zip((dar, dai, dbbr, dbbi), (shapes[0], shapes[0], shapes[3], shapes[3])))
    return _call(body, (), [(a, s, z(len(s))) for a, s in ins], [(s, F32, s, z(len(s))) for s in shapes],
                 name="s5_params_bwd", sem=())


def _s5_bproj_fwd(u, bd_re, bd_im, tr=512):
    T, D = u.shape
    tr = _tile(T, tr)
    nb = D // LANES

    def body(u_ref, br_ref, bi_ref, or_ref, oi_ref):
        ub = u_ref[...]
        or_ref[...] = _bdot(ub, br_ref[0])
        oi_ref[...] = _bdot(ub, bi_ref[0])

    blk = lambda i, j: (j, 0, 0)
    return _call(body, (T // tr, nb),
                 [(u, (tr, LANES), lambda i, j: (i, j)), (bd_re, (1, LANES, S5_SPB), blk), (bd_im, (1, LANES, S5_SPB), blk)],
                 [((T, nb * S5_SPB), F32, (tr, S5_SPB), lambda i, j: (i, j))] * 2, name="s5_bproj_fwd", sem=("parallel", "parallel"))


def _s5_scan(br, bi, lam, reverse, xr=None, xi=None, tl=512, bw=512):
    T, NCH = br.shape
    tl, bw = _tile(T, tl), _tile(NCH, bw)
    n_t, n_g = T // tl, tl // 8

    def body(*refs):
        if reverse:
            br_ref, bi_ref, lam_ref, sr_ref, si_ref, hr_ref, hi_ref, or_ref, oi_ref, dl_ref, cr, ci_ = refs
        else:
            br_ref, bi_ref, lam_ref, or_ref, oi_ref, cr, ci_ = refs
        t = pl.program_id(1)
        a_r = lam_ref[0:1, :]
        a_i = -lam_ref[1:2, :] if reverse else lam_ref[1:2, :]
        powers = [(a_r, a_i)]
        for _ in range(2):
            p_r, p_i = powers[-1]
            powers.append((p_r * p_r - p_i * p_i, 2.0 * p_r * p_i))
        row = lax.broadcasted_iota(jnp.int32, (8, bw), 0)

        def scan8(x_r, x_i):
            for level, (p_r, p_i) in enumerate(powers):
                s = 1 << level
                keep = (row < 8 - s) if reverse else (row >= s)
                s_r = jnp.where(keep, pltpu.roll(x_r, 8 - s if reverse else s, 0), 0.0)
                s_i = jnp.where(keep, pltpu.roll(x_i, 8 - s if reverse else s, 0), 0.0)
                x_r, x_i = x_r + p_r * s_r - p_i * s_i, x_i + p_r * s_i + p_i * s_r
            return x_r, x_i

        edge = 7 if reverse else 0
        tab_r, tab_i = scan8(jnp.where(row == edge, a_r, 0.0), jnp.where(row == edge, a_i, 0.0))

        @pl.when(t == 0)
        def _():
            cr[...] = jnp.zeros_like(cr)
            ci_[...] = jnp.zeros_like(ci_)
            if reverse:
                dl_ref[...] = jnp.zeros_like(dl_ref)

        if reverse:
            first_block = t == n_t - 1
            halo_r = jnp.where(first_block, 0.0, hr_ref[7:8, :])
            halo_i = jnp.where(first_block, 0.0, hi_ref[7:8, :])

        def group(n, carry):
            g = n_g - 1 - n if reverse else n
            rows = pl.ds(pl.multiple_of(g * 8, 8), 8)
            c_r, c_i = carry[0], carry[1]
            x_r, x_i = scan8(br_ref[rows, :], bi_ref[rows, :])
            x_r, x_i = x_r + tab_r * c_r - tab_i * c_i, x_i + tab_r * c_i + tab_i * c_r
            or_ref[rows, :], oi_ref[rows, :] = x_r, x_i
            out = 7 - edge
            nxt = (x_r[out:out + 1, :], x_i[out:out + 1, :])
            if not reverse:
                return nxt
            before = pl.ds(pl.multiple_of(jnp.maximum(g * 8 - 8, 0), 8), 8)
            h_r = jnp.where(g > 0, sr_ref[before, :][7:8, :], halo_r)
            h_i = jnp.where(g > 0, si_ref[before, :][7:8, :], halo_i)
            s_r = jnp.where(row == 0, h_r, pltpu.roll(sr_ref[rows, :], 1, 0))
            s_i = jnp.where(row == 0, h_i, pltpu.roll(si_ref[rows, :], 1, 0))
            return nxt + (carry[2] + s_r * x_r + s_i * x_i, carry[3] + s_r * x_i - s_i * x_r)

        init = (cr[0:1, :], ci_[0:1, :])
        if reverse:
            init = init + (jnp.zeros((8, bw), F32), jnp.zeros((8, bw), F32))
        fin = lax.fori_loop(0, n_g, group, init, unroll=4 if n_g % 4 == 0 else 1)
        cr[0:1, :], ci_[0:1, :] = fin[0], fin[1]
        if reverse:
            dl_ref[0:1, :] += jnp.sum(fin[2], axis=0, keepdims=True)
            dl_ref[1:2, :] += jnp.sum(fin[3], axis=0, keepdims=True)

    tmap = (lambda c, t: (n_t - 1 - t, c)) if reverse else (lambda c, t: (t, c))
    col = lambda c, t: (0, c)
    ins = [(br, (tl, bw), tmap), (bi, (tl, bw), tmap), (lam, (2, bw), col)]
    outs = [((T, NCH), F32, (tl, bw), tmap)] * 2
    if reverse:
        halo = lambda c, t: (jnp.maximum((n_t - 1 - t) * n_g - 1, 0), c)
        ins += [(xr, (tl, bw), tmap), (xi, (tl, bw), tmap), (xr, (8, bw), halo), (xi, (8, bw), halo)]
        outs += [((2, NCH), F32, (2, bw), col)]
    return _call(body, (NCH // bw, n_t), ins, outs, scratch=[pltpu.VMEM((8, bw), F32), pltpu.VMEM((8, bw), F32)],
                 name="s5_scan_bwd" if reverse else "s5_scan_fwd", sem=("parallel", "arbitrary"))


def _s5_cproj_fwd(xr, xi, cd_re, cd_im, u, d, tr=512):
    T, D = u.shape
    tr = _tile(T, tr)

    def body(xr_ref, xi_ref, cr_ref, ci_ref, u_ref, d_ref, y_ref, h_ref):
        y = _bdot(xr_ref[...], cr_ref[0]) + _bdot(xi_ref[...], ci_ref[0]) + d_ref[...] * u_ref[...]
        y_ref[...] = y
        h_ref[...] = jax.nn.gelu(y).astype(BF16)

    tile = lambda i, j: (i, j)
    blk = lambda i, j: (j, 0, 0)
    return _call(body, (T // tr, D // LANES),
                 [(xr, (tr, S5_SPB), tile), (xi, (tr, S5_SPB), tile), (cd_re, (1, S5_SPB, LANES), blk), (cd_im, (1, S5_SPB, LANES), blk),
                  (u, (tr, LANES), tile), (d.reshape(1, D), (1, LANES), lambda i, j: (0, j))],
                 [((T, D), F32, (tr, LANES), tile), ((T, D), BF16, (tr, LANES), tile)], name="s5_cproj_fwd", sem=("parallel", "parallel"))


def _s5_cproj_bwd(dy, xr, xi, cd_re, cd_im, u, d, tr=512):
    T, D = u.shape
    tr = _tile(T, tr)
    nb = D // LANES

    def body(dy_ref, xr_ref, xi_ref, cr_ref, ci_ref, u_ref, d_ref, dxr_ref, dxi_ref, du_ref, dd_ref, dcr_ref, dci_ref):
        g = dy_ref[...]
        dxr_ref[...] = _bdot(g, cr_ref[0], NT)
        dxi_ref[...] = _bdot(g, ci_ref[0], NT)
        du_ref[...] = g * d_ref[...]
        first = pl.program_id(1) == 0
        _accumulate(dd_ref, jnp.sum(g * u_ref[...], axis=0, keepdims=True), first)

        @pl.when(first)
        def _():
            dcr_ref[...] = jnp.zeros_like(dcr_ref)
            dci_ref[...] = jnp.zeros_like(dci_ref)

        dcr_ref[0] += _bdot(xr_ref[...], g, TN)
        dci_ref[0] += _bdot(xi_ref[...], g, TN)

    tile = lambda j, i: (i, j)
    blk = lambda j, i: (j, 0, 0)
    col = lambda j, i: (0, j)
    return _call(body, (nb, T // tr),
                 [(dy, (tr, LANES), tile), (xr, (tr, S5_SPB), tile), (xi, (tr, S5_SPB), tile),
                  (cd_re, (1, S5_SPB, LANES), blk), (cd_im, (1, S5_SPB, LANES), blk), (u, (tr, LANES), tile), (d.reshape(1, D), (1, LANES), col)],
                 [((T, nb * S5_SPB), F32, (tr, S5_SPB), tile)] * 2 + [((T, D), F32, (tr, LANES), tile), ((1, D), F32, (1, LANES), col)]
                 + [((nb, S5_SPB, LANES), F32, (1, S5_SPB, LANES), blk)] * 2, name="s5_cproj_bwd")


def _s5_bproj_bwd(dbr, dbi, bd_re, bd_im, u, du_skip, tr=512):
    T, D = u.shape
    tr = _tile(T, tr)
    nb = D // LANES

    def body(gr_ref, gi_ref, br_ref, bi_ref, u_ref, ds_ref, du_ref, dbr_ref, dbi_ref):
        g_r, g_i, ub = gr_ref[...], gi_ref[...], u_ref[...]
        du_ref[...] = (ds_ref[...] + _bdot(g_r, br_ref[0], NT) + _bdot(g_i, bi_ref[0], NT)).astype(BF16)

        @pl.when(pl.program_id(1) == 0)
        def _():
            dbr_ref[...] = jnp.zeros_like(dbr_ref)
            dbi_ref[...] = jnp.zeros_like(dbi_ref)

        dbr_ref[0] += _bdot(ub, g_r, TN)
        dbi_ref[0] += _bdot(ub, g_i, TN)

    tile = lambda j, i: (i, j)
    blk = lambda j, i: (j, 0, 0)
    return _call(body, (nb, T // tr),
                 [(dbr, (tr, S5_SPB), tile), (dbi, (tr, S5_SPB), tile), (bd_re, (1, LANES, S5_SPB), blk), (bd_im, (1, LANES, S5_SPB), blk),
                  (u, (tr, LANES), tile), (du_skip, (tr, LANES), tile)],
                 [((T, D), BF16, (tr, LANES), tile)] + [((nb, LANES, S5_SPB), F32, (1, LANES, S5_SPB), blk)] * 2, name="s5_bproj_bwd")


def _s5_gate_fwd(h, vg, tr=256):
    T, D = h.shape
    tr = _tile(T, tr)

    def body(h_ref, a_ref, b_ref, o_ref):
        o_ref[...] = h_ref[...] + a_ref[...] * jax.nn.sigmoid(b_ref[...])

    row = lambda i: (i, 0)
    return _call(body, (T // tr,), [(h, (tr, D), row), (vg, (tr, D), row), (vg, (tr, D), lambda i: (i, 1))],
                 [((T, D), F32, (tr, D), row)], name="s5_gate_fwd", sem=("parallel",))[0]


def _s5_gate_bwd(dh, vg, tr=256):
    T, D = dh.shape
    tr = _tile(T, tr)

    def body(d_ref, a_ref, b_ref, o_ref):
        d = d_ref[...]
        sig = jax.nn.sigmoid(b_ref[...])
        o_ref[:, :D] = (d * sig).astype(BF16)
        o_ref[:, D:] = (d * a_ref[...] * sig * (1.0 - sig)).astype(BF16)

    row = lambda i: (i, 0)
    return _call(body, (T // tr,), [(dh, (tr, D), row), (vg, (tr, D), row), (vg, (tr, D), lambda i: (i, 1))],
                 [((T, 2 * D), BF16, (tr, 2 * D), row)], name="s5_gate_bwd", sem=("parallel",))[0]


def _block_diag(w, transpose):
    g, a, b = w.shape
    if transpose:
        w = w.transpose(0, 2, 1)
        a, b = b, a
    eye = jnp.eye(S5_GPB, dtype=w.dtype)
    return jnp.einsum("jgab,gh->jgahb", w.reshape(g // S5_GPB, S5_GPB, a, b), eye).reshape(g // S5_GPB, S5_GPB * a, S5_GPB * b)


def _block_diag_extract(wd, a, b, transpose):
    if transpose:
        a, b = b, a
    nb = wd.shape[0]
    eye = jnp.eye(S5_GPB, dtype=wd.dtype)
    w = jnp.einsum("jgahb,gh->jgab", wd.reshape(nb, S5_GPB, a, S5_GPB, b), eye).reshape(nb * S5_GPB, a, b)
    return w.transpose(0, 2, 1) if transpose else w


def _mesh_position():
    return lax.axis_index("x"), lax.axis_index("y"), lax.axis_index("c")


def _my_index():
    x, y, c = _mesh_position()
    return 4 * x + 2 * y + c


def _hbm_call(body, arrays, out_shapes, n_sems, name):
    n = len(arrays)
    return pl.pallas_call(
        body, out_shape=[jax.ShapeDtypeStruct(s, d) for s, d in out_shapes],
        in_specs=[pl.BlockSpec(memory_space=pl.ANY)] * n, out_specs=[pl.BlockSpec(memory_space=pl.ANY)] * len(out_shapes),
        scratch_shapes=[pltpu.SemaphoreType.DMA((n_sems,)), pltpu.SemaphoreType.DMA((n_sems,)), pltpu.SemaphoreType.DMA((n,))],
        name=name)(*arrays)


def _all_gather(blocks, name):
    n = len(blocks)
    per = N_DEV - 1

    def body(*refs):
        x_refs, out_refs = refs[:n], refs[n:2 * n]
        send_sems, recv_sems, local_sems = refs[2 * n:]
        x, y, c = _mesh_position()
        me, sibling = (x, y, c), (x, y, 1 - c)
        chips = [(1 - x, y), (x, 1 - y), (1 - x, 1 - y)]

        def copy(a, k, blk, to, src=None):
            slot = out_refs[a].at[4 * blk[0] + 2 * blk[1] + blk[2]]
            return pltpu.make_async_remote_copy(
                src_ref=slot if src is None else src, dst_ref=slot, send_sem=send_sems.at[a * per + k],
                recv_sem=recv_sems.at[a * per + k], device_id=to, device_id_type=pl.DeviceIdType.MESH)

        mine = [pltpu.make_async_copy(x_refs[a], out_refs[a].at[4 * x + 2 * y + c], local_sems.at[a]) for a in range(n)]
        for cp in mine:
            cp.start()
        first = []
        for a in range(n):
            first.append(copy(a, 0, me, sibling, src=x_refs[a]))
            first += [copy(a, 1 + j, me, (*chip, c), src=x_refs[a]) for j, chip in enumerate(chips)]
        for cp in first:
            cp.start()
        passed = []
        for j, chip in enumerate(chips):
            for a in range(n):
                copy(a, 1 + j, (*chip, c), me).wait_recv()
                passed.append(copy(a, 4 + j, (*chip, c), sibling))
                passed[-1].start()
        for a in range(n):
            copy(a, 0, sibling, me).wait_recv()
            for j, chip in enumerate(chips):
                copy(a, 4 + j, (*chip, 1 - c), me).wait_recv()
        for cp in first + passed:
            cp.wait_send()
        for cp in mine:
            cp.wait()

    return _hbm_call(body, blocks, [((N_DEV,) + b.shape, b.dtype) for b in blocks], n * per, name)


def _pair_exchange(parts, name):
    n = len(parts)

    def body(*refs):
        g_refs, got_refs = refs[:n], refs[n:2 * n]
        send_sems, recv_sems = refs[2 * n:]
        x, y, c = _mesh_position()
        places = [(x, y), (1 - x, y), (x, 1 - y), (1 - x, 1 - y)]
        copies = [pltpu.make_async_remote_copy(
            src_ref=g_refs[a].at[4 * px + 2 * py + 1 - c], dst_ref=got_refs[a].at[k],
            send_sem=send_sems.at[4 * a + k], recv_sem=recv_sems.at[4 * a + k],
            device_id=(x, y, 1 - c), device_id_type=pl.DeviceIdType.MESH) for a in range(n) for k, (px, py) in enumerate(places)]
        for cp in copies:
            cp.start()
        for cp in copies:
            cp.wait()

    return pl.pallas_call(
        body, out_shape=[jax.ShapeDtypeStruct((4,) + p_.shape[1:], p_.dtype) for p_ in parts],
        in_specs=[pl.BlockSpec(memory_space=pl.ANY)] * n, out_specs=[pl.BlockSpec(memory_space=pl.ANY)] * n,
        scratch_shapes=[pltpu.SemaphoreType.DMA((4 * n,))] * 2, name=name)(*parts)


def _chip_sums(own, got, tr=PACK_ROWS):
    _, R, cw = own.shape
    cap = min(tr, 1 << (((PACK_ROWS * PACK_W) // cw).bit_length() - 1))
    tr = math.gcd(R, cap)

    def body(a_ref, b_ref, o_ref):
        o_ref[...] = (a_ref[...].astype(F32) + b_ref[...].astype(F32)).astype(o_ref.dtype)

    blk = lambda j, i: (1 + j, i, 0)
    return _call(body, (3, R // tr), [(own, (1, tr, cw), blk), (got, (1, tr, cw), blk)],
                 [((3, R, cw), own.dtype, (1, tr, cw), lambda j, i: (j, i, 0))], name="chip_sums", sem=("parallel", "parallel"))[0]


def _chip_exchange(sums, name):
    n = len(sums)

    def body(*refs):
        s_refs, out_refs = refs[:n], refs[n:2 * n]
        send_sems, recv_sems = refs[2 * n:]
        x, y, c = _mesh_position()
        chips = [(1 - x, y), (x, 1 - y), (1 - x, 1 - y)]
        copies = [pltpu.make_async_remote_copy(
            src_ref=s_refs[a].at[j], dst_ref=out_refs[a].at[j], send_sem=send_sems.at[3 * a + j], recv_sem=recv_sems.at[3 * a + j],
            device_id=(*chip, c), device_id_type=pl.DeviceIdType.MESH) for a in range(n) for j, chip in enumerate(chips)]
        for cp in copies:
            cp.start()
        for cp in copies:
            cp.wait()

    return pl.pallas_call(
        body, out_shape=[jax.ShapeDtypeStruct(s.shape, s.dtype) for s in sums],
        in_specs=[pl.BlockSpec(memory_space=pl.ANY)] * n, out_specs=[pl.BlockSpec(memory_space=pl.ANY)] * n,
        scratch_shapes=[pltpu.SemaphoreType.DMA((3 * n,))] * 2, name=name)(*sums)


def _adamw_math(g, w, m, v):
    nm = ADAM_B1 * m + (1.0 - ADAM_B1) * g
    nv = ADAM_B2 * v + (1.0 - ADAM_B2) * (g * g)
    c1 = 1.0 - ADAM_B1 ** ADAM_STEP
    c2 = 1.0 - ADAM_B2 ** ADAM_STEP
    return -ADAM_LR * ((nm / c1) / (jnp.sqrt(nv / c2) + ADAM_EPS) + ADAM_WD * w), nm, nv


def _adamw(parts, row0, w, m, v, name, window_n8=None):
    R, C = w.shape
    cw = parts[0][0].shape[2]
    cap = min(PACK_ROWS, 1 << (((PACK_ROWS * PACK_W) // cw).bit_length() - 1))
    tr = math.gcd(math.gcd(R, cap), row0 or R)
    assert R % tr == 0 and row0 % tr == 0
    n = len(parts)

    def body(*refs):
        p_refs = refs[:n]
        w_ref, m_ref, v_ref, g_ref, d_ref, nm_ref, nv_ref = refs[n:]
        g = None
        for p_ref, (_, slots) in zip(p_refs, parts):
            for s in range(len(slots)):
                term = p_ref[s].astype(F32)
                g = term if g is None else g + term
        if window_n8 is not None:
            off = (window_n8 * _my_index()) % LANES
            g = pltpu.roll(g, (cw - off) % cw, 1)[:, :C]
        d, nm, nv = _adamw_math(g, w_ref[...], m_ref[...], v_ref[...])
        g_ref[...], d_ref[...], nm_ref[...], nv_ref[...] = g, d, nm, nv

    row = lambda i: (i, 0)
    r0 = row0 // tr
    ins = []
    for arr, slots in parts:
        assert list(slots) == list(range(slots[0], slots[0] + len(slots))) and slots[0] % len(slots) == 0
        s0 = slots[0] // len(slots)
        ins.append((arr, (len(slots), tr, cw), lambda i, s0=s0: (s0, r0 + i, 0)))
    return _call(body, (R // tr,), ins + [(w, (tr, C), row), (m, (tr, C), row), (v, (tr, C), row)],
                 [((R, C), F32, (tr, C), row)] * 4, name=name, sem=("parallel",))


def _window_geometry(n8):
    offs = [(d * n8) % LANES for d in range(N_DEV)]
    starts = [(d * n8) // LANES for d in range(N_DEV)]
    blocks = max(-(-(o + n8) // LANES) for o in offs)
    return starts, blocks, max(starts) + blocks


def _to_window(wpad, n8, tr=256):
    R, cw = wpad.shape
    tr = _tile(R, tr)

    def body(x_ref, o_ref):
        o_ref[...] = pltpu.roll(x_ref[...], (n8 * _my_index()) % LANES, 1).astype(BF16)

    row = lambda i: (i, 0)
    return _call(body, (R // tr,), [(wpad, (tr, cw), row)], [((R, cw), BF16, (tr, cw), row)], name="to_window", sem=("parallel",))[0]


def _from_windows(win, row0, rows, n8, tr=128):
    starts, blocks, total = _window_geometry(n8)
    cw = win.shape[2]
    tr = _tile(rows, tr)
    r0 = row0 // tr

    def body(w_ref, o_ref):
        o_ref[...] = jnp.zeros_like(o_ref)
        for d in range(N_DEV):
            cols = pl.ds(starts[d] * LANES, cw)
            o_ref[:, cols] = (o_ref[:, cols].astype(F32) + w_ref[d].astype(F32)).astype(BF16)

    return _call(body, (rows // tr,), [(win, (N_DEV, tr, cw), lambda i: (0, r0 + i, 0))],
                 [((rows, total * LANES), BF16, (tr, total * LANES), lambda i: (i, 0))], name="from_windows", sem=("parallel",))[0]


def _pack(flat_pieces, dtype, lead=()):
    cat = jnp.concatenate([p_.astype(dtype) for p_ in flat_pieces], axis=-1)
    n = cat.shape[-1]
    quantum = PACK_ROWS * PACK_W
    total = -(-n // quantum) * quantum
    cat = jnp.pad(cat, [(0, 0)] * len(lead) + [(0, total - n)])
    return cat.reshape(lead + (total // PACK_W, PACK_W)), n


REPLICATED = ("norm_mix", "norm_mlp", "norm_ple", "norm_final", "gdn_a_log", "gdn_dt_bias", "gdn_o_norm",
              "s5_lam_re", "s5_lam_im", "s5_log_step", "s5_b_re", "s5_b_im", "s5_c_re", "s5_c_im")
WEIGHTS = ("norm_mix", "norm_mlp", "norm_ple", "norm_final", "gdn_w_in", "gdn_conv_w", "gdn_a_log", "gdn_dt_bias",
           "gdn_o_norm", "gdn_w_out", "s5_w_in", "s5_lam_re", "s5_lam_im", "s5_log_step", "s5_b_re", "s5_b_im",
           "s5_c_re", "s5_c_im", "s5_d", "s5_w_out", "mlp_w_up", "mlp_w_down", "ple_w_proj", "ple_w_gate")
ROW_GROUP = ("mlp_w_down", "gdn_w_out", "s5_w_in", "ple_w_gate")
COL_SHARDED = ("mlp_w_up", "s5_w_out", "ple_w_proj")


def _rows2d(a):
    return a.reshape(-1, a.shape[-1])


def _misc_pack(conv, s5d):
    cw = conv.shape[-1]
    rows = jnp.concatenate([_rows2d(conv), jnp.pad(s5d, ((0, 0), (0, cw - s5d.shape[-1])))], axis=0)
    return jnp.pad(rows, ((0, -rows.shape[0] % 8), (0, 0)))


def _misc_unpack(a, conv_rows, s5d_shape):
    return a[:conv_rows], a[conv_rows:conv_rows + s5d_shape[0], :s5d_shape[1]]


def kernel(x, p, norm_mix, norm_mlp, norm_ple, norm_final, gdn_w_in, gdn_conv_w, gdn_a_log, gdn_dt_bias, gdn_o_norm, gdn_w_out, s5_w_in, s5_lam_re, s5_lam_im, s5_log_step, s5_b_re, s5_b_im, s5_c_re, s5_c_im, s5_d, s5_w_out, mlp_w_up, mlp_w_down, ple_w_proj, ple_w_gate, loss_target, m_norm_mix, m_norm_mlp, m_norm_ple, m_norm_final, m_gdn_w_in, m_gdn_conv_w, m_gdn_a_log, m_gdn_dt_bias, m_gdn_o_norm, m_gdn_w_out, m_s5_w_in, m_s5_lam_re, m_s5_lam_im, m_s5_log_step, m_s5_b_re, m_s5_b_im, m_s5_c_re, m_s5_c_im, m_s5_d, m_s5_w_out, m_mlp_w_up, m_mlp_w_down, m_ple_w_proj, m_ple_w_gate, v_norm_mix, v_norm_mlp, v_norm_ple, v_norm_final, v_gdn_w_in, v_gdn_conv_w, v_gdn_a_log, v_gdn_dt_bias, v_gdn_o_norm, v_gdn_w_out, v_s5_w_in, v_s5_lam_re, v_s5_lam_im, v_s5_log_step, v_s5_b_re, v_s5_b_im, v_s5_c_re, v_s5_c_im, v_s5_d, v_s5_w_out, v_mlp_w_up, v_mlp_w_down, v_ple_w_proj, v_ple_w_gate):
    args = dict(locals())
    w = {n: args[n] for n in WEIGHTS}
    mom = {n: args["m_" + n] for n in WEIGHTS}
    vel = {n: args["v_" + n] for n in WEIGHTS}
    depth = norm_mix.shape[0]
    T, D = x.shape[1], x.shape[2]
    hv = gdn_a_log.shape[1]
    vd = hv * HEAD
    n_gdn, n_s5 = gdn_w_in.shape[0], s5_w_in.shape[0]
    cw = gdn_conv_w.shape[2]
    cd = cw * N_DEV
    hk = (cd - vd) // (2 * HEAD)
    assert hv == 2 * hk and 2 * hv <= LANES and T % SUPER == 0 and SUPER % CHUNK == 0
    G, P = s5_lam_re.shape[1], s5_lam_re.shape[2]
    assert P == S5_STATE and G * S5_CH == D and G % S5_GPB == 0 and D // N_DEV <= cw
    n8 = gdn_w_in.shape[2]
    win_starts, win_blocks, win_total = _window_geometry(n8)
    cwin = win_blocks * LANES
    assert win_total * LANES == cd + vd + LANES

    row_off, off = {}, 0
    for n in ROW_GROUP:
        row_off[n] = off
        off += w[n].shape[0] * w[n].shape[1]
    gathered = _all_gather(
        [_to_window(jnp.pad(_rows2d(gdn_w_in), ((0, 0), (0, cwin - n8))), n8),
         jnp.concatenate([_rows2d(w[n]) for n in ROW_GROUP], axis=0).astype(BF16)]
        + [_rows2d(w[n]).astype(BF16) for n in COL_SHARDED] + [_misc_pack(gdn_conv_w, s5_d)], "gather_weights")
    g_win, g_row, g_misc = gathered[0], gathered[1], gathered[-1]
    g_col = dict(zip(COL_SHARDED, gathered[2:-1]))
    conv_full = g_misc[:, :n_gdn * 4].reshape(N_DEV, n_gdn, 4, cw).transpose(1, 2, 0, 3).reshape(n_gdn, 4, cd)
    s5d_full = g_misc[:, n_gdn * 4:n_gdn * 4 + n_s5, :D // N_DEV].transpose(1, 0, 2).reshape(n_s5, D)

    def weight(name, l):
        r = w[name].shape[1]
        if name in ROW_GROUP:
            return Sharded(g_row, 0, row_off[name] + l * r, r)
        return Sharded(g_col[name], 1, l * r, r)

    h = x[0]
    tgt = loss_target[0]
    grads = {n: [None] * w[n].shape[0] for n in WEIGHTS if n != "norm_final"}
    saved = []
    add = lambda acc, r: (r + acc,)

    for i in range(depth):
        j = i // 2
        sv = {"h0": h}
        hn = _rms_fwd(h, norm_mix[i])
        sv["hn"] = hn
        if i % 2 == 0:
            w_in = _from_windows(g_win, j * D, D, n8)
            pq = _mm(hn, w_in[:, :cd], name="gdn_in_qkv")[0]
            pz = _mm(hn, w_in[:, cd:cd + vd], name="gdn_in_z")[0]
            ba = _mm(hn, w_in[:, cd + vd:], name="gdn_in_ba")[0]
            qkv = _gdn_pre_fwd(pq, conv_full[j], hk)
            pv = jnp.pad(jnp.stack([gdn_a_log[j], gdn_dt_bias[j]]), ((0, 0), (hv, LANES - 2 * hv)))
            g2 = _gates_fwd(ba, pv, hv)
            bb = jnp.repeat(g2[:, :hv], HEAD, axis=1)
            gb = jnp.repeat(g2[:, hv:2 * hv], HEAD, axis=1)
            u, ww, gc, tinv = _gdn_a_fwd(qkv, gb, bb, hk, hv)
            o, vn, sall = _gdn_b_fwd(qkv, u, ww, gc, hk, hv)
            on = _ogate_fwd(o, pz, gdn_o_norm[j])
            h = _mm(on, weight("gdn_w_out", j), epi=add, extras=(h,), name="gdn_out")[0]
            sv.update(w_in=w_in, pq=pq, pz=pz, ba=ba, pv=pv, qkv=qkv, bb=bb, u=u, ww=ww, gc=gc, tinv=tinv, o=o, vn=vn, sall=sall, on=on)
        else:
            uu = _mm(hn, weight("s5_w_in", j), name="s5_in")[0]
            b_re_t, b_im_t = s5_b_re[j].transpose(2, 0, 1), s5_b_im[j].transpose(2, 0, 1)
            ls = s5_log_step[j].reshape(G, 1)
            ar, ai, bbr, bbi = _s5_params_fwd(s5_lam_re[j], s5_lam_im[j], ls, b_re_t, b_im_t)
            lam = jnp.stack([ar.reshape(-1), ai.reshape(-1)])
            bd_re = _block_diag(bbr.transpose(1, 2, 0), transpose=True).astype(BF16)
            bd_im = _block_diag(bbi.transpose(1, 2, 0), transpose=True).astype(BF16)
            cd_re = _block_diag(s5_c_re[j], transpose=True).astype(BF16)
            cd_im = _block_diag(-s5_c_im[j], transpose=True).astype(BF16)
            bur, bui = _s5_bproj_fwd(uu, bd_re, bd_im)
            xr, xi = _s5_scan(bur, bui, lam, reverse=False)
            dsk = s5d_full[j]
            yy, hact = _s5_cproj_fwd(xr, xi, cd_re, cd_im, uu, dsk)
            vg = _mm(hact, weight("s5_w_out", j), name="s5_out")[0]
            h = _s5_gate_fwd(h, vg)
            sv.update(uu=uu, b_re_t=b_re_t, b_im_t=b_im_t, ls=ls, lam=lam, bd_re=bd_re, bd_im=bd_im, cd_re=cd_re, cd_im=cd_im,
                      xr=xr, xi=xi, dsk=dsk, yy=yy, hact=hact, vg=vg)
        sv["h1"] = h
        hm = _rms_fwd(h, norm_mlp[i])
        up, act = _mm(hm, weight("mlp_w_up", i), out_dtypes=(F32, BF16),
                      epi=lambda acc: (acc, jnp.square(jnp.maximum(acc, 0.0))), name="mlp_up")
        h = _mm(act, weight("mlp_w_down", i), epi=add, extras=(h,), name="mlp_down")[0]
        sv.update(hm=hm, up=up, act=act, h2=h)
        hp = _rms_fwd(h, norm_ple[i])
        s_gate = _mm(hp, weight("ple_w_gate", i), name="ple_gate")[0]
        pp = _mm(p[i, 0], weight("ple_w_proj", i), name="ple_proj")[0]
        h = _ple_fwd(h, s_gate, pp)
        sv.update(hp=hp, s_gate=s_gate, pp=pp)
        saved.append(sv)

    dh, d_norm_final, loss_part = _loss_fwd_bwd(h, norm_final, tgt)
    loss = lax.psum(loss_part[0, 0], MESH_AXES)

    dw = lambda a, b_, axis, name: _mm(a, b_, "tn", out_dtypes=(BF16,), out_axis=axis, name=name)[0]
    for i in reversed(range(depth)):
        j = i // 2
        sv = saved[i]
        ds, dpp = _ple_bwd(dh, sv["s_gate"], sv["pp"])
        grads["ple_w_proj"][i] = dw(p[i, 0], dpp, 1, "ple_proj_dw")
        grads["ple_w_gate"][i] = dw(sv["hp"], ds, 0, "ple_gate_dw")
        d_hp = _mm(ds, weight("ple_w_gate", i), "nt", name="ple_gate_dx")[0]
        dh, dh_b, grads["norm_ple"][i] = _rms_bwd(d_hp, sv["h2"], norm_ple[i], dh)
        grads["mlp_w_down"][i] = dw(sv["act"], dh_b, 0, "mlp_down_dw")
        d_up = _mm(dh_b, weight("mlp_w_down", i), "nt", out_dtypes=(BF16,),
                   epi=lambda acc, up_: (acc * 2.0 * jnp.maximum(up_, 0.0),), extras=(sv["up"],), name="mlp_down_dx")[0]
        grads["mlp_w_up"][i] = dw(sv["hm"], d_up, 1, "mlp_up_dw")
        d_hm = _mm(d_up, weight("mlp_w_up", i), "nt", name="mlp_up_dx")[0]
        dh, dh_b, grads["norm_mlp"][i] = _rms_bwd(d_hm, sv["h1"], norm_mlp[i], dh)
        if i % 2 == 0:
            grads["gdn_w_out"][j] = dw(sv["on"], dh_b, 0, "gdn_out_dw")
            d_on = _mm(dh_b, weight("gdn_w_out", j), "nt", name="gdn_out_dx")[0]
            d_o, d_z, grads["gdn_o_norm"][j] = _ogate_bwd(d_on, sv["o"], sv["pz"], gdn_o_norm[j])
            dq_b, dk_b, dgc_b, d_u, d_w = _gdn_b_bwd(d_o, sv["qkv"], sv["ww"], sv["gc"], sv["vn"], sv["sall"], hk, hv)
            dk_a, d_v, d_bb, d_gb = _gdn_a_bwd(d_u, d_w, dgc_b, sv["qkv"], sv["bb"], sv["gc"], sv["tinv"], sv["u"], sv["ww"], hk, hv)
            d_qkv = jnp.concatenate([_pair_sum(dq_b, None), _pair_sum(dk_a, dk_b), d_v], axis=1)
            d_c, grads["gdn_conv_w"][j] = _gdn_pre_bwd(d_qkv, sv["pq"], conv_full[j], hk)
            d_pq = _gdn_conv_bwd(d_c, conv_full[j])
            d_g2 = jnp.pad(jnp.concatenate([d_bb[:, ::HEAD], d_gb[:, ::HEAD]], axis=1), ((0, 0), (0, LANES - 2 * hv)))
            d_ba, d_pv = _gates_bwd(d_g2, sv["ba"], sv["pv"], hv)
            grads["gdn_a_log"][j] = d_pv[0, hv:2 * hv]
            grads["gdn_dt_bias"][j] = d_pv[1, hv:2 * hv]
            hn, w_in = sv["hn"], sv["w_in"]
            dw_nat = jnp.concatenate([dw(hn, d_pq, None, "gdn_in_qkv_dw"), dw(hn, d_z, None, "gdn_in_z_dw"),
                                      dw(hn, d_ba, None, "gdn_in_ba_dw")], axis=1)
            grads["gdn_w_in"][j] = jnp.stack([dw_nat[:, s * LANES:s * LANES + cwin] for s in win_starts])
            d_hn = _mm(d_pq, w_in[:, :cd], "nt", name="gdn_in_qkv_dx")[0]
            d_hn = _mm(d_z, w_in[:, cd:cd + vd], "nt", epi=add, extras=(d_hn,), name="gdn_in_z_dx")[0]
            d_hn = _mm(d_ba, w_in[:, cd + vd:], "nt", epi=add, extras=(d_hn,), name="gdn_in_ba_dx")[0]
        else:
            d_vg = _s5_gate_bwd(dh, sv["vg"])
            grads["s5_w_out"][j] = dw(sv["hact"], d_vg, 1, "s5_out_dw")

            def gelu_bwd(acc, y_):
                _, vjp = jax.vjp(jax.nn.gelu, y_)
                return (vjp(acc)[0],)

            d_y = _mm(d_vg, weight("s5_w_out", j), "nt", epi=gelu_bwd, extras=(sv["yy"],), name="s5_out_dx")[0]
            d_xr, d_xi, du_skip, d_dsk, d_cdr, d_cdi = _s5_cproj_bwd(d_y, sv["xr"], sv["xi"], sv["cd_re"], sv["cd_im"], sv["uu"], sv["dsk"])
            grads["s5_d"][j] = d_dsk
            grads["s5_c_re"][j] = _block_diag_extract(d_cdr, S5_CH, S5_STATE, transpose=True)
            grads["s5_c_im"][j] = -_block_diag_extract(d_cdi, S5_CH, S5_STATE, transpose=True)
            d_bur, d_bui, d_lam = _s5_scan(d_xr, d_xi, sv["lam"], reverse=True, xr=sv["xr"], xi=sv["xi"])
            d_uu, d_bdr, d_bdi = _s5_bproj_bwd(d_bur, d_bui, sv["bd_re"], sv["bd_im"], sv["uu"], du_skip)
            d_bbr = _block_diag_extract(d_bdr, S5_STATE, S5_CH, transpose=True).transpose(2, 0, 1)
            d_bbi = _block_diag_extract(d_bdi, S5_STATE, S5_CH, transpose=True).transpose(2, 0, 1)
            d_lr, d_li, d_ls, d_br_t, d_bi_t = _s5_params_bwd(
                s5_lam_re[j], s5_lam_im[j], sv["ls"], sv["b_re_t"], sv["b_im_t"],
                d_lam[0].reshape(G, P), d_lam[1].reshape(G, P), d_bbr, d_bbi)
            grads["s5_lam_re"][j], grads["s5_lam_im"][j], grads["s5_log_step"][j] = d_lr, d_li, d_ls.reshape(G)
            grads["s5_b_re"][j], grads["s5_b_im"][j] = d_br_t.transpose(1, 2, 0), d_bi_t.transpose(1, 2, 0)
            grads["s5_w_in"][j] = dw(sv["hn"], d_uu, 0, "s5_in_dw")
            d_hn = _mm(d_uu, weight("s5_w_in", j), "nt", name="s5_in_dx")[0]
        dh, _, grads["norm_mix"][i] = _rms_bwd(d_hn, sv["h0"], norm_mix[i], dh)

    out = {}
    layers = lambda name: jnp.concatenate(grads[name], axis=1)
    d_conv = jnp.stack(grads["gdn_conv_w"]).reshape(n_gdn * 4, N_DEV, cw).transpose(1, 0, 2)
    d_s5d = jnp.stack([g.reshape(N_DEV, D // N_DEV) for g in grads["s5_d"]], axis=1)
    d_misc = jnp.concatenate([d_conv, jnp.pad(d_s5d, ((0, 0), (0, 0), (0, cw - D // N_DEV)))], axis=1)
    d_misc = jnp.pad(d_misc, ((0, 0), (0, -d_misc.shape[1] % 8), (0, 0)))
    contributions = ([layers("gdn_w_in"), jnp.concatenate([layers(n) for n in ROW_GROUP], axis=1)]
                     + [layers(n) for n in COL_SHARDED] + [d_misc])
    got = _pair_exchange(contributions, "exchange_grads_pair")
    px_, py_, pc_ = _mesh_position()
    mine = [4 * qx + 2 * qy + pc_ for qx, qy in ((px_, py_), (1 - px_, py_), (px_, 1 - py_), (1 - px_, 1 - py_))]
    own = [jnp.stack([lax.dynamic_index_in_dim(a, d, 0, keepdims=False) for d in mine]) for a in contributions]
    arrived = _chip_exchange([_chip_sums(o, g) for o, g in zip(own, got)], "exchange_grads_chips")
    recv = [[(o, (0,)), (g, (0,)), (r, (0, 1, 2))] for o, g, r in zip(own, got, arrived)]
    r_win, r_row, r_misc = recv[0], recv[1], recv[-1]
    r_col = dict(zip(COL_SHARDED, recv[2:-1]))

    def update(name, parts, row0, **kw):
        res = _adamw(parts, row0, _rows2d(w[name]), _rows2d(mom[name]), _rows2d(vel[name]), "adamw_" + name, **kw)
        out[name] = [r.reshape(w[name].shape) for r in res]

    update("gdn_w_in", r_win, 0, window_n8=n8)
    for n in ROW_GROUP:
        update(n, r_row, row_off[n])
    for n in COL_SHARDED:
        update(n, r_col[n], 0)
    res = _adamw(r_misc, 0, _misc_pack(gdn_conv_w, s5_d), _misc_pack(mom["gdn_conv_w"], mom["s5_d"]),
                 _misc_pack(vel["gdn_conv_w"], vel["s5_d"]), "adamw_misc")
    unpacked = [_misc_unpack(r, n_gdn * 4, s5_d.shape) for r in res]
    out["gdn_conv_w"] = [u_[0].reshape(gdn_conv_w.shape) for u_ in unpacked]
    out["s5_d"] = [u_[1] for u_ in unpacked]

    rep_g = {n: (d_norm_final[0] if n == "norm_final" else jnp.stack([g.reshape(w[n].shape[1:]) for g in grads[n]])) for n in REPLICATED}
    flat_r = lambda d: [d[n].reshape(-1) for n in REPLICATED]
    pg, _ = _pack(flat_r(rep_g), F32)
    parts_r = _all_gather([pg], "gather_small_grads")[0]
    pw, _ = _pack(flat_r(w), F32)
    pm, _ = _pack(flat_r(mom), F32)
    pvv, _ = _pack(flat_r(vel), F32)
    res = [r.reshape(-1) for r in _adamw([(parts_r, tuple(range(N_DEV)))], 0, pw, pm, pvv, "adamw_replicated")]
    off = 0
    for name in REPLICATED:
        n = w[name].size
        out[name] = [res[k][off:off + n].reshape(w[name].shape) for k in range(4)]
        off += n

    grad_x = dh[None]
    return (loss, grad_x, *[out[n][0] for n in WEIGHTS], *[out[n][1] for n in WEIGHTS],
            *[out[n][2] for n in WEIGHTS], *[out[n][3] for n in WEIGHTS])
```

```python
import collections
import math

import jax
import jax.numpy as jnp
from jax import lax
from jax.experimental import pallas as pl
from jax.experimental.pallas import tpu as pltpu

F32, BF16 = jnp.float32, jnp.bfloat16
NN, NT, TN = ((1,), (0,)), ((1,), (1,)), ((0,), (0,))

N_DEV = 8
MESH_AXES = ("x", "y", "c")
LANES = 128
V7X_VMEM_BYTES = 64 * 1024 * 1024
VMEM_LIMIT = V7X_VMEM_BYTES - 8 * 1024 * 1024
CHUNK = 64
HEAD = 128
SUPER = 256
S5_CH = 16
S5_STATE = 64
S5_GPB = LANES // S5_CH
S5_SPB = S5_GPB * S5_STATE
NORM_EPS = 1e-6
L2_EPS = 1e-6
ADAM_LR, ADAM_B1, ADAM_B2, ADAM_EPS, ADAM_WD, ADAM_STEP = 0.001, 0.9, 0.999, 1e-08, 0.01, 10
PACK_W = 1024
PACK_ROWS = 256


def _dot(a, b, dims):
    return lax.dot_general(a, b, (dims, ((), ())), preferred_element_type=F32)


def _bdot(a, b, dims=NN):
    return _dot(a.astype(BF16), b.astype(BF16), dims)


def _call(body, grid, ins, outs, scratch=(), name=None, sem=None):
    res = pl.pallas_call(
        body,
        grid=grid,
        in_specs=[pl.BlockSpec(b, m) for _, b, m in ins],
        out_specs=[pl.BlockSpec(b, m) for _, _, b, m in outs],
        out_shape=[jax.ShapeDtypeStruct(s, d) for s, d, _, _ in outs],
        scratch_shapes=list(scratch),
        name=name,
        compiler_params=pltpu.CompilerParams(
            dimension_semantics=sem or ("arbitrary",) * len(grid), vmem_limit_bytes=VMEM_LIMIT),
    )(*[a for a, _, _ in ins])
    return res


def _tile(n, want):
    t = min(n, want)
    assert n % t == 0, (n, want)
    return t


def _accumulate(ref, val, first):
    @pl.when(first)
    def _():
        ref[...] = jnp.zeros_like(ref)
    ref[...] += val


class Sharded(collections.namedtuple("Sharded", "arr axis row0 rows")):
    @property
    def shape(self):
        c = self.arr.shape[2]
        return (self.rows, N_DEV * c) if self.axis == 1 else (N_DEV * self.rows, c)

    @property
    def units(self):
        return (math.gcd(self.rows, self.row0), self.arr.shape[2])


def _mm(a, b, mode="nn", out_dtypes=(F32,), epi=None, extras=(), name="mm", out_axis=None, tm=1024, tn=1024, tk=2048):
    sh = isinstance(b, Sharded)
    b_rows, b_cols = b.shape
    u_rows, u_cols = b.units if sh else b.shape
    if mode == "nn":
        (M, K), (K2, N), (uk, un) = a.shape, (b_rows, b_cols), (u_rows, u_cols)
    elif mode == "nt":
        (M, K), (N, K2), (un, uk) = a.shape, (b_rows, b_cols), (u_rows, u_cols)
    else:
        (K, M), (K2, N), (uk, un) = a.shape, (b_rows, b_cols), (u_rows, u_cols)
    assert K == K2, (a.shape, b.shape, mode)
    um = M
    if out_axis == 0:
        um = M // N_DEV
    elif out_axis == 1:
        un = N // N_DEV
    tm, tn = _tile(um, tm), _tile(un, tn)
    span = 1
    if sh and ((mode == "nn" and b.axis == 0 and uk == b.rows) or (mode == "nt" and b.axis == 1)) and uk < min(K, tk):
        span = min(K, tk) // uk
        assert N_DEV % span == 0
    tk = _tile(uk, tk)
    nk = K // (tk * span)
    a_spec = ((tk, tm), lambda i, j, k: (k, i)) if mode == "tn" else ((tm, tk * span), lambda i, j, k: (i, k))
    if not sh:
        b_arr = b
        b_spec = ((tn, tk), lambda i, j, k: (j, k)) if mode == "nt" else ((tk, tn), lambda i, j, k: (k, j))
    else:
        b_arr = b.arr
        tr_, tc_ = (tk, tn) if mode == "nn" else (tn, tk)
        r0, per_r, per_c = b.row0 // tr_, b.rows // tr_, b.arr.shape[2] // tc_
        assert b.row0 % tr_ == 0 and mode != "tn"
        if span > 1:
            place = (lambda r, c: (r, r0, c)) if b.axis == 0 else (lambda r, c: (c, r0 + r, 0))
        elif b.axis == 1:
            place = lambda r, c: (c // per_c, r0 + r, c % per_c)
        else:
            place = lambda r, c: (r // per_r, r0 + r % per_r, c)
        b_spec = ((span, tr_, tc_), (lambda i, j, k: place(k, j)) if mode == "nn" else (lambda i, j, k: place(j, k)))
    dims = {"nn": NN, "nt": NT, "tn": TN}[mode]
    n_ex, n_out = len(extras), len(out_dtypes)

    def body(*refs):
        a_ref, b_ref = refs[:2]
        ex = refs[2:2 + n_ex]
        outs = refs[2 + n_ex:2 + n_ex + n_out]

        def product():
            if not sh:
                return _bdot(a_ref[...], b_ref[...], dims)
            part = _bdot(a_ref[:, :tk], b_ref[0], dims)
            for s in range(1, span):
                part = part + _bdot(a_ref[:, s * tk:(s + 1) * tk], b_ref[s], dims)
            return part

        def finish(res):
            vals = epi(res, *[e[...] for e in ex]) if epi is not None else (res,)
            for r, v in zip(outs, vals):
                r[...] = v.astype(r.dtype)

        if nk == 1:
            finish(product())
            return
        acc = refs[-1]
        k = pl.program_id(2)

        @pl.when(k == 0)
        def _():
            acc[...] = jnp.zeros_like(acc)

        acc[...] += product()

        @pl.when(k == nk - 1)
        def _():
            finish(acc[...])

    tile = lambda i, j, k: (i, j)
    if out_axis is None:
        out_shape, out_block, out_map = (M, N), (tm, tn), tile
    elif out_axis == 0:
        per = um // tm
        out_shape, out_block, out_map = (N_DEV, um, N), (None, tm, tn), lambda i, j, k: (i // per, i % per, j)
    else:
        per = un // tn
        out_shape, out_block, out_map = (N_DEV, M, un), (None, tm, tn), lambda i, j, k: (j // per, i, j % per)
    return _call(
        body, (M // tm, N // tn, nk),
        [(a,) + a_spec, (b_arr,) + b_spec] + [(e, (tm, tn), tile) for e in extras],
        [(out_shape, d, out_block, out_map) for d in out_dtypes],
        scratch=[pltpu.VMEM((tm, tn), F32)] if nk > 1 else [], name=name,
        sem=("parallel", "parallel", "arbitrary"))


def _rms_fwd(h, g, tr=256):
    T, D = h.shape
    tr = _tile(T, tr)

    def body(h_ref, g_ref, o_ref):
        x = h_ref[...]
        r = lax.rsqrt(jnp.mean(x * x, axis=-1, keepdims=True) + NORM_EPS)
        o_ref[...] = (x * r * g_ref[...]).astype(BF16)

    row = lambda i: (i, 0)
    fix = lambda i: (0, 0)
    return _call(body, (T // tr,), [(h, (tr, D), row), (g.reshape(1, D), (1, D), fix)],
                 [((T, D), BF16, (tr, D), row)], name="rms_fwd", sem=("parallel",))[0]


def _rms_bwd_math(dy, x, g):
    r = lax.rsqrt(jnp.mean(x * x, axis=-1, keepdims=True) + NORM_EPS)
    xh = x * r
    dxh = dy * g
    dx = r * (dxh - xh * jnp.mean(dxh * xh, axis=-1, keepdims=True))
    dg = jnp.sum(dy * xh, axis=0, keepdims=True)
    return dx, dg


def _rms_bwd(dy, h, g, res, tr=256):
    T, D = h.shape
    tr = _tile(T, tr)

    def body(dy_ref, h_ref, g_ref, res_ref, dh_ref, dhb_ref, dg_ref):
        dx, dg = _rms_bwd_math(dy_ref[...], h_ref[...], g_ref[...])
        dh = res_ref[...] + dx
        dh_ref[...] = dh
        dhb_ref[...] = dh.astype(BF16)
        _accumulate(dg_ref, dg, pl.program_id(0) == 0)

    row = lambda i: (i, 0)
    fix = lambda i: (0, 0)
    return _call(body, (T // tr,),
                 [(dy, (tr, D), row), (h, (tr, D), row), (g.reshape(1, D), (1, D), fix), (res, (tr, D), row)],
                 [((T, D), F32, (tr, D), row), ((T, D), BF16, (tr, D), row), ((1, D), F32, (1, D), fix)], name="rms_bwd")


def _loss_fwd_bwd(h, g, tgt, tr=256):
    T, D = h.shape
    tr = _tile(T, tr)

    def body(h_ref, g_ref, t_ref, dh_ref, dg_ref, loss_ref):
        x, gg = h_ref[...], g_ref[...]
        r = lax.rsqrt(jnp.mean(x * x, axis=-1, keepdims=True) + NORM_EPS)
        diff = x * r * gg - t_ref[...]
        part = 0.5 * jnp.sum(jnp.mean(diff * diff, axis=-1, keepdims=True))
        dx, dg = _rms_bwd_math(diff * (1.0 / D), x, gg)
        dh_ref[...] = dx
        first = pl.program_id(0) == 0
        _accumulate(dg_ref, dg, first)
        _accumulate(loss_ref, jnp.full((1, LANES), part, F32), first)

    row = lambda i: (i, 0)
    fix = lambda i: (0, 0)
    return _call(body, (T // tr,),
                 [(h, (tr, D), row), (g.reshape(1, D), (1, D), fix), (tgt, (tr, D), row)],
                 [((T, D), F32, (tr, D), row), ((1, D), F32, (1, D), fix), ((1, LANES), F32, (1, LANES), fix)],
                 name="loss_fwd_bwd")


def _ple_fwd(h, s, pp, tr=256):
    T, D = h.shape
    tr = _tile(T, tr)

    def body(h_ref, s_ref, p_ref, o_ref):
        o_ref[...] = h_ref[...] + jax.nn.sigmoid(s_ref[...]) * p_ref[...]

    row = lambda i: (i, 0)
    return _call(body, (T // tr,), [(a, (tr, D), row) for a in (h, s, pp)],
                 [((T, D), F32, (tr, D), row)], name="ple_fwd", sem=("parallel",))[0]


def _ple_bwd(dh, s, pp, tr=256):
    T, D = dh.shape
    tr = _tile(T, tr)

    def body(dh_ref, s_ref, p_ref, ds_ref, dp_ref):
        d = dh_ref[...]
        gate = jax.nn.sigmoid(s_ref[...])
        ds_ref[...] = (d * p_ref[...] * gate * (1.0 - gate)).astype(BF16)
        dp_ref[...] = (d * gate).astype(BF16)

    row = lambda i: (i, 0)
    return _call(body, (T // tr,), [(a, (tr, D), row) for a in (dh, s, pp)],
                 [((T, D), BF16, (tr, D), row)] * 2, name="ple_bwd", sem=("parallel",))


def _conv_taps(xe, w, tr):
    c = w[3:4, :] * xe[8:, :]
    for j in range(3):
        c = c + w[j:j + 1, :] * pltpu.roll(xe, 3 - j, 0)[8:, :]
    return c


def _gdn_pre_fwd(pq, conv_w, hk, tr=2048):
    T, CD = pq.shape
    tr = _tile(T, tr)
    r8 = tr // 8

    def body(x_ref, halo_ref, w_ref, o_ref):
        j, r = pl.program_id(0), pl.program_id(1)
        halo = jnp.where(r > 0, halo_ref[...], 0.0)
        xe = jnp.concatenate([halo, x_ref[...]], axis=0)
        c = _conv_taps(xe, w_ref[...], tr)
        s = c * jax.nn.sigmoid(c)
        rn = lax.rsqrt(jnp.sum(s * s, axis=-1, keepdims=True) + L2_EPS)
        scale = jnp.where(j < hk, HEAD ** -0.5, 1.0)
        o_ref[...] = jnp.where(j < 2 * hk, s * rn * scale, s)

    tile = lambda j, r: (r, j)
    return _call(body, (CD // HEAD, T // tr),
                 [(pq, (tr, HEAD), tile), (pq, (8, HEAD), lambda j, r: (jnp.maximum(r * r8 - 1, 0), j)),
                  (conv_w, (4, HEAD), lambda j, r: (0, j))],
                 [((T, CD), F32, (tr, HEAD), tile)], name="gdn_pre_fwd", sem=("parallel", "parallel"))[0]


def _gdn_pre_bwd(dn, pq, conv_w, hk, tr=2048):
    T, CD = pq.shape
    tr = _tile(T, tr)
    r8 = tr // 8

    def body(dn_ref, x_ref, halo_ref, w_ref, dc_ref, dw_ref):
        j, r = pl.program_id(0), pl.program_id(1)
        halo = jnp.where(r > 0, halo_ref[...], 0.0)
        xe = jnp.concatenate([halo, x_ref[...]], axis=0)
        c = _conv_taps(xe, w_ref[...], tr)
        sig = jax.nn.sigmoid(c)
        s = c * sig
        rn = lax.rsqrt(jnp.sum(s * s, axis=-1, keepdims=True) + L2_EPS)
        scale = jnp.where(j < hk, HEAD ** -0.5, 1.0)
        d = dn_ref[...]
        y = s * rn
        dy = d * scale
        ds = jnp.where(j < 2 * hk, rn * (dy - y * jnp.sum(dy * y, axis=-1, keepdims=True)), d)
        dc = ds * sig * (1.0 + c * (1.0 - sig))
        dc_ref[...] = dc

        @pl.when(r == 0)
        def _():
            dw_ref[...] = jnp.zeros_like(dw_ref)

        for t in range(4):
            xs = xe[8:, :] if t == 3 else pltpu.roll(xe, 3 - t, 0)[8:, :]
            dw_ref[t:t + 1, :] += jnp.sum(dc * xs, axis=0, keepdims=True)

    tile = lambda j, r: (r, j)
    col = lambda j, r: (0, j)
    return _call(body, (CD // HEAD, T // tr),
                 [(dn, (tr, HEAD), tile), (pq, (tr, HEAD), tile),
                  (pq, (8, HEAD), lambda j, r: (jnp.maximum(r * r8 - 1, 0), j)), (conv_w, (4, HEAD), col)],
                 [((T, CD), F32, (tr, HEAD), tile), ((4, CD), F32, (4, HEAD), col)], name="gdn_pre_bwd")


def _gdn_conv_bwd(dc, conv_w, tr=2048):
    T, CD = dc.shape
    tr = _tile(T, tr)
    r8 = tr // 8
    n_r = T // tr

    def body(dc_ref, halo_ref, w_ref, dx_ref):
        r = pl.program_id(1)
        halo = jnp.where(r < n_r - 1, halo_ref[...], 0.0)
        de = jnp.concatenate([dc_ref[...], halo], axis=0)
        w = w_ref[...]
        dx = w[3:4, :] * de[:tr, :]
        for j in range(3):
            dx = dx + w[j:j + 1, :] * pltpu.roll(de, tr + 8 - (3 - j), 0)[:tr, :]
        dx_ref[...] = dx.astype(BF16)

    tile = lambda j, r: (r, j)
    return _call(body, (CD // HEAD, n_r),
                 [(dc, (tr, HEAD), tile), (dc, (8, HEAD), lambda j, r: (jnp.minimum((r + 1) * r8, T // 8 - 1), j)),
                  (conv_w, (4, HEAD), lambda j, r: (0, j))],
                 [((T, CD), BF16, (tr, HEAD), tile)], name="gdn_conv_bwd", sem=("parallel", "parallel"))[0]


def _gates_fwd(ba, pv, hv, tr=1024):
    T = ba.shape[0]
    tr = _tile(T, tr)

    def body(x_ref, pv_ref, o_ref):
        x = x_ref[...]
        lane = lax.broadcasted_iota(jnp.int32, x.shape, 1)
        g = -jnp.exp(pv_ref[0:1, :]) * jax.nn.softplus(x + pv_ref[1:2, :])
        o_ref[...] = jnp.where(lane < hv, jax.nn.sigmoid(x), jnp.where(lane < 2 * hv, g, 0.0))

    row = lambda i: (i, 0)
    return _call(body, (T // tr,), [(ba, (tr, LANES), row), (pv, (2, LANES), lambda i: (0, 0))],
                 [((T, LANES), F32, (tr, LANES), row)], name="gates_fwd", sem=("parallel",))[0]


def _gates_bwd(dg2, ba, pv, hv, tr=1024):
    T = ba.shape[0]
    tr = _tile(T, tr)

    def body(d_ref, x_ref, pv_ref, dx_ref, dpv_ref):
        x, d = x_ref[...], d_ref[...]
        lane = lax.broadcasted_iota(jnp.int32, x.shape, 1)
        is_a = (lane >= hv) & (lane < 2 * hv)
        beta = jax.nn.sigmoid(x)
        neg_a = -jnp.exp(pv_ref[0:1, :])
        z = x + pv_ref[1:2, :]
        da = d * neg_a * jax.nn.sigmoid(z)
        dx_ref[...] = jnp.where(lane < hv, d * beta * (1.0 - beta), jnp.where(is_a, da, 0.0))
        first = pl.program_id(0) == 0

        @pl.when(first)
        def _():
            dpv_ref[...] = jnp.zeros_like(dpv_ref)

        dpv_ref[0:1, :] += jnp.sum(jnp.where(is_a, d * neg_a * jax.nn.softplus(z), 0.0), axis=0, keepdims=True)
        dpv_ref[1:2, :] += jnp.sum(jnp.where(is_a, da, 0.0), axis=0, keepdims=True)

    row = lambda i: (i, 0)
    fix = lambda i: (0, 0)
    return _call(body, (T // tr,), [(dg2, (tr, LANES), row), (ba, (tr, LANES), row), (pv, (2, LANES), fix)],
                 [((T, LANES), F32, (tr, LANES), row), ((2, LANES), F32, (2, LANES), fix)], name="gates_bwd")


def _ogate_fwd(o, z, o_norm, tr=2048):
    T, VD = o.shape
    tr = _tile(T, tr)

    def body(o_ref, z_ref, g_ref, y_ref):
        x, zz = o_ref[...], z_ref[...]
        r = lax.rsqrt(jnp.mean(x * x, axis=-1, keepdims=True) + NORM_EPS)
        y_ref[...] = (x * r * g_ref[...] * (zz * jax.nn.sigmoid(zz))).astype(BF16)

    tile = lambda h, r: (r, h)
    return _call(body, (VD // HEAD, T // tr),
                 [(o, (tr, HEAD), tile), (z, (tr, HEAD), tile), (o_norm.reshape(1, HEAD), (1, HEAD), lambda h, r: (0, 0))],
                 [((T, VD), BF16, (tr, HEAD), tile)], name="ogate_fwd", sem=("parallel", "parallel"))[0]


def _ogate_bwd(dy, o, z, o_norm, tr=2048):
    T, VD = o.shape
    tr = _tile(T, tr)

    def body(dy_ref, o_ref, z_ref, g_ref, do_ref, dz_ref, dg_ref):
        d, x, zz, g = dy_ref[...], o_ref[...], z_ref[...], g_ref[...]
        sig = jax.nn.sigmoid(zz)
        silu = zz * sig
        dx, dg = _rms_bwd_math(d * silu, x, g)
        r = lax.rsqrt(jnp.mean(x * x, axis=-1, keepdims=True) + NORM_EPS)
        do_ref[...] = dx
        dz_ref[...] = (d * (x * r * g) * sig * (1.0 + zz * (1.0 - sig))).astype(BF16)
        _accumulate(dg_ref, dg, (pl.program_id(0) == 0) & (pl.program_id(1) == 0))

    tile = lambda h, r: (r, h)
    fix = lambda h, r: (0, 0)
    return _call(body, (VD // HEAD, T // tr),
                 [(dy, (tr, HEAD), tile), (o, (tr, HEAD), tile), (z, (tr, HEAD), tile), (o_norm.reshape(1, HEAD), (1, HEAD), fix)],
                 [((T, VD), F32, (tr, HEAD), tile), ((T, VD), BF16, (tr, HEAD), tile), ((1, HEAD), F32, (1, HEAD), fix)],
                 name="ogate_bwd")


def _chunk_iota():
    return (lax.broadcasted_iota(jnp.int32, (CHUNK, CHUNK), 0), lax.broadcasted_iota(jnp.int32, (CHUNK, CHUNK), 1))


def _decay(gc):
    ri, ci = _chunk_iota()
    gcol = gc[:, :CHUNK]
    grow = jnp.sum(jnp.where(ri == ci, gcol, 0.0), axis=0, keepdims=True)
    return jnp.where(ri >= ci, jnp.exp(jnp.minimum(gcol - grow, 0.0)), 0.0)


def _rowsum(x):
    return jnp.broadcast_to(jnp.sum(x, axis=1, keepdims=True), (x.shape[0], HEAD))


def _split3(x):
    h1 = x.astype(BF16)
    r1 = x - h1.astype(F32)
    h2 = r1.astype(BF16)
    return h1, h2, (r1 - h2.astype(F32)).astype(BF16)


def _sel_dot(sel, xs, dims=NN):
    s = sel.astype(BF16)
    parts = [_split3(x) for x in xs]
    if dims == NN:
        return [_dot(s, h1, NN) + _dot(s, h2, NN) + _dot(s, h3, NN) for h1, h2, h3 in parts]
    return [_dot(h1, s, dims) + _dot(h2, s, dims) + _dot(h3, s, dims) for h1, h2, h3 in parts]


def _colsum(es):
    return _sel_dot(jnp.ones((es[0].shape[0], HEAD), F32), es, TN)


def _super_iota():
    ri = lax.broadcasted_iota(jnp.int32, (SUPER, SUPER), 0)
    ci = lax.broadcasted_iota(jnp.int32, (SUPER, SUPER), 1)
    shift = int(math.log2(CHUNK))
    return ri, ci, jnp.right_shift(ri, shift) == jnp.right_shift(ci, shift)


def _decay_super(gc, ri, ci, same):
    gcol = jnp.concatenate([gc] * (SUPER // HEAD), axis=1)
    grow = jnp.sum(jnp.where(ri == ci, gcol, 0.0), axis=0, keepdims=True)
    return jnp.where(same & (ri >= ci), jnp.exp(jnp.minimum(gcol - grow, 0.0)), 0.0)


def _unit_lower_inverse(ms, eye):
    ps = [-m for m in ms]
    xs = [eye + p for p in ps]
    for _ in range(int(math.log2(CHUNK)) - 1):
        ps = [_bdot(p, p) for p in ps]
        xs = [x + _bdot(x, p) for x, p in zip(xs, ps)]
    resid = []
    for m, x in zip(ms, xs):
        (m1, m2, _), (x1, x2, _) = _split3(m), _split3(x)
        resid.append((eye - x) - (_dot(m1, x1, NN) + _dot(m1, x2, NN) + _dot(m2, x1, NN)))
    return [x + _bdot(x, r) for x, r in zip(xs, resid)]


def _gdn_a_fwd(qkv, gb, bb, hk, hv, tr=1024):
    T = qkv.shape[0]
    tr = _tile(T, tr)
    assert tr % SUPER == 0

    def body(k_ref, v_ref, g_ref, b_ref, u_ref, w_ref, gc_ref, ti_ref):
        ri, ci, same = _super_iota()
        ltri = jnp.where(same & (ri >= ci), 1.0, 0.0)
        eye = jnp.where(ri == ci, 1.0, 0.0)
        rows = [pl.ds(s * SUPER, SUPER) for s in range(tr // SUPER)]
        ks, vs, betas = [k_ref[r, :] for r in rows], [v_ref[r, :] for r in rows], [b_ref[r, :] for r in rows]
        gcs = _sel_dot(ltri, [g_ref[r, :] for r in rows])
        kbs = [k * beta for k, beta in zip(ks, betas)]
        ms = [jnp.where(same & (ri > ci), _bdot(kb, k, NT) * _decay_super(gc, ri, ci, same), 0.0)
              for kb, k, gc in zip(kbs, ks, gcs)]
        tinvs = _unit_lower_inverse(ms, eye)
        xs = [_bdot(tinv, jnp.concatenate([v * beta, kb * jnp.exp(gc)], axis=1))
              for tinv, v, beta, kb, gc in zip(tinvs, vs, betas, kbs, gcs)]
        for r, x, gc, tinv in zip(rows, xs, gcs, tinvs):
            u_ref[r, :] = x[:, :HEAD]
            w_ref[r, :] = x[:, HEAD:]
            gc_ref[r, :] = gc
            ti_ref[0, r, :] = tinv.astype(BF16)

    tile = lambda h, r: (r, h)
    vd = hv * HEAD
    return _call(body, (hv, T // tr),
                 [(qkv, (tr, HEAD), lambda h, r: (r, hk + h // 2)), (qkv, (tr, HEAD), lambda h, r: (r, 2 * hk + h)),
                  (gb, (tr, HEAD), tile), (bb, (tr, HEAD), tile)],
                 [((T, vd), F32, (tr, HEAD), tile)] * 3 + [((hv, T, SUPER), BF16, (1, tr, SUPER), lambda h, r: (h, r, 0))],
                 name="gdn_a_fwd", sem=("parallel", "parallel"))


def _gdn_b_fwd(qkv, u, w, gc, hk, hv, tr=512):
    T = qkv.shape[0]
    tr = _tile(T, tr)
    cpb = tr // CHUNK

    def body(q_ref, k_ref, u_ref, w_ref, gc_ref, o_ref, vn_ref, sall_ref, s_ref):
        ri, ci = _chunk_iota()

        @pl.when(pl.program_id(1) == 0)
        def _():
            s_ref[...] = jnp.zeros_like(s_ref)

        rows = [pl.ds(c * CHUNK, CHUNK) for c in range(cpb)]
        qs, ks, us, ws, gcs = ([ref[r, :] for r in rows] for ref in (q_ref, k_ref, u_ref, w_ref, gc_ref))
        gls = [gc[CHUNK - 1:CHUNK, :] for gc in gcs]
        kws = [_bdot(k * jnp.exp(gl - gc), jnp.concatenate([w_, u_], axis=1), TN)
               for k, gl, gc, w_, u_ in zip(ks, gls, gcs, ws, us)]
        qks = [jnp.where(ri >= ci, _bdot(q, k, NT) * _decay(gc), 0.0) for q, k, gc in zip(qs, ks, gcs)]
        s = s_ref[...]
        states = []
        for kw, gl in zip(kws, gls):
            states.append(s)
            s = s * jnp.exp(gl) - _bdot(kw[:, :HEAD], s) + kw[:, HEAD:]
        s_ref[...] = s
        vns = [u_ - _bdot(w_, st) for u_, w_, st in zip(us, ws, states)]
        outs = [_bdot(q * jnp.exp(gc), st) + _bdot(qk, vn) for q, gc, st, qk, vn in zip(qs, gcs, states, qks, vns)]
        for c, (r, o, vn, st) in enumerate(zip(rows, outs, vns, states)):
            o_ref[r, :] = o
            vn_ref[r, :] = vn
            sall_ref[0, c] = st

    tile = lambda h, r: (r, h)
    vd = hv * HEAD
    return _call(body, (hv, T // tr),
                 [(qkv, (tr, HEAD), lambda h, r: (r, h // 2)), (qkv, (tr, HEAD), lambda h, r: (r, hk + h // 2)),
                  (u, (tr, HEAD), tile), (w, (tr, HEAD), tile), (gc, (tr, HEAD), tile)],
                 [((T, vd), F32, (tr, HEAD), tile)] * 2 +
                 [((hv, T // CHUNK, HEAD, HEAD), F32, (1, cpb, HEAD, HEAD), lambda h, r: (h, r, 0, 0))],
                 scratch=[pltpu.VMEM((HEAD, HEAD), F32)], name="gdn_b_fwd", sem=("parallel", "arbitrary"))


def _gdn_b_bwd(do, qkv, w, gc, vn, sall, hk, hv, tr=512):
    T = qkv.shape[0]
    tr = _tile(T, tr)
    cpb = tr // CHUNK
    n_r = T // tr

    def body(do_ref, q_ref, k_ref, w_ref, gc_ref, vn_ref, sall_ref, dq_ref, dk_ref, dgc_ref, du_ref, dw_ref, ds_ref):
        ri, ci = _chunk_iota()
        row = lax.broadcasted_iota(jnp.int32, (CHUNK, HEAD), 0)

        @pl.when(pl.program_id(1) == 0)
        def _():
            ds_ref[...] = jnp.zeros_like(ds_ref)

        rows = [pl.ds(c * CHUNK, CHUNK) for c in range(cpb)]
        d_os, qs, ks, ws, gcs, vns = ([ref[r, :] for r in rows] for ref in (do_ref, q_ref, k_ref, w_ref, gc_ref, vn_ref))
        ss = [sall_ref[0, c] for c in range(cpb)]
        gls = [gc[CHUNK - 1:CHUNK, :] for gc in gcs]
        egcs = [jnp.exp(gc) for gc in gcs]
        ekds = [jnp.exp(gl - gc) for gl, gc in zip(gls, gcs)]
        egs = [jnp.exp(gl) for gl in gls]
        qgs = [q * e for q, e in zip(qs, egcs)]
        kds = [k * e for k, e in zip(ks, ekds)]
        decs = [_decay(gc) for gc in gcs]
        qks = [jnp.where(ri >= ci, _bdot(q, k, NT) * dec, 0.0) for q, k, dec in zip(qs, ks, decs)]
        wkds = [_bdot(w_, kd, TN) for w_, kd in zip(ws, kds)]
        qk_dos = [_bdot(qk, d_o, TN) for qk, d_o in zip(qks, d_os)]
        consts = [_bdot(qg, d_o, TN) - _bdot(w_, qd, TN) for qg, d_o, w_, qd in zip(qgs, d_os, ws, qk_dos)]
        ds = ds_ref[...]
        ds_nexts = [None] * cpb
        for c in reversed(range(cpb)):
            ds_nexts[c] = ds
            ds = ds * egs[c] - _bdot(wkds[c], ds) + consts[c]
        ds_ref[...] = ds
        d_vns = [qd + _bdot(kd, dsn) for qd, kd, dsn in zip(qk_dos, kds, ds_nexts)]
        d_kds = [_bdot(vn, dsn, NT) for vn, dsn in zip(vns, ds_nexts)]
        d_qgs = [_bdot(d_o, s, NT) for d_o, s in zip(d_os, ss)]
        d_qks = [jnp.where(ri >= ci, _bdot(d_o, vn, NT), 0.0) for d_o, vn in zip(d_os, vns)]
        e_qs = [d_qk * qk for d_qk, qk in zip(d_qks, qks)]
        cols = _colsum(e_qs)
        d_bs = [d_qk * dec for d_qk, dec in zip(d_qks, decs)]
        dqs = [d_qg * egc + _bdot(d_b, k) for d_qg, egc, d_b, k in zip(d_qgs, egcs, d_bs, ks)]
        dks = [d_kd * ekd + _bdot(d_b, q, TN) for d_kd, ekd, d_b, q in zip(d_kds, ekds, d_bs, qs)]
        dws = [-_bdot(d_vn, s, NT) for d_vn, s in zip(d_vns, ss)]
        for c, r in enumerate(rows):
            d_gl = jnp.sum(d_kds[c] * kds[c]) + jnp.sum(ss[c] * ds_nexts[c]) * egs[c]
            dq_ref[r, :] = dqs[c]
            dk_ref[r, :] = dks[c]
            dgc_ref[r, :] = (_rowsum(d_qgs[c] * qgs[c]) - _rowsum(d_kds[c] * kds[c]) + _rowsum(e_qs[c]) - cols[c]
                             + jnp.where(row == CHUNK - 1, d_gl, 0.0))
            du_ref[r, :] = d_vns[c]
            dw_ref[r, :] = dws[c]

    rtile = lambda h, r: (n_r - 1 - r, h)
    vd = hv * HEAD
    return _call(body, (hv, n_r),
                 [(do, (tr, HEAD), rtile), (qkv, (tr, HEAD), lambda h, r: (n_r - 1 - r, h // 2)),
                  (qkv, (tr, HEAD), lambda h, r: (n_r - 1 - r, hk + h // 2)),
                  (w, (tr, HEAD), rtile), (gc, (tr, HEAD), rtile), (vn, (tr, HEAD), rtile),
                  (sall, (1, cpb, HEAD, HEAD), lambda h, r: (h, n_r - 1 - r, 0, 0))],
                 [((T, vd), F32, (tr, HEAD), rtile)] * 5,
                 scratch=[pltpu.VMEM((HEAD, HEAD), F32)], name="gdn_b_bwd", sem=("parallel", "arbitrary"))


def _gdn_a_bwd(du, dw, dgc_b, qkv, bb, gc, tinv, u, w, hk, hv, tr=1024):
    T = qkv.shape[0]
    tr = _tile(T, tr)
    assert tr % SUPER == 0

    def body(du_ref, dw_ref, dgcb_ref, k_ref, v_ref, b_ref, gc_ref, ti_ref, u_ref, w_ref, dk_ref, dv_ref, db_ref, dg_ref):
        ri, ci, same = _super_iota()
        utri = jnp.where(same & (ci >= ri), 1.0, 0.0)
        strict = same & (ri > ci)
        rows = [pl.ds(s * SUPER, SUPER) for s in range(tr // SUPER)]
        ks, vs, betas, gcs = ([ref[r, :] for r in rows] for ref in (k_ref, v_ref, b_ref, gc_ref))
        egcs = [jnp.exp(gc) for gc in gcs]
        kbs = [k * beta for k, beta in zip(ks, betas)]
        decs = [_decay_super(gc, ri, ci, same) for gc in gcs]
        ms = [jnp.where(strict, _bdot(kb, k, NT) * dec, 0.0) for kb, k, dec in zip(kbs, ks, decs)]
        d_rs = [_bdot(ti_ref[0, r, :], jnp.concatenate([du_ref[r, :], dw_ref[r, :]], axis=1), TN) for r in rows]
        d_ms = [jnp.where(strict, -_bdot(d_r, jnp.concatenate([u_ref[r, :], w_ref[r, :]], axis=1), NT), 0.0)
                for d_r, r in zip(d_rs, rows)]
        d_as = [d_m * dec for d_m, dec in zip(d_ms, decs)]
        e_ms = [d_m * m for d_m, m in zip(d_ms, ms)]
        d_kbs = [_bdot(d_a, k) + d_r[:, HEAD:] * egc for d_a, k, d_r, egc in zip(d_as, ks, d_rs, egcs)]
        dks = [_bdot(d_a, kb, TN) + d_kb * beta for d_a, kb, d_kb, beta in zip(d_as, kbs, d_kbs, betas)]
        cols = _colsum(e_ms)
        d_gcs = [_rowsum(e_m) - col + _rowsum(d_r[:, HEAD:] * kb * egc) + dgcb_ref[r, :]
                 for e_m, col, d_r, kb, egc, r in zip(e_ms, cols, d_rs, kbs, egcs, rows)]
        dgs = _sel_dot(utri, d_gcs)
        for r, dk, d_r, beta, v, d_kb, k, dg in zip(rows, dks, d_rs, betas, vs, d_kbs, ks, dgs):
            dk_ref[r, :] = dk
            dv_ref[r, :] = d_r[:, :HEAD] * beta
            db_ref[r, :] = _rowsum(d_r[:, :HEAD] * v) + _rowsum(d_kb * k)
            dg_ref[r, :] = dg

    tile = lambda h, r: (r, h)
    vd = hv * HEAD
    return _call(body, (hv, T // tr),
                 [(du, (tr, HEAD), tile), (dw, (tr, HEAD), tile), (dgc_b, (tr, HEAD), tile),
                  (qkv, (tr, HEAD), lambda h, r: (r, hk + h // 2)), (qkv, (tr, HEAD), lambda h, r: (r, 2 * hk + h)),
                  (bb, (tr, HEAD), tile), (gc, (tr, HEAD), tile), (tinv, (1, tr, SUPER), lambda h, r: (h, r, 0)),
                  (u, (tr, HEAD), tile), (w, (tr, HEAD), tile)],
                 [((T, vd), F32, (tr, HEAD), tile)] * 4, name="gdn_a_bwd", sem=("parallel", "parallel"))


def _pair_sum(a, b_, tr=2048):
    T, vd = a.shape
    tr = _tile(T, tr)
    terms = [a] if b_ is None else [a, b_]
    n = len(terms)

    def body(*refs):
        acc = refs[0][...] + refs[1][...]
        for r in refs[2:2 * n]:
            acc = acc + r[...]
        refs[-1][...] = acc

    even = lambda j, r: (r, 2 * j)
    odd = lambda j, r: (r, 2 * j + 1)
    ins = [(t, (tr, HEAD), m) for t in terms for m in (even, odd)]
    return _call(body, (vd // HEAD // 2, T // tr), ins,
                 [((T, vd // 2), F32, (tr, HEAD), lambda j, r: (r, j))], name="gdn_pair_sum", sem=("parallel", "parallel"))[0]


def _s5_param_math(lr, li, ls, br, bi):
    step = jnp.exp(ls)
    zr, zi = lr * step, li * step
    mag = jnp.exp(zr)
    ar, ai = mag * jnp.cos(zi), mag * jnp.sin(zi)
    den = lr * lr + li * li
    nr, ni = ar - 1.0, ai
    cr, cim = (nr * lr + ni * li) / den, (ni * lr - nr * li) / den
    return ar, ai, br * cr - bi * cim, br * cim + bi * cr


def _s5_params_fwd(lr, li, ls, br, bi):
    G, P = lr.shape

    def body(lr_ref, li_ref, ls_ref, br_ref, bi_ref, ar_ref, ai_ref, bbr_ref, bbi_ref):
        ar, ai, bbr, bbi = _s5_param_math(lr_ref[...], li_ref[...], ls_ref[...], br_ref[...], bi_ref[...])
        ar_ref[...], ai_ref[...], bbr_ref[...], bbi_ref[...] = ar, ai, bbr, bbi

    shapes = [(G, P), (G, P), (G, 1), (S5_CH, G, P), (S5_CH, G, P)]
    z = lambda n: (lambda: (0,) * n)
    return _call(body, (), [(a, s, z(len(s))) for a, s in zip((lr, li, ls, br, bi), shapes)],
                 [(s, F32, s, z(len(s))) for s in (shapes[0], shapes[0], shapes[3], shapes[3])],
                 name="s5_params_fwd", sem=())


def _s5_params_bwd(lr, li, ls, br, bi, dar, dai, dbbr, dbbi):
    G, P = lr.shape

    def body(lr_ref, li_ref, ls_ref, br_ref, bi_ref, dar_ref, dai_ref, dbr_ref, dbi_ref, o0, o1, o2, o3, o4):
        _, vjp = jax.vjp(_s5_param_math, lr_ref[...], li_ref[...], ls_ref[...], br_ref[...], bi_ref[...])
        outs = vjp((dar_ref[...], dai_ref[...], dbr_ref[...], dbi_ref[...]))
        for r, v in zip((o0, o1, o2, o3, o4), outs):
            r[...] = v

    shapes = [(G, P), (G, P), (G, 1), (S5_CH, G, P), (S5_CH, G, P)]
    z = lambda n: (lambda: (0,) * n)
    ins = list(zip((lr, li, ls, br, bi), shapes)) + list(zip((dar, dai, dbbr, dbbi), (shapes[0], shapes[0], shapes[3], shapes[3])))
    return _call(body, (), [(a, s, z(len(s))) for a, s in ins], [(s, F32, s, z(len(s))) for s in shapes],
                 name="s5_params_bwd", sem=())


def _s5_bproj_fwd(u, bd_re, bd_im, tr=2048):
    T, D = u.shape
    tr = _tile(T, tr)
    nb = D // LANES

    def body(u_ref, br_ref, bi_ref, or_ref, oi_ref):
        ub = u_ref[...]
        or_ref[...] = _bdot(ub, br_ref[0])
        oi_ref[...] = _bdot(ub, bi_ref[0])

    blk = lambda i, j: (j, 0, 0)
    return _call(body, (T // tr, nb),
                 [(u, (tr, LANES), lambda i, j: (i, j)), (bd_re, (1, LANES, S5_SPB), blk), (bd_im, (1, LANES, S5_SPB), blk)],
                 [((T, nb * S5_SPB), F32, (tr, S5_SPB), lambda i, j: (i, j))] * 2, name="s5_bproj_fwd", sem=("parallel", "parallel"))


def _s5_scan(br, bi, lam, reverse, xr=None, xi=None, tl=512, bw=512):
    T, NCH = br.shape
    tl, bw = _tile(T, tl), _tile(NCH, bw)
    n_t, n_g = T // tl, tl // 8

    def body(*refs):
        if reverse:
            br_ref, bi_ref, lam_ref, sr_ref, si_ref, hr_ref, hi_ref, or_ref, oi_ref, dl_ref, cr, ci_ = refs
        else:
            br_ref, bi_ref, lam_ref, or_ref, oi_ref, cr, ci_ = refs
        t = pl.program_id(1)
        a_r = lam_ref[0:1, :]
        a_i = -lam_ref[1:2, :] if reverse else lam_ref[1:2, :]
        powers = [(a_r, a_i)]
        for _ in range(2):
            p_r, p_i = powers[-1]
            powers.append((p_r * p_r - p_i * p_i, 2.0 * p_r * p_i))
        row = lax.broadcasted_iota(jnp.int32, (8, bw), 0)

        def scan8(x_r, x_i):
            for level, (p_r, p_i) in enumerate(powers):
                s = 1 << level
                keep = (row < 8 - s) if reverse else (row >= s)
                s_r = jnp.where(keep, pltpu.roll(x_r, 8 - s if reverse else s, 0), 0.0)
                s_i = jnp.where(keep, pltpu.roll(x_i, 8 - s if reverse else s, 0), 0.0)
                x_r, x_i = x_r + p_r * s_r - p_i * s_i, x_i + p_r * s_i + p_i * s_r
            return x_r, x_i

        edge = 7 if reverse else 0
        tab_r, tab_i = scan8(jnp.where(row == edge, a_r, 0.0), jnp.where(row == edge, a_i, 0.0))

        @pl.when(t == 0)
        def _():
            cr[...] = jnp.zeros_like(cr)
            ci_[...] = jnp.zeros_like(ci_)
            if reverse:
                dl_ref[...] = jnp.zeros_like(dl_ref)

        if reverse:
            first_block = t == n_t - 1
            halo_r = jnp.where(first_block, 0.0, hr_ref[7:8, :])
            halo_i = jnp.where(first_block, 0.0, hi_ref[7:8, :])

        def group(n, carry):
            g = n_g - 1 - n if reverse else n
            rows = pl.ds(pl.multiple_of(g * 8, 8), 8)
            c_r, c_i = carry[0], carry[1]
            x_r, x_i = scan8(br_ref[rows, :], bi_ref[rows, :])
            x_r, x_i = x_r + tab_r * c_r - tab_i * c_i, x_i + tab_r * c_i + tab_i * c_r
            or_ref[rows, :], oi_ref[rows, :] = x_r, x_i
            out = 7 - edge
            nxt = (x_r[out:out + 1, :], x_i[out:out + 1, :])
            if not reverse:
                return nxt
            before = pl.ds(pl.multiple_of(jnp.maximum(g * 8 - 8, 0), 8), 8)
            h_r = jnp.where(g > 0, sr_ref[before, :][7:8, :], halo_r)
            h_i = jnp.where(g > 0, si_ref[before, :][7:8, :], halo_i)
            s_r = jnp.where(row == 0, h_r, pltpu.roll(sr_ref[rows, :], 1, 0))
            s_i = jnp.where(row == 0, h_i, pltpu.roll(si_ref[rows, :], 1, 0))
            return nxt + (carry[2] + s_r * x_r + s_i * x_i, carry[3] + s_r * x_i - s_i * x_r)

        init = (cr[0:1, :], ci_[0:1, :])
        if reverse:
            init = init + (jnp.zeros((8, bw), F32), jnp.zeros((8, bw), F32))
        fin = lax.fori_loop(0, n_g, group, init, unroll=4 if n_g % 4 == 0 else 1)
        cr[0:1, :], ci_[0:1, :] = fin[0], fin[1]
        if reverse:
            dl_ref[0:1, :] += jnp.sum(fin[2], axis=0, keepdims=True)
            dl_ref[1:2, :] += jnp.sum(fin[3], axis=0, keepdims=True)

    tmap = (lambda c, t: (n_t - 1 - t, c)) if reverse else (lambda c, t: (t, c))
    col = lambda c, t: (0, c)
    ins = [(br, (tl, bw), tmap), (bi, (tl, bw), tmap), (lam, (2, bw), col)]
    outs = [((T, NCH), F32, (tl, bw), tmap)] * 2
    if reverse:
        halo = lambda c, t: (jnp.maximum((n_t - 1 - t) * n_g - 1, 0), c)
        ins += [(xr, (tl, bw), tmap), (xi, (tl, bw), tmap), (xr, (8, bw), halo), (xi, (8, bw), halo)]
        outs += [((2, NCH), F32, (2, bw), col)]
    return _call(body, (NCH // bw, n_t), ins, outs, scratch=[pltpu.VMEM((8, bw), F32), pltpu.VMEM((8, bw), F32)],
                 name="s5_scan_bwd" if reverse else "s5_scan_fwd", sem=("parallel", "arbitrary"))


def _s5_cproj_fwd(xr, xi, cd_re, cd_im, u, d, tr=2048):
    T, D = u.shape
    tr = _tile(T, tr)

    def body(xr_ref, xi_ref, cr_ref, ci_ref, u_ref, d_ref, y_ref, h_ref):
        y = _bdot(xr_ref[...], cr_ref[0]) + _bdot(xi_ref[...], ci_ref[0]) + d_ref[...] * u_ref[...]
        y_ref[...] = y
        h_ref[...] = jax.nn.gelu(y).astype(BF16)

    tile = lambda i, j: (i, j)
    blk = lambda i, j: (j, 0, 0)
    return _call(body, (T // tr, D // LANES),
                 [(xr, (tr, S5_SPB), tile), (xi, (tr, S5_SPB), tile), (cd_re, (1, S5_SPB, LANES), blk), (cd_im, (1, S5_SPB, LANES), blk),
                  (u, (tr, LANES), tile), (d.reshape(1, D), (1, LANES), lambda i, j: (0, j))],
                 [((T, D), F32, (tr, LANES), tile), ((T, D), BF16, (tr, LANES), tile)], name="s5_cproj_fwd", sem=("parallel", "parallel"))


def _s5_cproj_bwd(dy, xr, xi, cd_re, cd_im, u, d, tr=1024):
    T, D = u.shape
    tr = _tile(T, tr)
    nb = D // LANES

    def body(dy_ref, xr_ref, xi_ref, cr_ref, ci_ref, u_ref, d_ref, dxr_ref, dxi_ref, du_ref, dd_ref, dcr_ref, dci_ref):
        g = dy_ref[...]
        dxr_ref[...] = _bdot(g, cr_ref[0], NT)
        dxi_ref[...] = _bdot(g, ci_ref[0], NT)
        du_ref[...] = g * d_ref[...]
        first = pl.program_id(1) == 0
        _accumulate(dd_ref, jnp.sum(g * u_ref[...], axis=0, keepdims=True), first)

        @pl.when(first)
        def _():
            dcr_ref[...] = jnp.zeros_like(dcr_ref)
            dci_ref[...] = jnp.zeros_like(dci_ref)

        dcr_ref[0] += _bdot(xr_ref[...], g, TN)
        dci_ref[0] += _bdot(xi_ref[...], g, TN)

    tile = lambda j, i: (i, j)
    blk = lambda j, i: (j, 0, 0)
    col = lambda j, i: (0, j)
    return _call(body, (nb, T // tr),
                 [(dy, (tr, LANES), tile), (xr, (tr, S5_SPB), tile), (xi, (tr, S5_SPB), tile),
                  (cd_re, (1, S5_SPB, LANES), blk), (cd_im, (1, S5_SPB, LANES), blk), (u, (tr, LANES), tile), (d.reshape(1, D), (1, LANES), col)],
                 [((T, nb * S5_SPB), F32, (tr, S5_SPB), tile)] * 2 + [((T, D), F32, (tr, LANES), tile), ((1, D), F32, (1, LANES), col)]
                 + [((nb, S5_SPB, LANES), F32, (1, S5_SPB, LANES), blk)] * 2, name="s5_cproj_bwd")


def _s5_bproj_bwd(dbr, dbi, bd_re, bd_im, u, du_skip, tr=1024):
    T, D = u.shape
    tr = _tile(T, tr)
    nb = D // LANES

    def body(gr_ref, gi_ref, br_ref, bi_ref, u_ref, ds_ref, du_ref, dbr_ref, dbi_ref):
        g_r, g_i, ub = gr_ref[...], gi_ref[...], u_ref[...]
        du_ref[...] = (ds_ref[...] + _bdot(g_r, br_ref[0], NT) + _bdot(g_i, bi_ref[0], NT)).astype(BF16)

        @pl.when(pl.program_id(1) == 0)
        def _():
            dbr_ref[...] = jnp.zeros_like(dbr_ref)
            dbi_ref[...] = jnp.zeros_like(dbi_ref)

        dbr_ref[0] += _bdot(ub, g_r, TN)
        dbi_ref[0] += _bdot(ub, g_i, TN)

    tile = lambda j, i: (i, j)
    blk = lambda j, i: (j, 0, 0)
    return _call(body, (nb, T // tr),
                 [(dbr, (tr, S5_SPB), tile), (dbi, (tr, S5_SPB), tile), (bd_re, (1, LANES, S5_SPB), blk), (bd_im, (1, LANES, S5_SPB), blk),
                  (u, (tr, LANES), tile), (du_skip, (tr, LANES), tile)],
                 [((T, D), BF16, (tr, LANES), tile)] + [((nb, LANES, S5_SPB), F32, (1, LANES, S5_SPB), blk)] * 2, name="s5_bproj_bwd")


def _s5_gate_fwd(h, vg, tr=256):
    T, D = h.shape
    tr = _tile(T, tr)

    def body(h_ref, a_ref, b_ref, o_ref):
        o_ref[...] = h_ref[...] + a_ref[...] * jax.nn.sigmoid(b_ref[...])

    row = lambda i: (i, 0)
    return _call(body, (T // tr,), [(h, (tr, D), row), (vg, (tr, D), row), (vg, (tr, D), lambda i: (i, 1))],
                 [((T, D), F32, (tr, D), row)], name="s5_gate_fwd", sem=("parallel",))[0]


def _s5_gate_bwd(dh, vg, tr=256):
    T, D = dh.shape
    tr = _tile(T, tr)

    def body(d_ref, a_ref, b_ref, o_ref):
        d = d_ref[...]
        sig = jax.nn.sigmoid(b_ref[...])
        o_ref[:, :D] = (d * sig).astype(BF16)
        o_ref[:, D:] = (d * a_ref[...] * sig * (1.0 - sig)).astype(BF16)

    row = lambda i: (i, 0)
    return _call(body, (T // tr,), [(dh, (tr, D), row), (vg, (tr, D), row), (vg, (tr, D), lambda i: (i, 1))],
                 [((T, 2 * D), BF16, (tr, 2 * D), row)], name="s5_gate_bwd", sem=("parallel",))[0]


def _block_diag(w, transpose):
    g, a, b = w.shape
    if transpose:
        w = w.transpose(0, 2, 1)
        a, b = b, a
    eye = jnp.eye(S5_GPB, dtype=w.dtype)
    return jnp.einsum("jgab,gh->jgahb", w.reshape(g // S5_GPB, S5_GPB, a, b), eye).reshape(g // S5_GPB, S5_GPB * a, S5_GPB * b)


def _block_diag_extract(wd, a, b, transpose):
    if transpose:
        a, b = b, a
    nb = wd.shape[0]
    eye = jnp.eye(S5_GPB, dtype=wd.dtype)
    w = jnp.einsum("jgahb,gh->jgab", wd.reshape(nb, S5_GPB, a, S5_GPB, b), eye).reshape(nb * S5_GPB, a, b)
    return w.transpose(0, 2, 1) if transpose else w


def _mesh_position():
    return lax.axis_index("x"), lax.axis_index("y"), lax.axis_index("c")


def _my_index():
    x, y, c = _mesh_position()
    return 4 * x + 2 * y + c


def _hbm_call(body, arrays, out_shapes, n_sems, name):
    n = len(arrays)
    return pl.pallas_call(
        body, out_shape=[jax.ShapeDtypeStruct(s, d) for s, d in out_shapes],
        in_specs=[pl.BlockSpec(memory_space=pl.ANY)] * n, out_specs=[pl.BlockSpec(memory_space=pl.ANY)] * len(out_shapes),
        scratch_shapes=[pltpu.SemaphoreType.DMA((n_sems,)), pltpu.SemaphoreType.DMA((n_sems,)), pltpu.SemaphoreType.DMA((n,))],
        name=name)(*arrays)


def _all_gather(blocks, name):
    n = len(blocks)
    per = N_DEV - 1

    def body(*refs):
        x_refs, out_refs = refs[:n], refs[n:2 * n]
        send_sems, recv_sems, local_sems = refs[2 * n:]
        x, y, c = _mesh_position()
        me, sibling = (x, y, c), (x, y, 1 - c)
        chips = [(1 - x, y), (x, 1 - y), (1 - x, 1 - y)]

        def copy(a, k, blk, to, src=None):
            slot = out_refs[a].at[4 * blk[0] + 2 * blk[1] + blk[2]]
            return pltpu.make_async_remote_copy(
                src_ref=slot if src is None else src, dst_ref=slot, send_sem=send_sems.at[a * per + k],
                recv_sem=recv_sems.at[a * per + k], device_id=to, device_id_type=pl.DeviceIdType.MESH)

        mine = [pltpu.make_async_copy(x_refs[a], out_refs[a].at[4 * x + 2 * y + c], local_sems.at[a]) for a in range(n)]
        for cp in mine:
            cp.start()
        first = []
        for a in range(n):
            first.append(copy(a, 0, me, sibling, src=x_refs[a]))
            first += [copy(a, 1 + j, me, (*chip, c), src=x_refs[a]) for j, chip in enumerate(chips)]
        for cp in first:
            cp.start()
        passed = []
        for j, chip in enumerate(chips):
            for a in range(n):
                copy(a, 1 + j, (*chip, c), me).wait_recv()
                passed.append(copy(a, 4 + j, (*chip, c), sibling))
                passed[-1].start()
        for a in range(n):
            copy(a, 0, sibling, me).wait_recv()
            for j, chip in enumerate(chips):
                copy(a, 4 + j, (*chip, 1 - c), me).wait_recv()
        for cp in first + passed:
            cp.wait_send()
        for cp in mine:
            cp.wait()

    return _hbm_call(body, blocks, [((N_DEV,) + b.shape, b.dtype) for b in blocks], n * per, name)


def _pair_exchange(parts, name):
    n = len(parts)

    def body(*refs):
        g_refs, got_refs = refs[:n], refs[n:2 * n]
        send_sems, recv_sems = refs[2 * n:]
        x, y, c = _mesh_position()
        places = [(x, y), (1 - x, y), (x, 1 - y), (1 - x, 1 - y)]
        copies = [pltpu.make_async_remote_copy(
            src_ref=g_refs[a].at[4 * px + 2 * py + 1 - c], dst_ref=got_refs[a].at[k],
            send_sem=send_sems.at[4 * a + k], recv_sem=recv_sems.at[4 * a + k],
            device_id=(x, y, 1 - c), device_id_type=pl.DeviceIdType.MESH) for a in range(n) for k, (px, py) in enumerate(places)]
        for cp in copies:
            cp.start()
        for cp in copies:
            cp.wait()

    return pl.pallas_call(
        body, out_shape=[jax.ShapeDtypeStruct((4,) + p_.shape[1:], p_.dtype) for p_ in parts],
        in_specs=[pl.BlockSpec(memory_space=pl.ANY)] * n, out_specs=[pl.BlockSpec(memory_space=pl.ANY)] * n,
        scratch_shapes=[pltpu.SemaphoreType.DMA((4 * n,))] * 2, name=name)(*parts)


def _chip_sums(own, got, tr=PACK_ROWS):
    _, R, cw = own.shape
    cap = min(tr, 1 << (((PACK_ROWS * PACK_W) // cw).bit_length() - 1))
    tr = math.gcd(R, cap)

    def body(a_ref, b_ref, o_ref):
        o_ref[...] = (a_ref[...].astype(F32) + b_ref[...].astype(F32)).astype(o_ref.dtype)

    blk = lambda j, i: (1 + j, i, 0)
    return _call(body, (3, R // tr), [(own, (1, tr, cw), blk), (got, (1, tr, cw), blk)],
                 [((3, R, cw), own.dtype, (1, tr, cw), lambda j, i: (j, i, 0))], name="chip_sums", sem=("parallel", "parallel"))[0]


def _chip_exchange(sums, name):
    n = len(sums)

    def body(*refs):
        s_refs, out_refs = refs[:n], refs[n:2 * n]
        send_sems, recv_sems = refs[2 * n:]
        x, y, c = _mesh_position()
        chips = [(1 - x, y), (x, 1 - y), (1 - x, 1 - y)]
        copies = [pltpu.make_async_remote_copy(
            src_ref=s_refs[a].at[j], dst_ref=out_refs[a].at[j], send_sem=send_sems.at[3 * a + j], recv_sem=recv_sems.at[3 * a + j],
            device_id=(*chip, c), device_id_type=pl.DeviceIdType.MESH) for a in range(n) for j, chip in enumerate(chips)]
        for cp in copies:
            cp.start()
        for cp in copies:
            cp.wait()

    return pl.pallas_call(
        body, out_shape=[jax.ShapeDtypeStruct(s.shape, s.dtype) for s in sums],
        in_specs=[pl.BlockSpec(memory_space=pl.ANY)] * n, out_specs=[pl.BlockSpec(memory_space=pl.ANY)] * n,
        scratch_shapes=[pltpu.SemaphoreType.DMA((3 * n,))] * 2, name=name)(*sums)


def _adamw_math(g, w, m, v):
    nm = ADAM_B1 * m + (1.0 - ADAM_B1) * g
    nv = ADAM_B2 * v + (1.0 - ADAM_B2) * (g * g)
    c1 = 1.0 - ADAM_B1 ** ADAM_STEP
    c2 = 1.0 - ADAM_B2 ** ADAM_STEP
    return -ADAM_LR * ((nm / c1) / (jnp.sqrt(nv / c2) + ADAM_EPS) + ADAM_WD * w), nm, nv


def _adamw(parts, row0, w, m, v, name, window_n8=None):
    R, C = w.shape
    cw = parts[0][0].shape[2]
    cap = min(PACK_ROWS, 1 << (((PACK_ROWS * PACK_W) // cw).bit_length() - 1))
    tr = math.gcd(math.gcd(R, cap), row0 or R)
    assert R % tr == 0 and row0 % tr == 0
    n = len(parts)

    def body(*refs):
        p_refs = refs[:n]
        w_ref, m_ref, v_ref, g_ref, d_ref, nm_ref, nv_ref = refs[n:]
        g = None
        for p_ref, (_, slots) in zip(p_refs, parts):
            for s in range(len(slots)):
                term = p_ref[s].astype(F32)
                g = term if g is None else g + term
        if window_n8 is not None:
            off = (window_n8 * _my_index()) % LANES
            g = pltpu.roll(g, (cw - off) % cw, 1)[:, :C]
        d, nm, nv = _adamw_math(g, w_ref[...], m_ref[...], v_ref[...])
        g_ref[...], d_ref[...], nm_ref[...], nv_ref[...] = g, d, nm, nv

    row = lambda i: (i, 0)
    r0 = row0 // tr
    ins = []
    for arr, slots in parts:
        assert list(slots) == list(range(slots[0], slots[0] + len(slots))) and slots[0] % len(slots) == 0
        s0 = slots[0] // len(slots)
        ins.append((arr, (len(slots), tr, cw), lambda i, s0=s0: (s0, r0 + i, 0)))
    return _call(body, (R // tr,), ins + [(w, (tr, C), row), (m, (tr, C), row), (v, (tr, C), row)],
                 [((R, C), F32, (tr, C), row)] * 4, name=name, sem=("parallel",))


def _window_geometry(n8):
    offs = [(d * n8) % LANES for d in range(N_DEV)]
    starts = [(d * n8) // LANES for d in range(N_DEV)]
    blocks = max(-(-(o + n8) // LANES) for o in offs)
    return starts, blocks, max(starts) + blocks


def _to_window(wpad, n8, tr=256):
    R, cw = wpad.shape
    tr = _tile(R, tr)

    def body(x_ref, o_ref):
        o_ref[...] = pltpu.roll(x_ref[...], (n8 * _my_index()) % LANES, 1).astype(BF16)

    row = lambda i: (i, 0)
    return _call(body, (R // tr,), [(wpad, (tr, cw), row)], [((R, cw), BF16, (tr, cw), row)], name="to_window", sem=("parallel",))[0]


def _from_windows(win, row0, rows, n8, tr=128):
    starts, blocks, total = _window_geometry(n8)
    cw = win.shape[2]
    tr = _tile(rows, tr)
    r0 = row0 // tr

    def body(w_ref, o_ref):
        o_ref[...] = jnp.zeros_like(o_ref)
        for d in range(N_DEV):
            cols = pl.ds(starts[d] * LANES, cw)
            o_ref[:, cols] = (o_ref[:, cols].astype(F32) + w_ref[d].astype(F32)).astype(BF16)

    return _call(body, (rows // tr,), [(win, (N_DEV, tr, cw), lambda i: (0, r0 + i, 0))],
                 [((rows, total * LANES), BF16, (tr, total * LANES), lambda i: (i, 0))], name="from_windows", sem=("parallel",))[0]


def _pack(flat_pieces, dtype, lead=()):
    cat = jnp.concatenate([p_.astype(dtype) for p_ in flat_pieces], axis=-1)
    n = cat.shape[-1]
    quantum = PACK_ROWS * PACK_W
    total = -(-n // quantum) * quantum
    cat = jnp.pad(cat, [(0, 0)] * len(lead) + [(0, total - n)])
    return cat.reshape(lead + (total // PACK_W, PACK_W)), n


REPLICATED = ("norm_mix", "norm_mlp", "norm_ple", "norm_final", "gdn_a_log", "gdn_dt_bias", "gdn_o_norm",
              "s5_lam_re", "s5_lam_im", "s5_log_step", "s5_b_re", "s5_b_im", "s5_c_re", "s5_c_im")
WEIGHTS = ("norm_mix", "norm_mlp", "norm_ple", "norm_final", "gdn_w_in", "gdn_conv_w", "gdn_a_log", "gdn_dt_bias",
           "gdn_o_norm", "gdn_w_out", "s5_w_in", "s5_lam_re", "s5_lam_im", "s5_log_step", "s5_b_re", "s5_b_im",
           "s5_c_re", "s5_c_im", "s5_d", "s5_w_out", "mlp_w_up", "mlp_w_down", "ple_w_proj", "ple_w_gate")
ROW_GROUP = ("mlp_w_down", "gdn_w_out", "s5_w_in", "ple_w_gate")
COL_SHARDED = ("mlp_w_up", "s5_w_out", "ple_w_proj")


def _rows2d(a):
    return a.reshape(-1, a.shape[-1])


def _misc_pack(conv, s5d):
    cw = conv.shape[-1]
    rows = jnp.concatenate([_rows2d(conv), jnp.pad(s5d, ((0, 0), (0, cw - s5d.shape[-1])))], axis=0)
    return jnp.pad(rows, ((0, -rows.shape[0] % 8), (0, 0)))


def _misc_unpack(a, conv_rows, s5d_shape):
    return a[:conv_rows], a[conv_rows:conv_rows + s5d_shape[0], :s5d_shape[1]]


def kernel(x, p, norm_mix, norm_mlp, norm_ple, norm_final, gdn_w_in, gdn_conv_w, gdn_a_log, gdn_dt_bias, gdn_o_norm, gdn_w_out, s5_w_in, s5_lam_re, s5_lam_im, s5_log_step, s5_b_re, s5_b_im, s5_c_re, s5_c_im, s5_d, s5_w_out, mlp_w_up, mlp_w_down, ple_w_proj, ple_w_gate, loss_target, m_norm_mix, m_norm_mlp, m_norm_ple, m_norm_final, m_gdn_w_in, m_gdn_conv_w, m_gdn_a_log, m_gdn_dt_bias, m_gdn_o_norm, m_gdn_w_out, m_s5_w_in, m_s5_lam_re, m_s5_lam_im, m_s5_log_step, m_s5_b_re, m_s5_b_im, m_s5_c_re, m_s5_c_im, m_s5_d, m_s5_w_out, m_mlp_w_up, m_mlp_w_down, m_ple_w_proj, m_ple_w_gate, v_norm_mix, v_norm_mlp, v_norm_ple, v_norm_final, v_gdn_w_in, v_gdn_conv_w, v_gdn_a_log, v_gdn_dt_bias, v_gdn_o_norm, v_gdn_w_out, v_s5_w_in, v_s5_lam_re, v_s5_lam_im, v_s5_log_step, v_s5_b_re, v_s5_b_im, v_s5_c_re, v_s5_c_im, v_s5_d, v_s5_w_out, v_mlp_w_up, v_mlp_w_down, v_ple_w_proj, v_ple_w_gate):
    args = dict(locals())
    w = {n: args[n] for n in WEIGHTS}
    mom = {n: args["m_" + n] for n in WEIGHTS}
    vel = {n: args["v_" + n] for n in WEIGHTS}
    depth = norm_mix.shape[0]
    T, D = x.shape[1], x.shape[2]
    hv = gdn_a_log.shape[1]
    vd = hv * HEAD
    n_gdn, n_s5 = gdn_w_in.shape[0], s5_w_in.shape[0]
    cw = gdn_conv_w.shape[2]
    cd = cw * N_DEV
    hk = (cd - vd) // (2 * HEAD)
    assert hv == 2 * hk and 2 * hv <= LANES and T % SUPER == 0 and SUPER % CHUNK == 0
    G, P = s5_lam_re.shape[1], s5_lam_re.shape[2]
    assert P == S5_STATE and G * S5_CH == D and G % S5_GPB == 0 and D // N_DEV <= cw
    n8 = gdn_w_in.shape[2]
    win_starts, win_blocks, win_total = _window_geometry(n8)
    cwin = win_blocks * LANES
    assert win_total * LANES == cd + vd + LANES

    row_off, off = {}, 0
    for n in ROW_GROUP:
        row_off[n] = off
        off += w[n].shape[0] * w[n].shape[1]
    gathered = _all_gather(
        [_to_window(jnp.pad(_rows2d(gdn_w_in), ((0, 0), (0, cwin - n8))), n8),
         jnp.concatenate([_rows2d(w[n]) for n in ROW_GROUP], axis=0).astype(BF16)]
        + [_rows2d(w[n]).astype(BF16) for n in COL_SHARDED] + [_misc_pack(gdn_conv_w, s5_d)], "gather_weights")
    g_win, g_row, g_misc = gathered[0], gathered[1], gathered[-1]
    g_col = dict(zip(COL_SHARDED, gathered[2:-1]))
    conv_full = g_misc[:, :n_gdn * 4].reshape(N_DEV, n_gdn, 4, cw).transpose(1, 2, 0, 3).reshape(n_gdn, 4, cd)
    s5d_full = g_misc[:, n_gdn * 4:n_gdn * 4 + n_s5, :D // N_DEV].transpose(1, 0, 2).reshape(n_s5, D)

    def weight(name, l):
        r = w[name].shape[1]
        if name in ROW_GROUP:
            return Sharded(g_row, 0, row_off[name] + l * r, r)
        return Sharded(g_col[name], 1, l * r, r)

    h = x[0]
    tgt = loss_target[0]
    grads = {n: [None] * w[n].shape[0] for n in WEIGHTS if n != "norm_final"}
    saved = []
    add = lambda acc, r: (r + acc,)

    for i in range(depth):
        j = i // 2
        sv = {"h0": h}
        hn = _rms_fwd(h, norm_mix[i])
        sv["hn"] = hn
        if i % 2 == 0:
            w_in = _from_windows(g_win, j * D, D, n8)
            pq = _mm(hn, w_in[:, :cd], name="gdn_in_qkv")[0]
            pz = _mm(hn, w_in[:, cd:cd + vd], name="gdn_in_z")[0]
            ba = _mm(hn, w_in[:, cd + vd:], name="gdn_in_ba")[0]
            qkv = _gdn_pre_fwd(pq, conv_full[j], hk)
            pv = jnp.pad(jnp.stack([gdn_a_log[j], gdn_dt_bias[j]]), ((0, 0), (hv, LANES - 2 * hv)))
            g2 = _gates_fwd(ba, pv, hv)
            bb = jnp.repeat(g2[:, :hv], HEAD, axis=1)
            gb = jnp.repeat(g2[:, hv:2 * hv], HEAD, axis=1)
            u, ww, gc, tinv = _gdn_a_fwd(qkv, gb, bb, hk, hv)
            o, vn, sall = _gdn_b_fwd(qkv, u, ww, gc, hk, hv)
            on = _ogate_fwd(o, pz, gdn_o_norm[j])
            h = _mm(on, weight("gdn_w_out", j), epi=add, extras=(h,), name="gdn_out")[0]
            sv.update(w_in=w_in, pq=pq, pz=pz, ba=ba, pv=pv, qkv=qkv, bb=bb, u=u, ww=ww, gc=gc, tinv=tinv, o=o, vn=vn, sall=sall, on=on)
        else:
            uu = _mm(hn, weight("s5_w_in", j), name="s5_in")[0]
            b_re_t, b_im_t = s5_b_re[j].transpose(2, 0, 1), s5_b_im[j].transpose(2, 0, 1)
            ls = s5_log_step[j].reshape(G, 1)
            ar, ai, bbr, bbi = _s5_params_fwd(s5_lam_re[j], s5_lam_im[j], ls, b_re_t, b_im_t)
            lam = jnp.stack([ar.reshape(-1), ai.reshape(-1)])
            bd_re = _block_diag(bbr.transpose(1, 2, 0), transpose=True).astype(BF16)
            bd_im = _block_diag(bbi.transpose(1, 2, 0), transpose=True).astype(BF16)
            cd_re = _block_diag(s5_c_re[j], transpose=True).astype(BF16)
            cd_im = _block_diag(-s5_c_im[j], transpose=True).astype(BF16)
            bur, bui = _s5_bproj_fwd(uu, bd_re, bd_im)
            xr, xi = _s5_scan(bur, bui, lam, reverse=False)
            dsk = s5d_full[j]
            yy, hact = _s5_cproj_fwd(xr, xi, cd_re, cd_im, uu, dsk)
            vg = _mm(hact, weight("s5_w_out", j), name="s5_out")[0]
            h = _s5_gate_fwd(h, vg)
            sv.update(uu=uu, b_re_t=b_re_t, b_im_t=b_im_t, ls=ls, lam=lam, bd_re=bd_re, bd_im=bd_im, cd_re=cd_re, cd_im=cd_im,
                      xr=xr, xi=xi, dsk=dsk, yy=yy, hact=hact, vg=vg)
        sv["h1"] = h
        hm = _rms_fwd(h, norm_mlp[i])
        up, act = _mm(hm, weight("mlp_w_up", i), out_dtypes=(F32, BF16),
                      epi=lambda acc: (acc, jnp.square(jnp.maximum(acc, 0.0))), name="mlp_up")
        h = _mm(act, weight("mlp_w_down", i), epi=add, extras=(h,), name="mlp_down")[0]
        sv.update(hm=hm, up=up, act=act, h2=h)
        hp = _rms_fwd(h, norm_ple[i])
        s_gate = _mm(hp, weight("ple_w_gate", i), name="ple_gate")[0]
        pp = _mm(p[i, 0], weight("ple_w_proj", i), name="ple_proj")[0]
        h = _ple_fwd(h, s_gate, pp)
        sv.update(hp=hp, s_gate=s_gate, pp=pp)
        saved.append(sv)

    dh, d_norm_final, loss_part = _loss_fwd_bwd(h, norm_final, tgt)
    loss = lax.psum(loss_part[0, 0], MESH_AXES)

    dw = lambda a, b_, axis, name: _mm(a, b_, "tn", out_dtypes=(BF16,), out_axis=axis, name=name)[0]
    for i in reversed(range(depth)):
        j = i // 2
        sv = saved[i]
        ds, dpp = _ple_bwd(dh, sv["s_gate"], sv["pp"])
        grads["ple_w_proj"][i] = dw(p[i, 0], dpp, 1, "ple_proj_dw")
        grads["ple_w_gate"][i] = dw(sv["hp"], ds, 0, "ple_gate_dw")
        d_hp = _mm(ds, weight("ple_w_gate", i), "nt", name="ple_gate_dx")[0]
        dh, dh_b, grads["norm_ple"][i] = _rms_bwd(d_hp, sv["h2"], norm_ple[i], dh)
        grads["mlp_w_down"][i] = dw(sv["act"], dh_b, 0, "mlp_down_dw")
        d_up = _mm(dh_b, weight("mlp_w_down", i), "nt", out_dtypes=(BF16,),
                   epi=lambda acc, up_: (acc * 2.0 * jnp.maximum(up_, 0.0),), extras=(sv["up"],), name="mlp_down_dx")[0]
        grads["mlp_w_up"][i] = dw(sv["hm"], d_up, 1, "mlp_up_dw")
        d_hm = _mm(d_up, weight("mlp_w_up", i), "nt", name="mlp_up_dx")[0]
        dh, dh_b, grads["norm_mlp"][i] = _rms_bwd(d_hm, sv["h1"], norm_mlp[i], dh)
        if i % 2 == 0:
            grads["gdn_w_out"][j] = dw(sv["on"], dh_b, 0, "gdn_out_dw")
            d_on = _mm(dh_b, weight("gdn_w_out", j), "nt", name="gdn_out_dx")[0]
            d_o, d_z, grads["gdn_o_norm"][j] = _ogate_bwd(d_on, sv["o"], sv["pz"], gdn_o_norm[j])
            dq_b, dk_b, dgc_b, d_u, d_w = _gdn_b_bwd(d_o, sv["qkv"], sv["ww"], sv["gc"], sv["vn"], sv["sall"], hk, hv)
            dk_a, d_v, d_bb, d_gb = _gdn_a_bwd(d_u, d_w, dgc_b, sv["qkv"], sv["bb"], sv["gc"], sv["tinv"], sv["u"], sv["ww"], hk, hv)
            d_qkv = jnp.concatenate([_pair_sum(dq_b, None), _pair_sum(dk_a, dk_b), d_v], axis=1)
            d_c, grads["gdn_conv_w"][j] = _gdn_pre_bwd(d_qkv, sv["pq"], conv_full[j], hk)
            d_pq = _gdn_conv_bwd(d_c, conv_full[j])
            d_g2 = jnp.pad(jnp.concatenate([d_bb[:, ::HEAD], d_gb[:, ::HEAD]], axis=1), ((0, 0), (0, LANES - 2 * hv)))
            d_ba, d_pv = _gates_bwd(d_g2, sv["ba"], sv["pv"], hv)
            grads["gdn_a_log"][j] = d_pv[0, hv:2 * hv]
            grads["gdn_dt_bias"][j] = d_pv[1, hv:2 * hv]
            hn, w_in = sv["hn"], sv["w_in"]
            dw_nat = jnp.concatenate([dw(hn, d_pq, None, "gdn_in_qkv_dw"), dw(hn, d_z, None, "gdn_in_z_dw"),
                                      dw(hn, d_ba, None, "gdn_in_ba_dw")], axis=1)
            grads["gdn_w_in"][j] = jnp.stack([dw_nat[:, s * LANES:s * LANES + cwin] for s in win_starts])
            d_hn = _mm(d_pq, w_in[:, :cd], "nt", name="gdn_in_qkv_dx")[0]
            d_hn = _mm(d_z, w_in[:, cd:cd + vd], "nt", epi=add, extras=(d_hn,), name="gdn_in_z_dx")[0]
            d_hn = _mm(d_ba, w_in[:, cd + vd:], "nt", epi=add, extras=(d_hn,), name="gdn_in_ba_dx")[0]
        else:
            d_vg = _s5_gate_bwd(dh, sv["vg"])
            grads["s5_w_out"][j] = dw(sv["hact"], d_vg, 1, "s5_out_dw")

            def gelu_bwd(acc, y_):
                _, vjp = jax.vjp(jax.nn.gelu, y_)
                return (vjp(acc)[0],)

            d_y = _mm(d_vg, weight("s5_w_out", j), "nt", epi=gelu_bwd, extras=(sv["yy"],), name="s5_out_dx")[0]
            d_xr, d_xi, du_skip, d_dsk, d_cdr, d_cdi = _s5_cproj_bwd(d_y, sv["xr"], sv["xi"], sv["cd_re"], sv["cd_im"], sv["uu"], sv["dsk"])
            grads["s5_d"][j] = d_dsk
            grads["s5_c_re"][j] = _block_diag_extract(d_cdr, S5_CH, S5_STATE, transpose=True)
            grads["s5_c_im"][j] = -_block_diag_extract(d_cdi, S5_CH, S5_STATE, transpose=True)
            d_bur, d_bui, d_lam = _s5_scan(d_xr, d_xi, sv["lam"], reverse=True, xr=sv["xr"], xi=sv["xi"])
            d_uu, d_bdr, d_bdi = _s5_bproj_bwd(d_bur, d_bui, sv["bd_re"], sv["bd_im"], sv["uu"], du_skip)
            d_bbr = _block_diag_extract(d_bdr, S5_STATE, S5_CH, transpose=True).transpose(2, 0, 1)
            d_bbi = _block_diag_extract(d_bdi, S5_STATE, S5_CH, transpose=True).transpose(2, 0, 1)
            d_lr, d_li, d_ls, d_br_t, d_bi_t = _s5_params_bwd(
                s5_lam_re[j], s5_lam_im[j], sv["ls"], sv["b_re_t"], sv["b_im_t"],
                d_lam[0].reshape(G, P), d_lam[1].reshape(G, P), d_bbr, d_bbi)
            grads["s5_lam_re"][j], grads["s5_lam_im"][j], grads["s5_log_step"][j] = d_lr, d_li, d_ls.reshape(G)
            grads["s5_b_re"][j], grads["s5_b_im"][j] = d_br_t.transpose(1, 2, 0), d_bi_t.transpose(1, 2, 0)
            grads["s5_w_in"][j] = dw(sv["hn"], d_uu, 0, "s5_in_dw")
            d_hn = _mm(d_uu, weight("s5_w_in", j), "nt", name="s5_in_dx")[0]
        dh, _, grads["norm_mix"][i] = _rms_bwd(d_hn, sv["h0"], norm_mix[i], dh)

    out = {}
    layers = lambda name: jnp.concatenate(grads[name], axis=1)
    d_conv = jnp.stack(grads["gdn_conv_w"]).reshape(n_gdn * 4, N_DEV, cw).transpose(1, 0, 2)
    d_s5d = jnp.stack([g.reshape(N_DEV, D // N_DEV) for g in grads["s5_d"]], axis=1)
    d_misc = jnp.concatenate([d_conv, jnp.pad(d_s5d, ((0, 0), (0, 0), (0, cw - D // N_DEV)))], axis=1)
    d_misc = jnp.pad(d_misc, ((0, 0), (0, -d_misc.shape[1] % 8), (0, 0)))
    contributions = ([layers("gdn_w_in"), jnp.concatenate([layers(n) for n in ROW_GROUP], axis=1)]
                     + [layers(n) for n in COL_SHARDED] + [d_misc])
    got = _pair_exchange(contributions, "exchange_grads_pair")
    px_, py_, pc_ = _mesh_position()
    mine = [4 * qx + 2 * qy + pc_ for qx, qy in ((px_, py_), (1 - px_, py_), (px_, 1 - py_), (1 - px_, 1 - py_))]
    own = [jnp.stack([lax.dynamic_index_in_dim(a, d, 0, keepdims=False) for d in mine]) for a in contributions]
    arrived = _chip_exchange([_chip_sums(o, g) for o, g in zip(own, got)], "exchange_grads_chips")
    recv = [[(o, (0,)), (g, (0,)), (r, (0, 1, 2))] for o, g, r in zip(own, got, arrived)]
    r_win, r_row, r_misc = recv[0], recv[1], recv[-1]
    r_col = dict(zip(COL_SHARDED, recv[2:-1]))

    def update(name, parts, row0, **kw):
        res = _adamw(parts, row0, _rows2d(w[name]), _rows2d(mom[name]), _rows2d(vel[name]), "adamw_" + name, **kw)
        out[name] = [r.reshape(w[name].shape) for r in res]

    update("gdn_w_in", r_win, 0, window_n8=n8)
    for n in ROW_GROUP:
        update(n, r_row, row_off[n])
    for n in COL_SHARDED:
        update(n, r_col[n], 0)
    res = _adamw(r_misc, 0, _misc_pack(gdn_conv_w, s5_d), _misc_pack(mom["gdn_conv_w"], mom["s5_d"]),
                 _misc_pack(vel["gdn_conv_w"], vel["s5_d"]), "adamw_misc")
    unpacked = [_misc_unpack(r, n_gdn * 4, s5_d.shape) for r in res]
    out["gdn_conv_w"] = [u_[0].reshape(gdn_conv_w.shape) for u_ in unpacked]
    out["s5_d"] = [u_[1] for u_ in unpacked]

    rep_g = {n: (d_norm_final[0] if n == "norm_final" else jnp.stack([g.reshape(w[n].shape[1:]) for g in grads[n]])) for n in REPLICATED}
    flat_r = lambda d: [d[n].reshape(-1) for n in REPLICATED]
    pg, _ = _pack(flat_r(rep_g), F32)
    parts_r = _all_gather([pg], "gather_small_grads")[0]
    pw, _ = _pack(flat_r(w), F32)
    pm, _ = _pack(flat_r(mom), F32)
    pvv, _ = _pack(flat_r(vel), F32)
    res = [r.reshape(-1) for r in _adamw([(parts_r, tuple(range(N_DEV)))], 0, pw, pm, pvv, "adamw_replicated")]
    off = 0
    for name in REPLICATED:
        n = w[name].size
        out[name] = [res[k][off:off + n].reshape(w[name].shape) for k in range(4)]
        off += n

    grad_x = dh[None]
    return (loss, grad_x, *[out[n][0] for n in WEIGHTS], *[out[n][1] for n in WEIGHTS],
            *[out[n][2] for n in WEIGHTS], *[out[n][3] for n in WEIGHTS])
```

```python
import collections
import math

import jax
import jax.numpy as jnp
from jax import lax
from jax.experimental import pallas as pl
from jax.experimental.pallas import tpu as pltpu

F32, BF16 = jnp.float32, jnp.bfloat16
NN, NT, TN = ((1,), (0,)), ((1,), (1,)), ((0,), (0,))

N_DEV = 8
MESH_AXES = ("x", "y", "c")
LANES = 128
V7X_VMEM_BYTES = 64 * 1024 * 1024
VMEM_LIMIT = V7X_VMEM_BYTES - 8 * 1024 * 1024
CHUNK = 64
HEAD = 128
SUPER = 256
S5_CH = 16
S5_STATE = 64
S5_GPB = LANES // S5_CH
S5_SPB = S5_GPB * S5_STATE
NORM_EPS = 1e-6
L2_EPS = 1e-6
ADAM_LR, ADAM_B1, ADAM_B2, ADAM_EPS, ADAM_WD, ADAM_STEP = 0.001, 0.9, 0.999, 1e-08, 0.01, 10
PACK_W = 1024
PACK_ROWS = 256


def _dot(a, b, dims):
    return lax.dot_general(a, b, (dims, ((), ())), preferred_element_type=F32)


def _bdot(a, b, dims=NN):
    return _dot(a.astype(BF16), b.astype(BF16), dims)


def _call(body, grid, ins, outs, scratch=(), name=None, sem=None):
    res = pl.pallas_call(
        body,
        grid=grid,
        in_specs=[pl.BlockSpec(b, m) for _, b, m in ins],
        out_specs=[pl.BlockSpec(b, m) for _, _, b, m in outs],
        out_shape=[jax.ShapeDtypeStruct(s, d) for s, d, _, _ in outs],
        scratch_shapes=list(scratch),
        name=name,
        compiler_params=pltpu.CompilerParams(
            dimension_semantics=sem or ("arbitrary",) * len(grid), vmem_limit_bytes=VMEM_LIMIT),
    )(*[a for a, _, _ in ins])
    return res


def _tile(n, want):
    t = min(n, want)
    assert n % t == 0, (n, want)
    return t


def _accumulate(ref, val, first):
    @pl.when(first)
    def _():
        ref[...] = jnp.zeros_like(ref)
    ref[...] += val


class Sharded(collections.namedtuple("Sharded", "arr axis row0 rows")):
    @property
    def shape(self):
        c = self.arr.shape[2]
        return (self.rows, N_DEV * c) if self.axis == 1 else (N_DEV * self.rows, c)

    @property
    def units(self):
        return (math.gcd(self.rows, self.row0), self.arr.shape[2])


def _mm(a, b, mode="nn", out_dtypes=(F32,), epi=None, extras=(), name="mm", out_axis=None, tm=1024, tn=1024, tk=2048):
    sh = isinstance(b, Sharded)
    b_rows, b_cols = b.shape
    u_rows, u_cols = b.units if sh else b.shape
    if mode == "nn":
        (M, K), (K2, N), (uk, un) = a.shape, (b_rows, b_cols), (u_rows, u_cols)
    elif mode == "nt":
        (M, K), (N, K2), (un, uk) = a.shape, (b_rows, b_cols), (u_rows, u_cols)
    else:
        (K, M), (K2, N), (uk, un) = a.shape, (b_rows, b_cols), (u_rows, u_cols)
    assert K == K2, (a.shape, b.shape, mode)
    um = M
    if out_axis == 0:
        um = M // N_DEV
    elif out_axis == 1:
        un = N // N_DEV
    ospan = 1
    if out_axis == 0 and um < min(M, tm):
        ospan = min(M, tm) // um
        assert N_DEV % ospan == 0
    tm, tn = _tile(um, tm) * ospan, _tile(un, tn)
    span = 1
    if sh and ((mode == "nn" and b.axis == 0 and uk == b.rows) or (mode == "nt" and b.axis == 1)) and uk < min(K, tk):
        span = min(K, tk) // uk
        assert N_DEV % span == 0
    tk = _tile(uk, tk)
    nk = K // (tk * span)
    a_spec = ((tk, tm), lambda i, j, k: (k, i)) if mode == "tn" else ((tm, tk * span), lambda i, j, k: (i, k))
    if not sh:
        b_arr = b
        b_spec = ((tn, tk), lambda i, j, k: (j, k)) if mode == "nt" else ((tk, tn), lambda i, j, k: (k, j))
    else:
        b_arr = b.arr
        tr_, tc_ = (tk, tn) if mode == "nn" else (tn, tk)
        r0, per_r, per_c = b.row0 // tr_, b.rows // tr_, b.arr.shape[2] // tc_
        assert b.row0 % tr_ == 0 and mode != "tn"
        if span > 1:
            place = (lambda r, c: (r, r0, c)) if b.axis == 0 else (lambda r, c: (c, r0 + r, 0))
        elif b.axis == 1:
            place = lambda r, c: (c // per_c, r0 + r, c % per_c)
        else:
            place = lambda r, c: (r // per_r, r0 + r % per_r, c)
        b_spec = ((span, tr_, tc_), (lambda i, j, k: place(k, j)) if mode == "nn" else (lambda i, j, k: place(j, k)))
    dims = {"nn": NN, "nt": NT, "tn": TN}[mode]
    n_ex, n_out = len(extras), len(out_dtypes)

    def body(*refs):
        a_ref, b_ref = refs[:2]
        ex = refs[2:2 + n_ex]
        outs = refs[2 + n_ex:2 + n_ex + n_out]

        def product():
            if not sh:
                return _bdot(a_ref[...], b_ref[...], dims)
            part = _bdot(a_ref[:, :tk], b_ref[0], dims)
            for s in range(1, span):
                part = part + _bdot(a_ref[:, s * tk:(s + 1) * tk], b_ref[s], dims)
            return part

        def finish(res):
            vals = epi(res, *[e[...] for e in ex]) if epi is not None else (res,)
            for r, v in zip(outs, vals):
                r[...] = v.astype(r.dtype).reshape(r.shape)

        if nk == 1:
            finish(product())
            return
        acc = refs[-1]
        k = pl.program_id(2)

        @pl.when(k == 0)
        def _():
            acc[...] = jnp.zeros_like(acc)

        acc[...] += product()

        @pl.when(k == nk - 1)
        def _():
            finish(acc[...])

    tile = lambda i, j, k: (i, j)
    if out_axis is None:
        out_shape, out_block, out_map = (M, N), (tm, tn), tile
    elif out_axis == 0 and ospan > 1:
        out_shape, out_block, out_map = (N_DEV, um, N), (ospan, um, tn), lambda i, j, k: (i, 0, j)
    elif out_axis == 0:
        per = um // tm
        out_shape, out_block, out_map = (N_DEV, um, N), (None, tm, tn), lambda i, j, k: (i // per, i % per, j)
    else:
        per = un // tn
        out_shape, out_block, out_map = (N_DEV, M, un), (None, tm, tn), lambda i, j, k: (j // per, i, j % per)
    return _call(
        body, (M // tm, N // tn, nk),
        [(a,) + a_spec, (b_arr,) + b_spec] + [(e, (tm, tn), tile) for e in extras],
        [(out_shape, d, out_block, out_map) for d in out_dtypes],
        scratch=[pltpu.VMEM((tm, tn), F32)] if nk > 1 else [], name=name,
        sem=("parallel", "parallel", "arbitrary"))


def _rms_fwd(h, g, tr=256):
    T, D = h.shape
    tr = _tile(T, tr)

    def body(h_ref, g_ref, o_ref):
        x = h_ref[...]
        r = lax.rsqrt(jnp.mean(x * x, axis=-1, keepdims=True) + NORM_EPS)
        o_ref[...] = (x * r * g_ref[...]).astype(BF16)

    row = lambda i: (i, 0)
    fix = lambda i: (0, 0)
    return _call(body, (T // tr,), [(h, (tr, D), row), (g.reshape(1, D), (1, D), fix)],
                 [((T, D), BF16, (tr, D), row)], name="rms_fwd", sem=("parallel",))[0]


def _rms_bwd_math(dy, x, g):
    r = lax.rsqrt(jnp.mean(x * x, axis=-1, keepdims=True) + NORM_EPS)
    xh = x * r
    dxh = dy * g
    dx = r * (dxh - xh * jnp.mean(dxh * xh, axis=-1, keepdims=True))
    dg = jnp.sum(dy * xh, axis=0, keepdims=True)
    return dx, dg


def _rms_bwd(dy, h, g, res, tr=256):
    T, D = h.shape
    tr = _tile(T, tr)

    def body(dy_ref, h_ref, g_ref, res_ref, dh_ref, dhb_ref, dg_ref):
        dx, dg = _rms_bwd_math(dy_ref[...], h_ref[...], g_ref[...])
        dh = res_ref[...] + dx
        dh_ref[...] = dh
        dhb_ref[...] = dh.astype(BF16)
        _accumulate(dg_ref, dg, pl.program_id(0) == 0)

    row = lambda i: (i, 0)
    fix = lambda i: (0, 0)
    return _call(body, (T // tr,),
                 [(dy, (tr, D), row), (h, (tr, D), row), (g.reshape(1, D), (1, D), fix), (res, (tr, D), row)],
                 [((T, D), F32, (tr, D), row), ((T, D), BF16, (tr, D), row), ((1, D), F32, (1, D), fix)], name="rms_bwd")


def _loss_fwd_bwd(h, g, tgt, tr=256):
    T, D = h.shape
    tr = _tile(T, tr)

    def body(h_ref, g_ref, t_ref, dh_ref, dg_ref, loss_ref):
        x, gg = h_ref[...], g_ref[...]
        r = lax.rsqrt(jnp.mean(x * x, axis=-1, keepdims=True) + NORM_EPS)
        diff = x * r * gg - t_ref[...]
        part = 0.5 * jnp.sum(jnp.mean(diff * diff, axis=-1, keepdims=True))
        dx, dg = _rms_bwd_math(diff * (1.0 / D), x, gg)
        dh_ref[...] = dx
        first = pl.program_id(0) == 0
        _accumulate(dg_ref, dg, first)
        _accumulate(loss_ref, jnp.full((1, LANES), part, F32), first)

    row = lambda i: (i, 0)
    fix = lambda i: (0, 0)
    return _call(body, (T // tr,),
                 [(h, (tr, D), row), (g.reshape(1, D), (1, D), fix), (tgt, (tr, D), row)],
                 [((T, D), F32, (tr, D), row), ((1, D), F32, (1, D), fix), ((1, LANES), F32, (1, LANES), fix)],
                 name="loss_fwd_bwd")


def _ple_fwd(h, s, pp, tr=256):
    T, D = h.shape
    tr = _tile(T, tr)

    def body(h_ref, s_ref, p_ref, o_ref):
        o_ref[...] = h_ref[...] + jax.nn.sigmoid(s_ref[...]) * p_ref[...]

    row = lambda i: (i, 0)
    return _call(body, (T // tr,), [(a, (tr, D), row) for a in (h, s, pp)],
                 [((T, D), F32, (tr, D), row)], name="ple_fwd", sem=("parallel",))[0]


def _ple_bwd(dh, s, pp, tr=256):
    T, D = dh.shape
    tr = _tile(T, tr)

    def body(dh_ref, s_ref, p_ref, ds_ref, dp_ref):
        d = dh_ref[...]
        gate = jax.nn.sigmoid(s_ref[...])
        ds_ref[...] = (d * p_ref[...] * gate * (1.0 - gate)).astype(BF16)
        dp_ref[...] = (d * gate).astype(BF16)

    row = lambda i: (i, 0)
    return _call(body, (T // tr,), [(a, (tr, D), row) for a in (dh, s, pp)],
                 [((T, D), BF16, (tr, D), row)] * 2, name="ple_bwd", sem=("parallel",))


def _conv_taps(xe, w, tr):
    c = w[3:4, :] * xe[8:, :]
    for j in range(3):
        c = c + w[j:j + 1, :] * pltpu.roll(xe, 3 - j, 0)[8:, :]
    return c


def _gdn_pre_fwd(pq, conv_w, hk, tr=2048):
    T, CD = pq.shape
    tr = _tile(T, tr)
    r8 = tr // 8

    def body(x_ref, halo_ref, w_ref, o_ref):
        j, r = pl.program_id(0), pl.program_id(1)
        halo = jnp.where(r > 0, halo_ref[...], 0.0)
        xe = jnp.concatenate([halo, x_ref[...]], axis=0)
        c = _conv_taps(xe, w_ref[...], tr)
        s = c * jax.nn.sigmoid(c)
        rn = lax.rsqrt(jnp.sum(s * s, axis=-1, keepdims=True) + L2_EPS)
        scale = jnp.where(j < hk, HEAD ** -0.5, 1.0)
        o_ref[...] = jnp.where(j < 2 * hk, s * rn * scale, s)

    tile = lambda j, r: (r, j)
    return _call(body, (CD // HEAD, T // tr),
                 [(pq, (tr, HEAD), tile), (pq, (8, HEAD), lambda j, r: (jnp.maximum(r * r8 - 1, 0), j)),
                  (conv_w, (4, HEAD), lambda j, r: (0, j))],
                 [((T, CD), F32, (tr, HEAD), tile)], name="gdn_pre_fwd", sem=("parallel", "parallel"))[0]


def _gdn_pre_bwd(dn, pq, conv_w, hk, tr=2048):
    T, CD = pq.shape
    tr = _tile(T, tr)
    r8 = tr // 8

    def body(dn_ref, x_ref, halo_ref, w_ref, dc_ref, dw_ref):
        j, r = pl.program_id(0), pl.program_id(1)
        halo = jnp.where(r > 0, halo_ref[...], 0.0)
        xe = jnp.concatenate([halo, x_ref[...]], axis=0)
        c = _conv_taps(xe, w_ref[...], tr)
        sig = jax.nn.sigmoid(c)
        s = c * sig
        rn = lax.rsqrt(jnp.sum(s * s, axis=-1, keepdims=True) + L2_EPS)
        scale = jnp.where(j < hk, HEAD ** -0.5, 1.0)
        d = dn_ref[...]
        y = s * rn
        dy = d * scale
        ds = jnp.where(j < 2 * hk, rn * (dy - y * jnp.sum(dy * y, axis=-1, keepdims=True)), d)
        dc = ds * sig * (1.0 + c * (1.0 - sig))
        dc_ref[...] = dc

        @pl.when(r == 0)
        def _():
            dw_ref[...] = jnp.zeros_like(dw_ref)

        for t in range(4):
            xs = xe[8:, :] if t == 3 else pltpu.roll(xe, 3 - t, 0)[8:, :]
            dw_ref[t:t + 1, :] += jnp.sum(dc * xs, axis=0, keepdims=True)

    tile = lambda j, r: (r, j)
    col = lambda j, r: (0, j)
    return _call(body, (CD // HEAD, T // tr),
                 [(dn, (tr, HEAD), tile), (pq, (tr, HEAD), tile),
                  (pq, (8, HEAD), lambda j, r: (jnp.maximum(r * r8 - 1, 0), j)), (conv_w, (4, HEAD), col)],
                 [((T, CD), F32, (tr, HEAD), tile), ((4, CD), F32, (4, HEAD), col)], name="gdn_pre_bwd")


def _gdn_conv_bwd(dc, conv_w, tr=2048):
    T, CD = dc.shape
    tr = _tile(T, tr)
    r8 = tr // 8
    n_r = T // tr

    def body(dc_ref, halo_ref, w_ref, dx_ref):
        r = pl.program_id(1)
        halo = jnp.where(r < n_r - 1, halo_ref[...], 0.0)
        de = jnp.concatenate([dc_ref[...], halo], axis=0)
        w = w_ref[...]
        dx = w[3:4, :] * de[:tr, :]
        for j in range(3):
            dx = dx + w[j:j + 1, :] * pltpu.roll(de, tr + 8 - (3 - j), 0)[:tr, :]
        dx_ref[...] = dx.astype(BF16)

    tile = lambda j, r: (r, j)
    return _call(body, (CD // HEAD, n_r),
                 [(dc, (tr, HEAD), tile), (dc, (8, HEAD), lambda j, r: (jnp.minimum((r + 1) * r8, T // 8 - 1), j)),
                  (conv_w, (4, HEAD), lambda j, r: (0, j))],
                 [((T, CD), BF16, (tr, HEAD), tile)], name="gdn_conv_bwd", sem=("parallel", "parallel"))[0]


def _gates_fwd(ba, pv, hv, tr=1024):
    T = ba.shape[0]
    tr = _tile(T, tr)

    def body(x_ref, pv_ref, o_ref):
        x = x_ref[...]
        lane = lax.broadcasted_iota(jnp.int32, x.shape, 1)
        g = -jnp.exp(pv_ref[0:1, :]) * jax.nn.softplus(x + pv_ref[1:2, :])
        o_ref[...] = jnp.where(lane < hv, jax.nn.sigmoid(x), jnp.where(lane < 2 * hv, g, 0.0))

    row = lambda i: (i, 0)
    return _call(body, (T // tr,), [(ba, (tr, LANES), row), (pv, (2, LANES), lambda i: (0, 0))],
                 [((T, LANES), F32, (tr, LANES), row)], name="gates_fwd", sem=("parallel",))[0]


def _gates_bwd(dg2, ba, pv, hv, tr=1024):
    T = ba.shape[0]
    tr = _tile(T, tr)

    def body(d_ref, x_ref, pv_ref, dx_ref, dpv_ref):
        x, d = x_ref[...], d_ref[...]
        lane = lax.broadcasted_iota(jnp.int32, x.shape, 1)
        is_a = (lane >= hv) & (lane < 2 * hv)
        beta = jax.nn.sigmoid(x)
        neg_a = -jnp.exp(pv_ref[0:1, :])
        z = x + pv_ref[1:2, :]
        da = d * neg_a * jax.nn.sigmoid(z)
        dx_ref[...] = jnp.where(lane < hv, d * beta * (1.0 - beta), jnp.where(is_a, da, 0.0))
        first = pl.program_id(0) == 0

        @pl.when(first)
        def _():
            dpv_ref[...] = jnp.zeros_like(dpv_ref)

        dpv_ref[0:1, :] += jnp.sum(jnp.where(is_a, d * neg_a * jax.nn.softplus(z), 0.0), axis=0, keepdims=True)
        dpv_ref[1:2, :] += jnp.sum(jnp.where(is_a, da, 0.0), axis=0, keepdims=True)

    row = lambda i: (i, 0)
    fix = lambda i: (0, 0)
    return _call(body, (T // tr,), [(dg2, (tr, LANES), row), (ba, (tr, LANES), row), (pv, (2, LANES), fix)],
                 [((T, LANES), F32, (tr, LANES), row), ((2, LANES), F32, (2, LANES), fix)], name="gates_bwd")


def _ogate_fwd(o, z, o_norm, tr=2048):
    T, VD = o.shape
    tr = _tile(T, tr)

    def body(o_ref, z_ref, g_ref, y_ref):
        x, zz = o_ref[...], z_ref[...]
        r = lax.rsqrt(jnp.mean(x * x, axis=-1, keepdims=True) + NORM_EPS)
        y_ref[...] = (x * r * g_ref[...] * (zz * jax.nn.sigmoid(zz))).astype(BF16)

    tile = lambda h, r: (r, h)
    return _call(body, (VD // HEAD, T // tr),
                 [(o, (tr, HEAD), tile), (z, (tr, HEAD), tile), (o_norm.reshape(1, HEAD), (1, HEAD), lambda h, r: (0, 0))],
                 [((T, VD), BF16, (tr, HEAD), tile)], name="ogate_fwd", sem=("parallel", "parallel"))[0]


def _ogate_bwd(dy, o, z, o_norm, tr=2048):
    T, VD = o.shape
    tr = _tile(T, tr)

    def body(dy_ref, o_ref, z_ref, g_ref, do_ref, dz_ref, dg_ref):
        d, x, zz, g = dy_ref[...], o_ref[...], z_ref[...], g_ref[...]
        sig = jax.nn.sigmoid(zz)
        silu = zz * sig
        dx, dg = _rms_bwd_math(d * silu, x, g)
        r = lax.rsqrt(jnp.mean(x * x, axis=-1, keepdims=True) + NORM_EPS)
        do_ref[...] = dx
        dz_ref[...] = (d * (x * r * g) * sig * (1.0 + zz * (1.0 - sig))).astype(BF16)
        _accumulate(dg_ref, dg, (pl.program_id(0) == 0) & (pl.program_id(1) == 0))

    tile = lambda h, r: (r, h)
    fix = lambda h, r: (0, 0)
    return _call(body, (VD // HEAD, T // tr),
                 [(dy, (tr, HEAD), tile), (o, (tr, HEAD), tile), (z, (tr, HEAD), tile), (o_norm.reshape(1, HEAD), (1, HEAD), fix)],
                 [((T, VD), F32, (tr, HEAD), tile), ((T, VD), BF16, (tr, HEAD), tile), ((1, HEAD), F32, (1, HEAD), fix)],
                 name="ogate_bwd")


def _chunk_iota():
    return (lax.broadcasted_iota(jnp.int32, (CHUNK, CHUNK), 0), lax.broadcasted_iota(jnp.int32, (CHUNK, CHUNK), 1))


def _decay(gc):
    ri, ci = _chunk_iota()
    gcol = gc[:, :CHUNK]
    grow = jnp.sum(jnp.where(ri == ci, gcol, 0.0), axis=0, keepdims=True)
    return jnp.where(ri >= ci, jnp.exp(jnp.minimum(gcol - grow, 0.0)), 0.0)


def _rowsum(x):
    return jnp.broadcast_to(jnp.sum(x, axis=1, keepdims=True), (x.shape[0], HEAD))


def _split3(x):
    h1 = x.astype(BF16)
    r1 = x - h1.astype(F32)
    h2 = r1.astype(BF16)
    return h1, h2, (r1 - h2.astype(F32)).astype(BF16)


def _sel_dot(sel, xs, dims=NN):
    s = sel.astype(BF16)
    parts = [_split3(x) for x in xs]
    if dims == NN:
        return [_dot(s, h1, NN) + _dot(s, h2, NN) + _dot(s, h3, NN) for h1, h2, h3 in parts]
    return [_dot(h1, s, dims) + _dot(h2, s, dims) + _dot(h3, s, dims) for h1, h2, h3 in parts]


def _colsum(es):
    return _sel_dot(jnp.ones((es[0].shape[0], HEAD), F32), es, TN)


def _super_iota():
    ri = lax.broadcasted_iota(jnp.int32, (SUPER, SUPER), 0)
    ci = lax.broadcasted_iota(jnp.int32, (SUPER, SUPER), 1)
    shift = int(math.log2(CHUNK))
    return ri, ci, jnp.right_shift(ri, shift) == jnp.right_shift(ci, shift)


def _decay_super(gc, ri, ci, same):
    gcol = jnp.concatenate([gc] * (SUPER // HEAD), axis=1)
    grow = jnp.sum(jnp.where(ri == ci, gcol, 0.0), axis=0, keepdims=True)
    return jnp.where(same & (ri >= ci), jnp.exp(jnp.minimum(gcol - grow, 0.0)), 0.0)


def _unit_lower_inverse(ms, eye):
    ps = [-m for m in ms]
    xs = [eye + p for p in ps]
    for _ in range(int(math.log2(CHUNK)) - 1):
        ps = [_bdot(p, p) for p in ps]
        xs = [x + _bdot(x, p) for x, p in zip(xs, ps)]
    resid = []
    for m, x in zip(ms, xs):
        (m1, m2, _), (x1, x2, _) = _split3(m), _split3(x)
        resid.append((eye - x) - (_dot(m1, x1, NN) + _dot(m1, x2, NN) + _dot(m2, x1, NN)))
    return [x + _bdot(x, r) for x, r in zip(xs, resid)]


def _gdn_a_fwd(qkv, gb, bb, hk, hv, tr=1024):
    T = qkv.shape[0]
    tr = _tile(T, tr)
    assert tr % SUPER == 0

    def body(k_ref, v_ref, g_ref, b_ref, u_ref, w_ref, gc_ref, ti_ref):
        ri, ci, same = _super_iota()
        ltri = jnp.where(same & (ri >= ci), 1.0, 0.0)
        eye = jnp.where(ri == ci, 1.0, 0.0)
        rows = [pl.ds(s * SUPER, SUPER) for s in range(tr // SUPER)]
        ks, vs, betas = [k_ref[r, :] for r in rows], [v_ref[r, :] for r in rows], [b_ref[r, :] for r in rows]
        gcs = _sel_dot(ltri, [g_ref[r, :] for r in rows])
        kbs = [k * beta for k, beta in zip(ks, betas)]
        ms = [jnp.where(same & (ri > ci), _bdot(kb, k, NT) * _decay_super(gc, ri, ci, same), 0.0)
              for kb, k, gc in zip(kbs, ks, gcs)]
        tinvs = _unit_lower_inverse(ms, eye)
        xs = [_bdot(tinv, jnp.concatenate([v * beta, kb * jnp.exp(gc)], axis=1))
              for tinv, v, beta, kb, gc in zip(tinvs, vs, betas, kbs, gcs)]
        for r, x, gc, tinv in zip(rows, xs, gcs, tinvs):
            u_ref[r, :] = x[:, :HEAD]
            w_ref[r, :] = x[:, HEAD:]
            gc_ref[r, :] = gc
            ti_ref[0, r, :] = tinv.astype(BF16)

    tile = lambda h, r: (r, h)
    vd = hv * HEAD
    return _call(body, (hv, T // tr),
                 [(qkv, (tr, HEAD), lambda h, r: (r, hk + h // 2)), (qkv, (tr, HEAD), lambda h, r: (r, 2 * hk + h)),
                  (gb, (tr, HEAD), tile), (bb, (tr, HEAD), tile)],
                 [((T, vd), F32, (tr, HEAD), tile)] * 3 + [((hv, T, SUPER), BF16, (1, tr, SUPER), lambda h, r: (h, r, 0))],
                 name="gdn_a_fwd", sem=("parallel", "parallel"))


def _gdn_b_fwd(qkv, u, w, gc, hk, hv, tr=512):
    T = qkv.shape[0]
    tr = _tile(T, tr)
    cpb = tr // CHUNK

    def body(q_ref, k_ref, u_ref, w_ref, gc_ref, o_ref, vn_ref, sall_ref, s_ref):
        ri, ci = _chunk_iota()

        @pl.when(pl.program_id(1) == 0)
        def _():
            s_ref[...] = jnp.zeros_like(s_ref)

        rows = [pl.ds(c * CHUNK, CHUNK) for c in range(cpb)]
        qs, ks, us, ws, gcs = ([ref[r, :] for r in rows] for ref in (q_ref, k_ref, u_ref, w_ref, gc_ref))
        gls = [gc[CHUNK - 1:CHUNK, :] for gc in gcs]
        kws = [_bdot(k * jnp.exp(gl - gc), jnp.concatenate([w_, u_], axis=1), TN)
               for k, gl, gc, w_, u_ in zip(ks, gls, gcs, ws, us)]
        qks = [jnp.where(ri >= ci, _bdot(q, k, NT) * _decay(gc), 0.0) for q, k, gc in zip(qs, ks, gcs)]
        s = s_ref[...]
        states = []
        for kw, gl in zip(kws, gls):
            states.append(s)
            s = s * jnp.exp(gl) - _bdot(kw[:, :HEAD], s) + kw[:, HEAD:]
        s_ref[...] = s
        vns = [u_ - _bdot(w_, st) for u_, w_, st in zip(us, ws, states)]
        outs = [_bdot(q * jnp.exp(gc), st) + _bdot(qk, vn) for q, gc, st, qk, vn in zip(qs, gcs, states, qks, vns)]
        for c, (r, o, vn, st) in enumerate(zip(rows, outs, vns, states)):
            o_ref[r, :] = o
            vn_ref[r, :] = vn
            sall_ref[0, c] = st

    tile = lambda h, r: (r, h)
    vd = hv * HEAD
    return _call(body, (hv, T // tr),
                 [(qkv, (tr, HEAD), lambda h, r: (r, h // 2)), (qkv, (tr, HEAD), lambda h, r: (r, hk + h // 2)),
                  (u, (tr, HEAD), tile), (w, (tr, HEAD), tile), (gc, (tr, HEAD), tile)],
                 [((T, vd), F32, (tr, HEAD), tile)] * 2 +
                 [((hv, T // CHUNK, HEAD, HEAD), F32, (1, cpb, HEAD, HEAD), lambda h, r: (h, r, 0, 0))],
                 scratch=[pltpu.VMEM((HEAD, HEAD), F32)], name="gdn_b_fwd", sem=("parallel", "arbitrary"))


def _gdn_b_bwd(do, qkv, w, gc, vn, sall, hk, hv, tr=512):
    T = qkv.shape[0]
    tr = _tile(T, tr)
    cpb = tr // CHUNK
    n_r = T // tr

    def body(do_ref, q_ref, k_ref, w_ref, gc_ref, vn_ref, sall_ref, dq_ref, dk_ref, dgc_ref, du_ref, dw_ref, ds_ref):
        ri, ci = _chunk_iota()
        row = lax.broadcasted_iota(jnp.int32, (CHUNK, HEAD), 0)

        @pl.when(pl.program_id(1) == 0)
        def _():
            ds_ref[...] = jnp.zeros_like(ds_ref)

        rows = [pl.ds(c * CHUNK, CHUNK) for c in range(cpb)]
        d_os, qs, ks, ws, gcs, vns = ([ref[r, :] for r in rows] for ref in (do_ref, q_ref, k_ref, w_ref, gc_ref, vn_ref))
        ss = [sall_ref[0, c] for c in range(cpb)]
        gls = [gc[CHUNK - 1:CHUNK, :] for gc in gcs]
        egcs = [jnp.exp(gc) for gc in gcs]
        ekds = [jnp.exp(gl - gc) for gl, gc in zip(gls, gcs)]
        egs = [jnp.exp(gl) for gl in gls]
        qgs = [q * e for q, e in zip(qs, egcs)]
        kds = [k * e for k, e in zip(ks, ekds)]
        decs = [_decay(gc) for gc in gcs]
        qks = [jnp.where(ri >= ci, _bdot(q, k, NT) * dec, 0.0) for q, k, dec in zip(qs, ks, decs)]
        wkds = [_bdot(w_, kd, TN) for w_, kd in zip(ws, kds)]
        qk_dos = [_bdot(qk, d_o, TN) for qk, d_o in zip(qks, d_os)]
        consts = [_bdot(qg, d_o, TN) - _bdot(w_, qd, TN) for qg, d_o, w_, qd in zip(qgs, d_os, ws, qk_dos)]
        ds = ds_ref[...]
        ds_nexts = [None] * cpb
        for c in reversed(range(cpb)):
            ds_nexts[c] = ds
            ds = ds * egs[c] - _bdot(wkds[c], ds) + consts[c]
        ds_ref[...] = ds
        d_vns = [qd + _bdot(kd, dsn) for qd, kd, dsn in zip(qk_dos, kds, ds_nexts)]
        d_kds = [_bdot(vn, dsn, NT) for vn, dsn in zip(vns, ds_nexts)]
        d_qgs = [_bdot(d_o, s, NT) for d_o, s in zip(d_os, ss)]
        d_qks = [jnp.where(ri >= ci, _bdot(d_o, vn, NT), 0.0) for d_o, vn in zip(d_os, vns)]
        e_qs = [d_qk * qk for d_qk, qk in zip(d_qks, qks)]
        cols = _colsum(e_qs)
        d_bs = [d_qk * dec for d_qk, dec in zip(d_qks, decs)]
        dqs = [d_qg * egc + _bdot(d_b, k) for d_qg, egc, d_b, k in zip(d_qgs, egcs, d_bs, ks)]
        dks = [d_kd * ekd + _bdot(d_b, q, TN) for d_kd, ekd, d_b, q in zip(d_kds, ekds, d_bs, qs)]
        dws = [-_bdot(d_vn, s, NT) for d_vn, s in zip(d_vns, ss)]
        for c, r in enumerate(rows):
            d_gl = jnp.sum(d_kds[c] * kds[c]) + jnp.sum(ss[c] * ds_nexts[c]) * egs[c]
            dq_ref[r, :] = dqs[c]
            dk_ref[r, :] = dks[c]
            dgc_ref[r, :] = (_rowsum(d_qgs[c] * qgs[c]) - _rowsum(d_kds[c] * kds[c]) + _rowsum(e_qs[c]) - cols[c]
                             + jnp.where(row == CHUNK - 1, d_gl, 0.0))
            du_ref[r, :] = d_vns[c]
            dw_ref[r, :] = dws[c]

    rtile = lambda h, r: (n_r - 1 - r, h)
    vd = hv * HEAD
    return _call(body, (hv, n_r),
                 [(do, (tr, HEAD), rtile), (qkv, (tr, HEAD), lambda h, r: (n_r - 1 - r, h // 2)),
                  (qkv, (tr, HEAD), lambda h, r: (n_r - 1 - r, hk + h // 2)),
                  (w, (tr, HEAD), rtile), (gc, (tr, HEAD), rtile), (vn, (tr, HEAD), rtile),
                  (sall, (1, cpb, HEAD, HEAD), lambda h, r: (h, n_r - 1 - r, 0, 0))],
                 [((T, vd), F32, (tr, HEAD), rtile)] * 5,
                 scratch=[pltpu.VMEM((HEAD, HEAD), F32)], name="gdn_b_bwd", sem=("parallel", "arbitrary"))


def _gdn_a_bwd(du, dw, dgc_b, qkv, bb, gc, tinv, u, w, hk, hv, tr=1024):
    T = qkv.shape[0]
    tr = _tile(T, tr)
    assert tr % SUPER == 0

    def body(du_ref, dw_ref, dgcb_ref, k_ref, v_ref, b_ref, gc_ref, ti_ref, u_ref, w_ref, dk_ref, dv_ref, db_ref, dg_ref):
        ri, ci, same = _super_iota()
        utri = jnp.where(same & (ci >= ri), 1.0, 0.0)
        strict = same & (ri > ci)
        rows = [pl.ds(s * SUPER, SUPER) for s in range(tr // SUPER)]
        ks, vs, betas, gcs = ([ref[r, :] for r in rows] for ref in (k_ref, v_ref, b_ref, gc_ref))
        egcs = [jnp.exp(gc) for gc in gcs]
        kbs = [k * beta for k, beta in zip(ks, betas)]
        decs = [_decay_super(gc, ri, ci, same) for gc in gcs]
        ms = [jnp.where(strict, _bdot(kb, k, NT) * dec, 0.0) for kb, k, dec in zip(kbs, ks, decs)]
        d_rs = [_bdot(ti_ref[0, r, :], jnp.concatenate([du_ref[r, :], dw_ref[r, :]], axis=1), TN) for r in rows]
        d_ms = [jnp.where(strict, -_bdot(d_r, jnp.concatenate([u_ref[r, :], w_ref[r, :]], axis=1), NT), 0.0)
                for d_r, r in zip(d_rs, rows)]
        d_as = [d_m * dec for d_m, dec in zip(d_ms, decs)]
        e_ms = [d_m * m for d_m, m in zip(d_ms, ms)]
        d_kbs = [_bdot(d_a, k) + d_r[:, HEAD:] * egc for d_a, k, d_r, egc in zip(d_as, ks, d_rs, egcs)]
        dks = [_bdot(d_a, kb, TN) + d_kb * beta for d_a, kb, d_kb, beta in zip(d_as, kbs, d_kbs, betas)]
        cols = _colsum(e_ms)
        d_gcs = [_rowsum(e_m) - col + _rowsum(d_r[:, HEAD:] * kb * egc) + dgcb_ref[r, :]
                 for e_m, col, d_r, kb, egc, r in zip(e_ms, cols, d_rs, kbs, egcs, rows)]
        dgs = _sel_dot(utri, d_gcs)
        for r, dk, d_r, beta, v, d_kb, k, dg in zip(rows, dks, d_rs, betas, vs, d_kbs, ks, dgs):
            dk_ref[r, :] = dk
            dv_ref[r, :] = d_r[:, :HEAD] * beta
            db_ref[r, :] = _rowsum(d_r[:, :HEAD] * v) + _rowsum(d_kb * k)
            dg_ref[r, :] = dg

    tile = lambda h, r: (r, h)
    vd = hv * HEAD
    return _call(body, (hv, T // tr),
                 [(du, (tr, HEAD), tile), (dw, (tr, HEAD), tile), (dgc_b, (tr, HEAD), tile),
                  (qkv, (tr, HEAD), lambda h, r: (r, hk + h // 2)), (qkv, (tr, HEAD), lambda h, r: (r, 2 * hk + h)),
                  (bb, (tr, HEAD), tile), (gc, (tr, HEAD), tile), (tinv, (1, tr, SUPER), lambda h, r: (h, r, 0)),
                  (u, (tr, HEAD), tile), (w, (tr, HEAD), tile)],
                 [((T, vd), F32, (tr, HEAD), tile)] * 4, name="gdn_a_bwd", sem=("parallel", "parallel"))


def _pair_sum(a, b_, tr=2048):
    T, vd = a.shape
    tr = _tile(T, tr)
    terms = [a] if b_ is None else [a, b_]
    n = len(terms)

    def body(*refs):
        acc = refs[0][...] + refs[1][...]
        for r in refs[2:2 * n]:
            acc = acc + r[...]
        refs[-1][...] = acc

    even = lambda j, r: (r, 2 * j)
    odd = lambda j, r: (r, 2 * j + 1)
    ins = [(t, (tr, HEAD), m) for t in terms for m in (even, odd)]
    return _call(body, (vd // HEAD // 2, T // tr), ins,
                 [((T, vd // 2), F32, (tr, HEAD), lambda j, r: (r, j))], name="gdn_pair_sum", sem=("parallel", "parallel"))[0]


def _s5_param_math(lr, li, ls, br, bi):
    step = jnp.exp(ls)
    zr, zi = lr * step, li * step
    mag = jnp.exp(zr)
    ar, ai = mag * jnp.cos(zi), mag * jnp.sin(zi)
    den = lr * lr + li * li
    nr, ni = ar - 1.0, ai
    cr, cim = (nr * lr + ni * li) / den, (ni * lr - nr * li) / den
    return ar, ai, br * cr - bi * cim, br * cim + bi * cr


def _s5_params_fwd(lr, li, ls, br, bi):
    G, P = lr.shape

    def body(lr_ref, li_ref, ls_ref, br_ref, bi_ref, ar_ref, ai_ref, bbr_ref, bbi_ref):
        ar, ai, bbr, bbi = _s5_param_math(lr_ref[...], li_ref[...], ls_ref[...], br_ref[...], bi_ref[...])
        ar_ref[...], ai_ref[...], bbr_ref[...], bbi_ref[...] = ar, ai, bbr, bbi

    shapes = [(G, P), (G, P), (G, 1), (S5_CH, G, P), (S5_CH, G, P)]
    z = lambda n: (lambda: (0,) * n)
    return _call(body, (), [(a, s, z(len(s))) for a, s in zip((lr, li, ls, br, bi), shapes)],
                 [(s, F32, s, z(len(s))) for s in (shapes[0], shapes[0], shapes[3], shapes[3])],
                 name="s5_params_fwd", sem=())


def _s5_params_bwd(lr, li, ls, br, bi, dar, dai, dbbr, dbbi):
    G, P = lr.shape

    def body(lr_ref, li_ref, ls_ref, br_ref, bi_ref, dar_ref, dai_ref, dbr_ref, dbi_ref, o0, o1, o2, o3, o4):
        _, vjp = jax.vjp(_s5_param_math, lr_ref[...], li_ref[...], ls_ref[...], br_ref[...], bi_ref[...])
        outs = vjp((dar_ref[...], dai_ref[...], dbr_ref[...], dbi_ref[...]))
        for r, v in zip((o0, o1, o2, o3, o4), outs):
            r[...] = v

    shapes = [(G, P), (G, P), (G, 1), (S5_CH, G, P), (S5_CH, G, P)]
    z = lambda n: (lambda: (0,) * n)
    ins = list(zip((lr, li, ls, br, bi), shapes)) + list(zip((dar, dai, dbbr, dbbi), (shapes[0], shapes[0], shapes[3], shapes[3])))
    return _call(body, (), [(a, s, z(len(s))) for a, s in ins], [(s, F32, s, z(len(s))) for s in shapes],
                 name="s5_params_bwd", sem=())


def _s5_bproj_fwd(u, bd_re, bd_im, tr=2048):
    T, D = u.shape
    tr = _tile(T, tr)
    nb = D // LANES

    def body(u_ref, br_ref, bi_ref, or_ref, oi_ref):
        ub = u_ref[...]
        or_ref[...] = _bdot(ub, br_ref[0])
        oi_ref[...] = _bdot(ub, bi_ref[0])

    blk = lambda i, j: (j, 0, 0)
    return _call(body, (T // tr, nb),
                 [(u, (tr, LANES), lambda i, j: (i, j)), (bd_re, (1, LANES, S5_SPB), blk), (bd_im, (1, LANES, S5_SPB), blk)],
                 [((T, nb * S5_SPB), F32, (tr, S5_SPB), lambda i, j: (i, j))] * 2, name="s5_bproj_fwd", sem=("parallel", "parallel"))


def _s5_scan(br, bi, lam, reverse, xr=None, xi=None, tl=1024, bw=512):
    T, NCH = br.shape
    tl, bw = _tile(T, tl), _tile(NCH, bw)
    n_t, n_g = T // tl, tl // 8

    def body(*refs):
        if reverse:
            br_ref, bi_ref, lam_ref, sr_ref, si_ref, hr_ref, hi_ref, or_ref, oi_ref, dl_ref, cr, ci_ = refs
        else:
            br_ref, bi_ref, lam_ref, or_ref, oi_ref, cr, ci_ = refs
        t = pl.program_id(1)
        a_r = lam_ref[0:1, :]
        a_i = -lam_ref[1:2, :] if reverse else lam_ref[1:2, :]
        powers = [(a_r, a_i)]
        for _ in range(2):
            p_r, p_i = powers[-1]
            powers.append((p_r * p_r - p_i * p_i, 2.0 * p_r * p_i))
        row = lax.broadcasted_iota(jnp.int32, (8, bw), 0)

        def scan8(x_r, x_i):
            for level, (p_r, p_i) in enumerate(powers):
                s = 1 << level
                keep = (row < 8 - s) if reverse else (row >= s)
                s_r = jnp.where(keep, pltpu.roll(x_r, 8 - s if reverse else s, 0), 0.0)
                s_i = jnp.where(keep, pltpu.roll(x_i, 8 - s if reverse else s, 0), 0.0)
                x_r, x_i = x_r + p_r * s_r - p_i * s_i, x_i + p_r * s_i + p_i * s_r
            return x_r, x_i

        edge = 7 if reverse else 0
        tab_r, tab_i = scan8(jnp.where(row == edge, a_r, 0.0), jnp.where(row == edge, a_i, 0.0))

        @pl.when(t == 0)
        def _():
            cr[...] = jnp.zeros_like(cr)
            ci_[...] = jnp.zeros_like(ci_)
            if reverse:
                dl_ref[...] = jnp.zeros_like(dl_ref)

        if reverse:
            first_block = t == n_t - 1
            halo_r = jnp.where(first_block, 0.0, hr_ref[7:8, :])
            halo_i = jnp.where(first_block, 0.0, hi_ref[7:8, :])

        def group(n, carry):
            g = n_g - 1 - n if reverse else n
            rows = pl.ds(pl.multiple_of(g * 8, 8), 8)
            c_r, c_i = carry[0], carry[1]
            x_r, x_i = scan8(br_ref[rows, :], bi_ref[rows, :])
            x_r, x_i = x_r + tab_r * c_r - tab_i * c_i, x_i + tab_r * c_i + tab_i * c_r
            or_ref[rows, :], oi_ref[rows, :] = x_r, x_i
            out = 7 - edge
            nxt = (x_r[out:out + 1, :], x_i[out:out + 1, :])
            if not reverse:
                return nxt
            before = pl.ds(pl.multiple_of(jnp.maximum(g * 8 - 8, 0), 8), 8)
            h_r = jnp.where(g > 0, sr_ref[before, :][7:8, :], halo_r)
            h_i = jnp.where(g > 0, si_ref[before, :][7:8, :], halo_i)
            s_r = jnp.where(row == 0, h_r, pltpu.roll(sr_ref[rows, :], 1, 0))
            s_i = jnp.where(row == 0, h_i, pltpu.roll(si_ref[rows, :], 1, 0))
            return nxt + (carry[2] + s_r * x_r + s_i * x_i, carry[3] + s_r * x_i - s_i * x_r)

        init = (cr[0:1, :], ci_[0:1, :])
        if reverse:
            init = init + (jnp.zeros((8, bw), F32), jnp.zeros((8, bw), F32))
        fin = lax.fori_loop(0, n_g, group, init, unroll=4 if n_g % 4 == 0 else 1)
        cr[0:1, :], ci_[0:1, :] = fin[0], fin[1]
        if reverse:
            dl_ref[0:1, :] += jnp.sum(fin[2], axis=0, keepdims=True)
            dl_ref[1:2, :] += jnp.sum(fin[3], axis=0, keepdims=True)

    tmap = (lambda c, t: (n_t - 1 - t, c)) if reverse else (lambda c, t: (t, c))
    col = lambda c, t: (0, c)
    ins = [(br, (tl, bw), tmap), (bi, (tl, bw), tmap), (lam, (2, bw), col)]
    outs = [((T, NCH), F32, (tl, bw), tmap)] * 2
    if reverse:
        halo = lambda c, t: (jnp.maximum((n_t - 1 - t) * n_g - 1, 0), c)
        ins += [(xr, (tl, bw), tmap), (xi, (tl, bw), tmap), (xr, (8, bw), halo), (xi, (8, bw), halo)]
        outs += [((2, NCH), F32, (2, bw), col)]
    return _call(body, (NCH // bw, n_t), ins, outs, scratch=[pltpu.VMEM((8, bw), F32), pltpu.VMEM((8, bw), F32)],
                 name="s5_scan_bwd" if reverse else "s5_scan_fwd", sem=("parallel", "arbitrary"))


def _s5_cproj_fwd(xr, xi, cd_re, cd_im, u, d, tr=2048):
    T, D = u.shape
    tr = _tile(T, tr)

    def body(xr_ref, xi_ref, cr_ref, ci_ref, u_ref, d_ref, y_ref, h_ref):
        y = _bdot(xr_ref[...], cr_ref[0]) + _bdot(xi_ref[...], ci_ref[0]) + d_ref[...] * u_ref[...]
        y_ref[...] = y
        h_ref[...] = jax.nn.gelu(y).astype(BF16)

    tile = lambda i, j: (i, j)
    blk = lambda i, j: (j, 0, 0)
    return _call(body, (T // tr, D // LANES),
                 [(xr, (tr, S5_SPB), tile), (xi, (tr, S5_SPB), tile), (cd_re, (1, S5_SPB, LANES), blk), (cd_im, (1, S5_SPB, LANES), blk),
                  (u, (tr, LANES), tile), (d.reshape(1, D), (1, LANES), lambda i, j: (0, j))],
                 [((T, D), F32, (tr, LANES), tile), ((T, D), BF16, (tr, LANES), tile)], name="s5_cproj_fwd", sem=("parallel", "parallel"))


def _s5_cproj_bwd(dy, xr, xi, cd_re, cd_im, u, d, tr=1024):
    T, D = u.shape
    tr = _tile(T, tr)
    nb = D // LANES

    def body(dy_ref, xr_ref, xi_ref, cr_ref, ci_ref, u_ref, d_ref, dxr_ref, dxi_ref, du_ref, dd_ref, dcr_ref, dci_ref):
        g = dy_ref[...]
        dxr_ref[...] = _bdot(g, cr_ref[0], NT)
        dxi_ref[...] = _bdot(g, ci_ref[0], NT)
        du_ref[...] = g * d_ref[...]
        first = pl.program_id(1) == 0
        _accumulate(dd_ref, jnp.sum(g * u_ref[...], axis=0, keepdims=True), first)

        @pl.when(first)
        def _():
            dcr_ref[...] = jnp.zeros_like(dcr_ref)
            dci_ref[...] = jnp.zeros_like(dci_ref)

        dcr_ref[0] += _bdot(xr_ref[...], g, TN)
        dci_ref[0] += _bdot(xi_ref[...], g, TN)

    tile = lambda j, i: (i, j)
    blk = lambda j, i: (j, 0, 0)
    col = lambda j, i: (0, j)
    return _call(body, (nb, T // tr),
                 [(dy, (tr, LANES), tile), (xr, (tr, S5_SPB), tile), (xi, (tr, S5_SPB), tile),
                  (cd_re, (1, S5_SPB, LANES), blk), (cd_im, (1, S5_SPB, LANES), blk), (u, (tr, LANES), tile), (d.reshape(1, D), (1, LANES), col)],
                 [((T, nb * S5_SPB), F32, (tr, S5_SPB), tile)] * 2 + [((T, D), F32, (tr, LANES), tile), ((1, D), F32, (1, LANES), col)]
                 + [((nb, S5_SPB, LANES), F32, (1, S5_SPB, LANES), blk)] * 2, name="s5_cproj_bwd")


def _s5_bproj_bwd(dbr, dbi, bd_re, bd_im, u, du_skip, tr=1024):
    T, D = u.shape
    tr = _tile(T, tr)
    nb = D // LANES

    def body(gr_ref, gi_ref, br_ref, bi_ref, u_ref, ds_ref, du_ref, dbr_ref, dbi_ref):
        g_r, g_i, ub = gr_ref[...], gi_ref[...], u_ref[...]
        du_ref[...] = (ds_ref[...] + _bdot(g_r, br_ref[0], NT) + _bdot(g_i, bi_ref[0], NT)).astype(BF16)

        @pl.when(pl.program_id(1) == 0)
        def _():
            dbr_ref[...] = jnp.zeros_like(dbr_ref)
            dbi_ref[...] = jnp.zeros_like(dbi_ref)

        dbr_ref[0] += _bdot(ub, g_r, TN)
        dbi_ref[0] += _bdot(ub, g_i, TN)

    tile = lambda j, i: (i, j)
    blk = lambda j, i: (j, 0, 0)
    return _call(body, (nb, T // tr),
                 [(dbr, (tr, S5_SPB), tile), (dbi, (tr, S5_SPB), tile), (bd_re, (1, LANES, S5_SPB), blk), (bd_im, (1, LANES, S5_SPB), blk),
                  (u, (tr, LANES), tile), (du_skip, (tr, LANES), tile)],
                 [((T, D), BF16, (tr, LANES), tile)] + [((nb, LANES, S5_SPB), F32, (1, LANES, S5_SPB), blk)] * 2, name="s5_bproj_bwd")


def _s5_gate_fwd(h, vg, tr=256):
    T, D = h.shape
    tr = _tile(T, tr)

    def body(h_ref, a_ref, b_ref, o_ref):
        o_ref[...] = h_ref[...] + a_ref[...] * jax.nn.sigmoid(b_ref[...])

    row = lambda i: (i, 0)
    return _call(body, (T // tr,), [(h, (tr, D), row), (vg, (tr, D), row), (vg, (tr, D), lambda i: (i, 1))],
                 [((T, D), F32, (tr, D), row)], name="s5_gate_fwd", sem=("parallel",))[0]


def _s5_gate_bwd(dh, vg, tr=256):
    T, D = dh.shape
    tr = _tile(T, tr)

    def body(d_ref, a_ref, b_ref, o_ref):
        d = d_ref[...]
        sig = jax.nn.sigmoid(b_ref[...])
        o_ref[:, :D] = (d * sig).astype(BF16)
        o_ref[:, D:] = (d * a_ref[...] * sig * (1.0 - sig)).astype(BF16)

    row = lambda i: (i, 0)
    return _call(body, (T // tr,), [(dh, (tr, D), row), (vg, (tr, D), row), (vg, (tr, D), lambda i: (i, 1))],
                 [((T, 2 * D), BF16, (tr, 2 * D), row)], name="s5_gate_bwd", sem=("parallel",))[0]


def _block_diag(w, transpose):
    g, a, b = w.shape
    if transpose:
        w = w.transpose(0, 2, 1)
        a, b = b, a
    eye = jnp.eye(S5_GPB, dtype=w.dtype)
    return jnp.einsum("jgab,gh->jgahb", w.reshape(g // S5_GPB, S5_GPB, a, b), eye).reshape(g // S5_GPB, S5_GPB * a, S5_GPB * b)


def _block_diag_extract(wd, a, b, transpose):
    if transpose:
        a, b = b, a
    nb = wd.shape[0]
    eye = jnp.eye(S5_GPB, dtype=wd.dtype)
    w = jnp.einsum("jgahb,gh->jgab", wd.reshape(nb, S5_GPB, a, S5_GPB, b), eye).reshape(nb * S5_GPB, a, b)
    return w.transpose(0, 2, 1) if transpose else w


def _mesh_position():
    return lax.axis_index("x"), lax.axis_index("y"), lax.axis_index("c")


def _my_index():
    x, y, c = _mesh_position()
    return 4 * x + 2 * y + c


def _hbm_call(body, arrays, out_shapes, n_sems, name):
    n = len(arrays)
    return pl.pallas_call(
        body, out_shape=[jax.ShapeDtypeStruct(s, d) for s, d in out_shapes],
        in_specs=[pl.BlockSpec(memory_space=pl.ANY)] * n, out_specs=[pl.BlockSpec(memory_space=pl.ANY)] * len(out_shapes),
        scratch_shapes=[pltpu.SemaphoreType.DMA((n_sems,)), pltpu.SemaphoreType.DMA((n_sems,)), pltpu.SemaphoreType.DMA((n,))],
        name=name)(*arrays)


def _all_gather(blocks, name):
    n = len(blocks)
    per = N_DEV - 1

    def body(*refs):
        x_refs, out_refs = refs[:n], refs[n:2 * n]
        send_sems, recv_sems, local_sems = refs[2 * n:]
        x, y, c = _mesh_position()
        me, sibling = (x, y, c), (x, y, 1 - c)
        chips = [(1 - x, y), (x, 1 - y), (1 - x, 1 - y)]

        def copy(a, k, blk, to, src=None):
            slot = out_refs[a].at[4 * blk[0] + 2 * blk[1] + blk[2]]
            return pltpu.make_async_remote_copy(
                src_ref=slot if src is None else src, dst_ref=slot, send_sem=send_sems.at[a * per + k],
                recv_sem=recv_sems.at[a * per + k], device_id=to, device_id_type=pl.DeviceIdType.MESH)

        mine = [pltpu.make_async_copy(x_refs[a], out_refs[a].at[4 * x + 2 * y + c], local_sems.at[a]) for a in range(n)]
        for cp in mine:
            cp.start()
        first = []
        for a in range(n):
            first.append(copy(a, 0, me, sibling, src=x_refs[a]))
            first += [copy(a, 1 + j, me, (*chip, c), src=x_refs[a]) for j, chip in enumerate(chips)]
        for cp in first:
            cp.start()
        passed = []
        for j, chip in enumerate(chips):
            for a in range(n):
                copy(a, 1 + j, (*chip, c), me).wait_recv()
                passed.append(copy(a, 4 + j, (*chip, c), sibling))
                passed[-1].start()
        for a in range(n):
            copy(a, 0, sibling, me).wait_recv()
            for j, chip in enumerate(chips):
                copy(a, 4 + j, (*chip, 1 - c), me).wait_recv()
        for cp in first + passed:
            cp.wait_send()
        for cp in mine:
            cp.wait()

    return _hbm_call(body, blocks, [((N_DEV,) + b.shape, b.dtype) for b in blocks], n * per, name)


def _pair_exchange(parts, name):
    n = len(parts)

    def body(*refs):
        g_refs, got_refs = refs[:n], refs[n:2 * n]
        send_sems, recv_sems = refs[2 * n:]
        x, y, c = _mesh_position()
        places = [(x, y), (1 - x, y), (x, 1 - y), (1 - x, 1 - y)]
        copies = [pltpu.make_async_remote_copy(
            src_ref=g_refs[a].at[4 * px + 2 * py + 1 - c], dst_ref=got_refs[a].at[k],
            send_sem=send_sems.at[4 * a + k], recv_sem=recv_sems.at[4 * a + k],
            device_id=(x, y, 1 - c), device_id_type=pl.DeviceIdType.MESH) for a in range(n) for k, (px, py) in enumerate(places)]
        for cp in copies:
            cp.start()
        for cp in copies:
            cp.wait()

    return pl.pallas_call(
        body, out_shape=[jax.ShapeDtypeStruct((4,) + p_.shape[1:], p_.dtype) for p_ in parts],
        in_specs=[pl.BlockSpec(memory_space=pl.ANY)] * n, out_specs=[pl.BlockSpec(memory_space=pl.ANY)] * n,
        scratch_shapes=[pltpu.SemaphoreType.DMA((4 * n,))] * 2, name=name)(*parts)


def _chip_sums(own, got, tr=PACK_ROWS):
    _, R, cw = own.shape
    cap = min(tr, 1 << (((PACK_ROWS * PACK_W) // cw).bit_length() - 1))
    tr = math.gcd(R, cap)

    def body(a_ref, b_ref, o_ref):
        o_ref[...] = (a_ref[...].astype(F32) + b_ref[...].astype(F32)).astype(o_ref.dtype)

    blk = lambda j, i: (1 + j, i, 0)
    return _call(body, (3, R // tr), [(own, (1, tr, cw), blk), (got, (1, tr, cw), blk)],
                 [((3, R, cw), own.dtype, (1, tr, cw), lambda j, i: (j, i, 0))], name="chip_sums", sem=("parallel", "parallel"))[0]


def _chip_exchange(sums, name):
    n = len(sums)

    def body(*refs):
        s_refs, out_refs = refs[:n], refs[n:2 * n]
        send_sems, recv_sems = refs[2 * n:]
        x, y, c = _mesh_position()
        chips = [(1 - x, y), (x, 1 - y), (1 - x, 1 - y)]
        copies = [pltpu.make_async_remote_copy(
            src_ref=s_refs[a].at[j], dst_ref=out_refs[a].at[j], send_sem=send_sems.at[3 * a + j], recv_sem=recv_sems.at[3 * a + j],
            device_id=(*chip, c), device_id_type=pl.DeviceIdType.MESH) for a in range(n) for j, chip in enumerate(chips)]
        for cp in copies:
            cp.start()
        for cp in copies:
            cp.wait()

    return pl.pallas_call(
        body, out_shape=[jax.ShapeDtypeStruct(s.shape, s.dtype) for s in sums],
        in_specs=[pl.BlockSpec(memory_space=pl.ANY)] * n, out_specs=[pl.BlockSpec(memory_space=pl.ANY)] * n,
        scratch_shapes=[pltpu.SemaphoreType.DMA((3 * n,))] * 2, name=name)(*sums)


def _adamw_math(g, w, m, v):
    nm = ADAM_B1 * m + (1.0 - ADAM_B1) * g
    nv = ADAM_B2 * v + (1.0 - ADAM_B2) * (g * g)
    c1 = 1.0 - ADAM_B1 ** ADAM_STEP
    c2 = 1.0 - ADAM_B2 ** ADAM_STEP
    return -ADAM_LR * ((nm / c1) / (jnp.sqrt(nv / c2) + ADAM_EPS) + ADAM_WD * w), nm, nv


def _adamw(parts, row0, w, m, v, name, window_n8=None):
    R, C = w.shape
    cw = parts[0][0].shape[2]
    cap = min(PACK_ROWS, 1 << (((PACK_ROWS * PACK_W) // cw).bit_length() - 1))
    tr = math.gcd(math.gcd(R, cap), row0 or R)
    assert R % tr == 0 and row0 % tr == 0
    n = len(parts)

    def body(*refs):
        p_refs = refs[:n]
        w_ref, m_ref, v_ref, g_ref, d_ref, nm_ref, nv_ref = refs[n:]
        g = None
        for p_ref, (_, slots) in zip(p_refs, parts):
            for s in range(len(slots)):
                term = p_ref[s].astype(F32)
                g = term if g is None else g + term
        if window_n8 is not None:
            off = (window_n8 * _my_index()) % LANES
            g = pltpu.roll(g, (cw - off) % cw, 1)[:, :C]
        d, nm, nv = _adamw_math(g, w_ref[...], m_ref[...], v_ref[...])
        g_ref[...], d_ref[...], nm_ref[...], nv_ref[...] = g, d, nm, nv

    row = lambda i: (i, 0)
    r0 = row0 // tr
    ins = []
    for arr, slots in parts:
        assert list(slots) == list(range(slots[0], slots[0] + len(slots))) and slots[0] % len(slots) == 0
        s0 = slots[0] // len(slots)
        ins.append((arr, (len(slots), tr, cw), lambda i, s0=s0: (s0, r0 + i, 0)))
    return _call(body, (R // tr,), ins + [(w, (tr, C), row), (m, (tr, C), row), (v, (tr, C), row)],
                 [((R, C), F32, (tr, C), row)] * 4, name=name, sem=("parallel",))


def _window_geometry(n8):
    offs = [(d * n8) % LANES for d in range(N_DEV)]
    starts = [(d * n8) // LANES for d in range(N_DEV)]
    blocks = max(-(-(o + n8) // LANES) for o in offs)
    return starts, blocks, max(starts) + blocks


def _to_window(wpad, n8, tr=256):
    R, cw = wpad.shape
    tr = _tile(R, tr)

    def body(x_ref, o_ref):
        o_ref[...] = pltpu.roll(x_ref[...], (n8 * _my_index()) % LANES, 1).astype(BF16)

    row = lambda i: (i, 0)
    return _call(body, (R // tr,), [(wpad, (tr, cw), row)], [((R, cw), BF16, (tr, cw), row)], name="to_window", sem=("parallel",))[0]


def _from_windows(win, row0, rows, n8, sections, tr=128):
    starts, blocks, total = _window_geometry(n8)
    cw = win.shape[2]
    tr = _tile(rows, tr)
    r0 = row0 // tr
    assert sum(sections) == total * LANES and all(s % LANES == 0 for s in sections)
    n_sec = len(sections)

    def body(*refs):
        w_ref, o_refs, acc = refs[0], refs[1:1 + n_sec], refs[-1]
        acc[...] = jnp.zeros_like(acc)
        for d in range(N_DEV):
            cols = pl.ds(starts[d] * LANES, cw)
            acc[:, cols] += w_ref[d].astype(F32)
        off = 0
        for o_ref, width in zip(o_refs, sections):
            o_ref[...] = acc[:, off:off + width].astype(BF16)
            off += width

    return _call(body, (rows // tr,), [(win, (N_DEV, tr, cw), lambda i: (0, r0 + i, 0))],
                 [((rows, s), BF16, (tr, s), lambda i: (i, 0)) for s in sections],
                 scratch=[pltpu.VMEM((tr, total * LANES), F32)], name="from_windows", sem=("parallel",))


def _pack(flat_pieces, dtype, lead=()):
    cat = jnp.concatenate([p_.astype(dtype) for p_ in flat_pieces], axis=-1)
    n = cat.shape[-1]
    quantum = PACK_ROWS * PACK_W
    total = -(-n // quantum) * quantum
    cat = jnp.pad(cat, [(0, 0)] * len(lead) + [(0, total - n)])
    return cat.reshape(lead + (total // PACK_W, PACK_W)), n


REPLICATED = ("norm_mix", "norm_mlp", "norm_ple", "norm_final", "gdn_a_log", "gdn_dt_bias", "gdn_o_norm",
              "s5_lam_re", "s5_lam_im", "s5_log_step", "s5_b_re", "s5_b_im", "s5_c_re", "s5_c_im")
WEIGHTS = ("norm_mix", "norm_mlp", "norm_ple", "norm_final", "gdn_w_in", "gdn_conv_w", "gdn_a_log", "gdn_dt_bias",
           "gdn_o_norm", "gdn_w_out", "s5_w_in", "s5_lam_re", "s5_lam_im", "s5_log_step", "s5_b_re", "s5_b_im",
           "s5_c_re", "s5_c_im", "s5_d", "s5_w_out", "mlp_w_up", "mlp_w_down", "ple_w_proj", "ple_w_gate")
ROW_GROUP = ("mlp_w_down", "gdn_w_out", "s5_w_in", "ple_w_gate")
COL_SHARDED = ("mlp_w_up", "s5_w_out", "ple_w_proj")


def _rows2d(a):
    return a.reshape(-1, a.shape[-1])


def _misc_pack(conv, s5d):
    cw = conv.shape[-1]
    rows = jnp.concatenate([_rows2d(conv), jnp.pad(s5d, ((0, 0), (0, cw - s5d.shape[-1])))], axis=0)
    return jnp.pad(rows, ((0, -rows.shape[0] % 8), (0, 0)))


def _misc_unpack(a, conv_rows, s5d_shape):
    return a[:conv_rows], a[conv_rows:conv_rows + s5d_shape[0], :s5d_shape[1]]


def kernel(x, p, norm_mix, norm_mlp, norm_ple, norm_final, gdn_w_in, gdn_conv_w, gdn_a_log, gdn_dt_bias, gdn_o_norm, gdn_w_out, s5_w_in, s5_lam_re, s5_lam_im, s5_log_step, s5_b_re, s5_b_im, s5_c_re, s5_c_im, s5_d, s5_w_out, mlp_w_up, mlp_w_down, ple_w_proj, ple_w_gate, loss_target, m_norm_mix, m_norm_mlp, m_norm_ple, m_norm_final, m_gdn_w_in, m_gdn_conv_w, m_gdn_a_log, m_gdn_dt_bias, m_gdn_o_norm, m_gdn_w_out, m_s5_w_in, m_s5_lam_re, m_s5_lam_im, m_s5_log_step, m_s5_b_re, m_s5_b_im, m_s5_c_re, m_s5_c_im, m_s5_d, m_s5_w_out, m_mlp_w_up, m_mlp_w_down, m_ple_w_proj, m_ple_w_gate, v_norm_mix, v_norm_mlp, v_norm_ple, v_norm_final, v_gdn_w_in, v_gdn_conv_w, v_gdn_a_log, v_gdn_dt_bias, v_gdn_o_norm, v_gdn_w_out, v_s5_w_in, v_s5_lam_re, v_s5_lam_im, v_s5_log_step, v_s5_b_re, v_s5_b_im, v_s5_c_re, v_s5_c_im, v_s5_d, v_s5_w_out, v_mlp_w_up, v_mlp_w_down, v_ple_w_proj, v_ple_w_gate):
    args = dict(locals())
    w = {n: args[n] for n in WEIGHTS}
    mom = {n: args["m_" + n] for n in WEIGHTS}
    vel = {n: args["v_" + n] for n in WEIGHTS}
    depth = norm_mix.shape[0]
    T, D = x.shape[1], x.shape[2]
    hv = gdn_a_log.shape[1]
    vd = hv * HEAD
    n_gdn, n_s5 = gdn_w_in.shape[0], s5_w_in.shape[0]
    cw = gdn_conv_w.shape[2]
    cd = cw * N_DEV
    hk = (cd - vd) // (2 * HEAD)
    assert hv == 2 * hk and 2 * hv <= LANES and T % SUPER == 0 and SUPER % CHUNK == 0
    G, P = s5_lam_re.shape[1], s5_lam_re.shape[2]
    assert P == S5_STATE and G * S5_CH == D and G % S5_GPB == 0 and D // N_DEV <= cw
    n8 = gdn_w_in.shape[2]
    win_starts, win_blocks, win_total = _window_geometry(n8)
    cwin = win_blocks * LANES
    assert win_total * LANES == cd + vd + LANES

    row_off, off = {}, 0
    for n in ROW_GROUP:
        row_off[n] = off
        off += w[n].shape[0] * w[n].shape[1]
    gathered = _all_gather(
        [_to_window(jnp.pad(_rows2d(gdn_w_in), ((0, 0), (0, cwin - n8))), n8),
         jnp.concatenate([_rows2d(w[n]) for n in ROW_GROUP], axis=0).astype(BF16)]
        + [_rows2d(w[n]).astype(BF16) for n in COL_SHARDED] + [_misc_pack(gdn_conv_w, s5_d)], "gather_weights")
    g_win, g_row, g_misc = gathered[0], gathered[1], gathered[-1]
    g_col = dict(zip(COL_SHARDED, gathered[2:-1]))
    conv_full = g_misc[:, :n_gdn * 4].reshape(N_DEV, n_gdn, 4, cw).transpose(1, 2, 0, 3).reshape(n_gdn, 4, cd)
    s5d_full = g_misc[:, n_gdn * 4:n_gdn * 4 + n_s5, :D // N_DEV].transpose(1, 0, 2).reshape(n_s5, D)

    def weight(name, l):
        r = w[name].shape[1]
        if name in ROW_GROUP:
            return Sharded(g_row, 0, row_off[name] + l * r, r)
        return Sharded(g_col[name], 1, l * r, r)

    h = x[0]
    tgt = loss_target[0]
    grads = {n: [None] * w[n].shape[0] for n in WEIGHTS if n != "norm_final"}
    saved = []
    add = lambda acc, r: (r + acc,)

    for i in range(depth):
        j = i // 2
        sv = {"h0": h}
        hn = _rms_fwd(h, norm_mix[i])
        sv["hn"] = hn
        if i % 2 == 0:
            w_in = _from_windows(g_win, j * D, D, n8, (cd, vd, LANES))
            pq = _mm(hn, w_in[0], name="gdn_in_qkv")[0]
            pz = _mm(hn, w_in[1], name="gdn_in_z")[0]
            ba = _mm(hn, w_in[2], name="gdn_in_ba")[0]
            qkv = _gdn_pre_fwd(pq, conv_full[j], hk)
            pv = jnp.pad(jnp.stack([gdn_a_log[j], gdn_dt_bias[j]]), ((0, 0), (hv, LANES - 2 * hv)))
            g2 = _gates_fwd(ba, pv, hv)
            bb = jnp.repeat(g2[:, :hv], HEAD, axis=1)
            gb = jnp.repeat(g2[:, hv:2 * hv], HEAD, axis=1)
            u, ww, gc, tinv = _gdn_a_fwd(qkv, gb, bb, hk, hv)
            o, vn, sall = _gdn_b_fwd(qkv, u, ww, gc, hk, hv)
            on = _ogate_fwd(o, pz, gdn_o_norm[j])
            h = _mm(on, weight("gdn_w_out", j), epi=add, extras=(h,), name="gdn_out")[0]
            sv.update(w_in=w_in, pq=pq, pz=pz, ba=ba, pv=pv, qkv=qkv, bb=bb, u=u, ww=ww, gc=gc, tinv=tinv, o=o, vn=vn, sall=sall, on=on)
        else:
            uu = _mm(hn, weight("s5_w_in", j), name="s5_in")[0]
            b_re_t, b_im_t = s5_b_re[j].transpose(2, 0, 1), s5_b_im[j].transpose(2, 0, 1)
            ls = s5_log_step[j].reshape(G, 1)
            ar, ai, bbr, bbi = _s5_params_fwd(s5_lam_re[j], s5_lam_im[j], ls, b_re_t, b_im_t)
            lam = jnp.stack([ar.reshape(-1), ai.reshape(-1)])
            bd_re = _block_diag(bbr.transpose(1, 2, 0), transpose=True).astype(BF16)
            bd_im = _block_diag(bbi.transpose(1, 2, 0), transpose=True).astype(BF16)
            cd_re = _block_diag(s5_c_re[j], transpose=True).astype(BF16)
            cd_im = _block_diag(-s5_c_im[j], transpose=True).astype(BF16)
            bur, bui = _s5_bproj_fwd(uu, bd_re, bd_im)
            xr, xi = _s5_scan(bur, bui, lam, reverse=False)
            dsk = s5d_full[j]
            yy, hact = _s5_cproj_fwd(xr, xi, cd_re, cd_im, uu, dsk)
            vg = _mm(hact, weight("s5_w_out", j), name="s5_out")[0]
            h = _s5_gate_fwd(h, vg)
            sv.update(uu=uu, b_re_t=b_re_t, b_im_t=b_im_t, ls=ls, lam=lam, bd_re=bd_re, bd_im=bd_im, cd_re=cd_re, cd_im=cd_im,
                      xr=xr, xi=xi, dsk=dsk, yy=yy, hact=hact, vg=vg)
        sv["h1"] = h
        hm = _rms_fwd(h, norm_mlp[i])
        up, act = _mm(hm, weight("mlp_w_up", i), out_dtypes=(F32, BF16),
                      epi=lambda acc: (acc, jnp.square(jnp.maximum(acc, 0.0))), name="mlp_up")
        h = _mm(act, weight("mlp_w_down", i), epi=add, extras=(h,), name="mlp_down")[0]
        sv.update(hm=hm, up=up, act=act, h2=h)
        hp = _rms_fwd(h, norm_ple[i])
        s_gate = _mm(hp, weight("ple_w_gate", i), name="ple_gate")[0]
        pp = _mm(p[i, 0], weight("ple_w_proj", i), name="ple_proj")[0]
        h = _ple_fwd(h, s_gate, pp)
        sv.update(hp=hp, s_gate=s_gate, pp=pp)
        saved.append(sv)

    dh, d_norm_final, loss_part = _loss_fwd_bwd(h, norm_final, tgt)
    loss = lax.psum(loss_part[0, 0], MESH_AXES)

    dw = lambda a, b_, axis, name: _mm(a, b_, "tn", out_dtypes=(BF16,), out_axis=axis, name=name)[0]
    for i in reversed(range(depth)):
        j = i // 2
        sv = saved[i]
        ds, dpp = _ple_bwd(dh, sv["s_gate"], sv["pp"])
        grads["ple_w_proj"][i] = dw(p[i, 0], dpp, 1, "ple_proj_dw")
        grads["ple_w_gate"][i] = dw(sv["hp"], ds, 0, "ple_gate_dw")
        d_hp = _mm(ds, weight("ple_w_gate", i), "nt", name="ple_gate_dx")[0]
        dh, dh_b, grads["norm_ple"][i] = _rms_bwd(d_hp, sv["h2"], norm_ple[i], dh)
        grads["mlp_w_down"][i] = dw(sv["act"], dh_b, 0, "mlp_down_dw")
        d_up = _mm(dh_b, weight("mlp_w_down", i), "nt", out_dtypes=(BF16,),
                   epi=lambda acc, up_: (acc * 2.0 * jnp.maximum(up_, 0.0),), extras=(sv["up"],), name="mlp_down_dx")[0]
        grads["mlp_w_up"][i] = dw(sv["hm"], d_up, 1, "mlp_up_dw")
        d_hm = _mm(d_up, weight("mlp_w_up", i), "nt", name="mlp_up_dx")[0]
        dh, dh_b, grads["norm_mlp"][i] = _rms_bwd(d_hm, sv["h1"], norm_mlp[i], dh)
        if i % 2 == 0:
            grads["gdn_w_out"][j] = dw(sv["on"], dh_b, 0, "gdn_out_dw")
            d_on = _mm(dh_b, weight("gdn_w_out", j), "nt", name="gdn_out_dx")[0]
            d_o, d_z, grads["gdn_o_norm"][j] = _ogate_bwd(d_on, sv["o"], sv["pz"], gdn_o_norm[j])
            dq_b, dk_b, dgc_b, d_u, d_w = _gdn_b_bwd(d_o, sv["qkv"], sv["ww"], sv["gc"], sv["vn"], sv["sall"], hk, hv)
            dk_a, d_v, d_bb, d_gb = _gdn_a_bwd(d_u, d_w, dgc_b, sv["qkv"], sv["bb"], sv["gc"], sv["tinv"], sv["u"], sv["ww"], hk, hv)
            d_qkv = jnp.concatenate([_pair_sum(dq_b, None), _pair_sum(dk_a, dk_b), d_v], axis=1)
            d_c, grads["gdn_conv_w"][j] = _gdn_pre_bwd(d_qkv, sv["pq"], conv_full[j], hk)
            d_pq = _gdn_conv_bwd(d_c, conv_full[j])
            d_g2 = jnp.pad(jnp.concatenate([d_bb[:, ::HEAD], d_gb[:, ::HEAD]], axis=1), ((0, 0), (0, LANES - 2 * hv)))
            d_ba, d_pv = _gates_bwd(d_g2, sv["ba"], sv["pv"], hv)
            grads["gdn_a_log"][j] = d_pv[0, hv:2 * hv]
            grads["gdn_dt_bias"][j] = d_pv[1, hv:2 * hv]
            hn, w_in = sv["hn"], sv["w_in"]
            dw_nat = jnp.concatenate([dw(hn, d_pq, None, "gdn_in_qkv_dw"), dw(hn, d_z, None, "gdn_in_z_dw"),
                                      dw(hn, d_ba, None, "gdn_in_ba_dw")], axis=1)
            grads["gdn_w_in"][j] = jnp.stack([dw_nat[:, s * LANES:s * LANES + cwin] for s in win_starts])
            d_hn = _mm(d_pq, w_in[0], "nt", name="gdn_in_qkv_dx")[0]
            d_hn = _mm(d_z, w_in[1], "nt", epi=add, extras=(d_hn,), name="gdn_in_z_dx")[0]
            d_hn = _mm(d_ba, w_in[2], "nt", epi=add, extras=(d_hn,), name="gdn_in_ba_dx")[0]
        else:
            d_vg = _s5_gate_bwd(dh, sv["vg"])
            grads["s5_w_out"][j] = dw(sv["hact"], d_vg, 1, "s5_out_dw")

            def gelu_bwd(acc, y_):
                _, vjp = jax.vjp(jax.nn.gelu, y_)
                return (vjp(acc)[0],)

            d_y = _mm(d_vg, weight("s5_w_out", j), "nt", epi=gelu_bwd, extras=(sv["yy"],), name="s5_out_dx")[0]
            d_xr, d_xi, du_skip, d_dsk, d_cdr, d_cdi = _s5_cproj_bwd(d_y, sv["xr"], sv["xi"], sv["cd_re"], sv["cd_im"], sv["uu"], sv["dsk"])
            grads["s5_d"][j] = d_dsk
            grads["s5_c_re"][j] = _block_diag_extract(d_cdr, S5_CH, S5_STATE, transpose=True)
            grads["s5_c_im"][j] = -_block_diag_extract(d_cdi, S5_CH, S5_STATE, transpose=True)
            d_bur, d_bui, d_lam = _s5_scan(d_xr, d_xi, sv["lam"], reverse=True, xr=sv["xr"], xi=sv["xi"])
            d_uu, d_bdr, d_bdi = _s5_bproj_bwd(d_bur, d_bui, sv["bd_re"], sv["bd_im"], sv["uu"], du_skip)
            d_bbr = _block_diag_extract(d_bdr, S5_STATE, S5_CH, transpose=True).transpose(2, 0, 1)
            d_bbi = _block_diag_extract(d_bdi, S5_STATE, S5_CH, transpose=True).transpose(2, 0, 1)
            d_lr, d_li, d_ls, d_br_t, d_bi_t = _s5_params_bwd(
                s5_lam_re[j], s5_lam_im[j], sv["ls"], sv["b_re_t"], sv["b_im_t"],
                d_lam[0].reshape(G, P), d_lam[1].reshape(G, P), d_bbr, d_bbi)
            grads["s5_lam_re"][j], grads["s5_lam_im"][j], grads["s5_log_step"][j] = d_lr, d_li, d_ls.reshape(G)
            grads["s5_b_re"][j], grads["s5_b_im"][j] = d_br_t.transpose(1, 2, 0), d_bi_t.transpose(1, 2, 0)
            grads["s5_w_in"][j] = dw(sv["hn"], d_uu, 0, "s5_in_dw")
            d_hn = _mm(d_uu, weight("s5_w_in", j), "nt", name="s5_in_dx")[0]
        dh, _, grads["norm_mix"][i] = _rms_bwd(d_hn, sv["h0"], norm_mix[i], dh)

    out = {}
    layers = lambda name: jnp.concatenate(grads[name], axis=1)
    d_conv = jnp.stack(grads["gdn_conv_w"]).reshape(n_gdn * 4, N_DEV, cw).transpose(1, 0, 2)
    d_s5d = jnp.stack([g.reshape(N_DEV, D // N_DEV) for g in grads["s5_d"]], axis=1)
    d_misc = jnp.concatenate([d_conv, jnp.pad(d_s5d, ((0, 0), (0, 0), (0, cw - D // N_DEV)))], axis=1)
    d_misc = jnp.pad(d_misc, ((0, 0), (0, -d_misc.shape[1] % 8), (0, 0)))
    contributions = ([layers("gdn_w_in"), jnp.concatenate([layers(n) for n in ROW_GROUP], axis=1)]
                     + [layers(n) for n in COL_SHARDED] + [d_misc])
    got = _pair_exchange(contributions, "exchange_grads_pair")
    px_, py_, pc_ = _mesh_position()
    mine = [4 * qx + 2 * qy + pc_ for qx, qy in ((px_, py_), (1 - px_, py_), (px_, 1 - py_), (1 - px_, 1 - py_))]
    own = [jnp.stack([lax.dynamic_index_in_dim(a, d, 0, keepdims=False) for d in mine]) for a in contributions]
    arrived = _chip_exchange([_chip_sums(o, g) for o, g in zip(own, got)], "exchange_grads_chips")
    recv = [[(o, (0,)), (g, (0,)), (r, (0, 1, 2))] for o, g, r in zip(own, got, arrived)]
    r_win, r_row, r_misc = recv[0], recv[1], recv[-1]
    r_col = dict(zip(COL_SHARDED, recv[2:-1]))

    def update(name, parts, row0, **kw):
        res = _adamw(parts, row0, _rows2d(w[name]), _rows2d(mom[name]), _rows2d(vel[name]), "adamw_" + name, **kw)
        out[name] = [r.reshape(w[name].shape) for r in res]

    update("gdn_w_in", r_win, 0, window_n8=n8)
    for n in ROW_GROUP:
        update(n, r_row, row_off[n])
    for n in COL_SHARDED:
        update(n, r_col[n], 0)
    res = _adamw(r_misc, 0, _misc_pack(gdn_conv_w, s5_d), _misc_pack(mom["gdn_conv_w"], mom["s5_d"]),
                 _misc_pack(vel["gdn_conv_w"], vel["s5_d"]), "adamw_misc")
    unpacked = [_misc_unpack(r, n_gdn * 4, s5_d.shape) for r in res]
    out["gdn_conv_w"] = [u_[0].reshape(gdn_conv_w.shape) for u_ in unpacked]
    out["s5_d"] = [u_[1] for u_ in unpacked]

    rep_g = {n: (d_norm_final[0] if n == "norm_final" else jnp.stack([g.reshape(w[n].shape[1:]) for g in grads[n]])) for n in REPLICATED}
    flat_r = lambda d: [d[n].reshape(-1) for n in REPLICATED]
    pg, _ = _pack(flat_r(rep_g), F32)
    parts_r = _all_gather([pg], "gather_small_grads")[0]
    pw, _ = _pack(flat_r(w), F32)
    pm, _ = _pack(flat_r(mom), F32)
    pvv, _ = _pack(flat_r(vel), F32)
    res = [r.reshape(-1) for r in _adamw([(parts_r, tuple(range(N_DEV)))], 0, pw, pm, pvv, "adamw_replicated")]
    off = 0
    for name in REPLICATED:
        n = w[name].size
        out[name] = [res[k][off:off + n].reshape(w[name].shape) for k in range(4)]
        off += n

    grad_x = dh[None]
    return (loss, grad_x, *[out[n][0] for n in WEIGHTS], *[out[n][1] for n in WEIGHTS],
            *[out[n][2] for n in WEIGHTS], *[out[n][3] for n in WEIGHTS])
```

```python
import collections
import math

import jax
import jax.numpy as jnp
from jax import lax
from jax.experimental import pallas as pl
from jax.experimental.pallas import tpu as pltpu

F32, BF16 = jnp.float32, jnp.bfloat16
NN, NT, TN = ((1,), (0,)), ((1,), (1,)), ((0,), (0,))

N_DEV = 8
MESH_AXES = ("x", "y", "c")
LANES = 128
V7X_VMEM_BYTES = 64 * 1024 * 1024
VMEM_LIMIT = V7X_VMEM_BYTES - 8 * 1024 * 1024
CHUNK = 64
HEAD = 128
SUPER = 256
S5_CH = 16
S5_STATE = 64
S5_GPB = LANES // S5_CH
S5_SPB = S5_GPB * S5_STATE
NORM_EPS = 1e-6
L2_EPS = 1e-6
ADAM_LR, ADAM_B1, ADAM_B2, ADAM_EPS, ADAM_WD, ADAM_STEP = 0.001, 0.9, 0.999, 1e-08, 0.01, 10
PACK_W = 1024
PACK_ROWS = 256


def _dot(a, b, dims):
    return lax.dot_general(a, b, (dims, ((), ())), preferred_element_type=F32)


def _bdot(a, b, dims=NN):
    return _dot(a.astype(BF16), b.astype(BF16), dims)


def _call(body, grid, ins, outs, scratch=(), name=None, sem=None):
    res = pl.pallas_call(
        body,
        grid=grid,
        in_specs=[pl.BlockSpec(b, m) for _, b, m in ins],
        out_specs=[pl.BlockSpec(b, m) for _, _, b, m in outs],
        out_shape=[jax.ShapeDtypeStruct(s, d) for s, d, _, _ in outs],
        scratch_shapes=list(scratch),
        name=name,
        compiler_params=pltpu.CompilerParams(
            dimension_semantics=sem or ("arbitrary",) * len(grid), vmem_limit_bytes=VMEM_LIMIT),
    )(*[a for a, _, _ in ins])
    return res


def _tile(n, want):
    t = min(n, want)
    assert n % t == 0, (n, want)
    return t


def _accumulate(ref, val, first):
    @pl.when(first)
    def _():
        ref[...] = jnp.zeros_like(ref)
    ref[...] += val


class Sharded(collections.namedtuple("Sharded", "arr axis row0 rows")):
    @property
    def shape(self):
        c = self.arr.shape[2]
        return (self.rows, N_DEV * c) if self.axis == 1 else (N_DEV * self.rows, c)

    @property
    def units(self):
        return (math.gcd(self.rows, self.row0), self.arr.shape[2])


def _mm(a, b, mode="nn", out_dtypes=(F32,), epi=None, extras=(), name="mm", out_axis=None, tm=1024, tn=1024, tk=2048):
    sh = isinstance(b, Sharded)
    b_rows, b_cols = b.shape
    u_rows, u_cols = b.units if sh else b.shape
    if mode == "nn":
        (M, K), (K2, N), (uk, un) = a.shape, (b_rows, b_cols), (u_rows, u_cols)
    elif mode == "nt":
        (M, K), (N, K2), (un, uk) = a.shape, (b_rows, b_cols), (u_rows, u_cols)
    else:
        (K, M), (K2, N), (uk, un) = a.shape, (b_rows, b_cols), (u_rows, u_cols)
    assert K == K2, (a.shape, b.shape, mode)
    um = M
    if out_axis == 0:
        um = M // N_DEV
    elif out_axis == 1:
        un = N // N_DEV
    ospan = 1
    if out_axis == 0 and um < min(M, tm):
        ospan = min(M, tm) // um
        assert N_DEV % ospan == 0
    tm, tn = _tile(um, tm) * ospan, _tile(un, tn)
    span = 1
    if sh and ((mode == "nn" and b.axis == 0 and uk == b.rows) or (mode == "nt" and b.axis == 1)) and uk < min(K, tk):
        span = min(K, tk) // uk
        assert N_DEV % span == 0
    tk = _tile(uk, tk)
    nk = K // (tk * span)
    a_spec = ((tk, tm), lambda i, j, k: (k, i)) if mode == "tn" else ((tm, tk * span), lambda i, j, k: (i, k))
    if not sh:
        b_arr = b
        b_spec = ((tn, tk), lambda i, j, k: (j, k)) if mode == "nt" else ((tk, tn), lambda i, j, k: (k, j))
    else:
        b_arr = b.arr
        tr_, tc_ = (tk, tn) if mode == "nn" else (tn, tk)
        r0, per_r, per_c = b.row0 // tr_, b.rows // tr_, b.arr.shape[2] // tc_
        assert b.row0 % tr_ == 0 and mode != "tn"
        if span > 1:
            place = (lambda r, c: (r, r0, c)) if b.axis == 0 else (lambda r, c: (c, r0 + r, 0))
        elif b.axis == 1:
            place = lambda r, c: (c // per_c, r0 + r, c % per_c)
        else:
            place = lambda r, c: (r // per_r, r0 + r % per_r, c)
        b_spec = ((span, tr_, tc_), (lambda i, j, k: place(k, j)) if mode == "nn" else (lambda i, j, k: place(j, k)))
    dims = {"nn": NN, "nt": NT, "tn": TN}[mode]
    n_ex, n_out = len(extras), len(out_dtypes)

    def body(*refs):
        a_ref, b_ref = refs[:2]
        ex = refs[2:2 + n_ex]
        outs = refs[2 + n_ex:2 + n_ex + n_out]

        def product():
            if not sh:
                return _bdot(a_ref[...], b_ref[...], dims)
            part = _bdot(a_ref[:, :tk], b_ref[0], dims)
            for s in range(1, span):
                part = part + _bdot(a_ref[:, s * tk:(s + 1) * tk], b_ref[s], dims)
            return part

        def finish(res):
            vals = epi(res, *[e[...] for e in ex]) if epi is not None else (res,)
            for r, v in zip(outs, vals):
                r[...] = v.astype(r.dtype).reshape(r.shape)

        if nk == 1:
            finish(product())
            return
        acc = refs[-1]
        k = pl.program_id(2)

        @pl.when(k == 0)
        def _():
            acc[...] = jnp.zeros_like(acc)

        acc[...] += product()

        @pl.when(k == nk - 1)
        def _():
            finish(acc[...])

    tile = lambda i, j, k: (i, j)
    if out_axis is None:
        out_shape, out_block, out_map = (M, N), (tm, tn), tile
    elif out_axis == 0 and ospan > 1:
        out_shape, out_block, out_map = (N_DEV, um, N), (ospan, um, tn), lambda i, j, k: (i, 0, j)
    elif out_axis == 0:
        per = um // tm
        out_shape, out_block, out_map = (N_DEV, um, N), (None, tm, tn), lambda i, j, k: (i // per, i % per, j)
    else:
        per = un // tn
        out_shape, out_block, out_map = (N_DEV, M, un), (None, tm, tn), lambda i, j, k: (j // per, i, j % per)
    return _call(
        body, (M // tm, N // tn, nk),
        [(a,) + a_spec, (b_arr,) + b_spec] + [(e, (tm, tn), tile) for e in extras],
        [(out_shape, d, out_block, out_map) for d in out_dtypes],
        scratch=[pltpu.VMEM((tm, tn), F32)] if nk > 1 else [], name=name,
        sem=("parallel", "parallel", "arbitrary"))


def _rms_fwd(h, g, tr=256):
    T, D = h.shape
    tr = _tile(T, tr)

    def body(h_ref, g_ref, o_ref):
        x = h_ref[...]
        r = lax.rsqrt(jnp.mean(x * x, axis=-1, keepdims=True) + NORM_EPS)
        o_ref[...] = (x * r * g_ref[...]).astype(BF16)

    row = lambda i: (i, 0)
    fix = lambda i: (0, 0)
    return _call(body, (T // tr,), [(h, (tr, D), row), (g.reshape(1, D), (1, D), fix)],
                 [((T, D), BF16, (tr, D), row)], name="rms_fwd", sem=("parallel",))[0]


def _rms_bwd_math(dy, x, g):
    r = lax.rsqrt(jnp.mean(x * x, axis=-1, keepdims=True) + NORM_EPS)
    xh = x * r
    dxh = dy * g
    dx = r * (dxh - xh * jnp.mean(dxh * xh, axis=-1, keepdims=True))
    dg = jnp.sum(dy * xh, axis=0, keepdims=True)
    return dx, dg


def _rms_bwd(dy, h, g, res, tr=256):
    T, D = h.shape
    tr = _tile(T, tr)

    def body(dy_ref, h_ref, g_ref, res_ref, dh_ref, dhb_ref, dg_ref):
        dx, dg = _rms_bwd_math(dy_ref[...], h_ref[...], g_ref[...])
        dh = res_ref[...] + dx
        dh_ref[...] = dh
        dhb_ref[...] = dh.astype(BF16)
        _accumulate(dg_ref, dg, pl.program_id(0) == 0)

    row = lambda i: (i, 0)
    fix = lambda i: (0, 0)
    return _call(body, (T // tr,),
                 [(dy, (tr, D), row), (h, (tr, D), row), (g.reshape(1, D), (1, D), fix), (res, (tr, D), row)],
                 [((T, D), F32, (tr, D), row), ((T, D), BF16, (tr, D), row), ((1, D), F32, (1, D), fix)], name="rms_bwd")


def _loss_fwd_bwd(h, g, tgt, tr=256):
    T, D = h.shape
    tr = _tile(T, tr)

    def body(h_ref, g_ref, t_ref, dh_ref, dg_ref, loss_ref):
        x, gg = h_ref[...], g_ref[...]
        r = lax.rsqrt(jnp.mean(x * x, axis=-1, keepdims=True) + NORM_EPS)
        diff = x * r * gg - t_ref[...]
        part = 0.5 * jnp.sum(jnp.mean(diff * diff, axis=-1, keepdims=True))
        dx, dg = _rms_bwd_math(diff * (1.0 / D), x, gg)
        dh_ref[...] = dx
        first = pl.program_id(0) == 0
        _accumulate(dg_ref, dg, first)
        _accumulate(loss_ref, jnp.full((1, LANES), part, F32), first)

    row = lambda i: (i, 0)
    fix = lambda i: (0, 0)
    return _call(body, (T // tr,),
                 [(h, (tr, D), row), (g.reshape(1, D), (1, D), fix), (tgt, (tr, D), row)],
                 [((T, D), F32, (tr, D), row), ((1, D), F32, (1, D), fix), ((1, LANES), F32, (1, LANES), fix)],
                 name="loss_fwd_bwd")


def _ple_fwd(h, s, pp, tr=256):
    T, D = h.shape
    tr = _tile(T, tr)

    def body(h_ref, s_ref, p_ref, o_ref):
        o_ref[...] = h_ref[...] + jax.nn.sigmoid(s_ref[...]) * p_ref[...]

    row = lambda i: (i, 0)
    return _call(body, (T // tr,), [(a, (tr, D), row) for a in (h, s, pp)],
                 [((T, D), F32, (tr, D), row)], name="ple_fwd", sem=("parallel",))[0]


def _ple_bwd(dh, s, pp, tr=256):
    T, D = dh.shape
    tr = _tile(T, tr)

    def body(dh_ref, s_ref, p_ref, ds_ref, dp_ref):
        d = dh_ref[...]
        gate = jax.nn.sigmoid(s_ref[...])
        ds_ref[...] = (d * p_ref[...] * gate * (1.0 - gate)).astype(BF16)
        dp_ref[...] = (d * gate).astype(BF16)

    row = lambda i: (i, 0)
    return _call(body, (T // tr,), [(a, (tr, D), row) for a in (dh, s, pp)],
                 [((T, D), BF16, (tr, D), row)] * 2, name="ple_bwd", sem=("parallel",))


def _conv_taps(xe, w, tr):
    c = w[3:4, :] * xe[8:, :]
    for j in range(3):
        c = c + w[j:j + 1, :] * pltpu.roll(xe, 3 - j, 0)[8:, :]
    return c


def _gdn_pre_fwd(pq, conv_w, hk, tr=2048):
    T, CD = pq.shape
    tr = _tile(T, tr)
    r8 = tr // 8

    def body(x_ref, halo_ref, w_ref, o_ref):
        j, r = pl.program_id(0), pl.program_id(1)
        halo = jnp.where(r > 0, halo_ref[...], 0.0)
        xe = jnp.concatenate([halo, x_ref[...]], axis=0)
        c = _conv_taps(xe, w_ref[...], tr)
        s = c * jax.nn.sigmoid(c)
        rn = lax.rsqrt(jnp.sum(s * s, axis=-1, keepdims=True) + L2_EPS)
        scale = jnp.where(j < hk, HEAD ** -0.5, 1.0)
        o_ref[...] = jnp.where(j < 2 * hk, s * rn * scale, s)

    tile = lambda j, r: (r, j)
    return _call(body, (CD // HEAD, T // tr),
                 [(pq, (tr, HEAD), tile), (pq, (8, HEAD), lambda j, r: (jnp.maximum(r * r8 - 1, 0), j)),
                  (conv_w, (4, HEAD), lambda j, r: (0, j))],
                 [((T, CD), F32, (tr, HEAD), tile)], name="gdn_pre_fwd", sem=("parallel", "parallel"))[0]


def _gdn_pre_bwd(dn, pq, conv_w, hk, tr=2048):
    T, CD = pq.shape
    tr = _tile(T, tr)
    r8 = tr // 8

    def body(dn_ref, x_ref, halo_ref, w_ref, dc_ref, dw_ref):
        j, r = pl.program_id(0), pl.program_id(1)
        halo = jnp.where(r > 0, halo_ref[...], 0.0)
        xe = jnp.concatenate([halo, x_ref[...]], axis=0)
        c = _conv_taps(xe, w_ref[...], tr)
        sig = jax.nn.sigmoid(c)
        s = c * sig
        rn = lax.rsqrt(jnp.sum(s * s, axis=-1, keepdims=True) + L2_EPS)
        scale = jnp.where(j < hk, HEAD ** -0.5, 1.0)
        d = dn_ref[...]
        y = s * rn
        dy = d * scale
        ds = jnp.where(j < 2 * hk, rn * (dy - y * jnp.sum(dy * y, axis=-1, keepdims=True)), d)
        dc = ds * sig * (1.0 + c * (1.0 - sig))
        dc_ref[...] = dc

        @pl.when(r == 0)
        def _():
            dw_ref[...] = jnp.zeros_like(dw_ref)

        for t in range(4):
            xs = xe[8:, :] if t == 3 else pltpu.roll(xe, 3 - t, 0)[8:, :]
            dw_ref[t:t + 1, :] += jnp.sum(dc * xs, axis=0, keepdims=True)

    tile = lambda j, r: (r, j)
    col = lambda j, r: (0, j)
    return _call(body, (CD // HEAD, T // tr),
                 [(dn, (tr, HEAD), tile), (pq, (tr, HEAD), tile),
                  (pq, (8, HEAD), lambda j, r: (jnp.maximum(r * r8 - 1, 0), j)), (conv_w, (4, HEAD), col)],
                 [((T, CD), F32, (tr, HEAD), tile), ((4, CD), F32, (4, HEAD), col)], name="gdn_pre_bwd")


def _gdn_conv_bwd(dc, conv_w, tr=2048):
    T, CD = dc.shape
    tr = _tile(T, tr)
    r8 = tr // 8
    n_r = T // tr

    def body(dc_ref, halo_ref, w_ref, dx_ref):
        r = pl.program_id(1)
        halo = jnp.where(r < n_r - 1, halo_ref[...], 0.0)
        de = jnp.concatenate([dc_ref[...], halo], axis=0)
        w = w_ref[...]
        dx = w[3:4, :] * de[:tr, :]
        for j in range(3):
            dx = dx + w[j:j + 1, :] * pltpu.roll(de, tr + 8 - (3 - j), 0)[:tr, :]
        dx_ref[...] = dx.astype(BF16)

    tile = lambda j, r: (r, j)
    return _call(body, (CD // HEAD, n_r),
                 [(dc, (tr, HEAD), tile), (dc, (8, HEAD), lambda j, r: (jnp.minimum((r + 1) * r8, T // 8 - 1), j)),
                  (conv_w, (4, HEAD), lambda j, r: (0, j))],
                 [((T, CD), BF16, (tr, HEAD), tile)], name="gdn_conv_bwd", sem=("parallel", "parallel"))[0]


def _gates_fwd(ba, pv, hv, tr=1024):
    T = ba.shape[0]
    tr = _tile(T, tr)

    def body(x_ref, pv_ref, o_ref):
        x = x_ref[...]
        lane = lax.broadcasted_iota(jnp.int32, x.shape, 1)
        g = -jnp.exp(pv_ref[0:1, :]) * jax.nn.softplus(x + pv_ref[1:2, :])
        o_ref[...] = jnp.where(lane < hv, jax.nn.sigmoid(x), jnp.where(lane < 2 * hv, g, 0.0))

    row = lambda i: (i, 0)
    return _call(body, (T // tr,), [(ba, (tr, LANES), row), (pv, (2, LANES), lambda i: (0, 0))],
                 [((T, LANES), F32, (tr, LANES), row)], name="gates_fwd", sem=("parallel",))[0]


def _gates_bwd(dg2, ba, pv, hv, tr=1024):
    T = ba.shape[0]
    tr = _tile(T, tr)

    def body(d_ref, x_ref, pv_ref, dx_ref, dpv_ref):
        x, d = x_ref[...], d_ref[...]
        lane = lax.broadcasted_iota(jnp.int32, x.shape, 1)
        is_a = (lane >= hv) & (lane < 2 * hv)
        beta = jax.nn.sigmoid(x)
        neg_a = -jnp.exp(pv_ref[0:1, :])
        z = x + pv_ref[1:2, :]
        da = d * neg_a * jax.nn.sigmoid(z)
        dx_ref[...] = jnp.where(lane < hv, d * beta * (1.0 - beta), jnp.where(is_a, da, 0.0))
        first = pl.program_id(0) == 0

        @pl.when(first)
        def _():
            dpv_ref[...] = jnp.zeros_like(dpv_ref)

        dpv_ref[0:1, :] += jnp.sum(jnp.where(is_a, d * neg_a * jax.nn.softplus(z), 0.0), axis=0, keepdims=True)
        dpv_ref[1:2, :] += jnp.sum(jnp.where(is_a, da, 0.0), axis=0, keepdims=True)

    row = lambda i: (i, 0)
    fix = lambda i: (0, 0)
    return _call(body, (T // tr,), [(dg2, (tr, LANES), row), (ba, (tr, LANES), row), (pv, (2, LANES), fix)],
                 [((T, LANES), F32, (tr, LANES), row), ((2, LANES), F32, (2, LANES), fix)], name="gates_bwd")


def _ogate_fwd(o, z, o_norm, tr=2048):
    T, VD = o.shape
    tr = _tile(T, tr)

    def body(o_ref, z_ref, g_ref, y_ref):
        x, zz = o_ref[...], z_ref[...]
        r = lax.rsqrt(jnp.mean(x * x, axis=-1, keepdims=True) + NORM_EPS)
        y_ref[...] = (x * r * g_ref[...] * (zz * jax.nn.sigmoid(zz))).astype(BF16)

    tile = lambda h, r: (r, h)
    return _call(body, (VD // HEAD, T // tr),
                 [(o, (tr, HEAD), tile), (z, (tr, HEAD), tile), (o_norm.reshape(1, HEAD), (1, HEAD), lambda h, r: (0, 0))],
                 [((T, VD), BF16, (tr, HEAD), tile)], name="ogate_fwd", sem=("parallel", "parallel"))[0]


def _ogate_bwd(dy, o, z, o_norm, tr=2048):
    T, VD = o.shape
    tr = _tile(T, tr)

    def body(dy_ref, o_ref, z_ref, g_ref, do_ref, dz_ref, dg_ref):
        d, x, zz, g = dy_ref[...], o_ref[...], z_ref[...], g_ref[...]
        sig = jax.nn.sigmoid(zz)
        silu = zz * sig
        dx, dg = _rms_bwd_math(d * silu, x, g)
        r = lax.rsqrt(jnp.mean(x * x, axis=-1, keepdims=True) + NORM_EPS)
        do_ref[...] = dx
        dz_ref[...] = (d * (x * r * g) * sig * (1.0 + zz * (1.0 - sig))).astype(BF16)
        _accumulate(dg_ref, dg, (pl.program_id(0) == 0) & (pl.program_id(1) == 0))

    tile = lambda h, r: (r, h)
    fix = lambda h, r: (0, 0)
    return _call(body, (VD // HEAD, T // tr),
                 [(dy, (tr, HEAD), tile), (o, (tr, HEAD), tile), (z, (tr, HEAD), tile), (o_norm.reshape(1, HEAD), (1, HEAD), fix)],
                 [((T, VD), F32, (tr, HEAD), tile), ((T, VD), BF16, (tr, HEAD), tile), ((1, HEAD), F32, (1, HEAD), fix)],
                 name="ogate_bwd")


def _chunk_iota():
    return (lax.broadcasted_iota(jnp.int32, (CHUNK, CHUNK), 0), lax.broadcasted_iota(jnp.int32, (CHUNK, CHUNK), 1))


def _decay(gc):
    ri, ci = _chunk_iota()
    gcol = gc[:, :CHUNK]
    grow = jnp.sum(jnp.where(ri == ci, gcol, 0.0), axis=0, keepdims=True)
    return jnp.where(ri >= ci, jnp.exp(jnp.minimum(gcol - grow, 0.0)), 0.0)


def _rowsum(x):
    return jnp.broadcast_to(jnp.sum(x, axis=1, keepdims=True), (x.shape[0], HEAD))


def _split3(x):
    h1 = x.astype(BF16)
    r1 = x - h1.astype(F32)
    h2 = r1.astype(BF16)
    return h1, h2, (r1 - h2.astype(F32)).astype(BF16)


def _sel_dot(sel, xs, dims=NN):
    s = sel.astype(BF16)
    parts = [_split3(x) for x in xs]
    if dims == NN:
        return [_dot(s, h1, NN) + _dot(s, h2, NN) + _dot(s, h3, NN) for h1, h2, h3 in parts]
    return [_dot(h1, s, dims) + _dot(h2, s, dims) + _dot(h3, s, dims) for h1, h2, h3 in parts]


def _colsum(es):
    return _sel_dot(jnp.ones((es[0].shape[0], HEAD), F32), es, TN)


def _super_iota():
    ri = lax.broadcasted_iota(jnp.int32, (SUPER, SUPER), 0)
    ci = lax.broadcasted_iota(jnp.int32, (SUPER, SUPER), 1)
    shift = int(math.log2(CHUNK))
    return ri, ci, jnp.right_shift(ri, shift) == jnp.right_shift(ci, shift)


def _decay_super(gc, ri, ci, same):
    gcol = jnp.concatenate([gc] * (SUPER // HEAD), axis=1)
    grow = jnp.sum(jnp.where(ri == ci, gcol, 0.0), axis=0, keepdims=True)
    return jnp.where(same & (ri >= ci), jnp.exp(jnp.minimum(gcol - grow, 0.0)), 0.0)


def _unit_lower_inverse(ms, eye):
    ps = [-m for m in ms]
    xs = [eye + p for p in ps]
    for _ in range(int(math.log2(CHUNK)) - 1):
        ps = [_bdot(p, p) for p in ps]
        xs = [x + _bdot(x, p) for x, p in zip(xs, ps)]
    resid = []
    for m, x in zip(ms, xs):
        (m1, m2, _), (x1, x2, _) = _split3(m), _split3(x)
        resid.append((eye - x) - (_dot(m1, x1, NN) + _dot(m1, x2, NN) + _dot(m2, x1, NN)))
    return [x + _bdot(x, r) for x, r in zip(xs, resid)]


def _gdn_a_fwd(qkv, gb, bb, hk, hv, tr=1024):
    T = qkv.shape[0]
    tr = _tile(T, tr)
    assert tr % SUPER == 0

    def body(k_ref, v_ref, g_ref, b_ref, u_ref, w_ref, gc_ref, ti_ref):
        ri, ci, same = _super_iota()
        ltri = jnp.where(same & (ri >= ci), 1.0, 0.0)
        eye = jnp.where(ri == ci, 1.0, 0.0)
        rows = [pl.ds(s * SUPER, SUPER) for s in range(tr // SUPER)]
        ks, vs, betas = [k_ref[r, :] for r in rows], [v_ref[r, :] for r in rows], [b_ref[r, :] for r in rows]
        gcs = _sel_dot(ltri, [g_ref[r, :] for r in rows])
        kbs = [k * beta for k, beta in zip(ks, betas)]
        ms = [jnp.where(same & (ri > ci), _bdot(kb, k, NT) * _decay_super(gc, ri, ci, same), 0.0)
              for kb, k, gc in zip(kbs, ks, gcs)]
        tinvs = _unit_lower_inverse(ms, eye)
        xs = [_bdot(tinv, jnp.concatenate([v * beta, kb * jnp.exp(gc)], axis=1))
              for tinv, v, beta, kb, gc in zip(tinvs, vs, betas, kbs, gcs)]
        for r, x, gc, tinv in zip(rows, xs, gcs, tinvs):
            u_ref[r, :] = x[:, :HEAD]
            w_ref[r, :] = x[:, HEAD:]
            gc_ref[r, :] = gc
            ti_ref[0, r, :] = tinv.astype(BF16)

    tile = lambda h, r: (r, h)
    vd = hv * HEAD
    return _call(body, (hv, T // tr),
                 [(qkv, (tr, HEAD), lambda h, r: (r, hk + h // 2)), (qkv, (tr, HEAD), lambda h, r: (r, 2 * hk + h)),
                  (gb, (tr, HEAD), tile), (bb, (tr, HEAD), tile)],
                 [((T, vd), F32, (tr, HEAD), tile)] * 3 + [((hv, T, SUPER), BF16, (1, tr, SUPER), lambda h, r: (h, r, 0))],
                 name="gdn_a_fwd", sem=("parallel", "parallel"))


def _gdn_b_fwd(qkv, u, w, gc, hk, hv, tr=512):
    T = qkv.shape[0]
    tr = _tile(T, tr)
    cpb = tr // CHUNK

    def body(q_ref, k_ref, u_ref, w_ref, gc_ref, o_ref, vn_ref, sall_ref, s_ref):
        ri, ci = _chunk_iota()

        @pl.when(pl.program_id(1) == 0)
        def _():
            s_ref[...] = jnp.zeros_like(s_ref)

        rows = [pl.ds(c * CHUNK, CHUNK) for c in range(cpb)]
        qs, ks, us, ws, gcs = ([ref[r, :] for r in rows] for ref in (q_ref, k_ref, u_ref, w_ref, gc_ref))
        gls = [gc[CHUNK - 1:CHUNK, :] for gc in gcs]
        kws = [_bdot(k * jnp.exp(gl - gc), jnp.concatenate([w_, u_], axis=1), TN)
               for k, gl, gc, w_, u_ in zip(ks, gls, gcs, ws, us)]
        qks = [jnp.where(ri >= ci, _bdot(q, k, NT) * _decay(gc), 0.0) for q, k, gc in zip(qs, ks, gcs)]
        s = s_ref[...]
        states = []
        for kw, gl in zip(kws, gls):
            states.append(s)
            s = s * jnp.exp(gl) - _bdot(kw[:, :HEAD], s) + kw[:, HEAD:]
        s_ref[...] = s
        vns = [u_ - _bdot(w_, st) for u_, w_, st in zip(us, ws, states)]
        outs = [_bdot(q * jnp.exp(gc), st) + _bdot(qk, vn) for q, gc, st, qk, vn in zip(qs, gcs, states, qks, vns)]
        for c, (r, o, vn, st) in enumerate(zip(rows, outs, vns, states)):
            o_ref[r, :] = o
            vn_ref[r, :] = vn
            sall_ref[0, c] = st

    tile = lambda h, r: (r, h)
    vd = hv * HEAD
    return _call(body, (hv, T // tr),
                 [(qkv, (tr, HEAD), lambda h, r: (r, h // 2)), (qkv, (tr, HEAD), lambda h, r: (r, hk + h // 2)),
                  (u, (tr, HEAD), tile), (w, (tr, HEAD), tile), (gc, (tr, HEAD), tile)],
                 [((T, vd), F32, (tr, HEAD), tile)] * 2 +
                 [((hv, T // CHUNK, HEAD, HEAD), F32, (1, cpb, HEAD, HEAD), lambda h, r: (h, r, 0, 0))],
                 scratch=[pltpu.VMEM((HEAD, HEAD), F32)], name="gdn_b_fwd", sem=("parallel", "arbitrary"))


def _gdn_b_bwd(do, qkv, w, gc, vn, sall, hk, hv, tr=512):
    T = qkv.shape[0]
    tr = _tile(T, tr)
    cpb = tr // CHUNK
    n_r = T // tr

    def body(do_ref, q_ref, k_ref, w_ref, gc_ref, vn_ref, sall_ref, dq_ref, dk_ref, dgc_ref, du_ref, dw_ref, ds_ref):
        ri, ci = _chunk_iota()
        row = lax.broadcasted_iota(jnp.int32, (CHUNK, HEAD), 0)

        @pl.when(pl.program_id(1) == 0)
        def _():
            ds_ref[...] = jnp.zeros_like(ds_ref)

        rows = [pl.ds(c * CHUNK, CHUNK) for c in range(cpb)]
        d_os, qs, ks, ws, gcs, vns = ([ref[r, :] for r in rows] for ref in (do_ref, q_ref, k_ref, w_ref, gc_ref, vn_ref))
        ss = [sall_ref[0, c] for c in range(cpb)]
        gls = [gc[CHUNK - 1:CHUNK, :] for gc in gcs]
        egcs = [jnp.exp(gc) for gc in gcs]
        ekds = [jnp.exp(gl - gc) for gl, gc in zip(gls, gcs)]
        egs = [jnp.exp(gl) for gl in gls]
        qgs = [q * e for q, e in zip(qs, egcs)]
        kds = [k * e for k, e in zip(ks, ekds)]
        decs = [_decay(gc) for gc in gcs]
        qks = [jnp.where(ri >= ci, _bdot(q, k, NT) * dec, 0.0) for q, k, dec in zip(qs, ks, decs)]
        wkds = [_bdot(w_, kd, TN) for w_, kd in zip(ws, kds)]
        qk_dos = [_bdot(qk, d_o, TN) for qk, d_o in zip(qks, d_os)]
        consts = [_bdot(qg, d_o, TN) - _bdot(w_, qd, TN) for qg, d_o, w_, qd in zip(qgs, d_os, ws, qk_dos)]
        ds = ds_ref[...]
        ds_nexts = [None] * cpb
        for c in reversed(range(cpb)):
            ds_nexts[c] = ds
            ds = ds * egs[c] - _bdot(wkds[c], ds) + consts[c]
        ds_ref[...] = ds
        d_vns = [qd + _bdot(kd, dsn) for qd, kd, dsn in zip(qk_dos, kds, ds_nexts)]
        d_kds = [_bdot(vn, dsn, NT) for vn, dsn in zip(vns, ds_nexts)]
        d_qgs = [_bdot(d_o, s, NT) for d_o, s in zip(d_os, ss)]
        d_qks = [jnp.where(ri >= ci, _bdot(d_o, vn, NT), 0.0) for d_o, vn in zip(d_os, vns)]
        e_qs = [d_qk * qk for d_qk, qk in zip(d_qks, qks)]
        cols = _colsum(e_qs)
        d_bs = [d_qk * dec for d_qk, dec in zip(d_qks, decs)]
        dqs = [d_qg * egc + _bdot(d_b, k) for d_qg, egc, d_b, k in zip(d_qgs, egcs, d_bs, ks)]
        dks = [d_kd * ekd + _bdot(d_b, q, TN) for d_kd, ekd, d_b, q in zip(d_kds, ekds, d_bs, qs)]
        dws = [-_bdot(d_vn, s, NT) for d_vn, s in zip(d_vns, ss)]
        for c, r in enumerate(rows):
            d_gl = jnp.sum(d_kds[c] * kds[c]) + jnp.sum(ss[c] * ds_nexts[c]) * egs[c]
            dq_ref[r, :] = dqs[c]
            dk_ref[r, :] = dks[c]
            dgc_ref[r, :] = (_rowsum(d_qgs[c] * qgs[c]) - _rowsum(d_kds[c] * kds[c]) + _rowsum(e_qs[c]) - cols[c]
                             + jnp.where(row == CHUNK - 1, d_gl, 0.0))
            du_ref[r, :] = d_vns[c]
            dw_ref[r, :] = dws[c]

    rtile = lambda h, r: (n_r - 1 - r, h)
    vd = hv * HEAD
    return _call(body, (hv, n_r),
                 [(do, (tr, HEAD), rtile), (qkv, (tr, HEAD), lambda h, r: (n_r - 1 - r, h // 2)),
                  (qkv, (tr, HEAD), lambda h, r: (n_r - 1 - r, hk + h // 2)),
                  (w, (tr, HEAD), rtile), (gc, (tr, HEAD), rtile), (vn, (tr, HEAD), rtile),
                  (sall, (1, cpb, HEAD, HEAD), lambda h, r: (h, n_r - 1 - r, 0, 0))],
                 [((T, vd), F32, (tr, HEAD), rtile)] * 5,
                 scratch=[pltpu.VMEM((HEAD, HEAD), F32)], name="gdn_b_bwd", sem=("parallel", "arbitrary"))


def _gdn_a_bwd(du, dw, dgc_b, qkv, bb, gc, tinv, u, w, hk, hv, tr=1024):
    T = qkv.shape[0]
    tr = _tile(T, tr)
    assert tr % SUPER == 0

    def body(du_ref, dw_ref, dgcb_ref, k_ref, v_ref, b_ref, gc_ref, ti_ref, u_ref, w_ref, dk_ref, dv_ref, db_ref, dg_ref):
        ri, ci, same = _super_iota()
        utri = jnp.where(same & (ci >= ri), 1.0, 0.0)
        strict = same & (ri > ci)
        rows = [pl.ds(s * SUPER, SUPER) for s in range(tr // SUPER)]
        ks, vs, betas, gcs = ([ref[r, :] for r in rows] for ref in (k_ref, v_ref, b_ref, gc_ref))
        egcs = [jnp.exp(gc) for gc in gcs]
        kbs = [k * beta for k, beta in zip(ks, betas)]
        decs = [_decay_super(gc, ri, ci, same) for gc in gcs]
        ms = [jnp.where(strict, _bdot(kb, k, NT) * dec, 0.0) for kb, k, dec in zip(kbs, ks, decs)]
        d_rs = [_bdot(ti_ref[0, r, :], jnp.concatenate([du_ref[r, :], dw_ref[r, :]], axis=1), TN) for r in rows]
        d_ms = [jnp.where(strict, -_bdot(d_r, jnp.concatenate([u_ref[r, :], w_ref[r, :]], axis=1), NT), 0.0)
                for d_r, r in zip(d_rs, rows)]
        d_as = [d_m * dec for d_m, dec in zip(d_ms, decs)]
        e_ms = [d_m * m for d_m, m in zip(d_ms, ms)]
        d_kbs = [_bdot(d_a, k) + d_r[:, HEAD:] * egc for d_a, k, d_r, egc in zip(d_as, ks, d_rs, egcs)]
        dks = [_bdot(d_a, kb, TN) + d_kb * beta for d_a, kb, d_kb, beta in zip(d_as, kbs, d_kbs, betas)]
        cols = _colsum(e_ms)
        d_gcs = [_rowsum(e_m) - col + _rowsum(d_r[:, HEAD:] * kb * egc) + dgcb_ref[r, :]
                 for e_m, col, d_r, kb, egc, r in zip(e_ms, cols, d_rs, kbs, egcs, rows)]
        dgs = _sel_dot(utri, d_gcs)
        for r, dk, d_r, beta, v, d_kb, k, dg in zip(rows, dks, d_rs, betas, vs, d_kbs, ks, dgs):
            dk_ref[r, :] = dk
            dv_ref[r, :] = d_r[:, :HEAD] * beta
            db_ref[r, :] = _rowsum(d_r[:, :HEAD] * v) + _rowsum(d_kb * k)
            dg_ref[r, :] = dg

    tile = lambda h, r: (r, h)
    vd = hv * HEAD
    return _call(body, (hv, T // tr),
                 [(du, (tr, HEAD), tile), (dw, (tr, HEAD), tile), (dgc_b, (tr, HEAD), tile),
                  (qkv, (tr, HEAD), lambda h, r: (r, hk + h // 2)), (qkv, (tr, HEAD), lambda h, r: (r, 2 * hk + h)),
                  (bb, (tr, HEAD), tile), (gc, (tr, HEAD), tile), (tinv, (1, tr, SUPER), lambda h, r: (h, r, 0)),
                  (u, (tr, HEAD), tile), (w, (tr, HEAD), tile)],
                 [((T, vd), F32, (tr, HEAD), tile)] * 4, name="gdn_a_bwd", sem=("parallel", "parallel"))


def _pair_sum(a, b_, tr=2048):
    T, vd = a.shape
    tr = _tile(T, tr)
    terms = [a] if b_ is None else [a, b_]
    n = len(terms)

    def body(*refs):
        acc = refs[0][...] + refs[1][...]
        for r in refs[2:2 * n]:
            acc = acc + r[...]
        refs[-1][...] = acc

    even = lambda j, r: (r, 2 * j)
    odd = lambda j, r: (r, 2 * j + 1)
    ins = [(t, (tr, HEAD), m) for t in terms for m in (even, odd)]
    return _call(body, (vd // HEAD // 2, T // tr), ins,
                 [((T, vd // 2), F32, (tr, HEAD), lambda j, r: (r, j))], name="gdn_pair_sum", sem=("parallel", "parallel"))[0]


def _s5_param_math(lr, li, ls, br, bi):
    step = jnp.exp(ls)
    zr, zi = lr * step, li * step
    mag = jnp.exp(zr)
    ar, ai = mag * jnp.cos(zi), mag * jnp.sin(zi)
    den = lr * lr + li * li
    nr, ni = ar - 1.0, ai
    cr, cim = (nr * lr + ni * li) / den, (ni * lr - nr * li) / den
    return ar, ai, br * cr - bi * cim, br * cim + bi * cr


def _s5_params_fwd(lr, li, ls, br, bi):
    G, P = lr.shape

    def body(lr_ref, li_ref, ls_ref, br_ref, bi_ref, ar_ref, ai_ref, bbr_ref, bbi_ref):
        ar, ai, bbr, bbi = _s5_param_math(lr_ref[...], li_ref[...], ls_ref[...], br_ref[...], bi_ref[...])
        ar_ref[...], ai_ref[...], bbr_ref[...], bbi_ref[...] = ar, ai, bbr, bbi

    shapes = [(G, P), (G, P), (G, 1), (S5_CH, G, P), (S5_CH, G, P)]
    z = lambda n: (lambda: (0,) * n)
    return _call(body, (), [(a, s, z(len(s))) for a, s in zip((lr, li, ls, br, bi), shapes)],
                 [(s, F32, s, z(len(s))) for s in (shapes[0], shapes[0], shapes[3], shapes[3])],
                 name="s5_params_fwd", sem=())


def _s5_params_bwd(lr, li, ls, br, bi, dar, dai, dbbr, dbbi):
    G, P = lr.shape

    def body(lr_ref, li_ref, ls_ref, br_ref, bi_ref, dar_ref, dai_ref, dbr_ref, dbi_ref, o0, o1, o2, o3, o4):
        _, vjp = jax.vjp(_s5_param_math, lr_ref[...], li_ref[...], ls_ref[...], br_ref[...], bi_ref[...])
        outs = vjp((dar_ref[...], dai_ref[...], dbr_ref[...], dbi_ref[...]))
        for r, v in zip((o0, o1, o2, o3, o4), outs):
            r[...] = v

    shapes = [(G, P), (G, P), (G, 1), (S5_CH, G, P), (S5_CH, G, P)]
    z = lambda n: (lambda: (0,) * n)
    ins = list(zip((lr, li, ls, br, bi), shapes)) + list(zip((dar, dai, dbbr, dbbi), (shapes[0], shapes[0], shapes[3], shapes[3])))
    return _call(body, (), [(a, s, z(len(s))) for a, s in ins], [(s, F32, s, z(len(s))) for s in shapes],
                 name="s5_params_bwd", sem=())


def _s5_bproj_fwd(u, bd_re, bd_im, tr=2048):
    T, D = u.shape
    tr = _tile(T, tr)
    nb = D // LANES

    def body(u_ref, br_ref, bi_ref, or_ref, oi_ref):
        ub = u_ref[...]
        or_ref[...] = _bdot(ub, br_ref[0])
        oi_ref[...] = _bdot(ub, bi_ref[0])

    blk = lambda i, j: (j, 0, 0)
    return _call(body, (T // tr, nb),
                 [(u, (tr, LANES), lambda i, j: (i, j)), (bd_re, (1, LANES, S5_SPB), blk), (bd_im, (1, LANES, S5_SPB), blk)],
                 [((T, nb * S5_SPB), F32, (tr, S5_SPB), lambda i, j: (i, j))] * 2, name="s5_bproj_fwd", sem=("parallel", "parallel"))


def _s5_scan(br, bi, lam, reverse, xr=None, xi=None, tl=1024, bw=512):
    T, NCH = br.shape
    tl, bw = _tile(T, tl), _tile(NCH, bw)
    n_t, n_g = T // tl, tl // 8

    def body(*refs):
        if reverse:
            br_ref, bi_ref, lam_ref, sr_ref, si_ref, hr_ref, hi_ref, or_ref, oi_ref, dl_ref, cr, ci_ = refs
        else:
            br_ref, bi_ref, lam_ref, or_ref, oi_ref, cr, ci_ = refs
        t = pl.program_id(1)
        a_r = lam_ref[0:1, :]
        a_i = -lam_ref[1:2, :] if reverse else lam_ref[1:2, :]
        powers = [(a_r, a_i)]
        for _ in range(2):
            p_r, p_i = powers[-1]
            powers.append((p_r * p_r - p_i * p_i, 2.0 * p_r * p_i))
        row = lax.broadcasted_iota(jnp.int32, (8, bw), 0)

        def scan8(x_r, x_i):
            for level, (p_r, p_i) in enumerate(powers):
                s = 1 << level
                keep = (row < 8 - s) if reverse else (row >= s)
                s_r = jnp.where(keep, pltpu.roll(x_r, 8 - s if reverse else s, 0), 0.0)
                s_i = jnp.where(keep, pltpu.roll(x_i, 8 - s if reverse else s, 0), 0.0)
                x_r, x_i = x_r + p_r * s_r - p_i * s_i, x_i + p_r * s_i + p_i * s_r
            return x_r, x_i

        edge = 7 if reverse else 0
        tab_r, tab_i = scan8(jnp.where(row == edge, a_r, 0.0), jnp.where(row == edge, a_i, 0.0))

        @pl.when(t == 0)
        def _():
            cr[...] = jnp.zeros_like(cr)
            ci_[...] = jnp.zeros_like(ci_)
            if reverse:
                dl_ref[...] = jnp.zeros_like(dl_ref)

        if reverse:
            first_block = t == n_t - 1
            halo_r = jnp.where(first_block, 0.0, hr_ref[7:8, :])
            halo_i = jnp.where(first_block, 0.0, hi_ref[7:8, :])

        def group(n, carry):
            g = n_g - 1 - n if reverse else n
            rows = pl.ds(pl.multiple_of(g * 8, 8), 8)
            c_r, c_i = carry[0], carry[1]
            x_r, x_i = scan8(br_ref[rows, :], bi_ref[rows, :])
            x_r, x_i = x_r + tab_r * c_r - tab_i * c_i, x_i + tab_r * c_i + tab_i * c_r
            or_ref[rows, :], oi_ref[rows, :] = x_r, x_i
            out = 7 - edge
            nxt = (x_r[out:out + 1, :], x_i[out:out + 1, :])
            if not reverse:
                return nxt
            before = pl.ds(pl.multiple_of(jnp.maximum(g * 8 - 8, 0), 8), 8)
            h_r = jnp.where(g > 0, sr_ref[before, :][7:8, :], halo_r)
            h_i = jnp.where(g > 0, si_ref[before, :][7:8, :], halo_i)
            s_r = jnp.where(row == 0, h_r, pltpu.roll(sr_ref[rows, :], 1, 0))
            s_i = jnp.where(row == 0, h_i, pltpu.roll(si_ref[rows, :], 1, 0))
            return nxt + (carry[2] + s_r * x_r + s_i * x_i, carry[3] + s_r * x_i - s_i * x_r)

        init = (cr[0:1, :], ci_[0:1, :])
        if reverse:
            init = init + (jnp.zeros((8, bw), F32), jnp.zeros((8, bw), F32))
        fin = lax.fori_loop(0, n_g, group, init, unroll=4 if n_g % 4 == 0 else 1)
        cr[0:1, :], ci_[0:1, :] = fin[0], fin[1]
        if reverse:
            dl_ref[0:1, :] += jnp.sum(fin[2], axis=0, keepdims=True)
            dl_ref[1:2, :] += jnp.sum(fin[3], axis=0, keepdims=True)

    tmap = (lambda c, t: (n_t - 1 - t, c)) if reverse else (lambda c, t: (t, c))
    col = lambda c, t: (0, c)
    ins = [(br, (tl, bw), tmap), (bi, (tl, bw), tmap), (lam, (2, bw), col)]
    outs = [((T, NCH), F32, (tl, bw), tmap)] * 2
    if reverse:
        halo = lambda c, t: (jnp.maximum((n_t - 1 - t) * n_g - 1, 0), c)
        ins += [(xr, (tl, bw), tmap), (xi, (tl, bw), tmap), (xr, (8, bw), halo), (xi, (8, bw), halo)]
        outs += [((2, NCH), F32, (2, bw), col)]
    return _call(body, (NCH // bw, n_t), ins, outs, scratch=[pltpu.VMEM((8, bw), F32), pltpu.VMEM((8, bw), F32)],
                 name="s5_scan_bwd" if reverse else "s5_scan_fwd", sem=("parallel", "arbitrary"))


def _s5_cproj_fwd(xr, xi, cd_re, cd_im, u, d, tr=2048):
    T, D = u.shape
    tr = _tile(T, tr)

    def body(xr_ref, xi_ref, cr_ref, ci_ref, u_ref, d_ref, y_ref, h_ref):
        y = _bdot(xr_ref[...], cr_ref[0]) + _bdot(xi_ref[...], ci_ref[0]) + d_ref[...] * u_ref[...]
        y_ref[...] = y
        h_ref[...] = jax.nn.gelu(y).astype(BF16)

    tile = lambda i, j: (i, j)
    blk = lambda i, j: (j, 0, 0)
    return _call(body, (T // tr, D // LANES),
                 [(xr, (tr, S5_SPB), tile), (xi, (tr, S5_SPB), tile), (cd_re, (1, S5_SPB, LANES), blk), (cd_im, (1, S5_SPB, LANES), blk),
                  (u, (tr, LANES), tile), (d.reshape(1, D), (1, LANES), lambda i, j: (0, j))],
                 [((T, D), F32, (tr, LANES), tile), ((T, D), BF16, (tr, LANES), tile)], name="s5_cproj_fwd", sem=("parallel", "parallel"))


def _s5_cproj_bwd(dy, xr, xi, cd_re, cd_im, u, d, tr=1024):
    T, D = u.shape
    tr = _tile(T, tr)
    nb = D // LANES

    def body(dy_ref, xr_ref, xi_ref, cr_ref, ci_ref, u_ref, d_ref, dxr_ref, dxi_ref, du_ref, dd_ref, dcr_ref, dci_ref):
        g = dy_ref[...]
        dxr_ref[...] = _bdot(g, cr_ref[0], NT)
        dxi_ref[...] = _bdot(g, ci_ref[0], NT)
        du_ref[...] = g * d_ref[...]
        first = pl.program_id(1) == 0
        _accumulate(dd_ref, jnp.sum(g * u_ref[...], axis=0, keepdims=True), first)

        @pl.when(first)
        def _():
            dcr_ref[...] = jnp.zeros_like(dcr_ref)
            dci_ref[...] = jnp.zeros_like(dci_ref)

        dcr_ref[0] += _bdot(xr_ref[...], g, TN)
        dci_ref[0] += _bdot(xi_ref[...], g, TN)

    tile = lambda j, i: (i, j)
    blk = lambda j, i: (j, 0, 0)
    col = lambda j, i: (0, j)
    return _call(body, (nb, T // tr),
                 [(dy, (tr, LANES), tile), (xr, (tr, S5_SPB), tile), (xi, (tr, S5_SPB), tile),
                  (cd_re, (1, S5_SPB, LANES), blk), (cd_im, (1, S5_SPB, LANES), blk), (u, (tr, LANES), tile), (d.reshape(1, D), (1, LANES), col)],
                 [((T, nb * S5_SPB), F32, (tr, S5_SPB), tile)] * 2 + [((T, D), F32, (tr, LANES), tile), ((1, D), F32, (1, LANES), col)]
                 + [((nb, S5_SPB, LANES), F32, (1, S5_SPB, LANES), blk)] * 2, name="s5_cproj_bwd")


def _s5_bproj_bwd(dbr, dbi, bd_re, bd_im, u, du_skip, tr=1024):
    T, D = u.shape
    tr = _tile(T, tr)
    nb = D // LANES

    def body(gr_ref, gi_ref, br_ref, bi_ref, u_ref, ds_ref, du_ref, dbr_ref, dbi_ref):
        g_r, g_i, ub = gr_ref[...], gi_ref[...], u_ref[...]
        du_ref[...] = (ds_ref[...] + _bdot(g_r, br_ref[0], NT) + _bdot(g_i, bi_ref[0], NT)).astype(BF16)

        @pl.when(pl.program_id(1) == 0)
        def _():
            dbr_ref[...] = jnp.zeros_like(dbr_ref)
            dbi_ref[...] = jnp.zeros_like(dbi_ref)

        dbr_ref[0] += _bdot(ub, g_r, TN)
        dbi_ref[0] += _bdot(ub, g_i, TN)

    tile = lambda j, i: (i, j)
    blk = lambda j, i: (j, 0, 0)
    return _call(body, (nb, T // tr),
                 [(dbr, (tr, S5_SPB), tile), (dbi, (tr, S5_SPB), tile), (bd_re, (1, LANES, S5_SPB), blk), (bd_im, (1, LANES, S5_SPB), blk),
                  (u, (tr, LANES), tile), (du_skip, (tr, LANES), tile)],
                 [((T, D), BF16, (tr, LANES), tile)] + [((nb, LANES, S5_SPB), F32, (1, LANES, S5_SPB), blk)] * 2, name="s5_bproj_bwd")


def _s5_gate_fwd(h, vg, tr=256):
    T, D = h.shape
    tr = _tile(T, tr)

    def body(h_ref, a_ref, b_ref, o_ref):
        o_ref[...] = h_ref[...] + a_ref[...] * jax.nn.sigmoid(b_ref[...])

    row = lambda i: (i, 0)
    return _call(body, (T // tr,), [(h, (tr, D), row), (vg, (tr, D), row), (vg, (tr, D), lambda i: (i, 1))],
                 [((T, D), F32, (tr, D), row)], name="s5_gate_fwd", sem=("parallel",))[0]


def _s5_gate_bwd(dh, vg, tr=256):
    T, D = dh.shape
    tr = _tile(T, tr)

    def body(d_ref, a_ref, b_ref, o_ref):
        d = d_ref[...]
        sig = jax.nn.sigmoid(b_ref[...])
        o_ref[:, :D] = (d * sig).astype(BF16)
        o_ref[:, D:] = (d * a_ref[...] * sig * (1.0 - sig)).astype(BF16)

    row = lambda i: (i, 0)
    return _call(body, (T // tr,), [(dh, (tr, D), row), (vg, (tr, D), row), (vg, (tr, D), lambda i: (i, 1))],
                 [((T, 2 * D), BF16, (tr, 2 * D), row)], name="s5_gate_bwd", sem=("parallel",))[0]


def _block_diag(w, transpose):
    g, a, b = w.shape
    if transpose:
        w = w.transpose(0, 2, 1)
        a, b = b, a
    eye = jnp.eye(S5_GPB, dtype=w.dtype)
    return jnp.einsum("jgab,gh->jgahb", w.reshape(g // S5_GPB, S5_GPB, a, b), eye).reshape(g // S5_GPB, S5_GPB * a, S5_GPB * b)


def _block_diag_extract(wd, a, b, transpose):
    if transpose:
        a, b = b, a
    nb = wd.shape[0]
    eye = jnp.eye(S5_GPB, dtype=wd.dtype)
    w = jnp.einsum("jgahb,gh->jgab", wd.reshape(nb, S5_GPB, a, S5_GPB, b), eye).reshape(nb * S5_GPB, a, b)
    return w.transpose(0, 2, 1) if transpose else w


def _mesh_position():
    return lax.axis_index("x"), lax.axis_index("y"), lax.axis_index("c")


def _my_index():
    x, y, c = _mesh_position()
    return 4 * x + 2 * y + c


def _hbm_call(body, arrays, out_shapes, n_sems, name):
    n = len(arrays)
    return pl.pallas_call(
        body, out_shape=[jax.ShapeDtypeStruct(s, d) for s, d in out_shapes],
        in_specs=[pl.BlockSpec(memory_space=pl.ANY)] * n, out_specs=[pl.BlockSpec(memory_space=pl.ANY)] * len(out_shapes),
        scratch_shapes=[pltpu.SemaphoreType.DMA((n_sems,)), pltpu.SemaphoreType.DMA((n_sems,)), pltpu.SemaphoreType.DMA((n,))],
        name=name)(*arrays)


def _all_gather(blocks, name):
    n = len(blocks)
    per = N_DEV - 1

    def body(*refs):
        x_refs, out_refs = refs[:n], refs[n:2 * n]
        send_sems, recv_sems, local_sems = refs[2 * n:]
        x, y, c = _mesh_position()
        me, sibling = (x, y, c), (x, y, 1 - c)
        chips = [(1 - x, y), (x, 1 - y), (1 - x, 1 - y)]

        def copy(a, k, blk, to, src=None):
            slot = out_refs[a].at[4 * blk[0] + 2 * blk[1] + blk[2]]
            return pltpu.make_async_remote_copy(
                src_ref=slot if src is None else src, dst_ref=slot, send_sem=send_sems.at[a * per + k],
                recv_sem=recv_sems.at[a * per + k], device_id=to, device_id_type=pl.DeviceIdType.MESH)

        mine = [pltpu.make_async_copy(x_refs[a], out_refs[a].at[4 * x + 2 * y + c], local_sems.at[a]) for a in range(n)]
        for cp in mine:
            cp.start()
        first = []
        for a in range(n):
            first.append(copy(a, 0, me, sibling, src=x_refs[a]))
            first += [copy(a, 1 + j, me, (*chip, c), src=x_refs[a]) for j, chip in enumerate(chips)]
        for cp in first:
            cp.start()
        passed = []
        for j, chip in enumerate(chips):
            for a in range(n):
                copy(a, 1 + j, (*chip, c), me).wait_recv()
                passed.append(copy(a, 4 + j, (*chip, c), sibling))
                passed[-1].start()
        for a in range(n):
            copy(a, 0, sibling, me).wait_recv()
            for j, chip in enumerate(chips):
                copy(a, 4 + j, (*chip, 1 - c), me).wait_recv()
        for cp in first + passed:
            cp.wait_send()
        for cp in mine:
            cp.wait()

    return _hbm_call(body, blocks, [((N_DEV,) + b.shape, b.dtype) for b in blocks], n * per, name)


def _pair_exchange(parts, name):
    n = len(parts)

    def body(*refs):
        g_refs, got_refs = refs[:n], refs[n:2 * n]
        send_sems, recv_sems = refs[2 * n:]
        x, y, c = _mesh_position()
        places = [(x, y), (1 - x, y), (x, 1 - y), (1 - x, 1 - y)]
        copies = [pltpu.make_async_remote_copy(
            src_ref=g_refs[a].at[4 * px + 2 * py + 1 - c], dst_ref=got_refs[a].at[k],
            send_sem=send_sems.at[4 * a + k], recv_sem=recv_sems.at[4 * a + k],
            device_id=(x, y, 1 - c), device_id_type=pl.DeviceIdType.MESH) for a in range(n) for k, (px, py) in enumerate(places)]
        for cp in copies:
            cp.start()
        for cp in copies:
            cp.wait()

    return pl.pallas_call(
        body, out_shape=[jax.ShapeDtypeStruct((4,) + p_.shape[1:], p_.dtype) for p_ in parts],
        in_specs=[pl.BlockSpec(memory_space=pl.ANY)] * n, out_specs=[pl.BlockSpec(memory_space=pl.ANY)] * n,
        scratch_shapes=[pltpu.SemaphoreType.DMA((4 * n,))] * 2, name=name)(*parts)


def _chip_sums(own, got, tr=PACK_ROWS):
    _, R, cw = own.shape
    cap = min(tr, 1 << (((PACK_ROWS * PACK_W) // cw).bit_length() - 1))
    tr = math.gcd(R, cap)

    def body(a_ref, b_ref, o_ref):
        o_ref[...] = (a_ref[...].astype(F32) + b_ref[...].astype(F32)).astype(o_ref.dtype)

    def own_blk(j, i):
        x, y, c = _mesh_position()
        return (4 * jnp.where(j == 1, x, 1 - x) + 2 * jnp.where(j == 0, y, 1 - y) + c, i, 0)

    return _call(body, (3, R // tr), [(own, (1, tr, cw), own_blk), (got, (1, tr, cw), lambda j, i: (1 + j, i, 0))],
                 [((3, R, cw), own.dtype, (1, tr, cw), lambda j, i: (j, i, 0))], name="chip_sums", sem=("parallel", "parallel"))[0]


def _chip_exchange(sums, name):
    n = len(sums)

    def body(*refs):
        s_refs, out_refs = refs[:n], refs[n:2 * n]
        send_sems, recv_sems = refs[2 * n:]
        x, y, c = _mesh_position()
        chips = [(1 - x, y), (x, 1 - y), (1 - x, 1 - y)]
        copies = [pltpu.make_async_remote_copy(
            src_ref=s_refs[a].at[j], dst_ref=out_refs[a].at[j], send_sem=send_sems.at[3 * a + j], recv_sem=recv_sems.at[3 * a + j],
            device_id=(*chip, c), device_id_type=pl.DeviceIdType.MESH) for a in range(n) for j, chip in enumerate(chips)]
        for cp in copies:
            cp.start()
        for cp in copies:
            cp.wait()

    return pl.pallas_call(
        body, out_shape=[jax.ShapeDtypeStruct(s.shape, s.dtype) for s in sums],
        in_specs=[pl.BlockSpec(memory_space=pl.ANY)] * n, out_specs=[pl.BlockSpec(memory_space=pl.ANY)] * n,
        scratch_shapes=[pltpu.SemaphoreType.DMA((3 * n,))] * 2, name=name)(*sums)


def _adamw_math(g, w, m, v):
    nm = ADAM_B1 * m + (1.0 - ADAM_B1) * g
    nv = ADAM_B2 * v + (1.0 - ADAM_B2) * (g * g)
    c1 = 1.0 - ADAM_B1 ** ADAM_STEP
    c2 = 1.0 - ADAM_B2 ** ADAM_STEP
    return -ADAM_LR * ((nm / c1) / (jnp.sqrt(nv / c2) + ADAM_EPS) + ADAM_WD * w), nm, nv


def _adamw(parts, row0, w, m, v, name, window_n8=None):
    R, C = w.shape
    cw = parts[0][0].shape[2]
    cap = min(PACK_ROWS, 1 << (((PACK_ROWS * PACK_W) // cw).bit_length() - 1))
    tr = math.gcd(math.gcd(R, cap), row0 or R)
    assert R % tr == 0 and row0 % tr == 0
    n = len(parts)

    def body(*refs):
        p_refs = refs[:n]
        w_ref, m_ref, v_ref, g_ref, d_ref, nm_ref, nv_ref = refs[n:]
        g = None
        for p_ref, (_, slots) in zip(p_refs, parts):
            for s in range(1 if slots == "me" else len(slots)):
                term = p_ref[s].astype(F32)
                g = term if g is None else g + term
        if window_n8 is not None:
            off = (window_n8 * _my_index()) % LANES
            g = pltpu.roll(g, (cw - off) % cw, 1)[:, :C]
        d, nm, nv = _adamw_math(g, w_ref[...], m_ref[...], v_ref[...])
        g_ref[...], d_ref[...], nm_ref[...], nv_ref[...] = g, d, nm, nv

    row = lambda i: (i, 0)
    r0 = row0 // tr
    ins = []
    for arr, slots in parts:
        if slots == "me":
            ins.append((arr, (1, tr, cw), lambda i: (_my_index(), r0 + i, 0)))
            continue
        assert list(slots) == list(range(slots[0], slots[0] + len(slots))) and slots[0] % len(slots) == 0
        s0 = slots[0] // len(slots)
        ins.append((arr, (len(slots), tr, cw), lambda i, s0=s0: (s0, r0 + i, 0)))
    return _call(body, (R // tr,), ins + [(w, (tr, C), row), (m, (tr, C), row), (v, (tr, C), row)],
                 [((R, C), F32, (tr, C), row)] * 4, name=name, sem=("parallel",))


def _window_geometry(n8):
    offs = [(d * n8) % LANES for d in range(N_DEV)]
    starts = [(d * n8) // LANES for d in range(N_DEV)]
    blocks = max(-(-(o + n8) // LANES) for o in offs)
    return starts, blocks, max(starts) + blocks


def _to_window(wpad, n8, tr=256):
    R, cw = wpad.shape
    tr = _tile(R, tr)

    def body(x_ref, o_ref):
        o_ref[...] = pltpu.roll(x_ref[...], (n8 * _my_index()) % LANES, 1).astype(BF16)

    row = lambda i: (i, 0)
    return _call(body, (R // tr,), [(wpad, (tr, cw), row)], [((R, cw), BF16, (tr, cw), row)], name="to_window", sem=("parallel",))[0]


def _from_windows(win, row0, rows, n8, sections, tr=128):
    starts, blocks, total = _window_geometry(n8)
    cw = win.shape[2]
    tr = _tile(rows, tr)
    r0 = row0 // tr
    assert sum(sections) == total * LANES and all(s % LANES == 0 for s in sections)
    n_sec = len(sections)

    def body(*refs):
        w_ref, o_refs, acc = refs[0], refs[1:1 + n_sec], refs[-1]
        acc[...] = jnp.zeros_like(acc)
        for d in range(N_DEV):
            cols = pl.ds(starts[d] * LANES, cw)
            acc[:, cols] += w_ref[d].astype(F32)
        off = 0
        for o_ref, width in zip(o_refs, sections):
            o_ref[...] = acc[:, off:off + width].astype(BF16)
            off += width

    return _call(body, (rows // tr,), [(win, (N_DEV, tr, cw), lambda i: (0, r0 + i, 0))],
                 [((rows, s), BF16, (tr, s), lambda i: (i, 0)) for s in sections],
                 scratch=[pltpu.VMEM((tr, total * LANES), F32)], name="from_windows", sem=("parallel",))


def _pack(flat_pieces, dtype, lead=()):
    cat = jnp.concatenate([p_.astype(dtype) for p_ in flat_pieces], axis=-1)
    n = cat.shape[-1]
    quantum = PACK_ROWS * PACK_W
    total = -(-n // quantum) * quantum
    cat = jnp.pad(cat, [(0, 0)] * len(lead) + [(0, total - n)])
    return cat.reshape(lead + (total // PACK_W, PACK_W)), n


REPLICATED = ("norm_mix", "norm_mlp", "norm_ple", "norm_final", "gdn_a_log", "gdn_dt_bias", "gdn_o_norm",
              "s5_lam_re", "s5_lam_im", "s5_log_step", "s5_b_re", "s5_b_im", "s5_c_re", "s5_c_im")
WEIGHTS = ("norm_mix", "norm_mlp", "norm_ple", "norm_final", "gdn_w_in", "gdn_conv_w", "gdn_a_log", "gdn_dt_bias",
           "gdn_o_norm", "gdn_w_out", "s5_w_in", "s5_lam_re", "s5_lam_im", "s5_log_step", "s5_b_re", "s5_b_im",
           "s5_c_re", "s5_c_im", "s5_d", "s5_w_out", "mlp_w_up", "mlp_w_down", "ple_w_proj", "ple_w_gate")
ROW_GROUP = ("mlp_w_down", "gdn_w_out", "s5_w_in", "ple_w_gate")
COL_SHARDED = ("mlp_w_up", "s5_w_out", "ple_w_proj")


def _rows2d(a):
    return a.reshape(-1, a.shape[-1])


def _misc_pack(conv, s5d):
    cw = conv.shape[-1]
    rows = jnp.concatenate([_rows2d(conv), jnp.pad(s5d, ((0, 0), (0, cw - s5d.shape[-1])))], axis=0)
    return jnp.pad(rows, ((0, -rows.shape[0] % 8), (0, 0)))


def _misc_unpack(a, conv_rows, s5d_shape):
    return a[:conv_rows], a[conv_rows:conv_rows + s5d_shape[0], :s5d_shape[1]]


def kernel(x, p, norm_mix, norm_mlp, norm_ple, norm_final, gdn_w_in, gdn_conv_w, gdn_a_log, gdn_dt_bias, gdn_o_norm, gdn_w_out, s5_w_in, s5_lam_re, s5_lam_im, s5_log_step, s5_b_re, s5_b_im, s5_c_re, s5_c_im, s5_d, s5_w_out, mlp_w_up, mlp_w_down, ple_w_proj, ple_w_gate, loss_target, m_norm_mix, m_norm_mlp, m_norm_ple, m_norm_final, m_gdn_w_in, m_gdn_conv_w, m_gdn_a_log, m_gdn_dt_bias, m_gdn_o_norm, m_gdn_w_out, m_s5_w_in, m_s5_lam_re, m_s5_lam_im, m_s5_log_step, m_s5_b_re, m_s5_b_im, m_s5_c_re, m_s5_c_im, m_s5_d, m_s5_w_out, m_mlp_w_up, m_mlp_w_down, m_ple_w_proj, m_ple_w_gate, v_norm_mix, v_norm_mlp, v_norm_ple, v_norm_final, v_gdn_w_in, v_gdn_conv_w, v_gdn_a_log, v_gdn_dt_bias, v_gdn_o_norm, v_gdn_w_out, v_s5_w_in, v_s5_lam_re, v_s5_lam_im, v_s5_log_step, v_s5_b_re, v_s5_b_im, v_s5_c_re, v_s5_c_im, v_s5_d, v_s5_w_out, v_mlp_w_up, v_mlp_w_down, v_ple_w_proj, v_ple_w_gate):
    args = dict(locals())
    w = {n: args[n] for n in WEIGHTS}
    mom = {n: args["m_" + n] for n in WEIGHTS}
    vel = {n: args["v_" + n] for n in WEIGHTS}
    depth = norm_mix.shape[0]
    T, D = x.shape[1], x.shape[2]
    hv = gdn_a_log.shape[1]
    vd = hv * HEAD
    n_gdn, n_s5 = gdn_w_in.shape[0], s5_w_in.shape[0]
    cw = gdn_conv_w.shape[2]
    cd = cw * N_DEV
    hk = (cd - vd) // (2 * HEAD)
    assert hv == 2 * hk and 2 * hv <= LANES and T % SUPER == 0 and SUPER % CHUNK == 0
    G, P = s5_lam_re.shape[1], s5_lam_re.shape[2]
    assert P == S5_STATE and G * S5_CH == D and G % S5_GPB == 0 and D // N_DEV <= cw
    n8 = gdn_w_in.shape[2]
    win_starts, win_blocks, win_total = _window_geometry(n8)
    cwin = win_blocks * LANES
    assert win_total * LANES == cd + vd + LANES

    row_off, off = {}, 0
    for n in ROW_GROUP:
        row_off[n] = off
        off += w[n].shape[0] * w[n].shape[1]
    gathered = _all_gather(
        [_to_window(jnp.pad(_rows2d(gdn_w_in), ((0, 0), (0, cwin - n8))), n8),
         jnp.concatenate([_rows2d(w[n]) for n in ROW_GROUP], axis=0).astype(BF16)]
        + [_rows2d(w[n]).astype(BF16) for n in COL_SHARDED] + [_misc_pack(gdn_conv_w, s5_d)], "gather_weights")
    g_win, g_row, g_misc = gathered[0], gathered[1], gathered[-1]
    g_col = dict(zip(COL_SHARDED, gathered[2:-1]))
    conv_full = g_misc[:, :n_gdn * 4].reshape(N_DEV, n_gdn, 4, cw).transpose(1, 2, 0, 3).reshape(n_gdn, 4, cd)
    s5d_full = g_misc[:, n_gdn * 4:n_gdn * 4 + n_s5, :D // N_DEV].transpose(1, 0, 2).reshape(n_s5, D)

    def weight(name, l):
        r = w[name].shape[1]
        if name in ROW_GROUP:
            return Sharded(g_row, 0, row_off[name] + l * r, r)
        return Sharded(g_col[name], 1, l * r, r)

    h = x[0]
    tgt = loss_target[0]
    grads = {n: [None] * w[n].shape[0] for n in WEIGHTS if n != "norm_final"}
    saved = []
    add = lambda acc, r: (r + acc,)

    for i in range(depth):
        j = i // 2
        sv = {"h0": h}
        hn = _rms_fwd(h, norm_mix[i])
        sv["hn"] = hn
        if i % 2 == 0:
            w_in = _from_windows(g_win, j * D, D, n8, (cd, vd, LANES))
            pq = _mm(hn, w_in[0], name="gdn_in_qkv")[0]
            pz = _mm(hn, w_in[1], name="gdn_in_z")[0]
            ba = _mm(hn, w_in[2], name="gdn_in_ba")[0]
            qkv = _gdn_pre_fwd(pq, conv_full[j], hk)
            pv = jnp.pad(jnp.stack([gdn_a_log[j], gdn_dt_bias[j]]), ((0, 0), (hv, LANES - 2 * hv)))
            g2 = _gates_fwd(ba, pv, hv)
            bb = jnp.repeat(g2[:, :hv], HEAD, axis=1)
            gb = jnp.repeat(g2[:, hv:2 * hv], HEAD, axis=1)
            u, ww, gc, tinv = _gdn_a_fwd(qkv, gb, bb, hk, hv)
            o, vn, sall = _gdn_b_fwd(qkv, u, ww, gc, hk, hv)
            on = _ogate_fwd(o, pz, gdn_o_norm[j])
            h = _mm(on, weight("gdn_w_out", j), epi=add, extras=(h,), name="gdn_out")[0]
            sv.update(w_in=w_in, pq=pq, pz=pz, ba=ba, pv=pv, qkv=qkv, bb=bb, u=u, ww=ww, gc=gc, tinv=tinv, o=o, vn=vn, sall=sall, on=on)
        else:
            uu = _mm(hn, weight("s5_w_in", j), name="s5_in")[0]
            b_re_t, b_im_t = s5_b_re[j].transpose(2, 0, 1), s5_b_im[j].transpose(2, 0, 1)
            ls = s5_log_step[j].reshape(G, 1)
            ar, ai, bbr, bbi = _s5_params_fwd(s5_lam_re[j], s5_lam_im[j], ls, b_re_t, b_im_t)
            lam = jnp.stack([ar.reshape(-1), ai.reshape(-1)])
            bd_re = _block_diag(bbr.transpose(1, 2, 0), transpose=True).astype(BF16)
            bd_im = _block_diag(bbi.transpose(1, 2, 0), transpose=True).astype(BF16)
            cd_re = _block_diag(s5_c_re[j], transpose=True).astype(BF16)
            cd_im = _block_diag(-s5_c_im[j], transpose=True).astype(BF16)
            bur, bui = _s5_bproj_fwd(uu, bd_re, bd_im)
            xr, xi = _s5_scan(bur, bui, lam, reverse=False)
            dsk = s5d_full[j]
            yy, hact = _s5_cproj_fwd(xr, xi, cd_re, cd_im, uu, dsk)
            vg = _mm(hact, weight("s5_w_out", j), name="s5_out")[0]
            h = _s5_gate_fwd(h, vg)
            sv.update(uu=uu, b_re_t=b_re_t, b_im_t=b_im_t, ls=ls, lam=lam, bd_re=bd_re, bd_im=bd_im, cd_re=cd_re, cd_im=cd_im,
                      xr=xr, xi=xi, dsk=dsk, yy=yy, hact=hact, vg=vg)
        sv["h1"] = h
        hm = _rms_fwd(h, norm_mlp[i])
        up, act = _mm(hm, weight("mlp_w_up", i), out_dtypes=(F32, BF16),
                      epi=lambda acc: (acc, jnp.square(jnp.maximum(acc, 0.0))), name="mlp_up")
        h = _mm(act, weight("mlp_w_down", i), epi=add, extras=(h,), name="mlp_down")[0]
        sv.update(hm=hm, up=up, act=act, h2=h)
        hp = _rms_fwd(h, norm_ple[i])
        s_gate = _mm(hp, weight("ple_w_gate", i), name="ple_gate")[0]
        pp = _mm(p[i, 0], weight("ple_w_proj", i), name="ple_proj")[0]
        h = _ple_fwd(h, s_gate, pp)
        sv.update(hp=hp, s_gate=s_gate, pp=pp)
        saved.append(sv)

    dh, d_norm_final, loss_part = _loss_fwd_bwd(h, norm_final, tgt)
    loss = lax.psum(loss_part[0, 0], MESH_AXES)

    dw = lambda a, b_, axis, name: _mm(a, b_, "tn", out_dtypes=(BF16,), out_axis=axis, name=name)[0]
    for i in reversed(range(depth)):
        j = i // 2
        sv = saved[i]
        ds, dpp = _ple_bwd(dh, sv["s_gate"], sv["pp"])
        grads["ple_w_proj"][i] = dw(p[i, 0], dpp, 1, "ple_proj_dw")
        grads["ple_w_gate"][i] = dw(sv["hp"], ds, 0, "ple_gate_dw")
        d_hp = _mm(ds, weight("ple_w_gate", i), "nt", name="ple_gate_dx")[0]
        dh, dh_b, grads["norm_ple"][i] = _rms_bwd(d_hp, sv["h2"], norm_ple[i], dh)
        grads["mlp_w_down"][i] = dw(sv["act"], dh_b, 0, "mlp_down_dw")
        d_up = _mm(dh_b, weight("mlp_w_down", i), "nt", out_dtypes=(BF16,),
                   epi=lambda acc, up_: (acc * 2.0 * jnp.maximum(up_, 0.0),), extras=(sv["up"],), name="mlp_down_dx")[0]
        grads["mlp_w_up"][i] = dw(sv["hm"], d_up, 1, "mlp_up_dw")
        d_hm = _mm(d_up, weight("mlp_w_up", i), "nt", name="mlp_up_dx")[0]
        dh, dh_b, grads["norm_mlp"][i] = _rms_bwd(d_hm, sv["h1"], norm_mlp[i], dh)
        if i % 2 == 0:
            grads["gdn_w_out"][j] = dw(sv["on"], dh_b, 0, "gdn_out_dw")
            d_on = _mm(dh_b, weight("gdn_w_out", j), "nt", name="gdn_out_dx")[0]
            d_o, d_z, grads["gdn_o_norm"][j] = _ogate_bwd(d_on, sv["o"], sv["pz"], gdn_o_norm[j])
            dq_b, dk_b, dgc_b, d_u, d_w = _gdn_b_bwd(d_o, sv["qkv"], sv["ww"], sv["gc"], sv["vn"], sv["sall"], hk, hv)
            dk_a, d_v, d_bb, d_gb = _gdn_a_bwd(d_u, d_w, dgc_b, sv["qkv"], sv["bb"], sv["gc"], sv["tinv"], sv["u"], sv["ww"], hk, hv)
            d_qkv = jnp.concatenate([_pair_sum(dq_b, None), _pair_sum(dk_a, dk_b), d_v], axis=1)
            d_c, grads["gdn_conv_w"][j] = _gdn_pre_bwd(d_qkv, sv["pq"], conv_full[j], hk)
            d_pq = _gdn_conv_bwd(d_c, conv_full[j])
            d_g2 = jnp.pad(jnp.concatenate([d_bb[:, ::HEAD], d_gb[:, ::HEAD]], axis=1), ((0, 0), (0, LANES - 2 * hv)))
            d_ba, d_pv = _gates_bwd(d_g2, sv["ba"], sv["pv"], hv)
            grads["gdn_a_log"][j] = d_pv[0, hv:2 * hv]
            grads["gdn_dt_bias"][j] = d_pv[1, hv:2 * hv]
            hn, w_in = sv["hn"], sv["w_in"]
            dw_nat = jnp.concatenate([dw(hn, d_pq, None, "gdn_in_qkv_dw"), dw(hn, d_z, None, "gdn_in_z_dw"),
                                      dw(hn, d_ba, None, "gdn_in_ba_dw")], axis=1)
            grads["gdn_w_in"][j] = jnp.stack([dw_nat[:, s * LANES:s * LANES + cwin] for s in win_starts])
            d_hn = _mm(d_pq, w_in[0], "nt", name="gdn_in_qkv_dx")[0]
            d_hn = _mm(d_z, w_in[1], "nt", epi=add, extras=(d_hn,), name="gdn_in_z_dx")[0]
            d_hn = _mm(d_ba, w_in[2], "nt", epi=add, extras=(d_hn,), name="gdn_in_ba_dx")[0]
        else:
            d_vg = _s5_gate_bwd(dh, sv["vg"])
            grads["s5_w_out"][j] = dw(sv["hact"], d_vg, 1, "s5_out_dw")

            def gelu_bwd(acc, y_):
                _, vjp = jax.vjp(jax.nn.gelu, y_)
                return (vjp(acc)[0],)

            d_y = _mm(d_vg, weight("s5_w_out", j), "nt", epi=gelu_bwd, extras=(sv["yy"],), name="s5_out_dx")[0]
            d_xr, d_xi, du_skip, d_dsk, d_cdr, d_cdi = _s5_cproj_bwd(d_y, sv["xr"], sv["xi"], sv["cd_re"], sv["cd_im"], sv["uu"], sv["dsk"])
            grads["s5_d"][j] = d_dsk
            grads["s5_c_re"][j] = _block_diag_extract(d_cdr, S5_CH, S5_STATE, transpose=True)
            grads["s5_c_im"][j] = -_block_diag_extract(d_cdi, S5_CH, S5_STATE, transpose=True)
            d_bur, d_bui, d_lam = _s5_scan(d_xr, d_xi, sv["lam"], reverse=True, xr=sv["xr"], xi=sv["xi"])
            d_uu, d_bdr, d_bdi = _s5_bproj_bwd(d_bur, d_bui, sv["bd_re"], sv["bd_im"], sv["uu"], du_skip)
            d_bbr = _block_diag_extract(d_bdr, S5_STATE, S5_CH, transpose=True).transpose(2, 0, 1)
            d_bbi = _block_diag_extract(d_bdi, S5_STATE, S5_CH, transpose=True).transpose(2, 0, 1)
            d_lr, d_li, d_ls, d_br_t, d_bi_t = _s5_params_bwd(
                s5_lam_re[j], s5_lam_im[j], sv["ls"], sv["b_re_t"], sv["b_im_t"],
                d_lam[0].reshape(G, P), d_lam[1].reshape(G, P), d_bbr, d_bbi)
            grads["s5_lam_re"][j], grads["s5_lam_im"][j], grads["s5_log_step"][j] = d_lr, d_li, d_ls.reshape(G)
            grads["s5_b_re"][j], grads["s5_b_im"][j] = d_br_t.transpose(1, 2, 0), d_bi_t.transpose(1, 2, 0)
            grads["s5_w_in"][j] = dw(sv["hn"], d_uu, 0, "s5_in_dw")
            d_hn = _mm(d_uu, weight("s5_w_in", j), "nt", name="s5_in_dx")[0]
        dh, _, grads["norm_mix"][i] = _rms_bwd(d_hn, sv["h0"], norm_mix[i], dh)

    out = {}
    layers = lambda name: jnp.concatenate(grads[name], axis=1)
    d_conv = jnp.stack(grads["gdn_conv_w"]).reshape(n_gdn * 4, N_DEV, cw).transpose(1, 0, 2)
    d_s5d = jnp.stack([g.reshape(N_DEV, D // N_DEV) for g in grads["s5_d"]], axis=1)
    d_misc = jnp.concatenate([d_conv, jnp.pad(d_s5d, ((0, 0), (0, 0), (0, cw - D // N_DEV)))], axis=1)
    d_misc = jnp.pad(d_misc, ((0, 0), (0, -d_misc.shape[1] % 8), (0, 0)))
    contributions = ([layers("gdn_w_in"), jnp.concatenate([layers(n) for n in ROW_GROUP], axis=1)]
                     + [layers(n) for n in COL_SHARDED] + [d_misc])
    got = _pair_exchange(contributions, "exchange_grads_pair")
    arrived = _chip_exchange([_chip_sums(a, g) for a, g in zip(contributions, got)], "exchange_grads_chips")
    recv = [[(a, "me"), (g, (0,)), (r, (0, 1, 2))] for a, g, r in zip(contributions, got, arrived)]
    r_win, r_row, r_misc = recv[0], recv[1], recv[-1]
    r_col = dict(zip(COL_SHARDED, recv[2:-1]))

    def update(name, parts, row0, **kw):
        res = _adamw(parts, row0, _rows2d(w[name]), _rows2d(mom[name]), _rows2d(vel[name]), "adamw_" + name, **kw)
        out[name] = [r.reshape(w[name].shape) for r in res]

    update("gdn_w_in", r_win, 0, window_n8=n8)
    for n in ROW_GROUP:
        update(n, r_row, row_off[n])
    for n in COL_SHARDED:
        update(n, r_col[n], 0)
    res = _adamw(r_misc, 0, _misc_pack(gdn_conv_w, s5_d), _misc_pack(mom["gdn_conv_w"], mom["s5_d"]),
                 _misc_pack(vel["gdn_conv_w"], vel["s5_d"]), "adamw_misc")
    unpacked = [_misc_unpack(r, n_gdn * 4, s5_d.shape) for r in res]
    out["gdn_conv_w"] = [u_[0].reshape(gdn_conv_w.shape) for u_ in unpacked]
    out["s5_d"] = [u_[1] for u_ in unpacked]

    rep_g = {n: (d_norm_final[0] if n == "norm_final" else jnp.stack([g.reshape(w[n].shape[1:]) for g in grads[n]])) for n in REPLICATED}
    flat_r = lambda d: [d[n].reshape(-1) for n in REPLICATED]
    pg, _ = _pack(flat_r(rep_g), F32)
    parts_r = _all_gather([pg], "gather_small_grads")[0]
    pw, _ = _pack(flat_r(w), F32)
    pm, _ = _pack(flat_r(mom), F32)
    pvv, _ = _pack(flat_r(vel), F32)
    res = [r.reshape(-1) for r in _adamw([(parts_r, tuple(range(N_DEV)))], 0, pw, pm, pvv, "adamw_replicated")]
    off = 0
    for name in REPLICATED:
        n = w[name].size
        out[name] = [res[k][off:off + n].reshape(w[name].shape) for k in range(4)]
        off += n

    grad_x = dh[None]
    return (loss, grad_x, *[out[n][0] for n in WEIGHTS], *[out[n][1] for n in WEIGHTS],
            *[out[n][2] for n in WEIGHTS], *[out[n][3] for n in WEIGHTS])
```
